```python
import jax, jax.numpy as jnp
from jax import lax
import numpy as np

D_MODEL = 2048
BATCH = 2
SEQ = 4096
DEPTH = 4
DEC_BATCH = 8
DEC_SEQ = 4
PAST_LEN = 16384
PAGE_SIZE = 128

N_EVEN = (DEPTH + 1) // 2
N_ODD = DEPTH // 2
GDN_HK = 8
GDN_HV = 16
GDN_DK = 128
GDN_DV = 128
GDN_CONV = 4
GDN_CHUNK = 64
GDN_CONV_DIM = 2 * GDN_HK * GDN_DK + GDN_HV * GDN_DV
SC_DIM = D_MODEL // 2
SC_WIDTH = 3
IN_E = GDN_CONV_DIM + GDN_HV * GDN_DV + 2 * GDN_HV + 3 * SC_DIM
MIX_E = GDN_HV * GDN_DV + SC_DIM
NSA_H = 16
NSA_HKV = 4
NSA_G = NSA_H // NSA_HKV
NSA_HD = 128
CMP_BLOCK = 64
SEL_TOPK = 16
WINDOW = 512
WIN_QBLOCK = 128
SEL_QBLOCK = 64
KV_ROW = 2 * NSA_HKV * NSA_HD
IN_O = NSA_H * NSA_HD + 3 * KV_ROW + 3 * NSA_H
D_FF = 4 * D_MODEL
EPS = 1e-6
NEG = -1e30
FORCE = 1e6

kernel_name = 'hybrid_gdn_shortconv_nsa_adaln_step'


def _rmsnorm(x, g):
    xf = x.astype(jnp.float32)
    y = xf * lax.rsqrt(jnp.mean(xf * xf, axis=-1, keepdims=True) + EPS)
    return (y * g.astype(jnp.float32)).astype(x.dtype)


def _l2norm(x):
    xf = x.astype(jnp.float32)
    return xf * lax.rsqrt(jnp.sum(xf * xf, axis=-1, keepdims=True) + EPS)


def _causal_dwconv(x, w, buf):
    width = w.shape[0]
    T = x.shape[1]
    xp = jnp.concatenate([buf.astype(x.dtype), x], axis=1)
    y = xp[:, 0:T] * w[0]
    for j in range(1, width):
        y = y + xp[:, j:j + T] * w[j]
    return y, xp[:, xp.shape[1] - (width - 1):]


def _alibi_slopes():
    h = jnp.arange(1, NSA_H + 1, dtype=jnp.float32)
    return jnp.exp2(-8.0 * h / NSA_H).reshape(NSA_HKV, NSA_G)


def _masked_softmax(s, valid):
    s = jnp.where(valid, s, NEG)
    m = jnp.max(s, axis=-1, keepdims=True)
    p = jnp.where(valid, jnp.exp(s - m), 0.0)
    return p / jnp.maximum(jnp.sum(p, axis=-1, keepdims=True), 1e-30)


def _gdn_chunked(q, k, v, g, beta, s0):
    B, T, H, DK = q.shape
    DV = v.shape[-1]
    C = GDN_CHUNK
    n = -(-T // C)
    pad = n * C - T

    def blk(a):
        a = jnp.pad(a, [(0, 0), (0, pad)] + [(0, 0)] * (a.ndim - 2))
        a = a.reshape((B, n, C) + a.shape[2:])
        return a.transpose((1, 0, 3, 2, 4) if a.ndim == 5 else (1, 0, 3, 2))

    q, k, v, g, beta = blk(q), blk(k), blk(v), blk(g), blk(beta)
    q = q * (DK ** -0.5)
    gc = jnp.cumsum(g, axis=-1)
    tril = jnp.tril(jnp.ones((C, C), dtype=bool))
    strict = jnp.tril(jnp.ones((C, C), dtype=bool), -1)
    diff = gc[..., :, None] - gc[..., None, :]
    decay = jnp.where(tril, jnp.exp(jnp.where(tril, diff, 0.0)), 0.0)
    kb = k * beta[..., None]
    a_mat = jnp.einsum('nbhid,nbhjd->nbhij', kb, k) * decay
    m = jnp.eye(C, dtype=jnp.float32) + jnp.where(strict, a_mat, 0.0)
    rhs = jnp.concatenate([v * beta[..., None], kb * jnp.exp(gc)[..., None]], axis=-1)
    sol = lax.linalg.triangular_solve(m, rhs, left_side=True, lower=True, unit_diagonal=True)
    u, w = sol[..., :DV], sol[..., DV:]
    qk = jnp.einsum('nbhid,nbhjd->nbhij', q, k) * decay
    qg = q * jnp.exp(gc)[..., None]
    kd = k * jnp.exp(gc[..., -1:] - gc)[..., None]
    glast = jnp.exp(gc[..., -1])

    def step(S, inp):
        qg_i, kd_i, u_i, w_i, qk_i, gl_i = inp
        v_new = u_i - jnp.einsum('bhck,bhkv->bhcv', w_i, S)
        o = jnp.einsum('bhck,bhkv->bhcv', qg_i, S) + jnp.einsum('bhij,bhjv->bhiv', qk_i, v_new)
        S = S * gl_i[..., None, None] + jnp.einsum('bhck,bhcv->bhkv', kd_i, v_new)
        return S, o

    S, o = lax.scan(step, s0, (qg, kd, u, w, qk, glast))
    o = o.transpose(1, 0, 3, 2, 4).reshape(B, n * C, H, DV)[:, :T]
    return o, S


def _even_mixer(h, conv_buf, s0, sc_buf, w_in, conv_w, a_log, dt_bias, norm_g, sc_w, w_out):
    B, T, _ = h.shape
    nqk = GDN_HK * GDN_DK
    nv = GDN_HV * GDN_DV
    proj = h @ w_in
    offs = np.cumsum([GDN_CONV_DIM, nv, GDN_HV, GDN_HV, SC_DIM, SC_DIM]).tolist()
    qkv, z, a, b, gate_b, gate_c, hx = jnp.split(proj, offs, axis=-1)
    qkv, conv_new = _causal_dwconv(qkv, conv_w, conv_buf)
    qkv = jax.nn.silu(qkv.astype(jnp.float32))
    q = _l2norm(qkv[..., :nqk].reshape(B, T, GDN_HK, GDN_DK))
    k = _l2norm(qkv[..., nqk:2 * nqk].reshape(B, T, GDN_HK, GDN_DK))
    v = qkv[..., 2 * nqk:].reshape(B, T, GDN_HV, GDN_DV)
    rep = GDN_HV // GDN_HK
    q = jnp.repeat(q, rep, axis=2)
    k = jnp.repeat(k, rep, axis=2)
    beta = jax.nn.sigmoid(b.astype(jnp.float32))
    g = -jnp.exp(a_log.astype(jnp.float32)) * jax.nn.softplus(a.astype(jnp.float32) + dt_bias.astype(jnp.float32))
    o, s_new = _gdn_chunked(q, k, v, g, beta, s0.astype(jnp.float32))
    o = _rmsnorm(o, norm_g) * jax.nn.silu(z.astype(jnp.float32).reshape(B, T, GDN_HV, GDN_DV))
    u = gate_c * hx
    cu, sc_new = _causal_dwconv(u, sc_w, sc_buf)
    y_b = gate_b * cu
    mix = jnp.concatenate([o.reshape(B, T, nv).astype(h.dtype), y_b.astype(h.dtype)], axis=-1) @ w_out
    return mix, s_new, conv_new, sc_new


def _nsa_project(h, w_in):
    B, T, _ = h.shape
    proj = h @ w_in
    nq = NSA_H * NSA_HD
    q = proj[..., :nq].reshape(B, T, NSA_HKV, NSA_G, NSA_HD)
    kv = proj[..., nq:nq + 3 * KV_ROW].reshape(B, T, 3, 2, NSA_HKV, NSA_HD)
    gates = jax.nn.sigmoid(proj[..., nq + 3 * KV_ROW:].astype(jnp.float32)).reshape(B, T, 3, NSA_HKV, NSA_G, 1)
    return q, kv[:, :, 0], kv[:, :, 1], kv[:, :, 2], gates


def _attn(q, k, v, dist, valid, slopes):
    s = jnp.einsum('...qhgd,...khd->...qhgk', q, k).astype(jnp.float32) * (NSA_HD ** -0.5)
    s = s - slopes[:, :, None] * jnp.abs(dist)[..., :, None, None, :]
    p = _masked_softmax(s, valid[..., :, None, None, :])
    o = jnp.einsum('...qhgk,...khd->...qhgd', p.astype(v.dtype), v)
    return o, p


def _cmp_branch(q, kv_full, qpos, pool_w, proj_w, slopes):
    B, Tk = kv_full.shape[:2]
    nc = Tk // CMP_BLOCK
    blocks = kv_full[:, :nc * CMP_BLOCK].reshape(B, nc, CMP_BLOCK, 2, NSA_HKV, NSA_HD)
    pooled = jnp.einsum('bnlchd,lch->bnchd', blocks, pool_w)
    cmp = jnp.einsum('bnchd,chde->bnche', pooled, proj_w)
    start = jnp.arange(nc) * CMP_BLOCK
    valid = (start + CMP_BLOCK - 1)[None, :] <= qpos[:, None]
    dist = qpos[:, None] - (start + (CMP_BLOCK - 1) / 2)[None, :]
    return _attn(q, cmp[:, :, 0], cmp[:, :, 1], dist, valid, slopes)


def _select(p_cmp, qpos, n_blocks):
    imp = jnp.sum(p_cmp, axis=-2)
    imp = jnp.pad(imp, [(0, 0), (0, 0), (0, 0), (0, n_blocks - imp.shape[-1])])
    blk = jnp.arange(n_blocks)
    cur = (qpos // CMP_BLOCK)[:, None, None]
    score = jnp.where(blk < cur, imp, NEG)
    score = jnp.where((blk == 0) | (blk == cur), FORCE, score)
    vals, idx = lax.top_k(score, min(SEL_TOPK, n_blocks))
    return idx, vals > 0.5 * NEG


def _sel_attn(q, kvg, kpos, valid, qpos, slopes):
    B, Q = q.shape[:2]
    nk = kvg.shape[3]
    dist = qpos[None, :, None, None, None] - kpos
    ok = (valid[..., None] & (dist >= 0)).reshape(B, Q, NSA_HKV, 1, nk * CMP_BLOCK)
    s = jnp.einsum('bqhgd,bqhkld->bqhgkl', q, kvg[..., 0, :]).astype(jnp.float32) * (NSA_HD ** -0.5)
    s = s.reshape(B, Q, NSA_HKV, NSA_G, nk * CMP_BLOCK) - slopes[:, :, None] * dist.reshape(B, Q, NSA_HKV, 1, nk * CMP_BLOCK)
    p = _masked_softmax(s, ok).reshape(B, Q, NSA_HKV, NSA_G, nk, CMP_BLOCK)
    return jnp.einsum('bqhgkl,bqhkld->bqhgd', p.astype(kvg.dtype), kvg[..., 1, :])


def _combine(h, q, o_c, o_s, o_w, gates, w_out):
    B, T = h.shape[:2]
    o = gates[:, :, 0] * o_c + gates[:, :, 1] * o_s + gates[:, :, 2] * o_w
    return o.reshape(B, T, NSA_H * NSA_HD).astype(h.dtype) @ w_out


def _nsa_prompt(h, w_in, pool_w, proj_w, w_out):
    B, T, _ = h.shape
    slopes = _alibi_slopes()
    q, kv_c, kv_s, kv_w, gates = _nsa_project(h, w_in)
    qpos = jnp.arange(T)
    o_c, p_c = _cmp_branch(q, kv_c, qpos, pool_w, proj_w, slopes)
    idx, valid = _select(p_c, qpos, -(-T // CMP_BLOCK))
    nq = T // SEL_QBLOCK
    b_ix = jnp.arange(B)[:, None, None, None, None]
    h_ix = jnp.arange(NSA_HKV)[None, None, :, None, None]

    def sel_block(args):
        qb, ib, vb, pb = args
        kpos = ib[..., None] * CMP_BLOCK + jnp.arange(CMP_BLOCK)
        kvg = kv_s[b_ix, kpos, :, h_ix]
        return _sel_attn(qb, kvg, kpos, vb, pb, slopes)

    split = lambda a: jnp.moveaxis(a.reshape((B, nq, SEL_QBLOCK) + a.shape[2:]), 1, 0)
    o_s = lax.map(sel_block, (split(q), split(idx), split(valid), qpos.reshape(nq, SEL_QBLOCK)))
    o_s = jnp.moveaxis(o_s, 0, 1).reshape(q.shape)
    nw = T // WIN_QBLOCK
    kvp = jnp.pad(kv_w, [(0, 0), (WINDOW, 0), (0, 0), (0, 0), (0, 0)])
    kidx = jnp.arange(nw)[:, None] * WIN_QBLOCK + jnp.arange(WIN_QBLOCK + WINDOW)[None, :]
    kwb = kvp[:, kidx]
    kpos = kidx - WINDOW
    dist = qpos.reshape(nw, WIN_QBLOCK)[:, :, None] - kpos[:, None, :]
    valid_w = (kpos[:, None, :] >= 0) & (dist >= 0) & (dist < WINDOW)
    o_w, _ = _attn(q.reshape(B, nw, WIN_QBLOCK, NSA_HKV, NSA_G, NSA_HD), kwb[:, :, :, 0], kwb[:, :, :, 1], dist, valid_w, slopes)
    o_w = o_w.reshape(q.shape)
    y = _combine(h, q, o_c, o_s, o_w, gates, w_out)
    wb = min(WINDOW, T)
    return y, kv_c, kv_s, kv_w[:, T - wb:]


def _nsa_sample(h, cache_cmp, cache_sel, layer, win_buf, page_table, w_in, pool_w, proj_w, w_out):
    Bd, Q, _ = h.shape
    page = cache_cmp.shape[2]
    past = page_table.shape[1] * page
    slopes = _alibi_slopes()
    q, kv_c, kv_s, kv_w, gates = _nsa_project(h, w_in)
    qpos = past + jnp.arange(Q)
    past_c = cache_cmp[layer, page_table].reshape(Bd, past, 2, NSA_HKV, NSA_HD)
    full_c = jnp.concatenate([past_c.astype(kv_c.dtype), kv_c], axis=1)
    o_c, p_c = _cmp_branch(q, full_c, qpos, pool_w, proj_w, slopes)
    idx, valid = _select(p_c, qpos, -(-(past + Q) // CMP_BLOCK))
    kpos = idx[..., None] * CMP_BLOCK + jnp.arange(CMP_BLOCK)
    b_ix = jnp.arange(Bd)[:, None, None, None, None]
    h_ix = jnp.arange(NSA_HKV)[None, None, :, None, None]
    pp = jnp.minimum(kpos, past - 1)
    phys = page_table[b_ix, pp // page]
    g_past = cache_sel[layer, phys, pp % page, :, h_ix]
    g_new = kv_s[b_ix, jnp.clip(kpos - past, 0, Q - 1), :, h_ix]
    kvg = jnp.where((kpos >= past)[..., None, None], g_new, g_past.astype(g_new.dtype))
    o_s = _sel_attn(q, kvg, kpos, valid, qpos, slopes)
    wb = win_buf.shape[1]
    kv_all = jnp.concatenate([win_buf.astype(kv_w.dtype), kv_w], axis=1)
    kpos_w = past - wb + jnp.arange(wb + Q)
    dist = qpos[:, None] - kpos_w[None, :]
    valid_w = (dist >= 0) & (dist < WINDOW)
    o_w, _ = _attn(q, kv_all[:, :, 0], kv_all[:, :, 1], dist, valid_w, slopes)
    y = _combine(h, q, o_c, o_s, o_w, gates, w_out)
    return y, kv_c, kv_s, kv_all[:, Q:]


def _sq_relu_mlp(x, w1, w2):
    return jnp.square(jax.nn.relu(x @ w1)) @ w2


def setup_inputs(seed: int = 0) -> dict:
    key = jax.random.key(seed)
    ks = iter(jax.random.split(key, 40))
    nrm = lambda shape, scale: scale * jax.random.normal(next(ks), shape, jnp.float32)
    n_pages = PAST_LEN // PAGE_SIZE
    n_used = DEC_BATCH * n_pages
    n_phys = n_used + (n_used + 3) // 4
    wbuf = min(WINDOW, PAST_LEN)
    page_table = jax.random.permutation(next(ks), n_phys)[:n_used].reshape(DEC_BATCH, n_pages).astype(jnp.int32)
    a_log = jnp.log(jax.random.uniform(next(ks), (N_EVEN, GDN_HV), jnp.float32, 1.0, 16.0))
    dt = jnp.exp(jax.random.uniform(next(ks), (N_EVEN, GDN_HV), jnp.float32, np.log(1e-3), np.log(1e-1)))
    dt_bias = dt + jnp.log(-jnp.expm1(-dt))
    return {
        'x_prompt': nrm((BATCH, SEQ, D_MODEL), 1.0),
        'x_sample': nrm((DEC_BATCH, DEC_SEQ, D_MODEL), 1.0),
        'c_prompt': nrm((BATCH, D_MODEL), 1.0),
        'c_sample': nrm((DEC_BATCH, D_MODEL), 1.0),
        'state_gdn': nrm((N_EVEN, DEC_BATCH, GDN_HV, GDN_DK, GDN_DV), 0.5),
        'state_gdn_conv': nrm((N_EVEN, DEC_BATCH, GDN_CONV - 1, GDN_CONV_DIM), 1.0),
        'state_sconv': nrm((N_EVEN, DEC_BATCH, SC_WIDTH - 1, SC_DIM), 1.0),
        'cache_kv_cmp': nrm((N_ODD, n_phys, PAGE_SIZE, 2, NSA_HKV, NSA_HD), 1.0),
        'cache_kv_sel': nrm((N_ODD, n_phys, PAGE_SIZE, 2, NSA_HKV, NSA_HD), 1.0),
        'state_kv_win': nrm((N_ODD, DEC_BATCH, wbuf, 2, NSA_HKV, NSA_HD), 1.0),
        'page_table': page_table,
        'norm_g': 1.0 + nrm((DEPTH, 2, D_MODEL), 0.02),
        'w_ada': nrm((DEPTH, D_MODEL, 6 * D_MODEL), D_MODEL ** -0.5),
        'b_ada': nrm((DEPTH, 6 * D_MODEL), 0.02),
        'w_in_e': nrm((N_EVEN, D_MODEL, IN_E), D_MODEL ** -0.5),
        'conv_w_gdn': nrm((N_EVEN, GDN_CONV, GDN_CONV_DIM), GDN_CONV ** -0.5),
        'a_log': a_log,
        'dt_bias': dt_bias,
        'gdn_norm_g': 1.0 + nrm((N_EVEN, GDN_DV), 0.02),
        'conv_w_sc': nrm((N_EVEN, SC_WIDTH, SC_DIM), SC_WIDTH ** -0.5),
        'w_out_e': nrm((N_EVEN, MIX_E, D_MODEL), MIX_E ** -0.5),
        'w_in_o': nrm((N_ODD, D_MODEL, IN_O), D_MODEL ** -0.5),
        'cmp_pool': 1.0 / CMP_BLOCK + nrm((N_ODD, CMP_BLOCK, 2, NSA_HKV), CMP_BLOCK ** -0.5),
        'cmp_proj': nrm((N_ODD, 2, NSA_HKV, NSA_HD, NSA_HD), NSA_HD ** -0.5),
        'w_out_o': nrm((N_ODD, NSA_H * NSA_HD, D_MODEL), (NSA_H * NSA_HD) ** -0.5),
        'w_mlp1': nrm((DEPTH, D_MODEL, D_FF), D_MODEL ** -0.5),
        'w_mlp2': nrm((DEPTH, D_FF, D_MODEL), D_FF ** -0.5),
        'final_g': 1.0 + nrm((D_MODEL,), 0.02),
    }


def reference(x_prompt, x_sample, c_prompt, c_sample, state_gdn, state_gdn_conv, state_sconv,
              cache_kv_cmp, cache_kv_sel, state_kv_win, page_table, norm_g, w_ada, b_ada, w_in_e,
              conv_w_gdn, a_log, dt_bias, gdn_norm_g, conv_w_sc, w_out_e, w_in_o, cmp_pool, cmp_proj,
              w_out_o, w_mlp1, w_mlp2, final_g):

    def trunk(x, c, even_mixer, odd_mixer):
        even_out, odd_out = [], []
        mod_in = jax.nn.silu(c.astype(jnp.float32))
        for l in range(DEPTH):
            i = l // 2
            mod = (mod_in @ w_ada[l] + b_ada[l]).astype(x.dtype).reshape(c.shape[0], 6, 1, D_MODEL)
            shift1, scale1, gate1, shift2, scale2, gate2 = [mod[:, j] for j in range(6)]
            hn = _rmsnorm(x, norm_g[l, 0]) * (1.0 + scale1) + shift1
            if l % 2 == 0:
                mix, s_a, s_b, s_c = even_mixer(i, hn)
                even_out.append((s_a, s_b, s_c))
            else:
                mix, s_a, s_b, s_c = odd_mixer(i, hn)
                odd_out.append((s_a, s_b, s_c))
            x = x + gate1 * mix.astype(x.dtype)
            hn = _rmsnorm(x, norm_g[l, 1]) * (1.0 + scale2) + shift2
            x = x + gate2 * _sq_relu_mlp(hn, w_mlp1[l], w_mlp2[l]).astype(x.dtype)
        y = _rmsnorm(x, final_g)
        ev = [jnp.stack([s[j] for s in even_out]) for j in range(3)]
        od = [jnp.stack([s[j] for s in odd_out]) for j in range(3)]
        return y, ev, od

    def even_prompt(i, hn):
        B = hn.shape[0]
        return _even_mixer(hn, jnp.zeros((B, GDN_CONV - 1, GDN_CONV_DIM), hn.dtype),
                           jnp.zeros((B, GDN_HV, GDN_DK, GDN_DV), jnp.float32),
                           jnp.zeros((B, SC_WIDTH - 1, SC_DIM), hn.dtype),
                           w_in_e[i], conv_w_gdn[i], a_log[i], dt_bias[i], gdn_norm_g[i], conv_w_sc[i], w_out_e[i])

    def even_sample(i, hn):
        return _even_mixer(hn, state_gdn_conv[i], state_gdn[i], state_sconv[i],
                           w_in_e[i], conv_w_gdn[i], a_log[i], dt_bias[i], gdn_norm_g[i], conv_w_sc[i], w_out_e[i])

    def odd_prompt(i, hn):
        return _nsa_prompt(hn, w_in_o[i], cmp_pool[i], cmp_proj[i], w_out_o[i])

    def odd_sample(i, hn):
        return _nsa_sample(hn, cache_kv_cmp, cache_kv_sel, i, state_kv_win[i], page_table,
                           w_in_o[i], cmp_pool[i], cmp_proj[i], w_out_o[i])

    y_prompt, (gdn_p, gconv_p, sconv_p), (kvc_p, kvs_p, kvw_p) = trunk(x_prompt, c_prompt, even_prompt, odd_prompt)
    y_sample, (gdn_s, gconv_s, sconv_s), (kvc_s, kvs_s, kvw_s) = trunk(x_sample, c_sample, even_sample, odd_sample)
    return (y_prompt, y_sample, gdn_p, gdn_s, gconv_p, gconv_s, sconv_p, sconv_s, kvc_p, kvc_s, kvs_p, kvs_s, kvw_p, kvw_s)
```

```python
import functools

import jax
import jax.numpy as jnp
from jax import lax
from jax.experimental import pallas as pl
from jax.experimental.pallas import tpu as pltpu

F32 = jnp.float32
BF16 = jnp.bfloat16

EPS = 1e-6
NEG = -1e30
FORCE = 1e6
DEAD = -3e38
SEL_TOPK = 16
WINDOW = 512
GDN_CHUNK = 64
LANE = 128
SUBLANE = 8
VMEM_BUDGET = 56 * 1024 * 1024


def _params(sem):
    return pltpu.CompilerParams(dimension_semantics=sem, vmem_limit_bytes=VMEM_BUDGET)


def _pick(n, cands):
    for c in cands:
        if n % c == 0:
            return c
    return n


def _bdot(a, b):
    return jnp.dot(a.astype(BF16), b.astype(BF16), preferred_element_type=F32)


def _bdot_nt(a, b):
    return lax.dot_general(a.astype(BF16), b.astype(BF16), (((1,), (1,)), ((), ())),
                           preferred_element_type=F32)


def _silu(x):
    return x * jax.nn.sigmoid(x)


def _modnorm(x, g, scale, shift):
    y = x * lax.rsqrt(jnp.mean(x * x, axis=-1, keepdims=True) + EPS)
    return (y * g) * (1.0 + scale) + shift


def _masked_softmax(s, valid):
    s = jnp.where(valid, s, NEG)
    m = jnp.max(s, axis=-1, keepdims=True)
    p = jnp.where(valid, jnp.exp(s - m), 0.0)
    return p / jnp.maximum(jnp.sum(p, axis=-1, keepdims=True), 1e-30)


def _ada_kernel(c_ref, w_ref, b_ref, o_ref):
    o_ref[...] = _bdot(_silu(c_ref[...]), w_ref[...]) + b_ref[...]


def _ada_mod(c_all, w_ada, b_ada):
    depth, d, n6 = w_ada.shape
    rows = c_all.shape[0]
    tn = _pick(n6, (1024, 512, 256, 128))
    return pl.pallas_call(
        _ada_kernel,
        grid=(depth, n6 // tn),
        in_specs=[pl.BlockSpec((rows, d), lambda l, j: (0, 0)),
                  pl.BlockSpec((None, d, tn), lambda l, j: (l, 0, j)),
                  pl.BlockSpec((None, 1, tn), lambda l, j: (l, 0, j))],
        out_specs=pl.BlockSpec((None, rows, tn), lambda l, j: (l, 0, j)),
        out_shape=jax.ShapeDtypeStruct((depth, rows, n6), F32),
        compiler_params=_params(("arbitrary", "arbitrary")),
        name="ada_mod",
    )(c_all, w_ada, b_ada.reshape(depth, 1, n6))


def _mm1_kernel(x_ref, g_ref, sc_ref, sh_ref, w_ref, o_ref, hn_ref):
    @pl.when(pl.program_id(1) == 0)
    def _():
        hn_ref[...] = _modnorm(x_ref[...], g_ref[...], sc_ref[...], sh_ref[...]).astype(BF16)

    o_ref[...] = jnp.dot(hn_ref[...], w_ref[...].astype(BF16), preferred_element_type=F32)


def _mod_spec(mod, tm, rows_per_group, width, col):
    r = mod.shape[1]
    tiles = rows_per_group // tm
    if col:
        return pl.BlockSpec((None, r, width), lambda i, j: (i // tiles, 0, j))
    return pl.BlockSpec((None, r, width), lambda i, j: (i // tiles, 0, 0))


def _norm_mod_matmul(x, g, scale, shift, w):
    m, d = x.shape
    n = w.shape[1]
    groups = scale.shape[0]
    rpg = m // groups
    tm = _pick(rpg, (512, 256, 128, 64, 32, 16, 8))
    tn = 512 if n >= 512 else n
    return pl.pallas_call(
        _mm1_kernel,
        grid=(m // tm, pl.cdiv(n, tn)),
        in_specs=[pl.BlockSpec((tm, d), lambda i, j: (i, 0)),
                  pl.BlockSpec((1, d), lambda i, j: (0, 0)),
                  _mod_spec(scale, tm, rpg, d, False),
                  _mod_spec(shift, tm, rpg, d, False),
                  pl.BlockSpec((d, tn), lambda i, j: (0, j))],
        out_specs=pl.BlockSpec((tm, tn), lambda i, j: (i, j)),
        out_shape=jax.ShapeDtypeStruct((m, n), F32),
        scratch_shapes=[pltpu.VMEM((tm, d), BF16)],
        compiler_params=_params(("arbitrary", "arbitrary")),
        name="norm_mod_matmul",
    )(x, g.reshape(1, d), scale, shift, w)


def _mm2_kernel(a_ref, w_ref, x_ref, gate_ref, o_ref):
    y = jnp.dot(a_ref[...], w_ref[...].astype(BF16), preferred_element_type=F32)
    o_ref[...] = x_ref[...] + gate_ref[...] * y


def _matmul_gated_residual(a, w, x, gate):
    m, k = a.shape
    d = w.shape[1]
    groups = gate.shape[0]
    rpg = m // groups
    tm = _pick(rpg, (512, 256, 128, 64, 32, 16))
    tn = _pick(d, (512, 256, 128))
    return pl.pallas_call(
        _mm2_kernel,
        grid=(m // tm, d // tn),
        in_specs=[pl.BlockSpec((tm, k), lambda i, j: (i, 0)),
                  pl.BlockSpec((k, tn), lambda i, j: (0, j)),
                  pl.BlockSpec((tm, tn), lambda i, j: (i, j)),
                  _mod_spec(gate, tm, rpg, tn, True)],
        out_specs=pl.BlockSpec((tm, tn), lambda i, j: (i, j)),
        out_shape=jax.ShapeDtypeStruct((m, d), F32),
        compiler_params=_params(("arbitrary", "arbitrary")),
        name="matmul_gated_residual",
    )(a, w, x, gate)


def _mlp_kernel(x_ref, g_ref, sc_ref, sh_ref, gate_ref, w1_ref, w2_ref, o_ref, hn_ref, acc_ref):
    j = pl.program_id(1)

    @pl.when(j == 0)
    def _():
        hn_ref[...] = _modnorm(x_ref[...], g_ref[...], sc_ref[...], sh_ref[...]).astype(BF16)
        acc_ref[...] = jnp.zeros_like(acc_ref)

    h = jnp.dot(hn_ref[...], w1_ref[...].astype(BF16), preferred_element_type=F32)
    h = jnp.square(jnp.maximum(h, 0.0))
    acc_ref[...] += jnp.dot(h.astype(BF16), w2_ref[...].astype(BF16), preferred_element_type=F32)

    @pl.when(j == pl.num_programs(1) - 1)
    def _():
        o_ref[...] = x_ref[...] + gate_ref[...] * acc_ref[...]


def _mlp(x, g, scale, shift, gate, w1, w2):
    m, d = x.shape
    f = w1.shape[1]
    groups = scale.shape[0]
    rpg = m // groups
    tm = _pick(rpg, (512, 256, 128, 64, 32, 16, 8))
    tf = _pick(f, (512, 256, 128))
    return pl.pallas_call(
        _mlp_kernel,
        grid=(m // tm, f // tf),
        in_specs=[pl.BlockSpec((tm, d), lambda i, j: (i, 0)),
                  pl.BlockSpec((1, d), lambda i, j: (0, 0)),
                  _mod_spec(scale, tm, rpg, d, False),
                  _mod_spec(shift, tm, rpg, d, False),
                  _mod_spec(gate, tm, rpg, d, False),
                  pl.BlockSpec((d, tf), lambda i, j: (0, j)),
                  pl.BlockSpec((tf, d), lambda i, j: (j, 0))],
        out_specs=pl.BlockSpec((tm, d), lambda i, j: (i, 0)),
        out_shape=jax.ShapeDtypeStruct((m, d), F32),
        scratch_shapes=[pltpu.VMEM((tm, d), BF16), pltpu.VMEM((tm, d), F32)],
        compiler_params=_params(("arbitrary", "arbitrary")),
        name="mlp",
    )(x, g.reshape(1, d), scale, shift, gate, w1, w2)


def _final_kernel(x_ref, g_ref, o_ref):
    x = x_ref[...]
    o_ref[...] = (x * lax.rsqrt(jnp.mean(x * x, axis=-1, keepdims=True) + EPS)) * g_ref[...]


def _final_norm(x, g):
    m, d = x.shape
    tm = _pick(m, (512, 256, 128, 64, 32, 16, 8))
    return pl.pallas_call(
        _final_kernel,
        grid=(m // tm,),
        in_specs=[pl.BlockSpec((tm, d), lambda i: (i, 0)), pl.BlockSpec((1, d), lambda i: (0, 0))],
        out_specs=pl.BlockSpec((tm, d), lambda i: (i, 0)),
        out_shape=jax.ShapeDtypeStruct((m, d), F32),
        compiler_params=_params(("arbitrary",)),
        name="final_norm",
    )(x, g.reshape(1, d))


def _split3(x):
    hi = x.astype(BF16)
    r = x - hi.astype(F32)
    mid = r.astype(BF16)
    lo = (r - mid.astype(F32)).astype(BF16)
    return hi, mid, lo


def _gdn_kernel(qkv_ref, z_ref, ab_ref, gb_ref, gcg_ref, hx_ref, cst_ref, scst_ref, cw_ref, scw_ref,
                alog_ref, dtb_ref, ng_ref, s0_ref,
                mix_ref, sout_ref, ulast_ref,
                xbuf, ubuf, s_ref, *, t_valid, n_kh, n_vh, dk, dv):
    c = GDN_CHUNK
    n = pl.program_id(1)
    nqk = n_kh * dk
    nv = n_vh * dv
    rep = n_vh // n_kh
    hpg = 4
    gw = hpg * c
    n_conv = cw_ref.shape[0]
    n_sc = scw_ref.shape[0]

    @pl.when(n == 0)
    def _():
        xbuf[0:SUBLANE, :] = cst_ref[...]
        ubuf[0:SUBLANE, :] = scst_ref[...]
        s_ref[...] = s0_ref[...]

    xbuf[SUBLANE:SUBLANE + c, :] = qkv_ref[...]
    cw = cw_ref[...]
    off = SUBLANE - (n_conv - 1)
    xc = cw[0:1, :] * xbuf[off:off + c, :]
    for j in range(1, n_conv):
        xc = xc + cw[j:j + 1, :] * xbuf[off + j:off + j + c, :]
    xbuf[0:SUBLANE, :] = xbuf[c:c + SUBLANE, :]
    xc = _silu(xc)

    u = gcg_ref[...] * hx_ref[...]
    ubuf[SUBLANE:SUBLANE + c, :] = u
    scw = scw_ref[...]
    offs = SUBLANE - (n_sc - 1)
    cu = scw[0:1, :] * ubuf[offs:offs + c, :]
    for j in range(1, n_sc):
        cu = cu + scw[j:j + 1, :] * ubuf[offs + j:offs + j + c, :]
    ubuf[0:SUBLANE, :] = ubuf[c:c + SUBLANE, :]
    ulast_ref[...] = u
    mix_ref[:, nv:] = (gb_ref[...] * cu).astype(mix_ref.dtype)

    ab = ab_ref[...]
    g_all = -jnp.exp(alog_ref[...]) * jax.nn.softplus(ab + dtb_ref[...])
    beta_all = jax.nn.sigmoid(ab)
    if t_valid < c:
        rowmask = lax.broadcasted_iota(jnp.int32, (c, 1), 0) < t_valid
        xc = jnp.where(rowmask, xc, 0.0)
        g_all = jnp.where(rowmask, g_all, 0.0)
        beta_all = jnp.where(rowmask, beta_all, 0.0)

    ri = lax.broadcasted_iota(jnp.int32, (c, c), 0)
    ci = lax.broadcasted_iota(jnp.int32, (c, c), 1)
    tril = jnp.where(ri >= ci, 1.0, 0.0).astype(BF16)
    ghi, gmid, glo = _split3(g_all)
    gcum = (jnp.dot(tril, ghi, preferred_element_type=F32)
            + jnp.dot(tril, gmid, preferred_element_type=F32)
            + jnp.dot(tril, glo, preferred_element_type=F32))
    glast = gcum[c - 1:c, :]

    def l2n(x):
        return x * lax.rsqrt(jnp.sum(x * x, axis=-1, keepdims=True) + EPS)

    gi = lax.broadcasted_iota(jnp.int32, (gw, gw), 0)
    gj = lax.broadcasted_iota(jnp.int32, (gw, gw), 1)
    same = (gi // c) == (gj // c)
    low_incl = same & (gi >= gj)
    low_strict = same & (gi > gj)
    lane_blk = lax.broadcasted_iota(jnp.int32, (dk, gw), 1) // c
    level_mask = []
    size = 1
    while size < c:
        level_mask.append(((gi // (2 * size)) == (gj // (2 * size))) & ((gi // size) != (gj // size)) & (gi > gj))
        size *= 2

    def stack(cols):
        return jnp.concatenate(cols, axis=0)

    for grp in range(n_vh // hpg):
        heads = [grp * hpg + j for j in range(hpg)]
        kheads = [h // rep for h in heads]
        qs = {kh: l2n(xc[:, kh * dk:(kh + 1) * dk]) * (dk ** -0.5) for kh in set(kheads)}
        ks = {kh: l2n(xc[:, nqk + kh * dk:nqk + (kh + 1) * dk]) for kh in set(kheads)}
        q_st = stack([qs[kh] for kh in kheads])
        k_st = stack([ks[kh] for kh in kheads])
        v_st = stack([xc[:, 2 * nqk + h * dv:2 * nqk + (h + 1) * dv] for h in heads])
        beta_st = stack([beta_all[:, n_vh + h:n_vh + h + 1] for h in heads])
        gc_st = stack([gcum[:, h:h + 1] for h in heads])
        gl_st = stack([jnp.broadcast_to(glast[:, h:h + 1], (c, 1)) for h in heads])
        gc_row = jnp.broadcast_to(gc_st, (gw, LANE)).T[0:1, :]
        diff = gc_st - gc_row
        decay = jnp.where(low_incl, jnp.exp(jnp.where(low_incl, diff, 0.0)), 0.0)
        egc = jnp.exp(gc_st)
        kb_st = k_st * beta_st

        nmat = _bdot_nt(kb_st, k_st) * jnp.where(low_strict, decay, 0.0)
        qk = _bdot_nt(q_st, k_st) * decay
        tm = -jnp.where(level_mask[0], nmat, 0.0)
        for lm in level_mask[1:]:
            ll = jnp.where(lm, nmat, 0.0)
            y = ll + _bdot(tm, ll)
            tm = tm - y - _bdot(y, tm)
        rhs = jnp.concatenate([v_st * beta_st, kb_st * egc], axis=1)
        sol = rhs + _bdot(tm, rhs)
        u_st = sol[:, :dv]
        w_st = sol[:, dv:]
        qg_st = q_st * egc
        kd_st = k_st * jnp.exp(gl_st - gc_st)

        vnew, qs_out = [], []
        for j, h in enumerate(heads):
            wq = jnp.concatenate([w_st[j * c:(j + 1) * c], qg_st[j * c:(j + 1) * c]], axis=0)
            r2 = _bdot(wq, s_ref[h])
            vnew.append(u_st[j * c:(j + 1) * c] - r2[:c])
            qs_out.append(r2[c:])
        vnew_st = stack(vnew)
        o_st = stack(qs_out) + _bdot(qk, vnew_st)
        kd_t = kd_st.T
        for j, h in enumerate(heads):
            upd = _bdot(jnp.where(lane_blk == j, kd_t, 0.0), vnew_st)
            s_ref[h] = s_ref[h] * jnp.exp(glast[:, h:h + 1]) + upd
            o_h = o_st[j * c:(j + 1) * c]
            on = (o_h * lax.rsqrt(jnp.mean(o_h * o_h, axis=-1, keepdims=True) + EPS)) * ng_ref[...]
            zz = z_ref[:, h * dv:(h + 1) * dv]
            mix_ref[:, h * dv:(h + 1) * dv] = (on * _silu(zz)).astype(mix_ref.dtype)

    sout_ref[...] = s_ref[...]


def _gdn_sconv(proj, conv_state, sc_state, s0, conv_w, sc_w, a_log, dt_bias, norm_g, t_valid):
    bsz, n_vh, dk, dv = s0.shape
    n_conv, conv_dim = conv_w.shape
    n_sc, sc = sc_w.shape
    nv = n_vh * dv
    n_kh = (conv_dim - nv) // (2 * dk)
    c = GDN_CHUNK
    rows = proj.shape[0]
    nchunk = rows // (bsz * c)
    assert conv_dim % nv == 0 and (conv_dim + nv) % sc == 0 and n_vh % 4 == 0 and 2 * n_vh <= LANE
    zb = conv_dim // nv
    gbb = (conv_dim + nv) // sc
    abb = (conv_dim + nv + 3 * sc) // LANE
    pad_lane = lambda v: jnp.pad(v.reshape(1, -1), ((0, 0), (0, LANE - v.shape[-1])))
    cst = jnp.pad(conv_state.astype(F32), ((0, 0), (SUBLANE - (n_conv - 1), 0), (0, 0)))
    scst = jnp.pad(sc_state.astype(F32), ((0, 0), (SUBLANE - (n_sc - 1), 0), (0, 0)))
    row = lambda b, n: b * nchunk + n
    kern = functools.partial(_gdn_kernel, t_valid=t_valid, n_kh=n_kh, n_vh=n_vh, dk=dk, dv=dv)
    return pl.pallas_call(
        kern,
        grid=(bsz, nchunk),
        in_specs=[pl.BlockSpec((c, conv_dim), lambda b, n: (row(b, n), 0)),
                  pl.BlockSpec((c, nv), lambda b, n: (row(b, n), zb)),
                  pl.BlockSpec((c, LANE), lambda b, n: (row(b, n), abb)),
                  pl.BlockSpec((c, sc), lambda b, n: (row(b, n), gbb)),
                  pl.BlockSpec((c, sc), lambda b, n: (row(b, n), gbb + 1)),
                  pl.BlockSpec((c, sc), lambda b, n: (row(b, n), gbb + 2)),
                  pl.BlockSpec((None, SUBLANE, conv_dim), lambda b, n: (b, 0, 0)),
                  pl.BlockSpec((None, SUBLANE, sc), lambda b, n: (b, 0, 0)),
                  pl.BlockSpec((n_conv, conv_dim), lambda b, n: (0, 0)),
                  pl.BlockSpec((n_sc, sc), lambda b, n: (0, 0)),
                  pl.BlockSpec((1, LANE), lambda b, n: (0, 0)),
                  pl.BlockSpec((1, LANE), lambda b, n: (0, 0)),
                  pl.BlockSpec((1, dv), lambda b, n: (0, 0)),
                  pl.BlockSpec((None, n_vh, dk, dv), lambda b, n: (b, 0, 0, 0))],
        out_specs=[pl.BlockSpec((c, nv + sc), lambda b, n: (row(b, n), 0)),
                   pl.BlockSpec((None, n_vh, dk, dv), lambda b, n: (b, 0, 0, 0)),
                   pl.BlockSpec((None, c, sc), lambda b, n: (b, 0, 0))],
        out_shape=[jax.ShapeDtypeStruct((rows, nv + sc), BF16),
                   jax.ShapeDtypeStruct((bsz, n_vh, dk, dv), F32),
                   jax.ShapeDtypeStruct((bsz, c, sc), F32)],
        scratch_shapes=[pltpu.VMEM((c + SUBLANE, conv_dim), F32),
                        pltpu.VMEM((c + SUBLANE, sc), F32),
                        pltpu.VMEM((n_vh, dk, dv), F32)],
        compiler_params=_params(("arbitrary", "arbitrary")),
        name="gdn_sconv",
    )(proj, proj, proj, proj, proj, proj, cst, scst, conv_w, sc_w,
      pad_lane(a_log), pad_lane(dt_bias), norm_g.reshape(1, dv), s0.astype(F32))


def _cmp_prompt_kernel(kv_ref, pw_ref, pj_ref, o_ref, *, blk):
    t, hd = kv_ref.shape
    x = kv_ref[...].reshape(t // blk, blk, hd) * pw_ref[...][None]
    o_ref[...] = _bdot(jnp.sum(x, axis=1), pj_ref[...])


def _cmp_prompt(proj, bsz, t, pw, pj, col0):
    nch, blk, hd = pw.shape
    return pl.pallas_call(
        functools.partial(_cmp_prompt_kernel, blk=blk),
        grid=(bsz, nch),
        in_specs=[pl.BlockSpec((t, hd), lambda b, ch: (b, col0 + ch)),
                  pl.BlockSpec((None, blk, hd), lambda b, ch: (ch, 0, 0)),
                  pl.BlockSpec((None, hd, hd), lambda b, ch: (ch, 0, 0))],
        out_specs=pl.BlockSpec((None, None, t // blk, hd), lambda b, ch: (b, ch, 0, 0)),
        out_shape=jax.ShapeDtypeStruct((bsz, nch, t // blk, hd), F32),
        compiler_params=_params(("arbitrary", "arbitrary")),
        name="cmp_prompt",
    )(proj, pw, pj)


def _attn_prompt_kernel(q_ref, kc_ref, vc_ref, ks_ref, vs_ref, kw_ref, vw_ref, gt_ref, sl_ref, o_ref,
                        *, tq, tk, tkw, blk, n_g, hd):
    qt = pl.program_id(2)
    q0 = qt * tq
    nc = kc_ref.shape[0]
    rows = n_g * tq
    rep = lambda x: jnp.concatenate([x] * n_g, axis=0)

    qb = jnp.concatenate([q_ref[:, g * hd:(g + 1) * hd] for g in range(n_g)], axis=0)
    qb = (qb * (hd ** -0.5)).astype(BF16)
    sl = sl_ref[...]
    slope = jnp.concatenate([jnp.broadcast_to(sl[:, g:g + 1], (tq, 1)) for g in range(n_g)], axis=0)
    qpos = q0 + lax.broadcasted_iota(jnp.int32, (tq, 1), 0)
    qpos_r = rep(qpos)

    bidx = lax.broadcasted_iota(jnp.int32, (1, nc), 1)
    start = bidx * blk
    s = _bdot_nt(qb, kc_ref[...])
    dist_c = qpos_r.astype(F32) - (start.astype(F32) + (blk - 1) / 2)
    s = s - slope * jnp.abs(dist_c)
    p_c = _masked_softmax(s, (start + (blk - 1)) <= qpos_r)
    o_c = _bdot(p_c, vc_ref[...])
    imp = p_c[0:tq]
    for g in range(1, n_g):
        imp = imp + p_c[g * tq:(g + 1) * tq]

    cur = qpos // blk
    score = jnp.where(bidx < cur, imp, NEG)
    score = jnp.where((bidx == 0) | (bidx == cur), FORCE, score)
    rank = jnp.zeros((tq, nc), F32)
    for i in range(nc):
        ci = score[:, i:i + 1]
        rank = rank + jnp.where((ci > score) | ((ci == score) & (bidx > i)), 1.0, 0.0)
    sel = jnp.where((rank < min(SEL_TOPK, nc)) & (score > 0.5 * NEG), 1.0, 0.0)

    def flash(k_ref, v_ref, lo, hi, step, mask_fn):
        def body(ci, carry):
            m, l, acc = carry
            k0 = pl.multiple_of(ci * step, step)
            k = k_ref[pl.ds(k0, step), :]
            v = v_ref[pl.ds(k0, step), :]
            kpos = k0 + lax.broadcasted_iota(jnp.int32, (1, step), 1)
            dist = qpos_r - kpos
            mask = mask_fn(dist, k0)
            sc = jnp.where(mask, _bdot_nt(qb, k) - slope * dist.astype(F32), NEG)
            m_new = jnp.maximum(m, jnp.max(sc, axis=-1, keepdims=True))
            alpha = jnp.exp(m - m_new)
            p = jnp.where(mask, jnp.exp(sc - m_new), 0.0)
            l = alpha * l + jnp.sum(p, axis=-1, keepdims=True)
            acc = alpha * acc + _bdot(p, v)
            return m_new, l, acc

        init = (jnp.full((rows, 1), NEG, F32), jnp.zeros((rows, 1), F32), jnp.zeros((rows, hd), F32))
        _, l, acc = lax.fori_loop(lo, hi, body, init)
        return acc / jnp.maximum(l, 1e-30)

    def sel_mask(dist, k0):
        kb = (k0 + lax.broadcasted_iota(jnp.int32, (nc, tk), 1)) // blk
        expand = jnp.where(kb == lax.broadcasted_iota(jnp.int32, (nc, tk), 0), 1.0, 0.0)
        return (rep(_bdot(sel, expand)) > 0.5) & (dist >= 0)

    o_s = flash(ks_ref, vs_ref, 0, (q0 + tq + tk - 1) // tk, tk, sel_mask)

    def win_mask(dist, k0):
        return (dist >= 0) & (dist < WINDOW)

    o_w = flash(kw_ref, vw_ref, jnp.maximum(q0 - WINDOW + 1, 0) // tkw, (q0 + tq + tkw - 1) // tkw, tkw,
                win_mask)

    gt = jax.nn.sigmoid(gt_ref[...])
    for g in range(n_g):
        r = slice(g * tq, (g + 1) * tq)
        o = (gt[:, g:g + 1] * o_c[r] + gt[:, n_g + g:n_g + g + 1] * o_s[r]
             + gt[:, 2 * n_g + g:2 * n_g + g + 1] * o_w[r])
        o_ref[:, g * hd:(g + 1) * hd] = o.astype(o_ref.dtype)


def _attn_prompt(proj, cmp, gates, slopes, bsz, t, n_h, n_kv, hd, blk):
    n_g = n_h // n_kv
    tq = _pick(t, (128, 64))
    tk = _pick(t, (256, 128, 64))
    tkw = _pick(t, (128, 64))
    nqt = t // tq
    nc = t // blk
    kvb = n_h
    kern = functools.partial(_attn_prompt_kernel, tq=tq, tk=tk, tkw=tkw, blk=blk, n_g=n_g, hd=hd)
    kv_spec = lambda off: pl.BlockSpec((t, hd), lambda b, h, q: (b, kvb + off + h))
    return pl.pallas_call(
        kern,
        grid=(bsz, n_kv, nqt),
        in_specs=[pl.BlockSpec((tq, n_g * hd), lambda b, h, q: (b * nqt + q, h)),
                  pl.BlockSpec((None, None, nc, hd), lambda b, h, q: (b, h, 0, 0)),
                  pl.BlockSpec((None, None, nc, hd), lambda b, h, q: (b, n_kv + h, 0, 0)),
                  kv_spec(2 * n_kv), kv_spec(3 * n_kv), kv_spec(4 * n_kv), kv_spec(5 * n_kv),
                  pl.BlockSpec((None, tq, LANE), lambda b, h, q: (h, b * nqt + q, 0)),
                  pl.BlockSpec((None, 1, LANE), lambda b, h, q: (h, 0, 0))],
        out_specs=pl.BlockSpec((tq, n_g * hd), lambda b, h, q: (b * nqt + q, h)),
        out_shape=jax.ShapeDtypeStruct((bsz * t, n_h * hd), BF16),
        compiler_params=_params(("arbitrary", "arbitrary", "arbitrary")),
        name="attn_prompt",
    )(proj, cmp, cmp, proj, proj, proj, proj, gates, slopes)


def _pool_pages_kernel(pt_ref, *refs, n_pp, blk):
    pw = refs[n_pp][...]
    o_ref = refs[n_pp + 1]
    per_page = refs[0].shape[0] // blk
    for k in range(n_pp):
        for hb in range(per_page):
            r = k * per_page + hb
            o_ref[r:r + 1, :] = jnp.sum(refs[k][hb * blk:(hb + 1) * blk, :] * pw, axis=0, keepdims=True)


def _page_map(b, p, pt, *, layer, k, n_pp):
    return (layer, pt[b, p * n_pp + k], 0, 0)


def _pool_pages(cache, layer, page_table, pw_rows):
    _, _, page, roww = cache.shape
    bd, n_pages = page_table.shape
    blk = pw_rows.shape[0]
    n_pp = _pick(n_pages, (8, 4, 2, 1))
    per_page = page // blk
    in_specs = [pl.BlockSpec((None, None, page, roww),
                             functools.partial(_page_map, layer=layer, k=k, n_pp=n_pp)) for k in range(n_pp)]
    in_specs.append(pl.BlockSpec((blk, roww), lambda b, p, pt: (0, 0)))
    return pl.pallas_call(
        functools.partial(_pool_pages_kernel, n_pp=n_pp, blk=blk),
        grid_spec=pltpu.PrefetchScalarGridSpec(
            num_scalar_prefetch=1, grid=(bd, n_pages // n_pp), in_specs=in_specs,
            out_specs=pl.BlockSpec((None, n_pp * per_page, roww), lambda b, p, pt: (b, p, 0))),
        out_shape=jax.ShapeDtypeStruct((bd, n_pages * per_page, roww), F32),
        compiler_params=_params(("arbitrary", "arbitrary")),
        name="pool_pages",
    )(page_table, *([cache] * n_pp), pw_rows)


def _attn_sample_a_kernel(q_ref, pooled_ref, pj_ref, kvw_ref, gt_ref, sl_ref, ocw_ref, selm_ref,
                          score_buf, *, past, n_q, n_kv, n_g, hd, blk, wb, nbp):
    nc = pooled_ref.shape[0]
    n_blocks = (past + n_q + blk - 1) // blk
    rows = n_g * n_q
    kvw_rows = kvw_ref.shape[0]
    half = n_kv * hd
    rep = lambda x: jnp.concatenate([x] * n_g, axis=0)
    sl = sl_ref[...]
    gt = jax.nn.sigmoid(gt_ref[...])
    qpos = past + lax.broadcasted_iota(jnp.int32, (n_q, 1), 0)
    qpos_r = rep(qpos)
    bidx_c = lax.broadcasted_iota(jnp.int32, (1, nc), 1)
    start = bidx_c * blk
    bidx = lax.broadcasted_iota(jnp.int32, (1, nbp), 1)
    cur = qpos // blk
    score_buf[...] = jnp.full(score_buf.shape, DEAD, F32)

    qbs = []
    for h in range(n_kv):
        qb = jnp.concatenate([q_ref[:, (h * n_g + g) * hd:(h * n_g + g + 1) * hd] for g in range(n_g)], axis=0)
        qb = (qb * (hd ** -0.5)).astype(BF16)
        qbs.append(qb)
        slope = jnp.concatenate(
            [jnp.broadcast_to(sl[:, h * n_g + g:h * n_g + g + 1], (n_q, 1)) for g in range(n_g)], axis=0)
        kc = _bdot(pooled_ref[:, h * hd:(h + 1) * hd], pj_ref[h])
        vc = _bdot(pooled_ref[:, half + h * hd:half + (h + 1) * hd], pj_ref[n_kv + h])
        s = _bdot_nt(qb, kc)
        dist_c = qpos_r.astype(F32) - (start.astype(F32) + (blk - 1) / 2)
        s = s - slope * jnp.abs(dist_c)
        p_c = _masked_softmax(s, (start + (blk - 1)) <= qpos_r)
        o_c = _bdot(p_c, vc)
        imp = p_c[0:n_q]
        for g in range(1, n_g):
            imp = imp + p_c[g * n_q:(g + 1) * n_q]
        imp = jnp.concatenate([imp, jnp.zeros((n_q, nbp - nc), F32)], axis=1)
        score = jnp.where(bidx < cur, imp, NEG)
        score = jnp.where((bidx == 0) | (bidx == cur), FORCE, score)
        score = jnp.where(bidx < n_blocks, score, DEAD)
        score_buf[h * n_q:(h + 1) * n_q, :] = score

        kw = kvw_ref[:, h * hd:(h + 1) * hd]
        vw = kvw_ref[:, half + h * hd:half + (h + 1) * hd]
        kidx = lax.broadcasted_iota(jnp.int32, (1, kvw_rows), 1)
        dist = qpos_r - (past - wb + kidx)
        valid = (dist >= 0) & (dist < WINDOW) & (kidx < wb + n_q)
        sw = _bdot_nt(qb, kw) - slope * jnp.abs(dist).astype(F32)
        o_w = _bdot(_masked_softmax(sw, valid), vw)
        for g in range(n_g):
            hh = h * n_g + g
            r = slice(g * n_q, (g + 1) * n_q)
            ocw_ref[:, hh * hd:(hh + 1) * hd] = (gt[:, hh:hh + 1] * o_c[r]
                                                 + gt[:, 2 * n_kv * n_g + hh:2 * n_kv * n_g + hh + 1] * o_w[r])

    sc_all = score_buf[...]
    sc_t = sc_all.T
    ii = lax.broadcasted_iota(jnp.int32, (nbp, 1), 0)
    for h in range(n_kv):
        for t in range(n_q):
            r = h * n_q + t
            col = sc_t[:, r:r + 1]
            row = sc_all[r:r + 1, :]
            before = (col > row) | ((col == row) & (ii < bidx))
            rank = jnp.sum(jnp.where(before, 1.0, 0.0), axis=0, keepdims=True)
            sel = jnp.where((rank < min(SEL_TOPK, n_blocks)) & (row > 0.5 * NEG), 1.0, 0.0)
            for g in range(n_g):
                selm_ref[h, g * n_q + t:g * n_q + t + 1, :] = sel


def _attn_sample_a(proj3, pooled, pj, kv_all, slopes_row, past, n_h, n_kv, hd, blk, wb):
    bd, n_q, _ = proj3.shape
    n_g = n_h // n_kv
    nc = pooled.shape[1]
    n_blocks = (past + n_q + blk - 1) // blk
    nbp = pl.cdiv(n_blocks, LANE) * LANE
    kvw_rows = kv_all.shape[1]
    qw = n_h * hd
    kern = functools.partial(_attn_sample_a_kernel, past=past, n_q=n_q, n_kv=n_kv, n_g=n_g, hd=hd, blk=blk,
                             wb=wb, nbp=nbp)
    return pl.pallas_call(
        kern,
        grid=(bd,),
        in_specs=[pl.BlockSpec((None, n_q, qw), lambda b: (b, 0, 0)),
                  pl.BlockSpec((None, nc, 2 * n_kv * hd), lambda b: (b, 0, 0)),
                  pl.BlockSpec((2 * n_kv, hd, hd), lambda b: (0, 0, 0)),
                  pl.BlockSpec((None, kvw_rows, 2 * n_kv * hd), lambda b: (b, 0, 0)),
                  pl.BlockSpec((None, n_q, LANE), lambda b: (b, 0, (qw + 6 * n_kv * hd) // LANE)),
                  pl.BlockSpec((1, LANE), lambda b: (0, 0))],
        out_specs=[pl.BlockSpec((None, n_q, qw), lambda b: (b, 0, 0)),
                   pl.BlockSpec((None, n_kv, n_g * n_q, nbp), lambda b: (b, 0, 0, 0))],
        out_shape=[jax.ShapeDtypeStruct((bd, n_q, qw), F32),
                   jax.ShapeDtypeStruct((bd, n_kv, n_g * n_q, nbp), F32)],
        scratch_shapes=[pltpu.VMEM((LANE, nbp), F32)],
        compiler_params=_params(("arbitrary",)),
        name="attn_sample_a",
    )(proj3, pooled, pj, kv_all, proj3, slopes_row)


def _attn_sample_sel_kernel(pt_ref, *refs, n_pp, past, n_q, n_kv, n_g, hd, blk):
    pages = refs[:n_pp]
    q_ref, selm_ref, new_ref, ocw_ref, gt_ref, sl_ref, o_ref, m_ref, l_ref, acc_ref = refs[n_pp:]
    p = pl.program_id(1)
    page = pages[0].shape[0]
    nbp = selm_ref.shape[-1]
    rows = n_g * n_q
    half = n_kv * hd
    rep = lambda x: jnp.concatenate([x] * n_g, axis=0)
    sl = sl_ref[...]
    qpos_r = rep(past + lax.broadcasted_iota(jnp.int32, (n_q, 1), 0))

    @pl.when(p == 0)
    def _():
        m_ref[...] = jnp.full(m_ref.shape, NEG, F32)
        l_ref[...] = jnp.zeros_like(l_ref)
        acc_ref[...] = jnp.zeros_like(acc_ref)

    def heads():
        for h in range(n_kv):
            qb = jnp.concatenate([q_ref[:, (h * n_g + g) * hd:(h * n_g + g + 1) * hd] for g in range(n_g)],
                                 axis=0) * (hd ** -0.5)
            slope = jnp.concatenate(
                [jnp.broadcast_to(sl[:, h * n_g + g:h * n_g + g + 1], (n_q, 1)) for g in range(n_g)], axis=0)
            yield h, qb, slope

    def update(h, sc, mask, pv_fn):
        sc = jnp.where(mask, sc, NEG)
        m = m_ref[h]
        m_new = jnp.maximum(m, jnp.max(sc, axis=-1, keepdims=True))
        alpha = jnp.exp(m - m_new)
        pr = jnp.where(mask, jnp.exp(sc - m_new), 0.0)
        l_ref[h] = alpha * l_ref[h] + jnp.sum(pr, axis=-1, keepdims=True)
        acc_ref[h] = alpha * acc_ref[h] + pv_fn(pr)
        m_ref[h] = m_new

    for h, qb, slope in heads():
        selh = selm_ref[h]
        for k in range(n_pp):
            pg = p * n_pp + k
            k0 = pg * page
            kk = lax.broadcasted_iota(jnp.int32, (nbp, page), 1)
            nn = lax.broadcasted_iota(jnp.int32, (nbp, page), 0)
            expand = jnp.where(nn == (k0 + kk) // blk, 1.0, 0.0)
            mask = _bdot(selh, expand) > 0.5
            kpos = k0 + lax.broadcasted_iota(jnp.int32, (1, page), 1)
            dist = qpos_r - kpos
            kx = pages[k][:, h * hd:(h + 1) * hd]
            vx = pages[k][:, half + h * hd:half + (h + 1) * hd]
            sc = _bdot_nt(qb, kx) - slope * dist.astype(F32)
            update(h, sc, mask & (dist >= 0), lambda pr, vx=vx: _bdot(pr, vx))

    @pl.when(p == pl.num_programs(1) - 1)
    def _():
        gt = jax.nn.sigmoid(gt_ref[...])
        ri = lax.broadcasted_iota(jnp.int32, (rows, 1), 0)
        for h, qb, slope in heads():
            selh = selm_ref[h]
            nb0 = past // blk
            seln = selh[:, nb0:nb0 + 1] > 0.5
            for j in range(n_q):
                kj = new_ref[j:j + 1, h * hd:(h + 1) * hd]
                vj = new_ref[j:j + 1, half + h * hd:half + (h + 1) * hd]
                dist = qpos_r - (past + j)
                sc = jnp.sum(qb.astype(BF16).astype(F32) * kj.astype(BF16).astype(F32), axis=-1,
                             keepdims=True) - slope * dist.astype(F32)
                update(h, sc, seln & (dist >= 0),
                       lambda pr, vj=vj: pr.astype(BF16).astype(F32) * vj.astype(BF16).astype(F32))
            o_s = acc_ref[h] / jnp.maximum(l_ref[h], 1e-30)
            for g in range(n_g):
                hh = h * n_g + g
                r = slice(g * n_q, (g + 1) * n_q)
                gcol = n_kv * n_g + hh
                o = ocw_ref[:, hh * hd:(hh + 1) * hd] + gt[:, gcol:gcol + 1] * o_s[r]
                o_ref[:, hh * hd:(hh + 1) * hd] = o.astype(o_ref.dtype)


def _attn_sample_sel(cache, layer, page_table, proj3, selm, new_rows, ocw, slopes_row, past, n_h, n_kv, hd, blk):
    _, _, page, roww = cache.shape
    bd, n_pages = page_table.shape
    n_q = proj3.shape[1]
    n_g = n_h // n_kv
    qw = n_h * hd
    nbp = selm.shape[-1]
    n_pp = _pick(n_pages, (8, 4, 2, 1))
    rows = n_g * n_q
    in_specs = [pl.BlockSpec((None, None, page, roww),
                             functools.partial(_page_map, layer=layer, k=k, n_pp=n_pp)) for k in range(n_pp)]
    in_specs += [pl.BlockSpec((None, n_q, qw), lambda b, p, pt: (b, 0, 0)),
                 pl.BlockSpec((None, n_kv, rows, nbp), lambda b, p, pt: (b, 0, 0, 0)),
                 pl.BlockSpec((None, new_rows.shape[1], roww), lambda b, p, pt: (b, 0, 0)),
                 pl.BlockSpec((None, n_q, qw), lambda b, p, pt: (b, 0, 0)),
                 pl.BlockSpec((None, n_q, LANE), lambda b, p, pt: (b, 0, (qw + 6 * n_kv * hd) // LANE)),
                 pl.BlockSpec((1, LANE), lambda b, p, pt: (0, 0))]
    kern = functools.partial(_attn_sample_sel_kernel, n_pp=n_pp, past=past, n_q=n_q, n_kv=n_kv, n_g=n_g,
                             hd=hd, blk=blk)
    return pl.pallas_call(
        kern,
        grid_spec=pltpu.PrefetchScalarGridSpec(
            num_scalar_prefetch=1, grid=(bd, n_pages // n_pp), in_specs=in_specs,
            out_specs=pl.BlockSpec((None, n_q, qw), lambda b, p, pt: (b, 0, 0)),
            scratch_shapes=[pltpu.VMEM((n_kv, rows, 1), F32), pltpu.VMEM((n_kv, rows, 1), F32),
                            pltpu.VMEM((n_kv, rows, hd), F32)]),
        out_shape=jax.ShapeDtypeStruct((bd, n_q, qw), BF16),
        compiler_params=_params(("arbitrary", "arbitrary")),
        name="attn_sample_sel",
    )(page_table, *([cache] * n_pp), proj3, selm, new_rows, ocw, proj3, slopes_row)


def _alibi(n_h):
    h = jnp.arange(1, n_h + 1, dtype=F32)
    return jnp.exp2(-8.0 * h / n_h)


def kernel(x_prompt, x_sample, c_prompt, c_sample, state_gdn, state_gdn_conv, state_sconv, cache_kv_cmp,
           cache_kv_sel, state_kv_win, page_table, norm_g, w_ada, b_ada, w_in_e, conv_w_gdn, a_log, dt_bias,
           gdn_norm_g, conv_w_sc, w_out_e, w_in_o, cmp_pool, cmp_proj, w_out_o, w_mlp1, w_mlp2, final_g):
    bsz, seq, d = x_prompt.shape
    bd, n_q, _ = x_sample.shape
    depth = norm_g.shape[0]
    n_vh, dk, dv = state_gdn.shape[2:]
    conv_dim = conv_w_gdn.shape[2]
    sc_dim = conv_w_sc.shape[2]
    nv = n_vh * dv
    blk, _, n_kv = cmp_pool.shape[1:]
    hd = cmp_proj.shape[-1]
    n_h = w_out_o.shape[1] // hd
    kv_row = 2 * n_kv * hd
    qw = n_h * hd
    page = cache_kv_cmp.shape[2]
    past = page_table.shape[1] * page
    wb = state_kv_win.shape[2]
    c = GDN_CHUNK
    assert seq % c == 0 and seq % blk == 0 and seq >= WINDOW and n_q <= c and n_q < blk and past % blk == 0
    assert n_q >= conv_w_gdn.shape[1] - 1 and hd == LANE and dk == LANE and dv == LANE

    rows_c = -(-(bsz + bd) // SUBLANE) * SUBLANE
    c_all = jnp.pad(jnp.concatenate([c_prompt, c_sample], axis=0).astype(F32), ((0, rows_c - bsz - bd), (0, 0)))
    mod = _ada_mod(c_all, w_ada, b_ada).reshape(depth, rows_c, 6, d)

    def mods(l, sample):
        if sample:
            return [jnp.repeat(mod[l, bsz:bsz + bd, j], n_q, axis=0)[None] for j in range(6)]
        return [mod[l, :bsz, j][:, None, :] for j in range(6)]

    slopes = _alibi(n_h)
    slopes_kv = jnp.pad(slopes.reshape(n_kv, 1, n_h // n_kv), ((0, 0), (0, 0), (0, LANE - n_h // n_kv)))
    slopes_row = jnp.pad(slopes.reshape(1, n_h), ((0, 0), (0, LANE - n_h)))
    cache_cmp = cache_kv_cmp.reshape(cache_kv_cmp.shape[:3] + (kv_row,))
    cache_sel = cache_kv_sel.reshape(cache_kv_sel.shape[:3] + (kv_row,))

    def even_weights(i):
        w = w_in_e[i]
        o_ab = conv_dim + nv
        o_rest = o_ab + 2 * n_vh
        return jnp.concatenate([w[:, :o_ab], w[:, o_rest:], w[:, o_ab:o_rest],
                                jnp.zeros((d, LANE - 2 * n_vh), w.dtype)], axis=1)

    def run(x3, sample):
        nb, t, _ = x3.shape
        x = x3.reshape(nb * t, d).astype(F32)
        ev, od = [], []
        for l in range(depth):
            i = l // 2
            sh1, sc1, g1, sh2, sc2, g2 = mods(l, sample)
            if l % 2 == 0:
                proj = _norm_mod_matmul(x, norm_g[l, 0], sc1, sh1, even_weights(i))
                qkv_rows = proj[:, :conv_dim].reshape(nb, t, conv_dim)
                if sample:
                    projp = jnp.pad(proj.reshape(nb, t, -1), ((0, 0), (0, c - t), (0, 0))).reshape(nb * c, -1)
                    cst, scst, s0, tv = state_gdn_conv[i], state_sconv[i], state_gdn[i], t
                else:
                    projp = proj
                    cst = jnp.zeros((nb, conv_w_gdn.shape[1] - 1, conv_dim), F32)
                    scst = jnp.zeros((nb, conv_w_sc.shape[1] - 1, sc_dim), F32)
                    s0, tv = jnp.zeros((nb, n_vh, dk, dv), F32), c
                mix, s_new, u_last = _gdn_sconv(projp, cst, scst, s0, conv_w_gdn[i], conv_w_sc[i], a_log[i],
                                                dt_bias[i], gdn_norm_g[i], tv)
                if sample:
                    mix = mix.reshape(nb, c, -1)[:, :t].reshape(nb * t, -1)
                conv_new = qkv_rows[:, t - cst.shape[1]:]
                sc_new = u_last[:, tv - scst.shape[1]:tv]
                ev.append((s_new, conv_new, sc_new))
                x = _matmul_gated_residual(mix, w_out_e[i], x, g1)
            else:
                proj = _norm_mod_matmul(x, norm_g[l, 0], sc1, sh1, w_in_o[i])
                kv_c = proj[:, qw:qw + kv_row].reshape(nb, t, 2, n_kv, hd)
                kv_s = proj[:, qw + kv_row:qw + 2 * kv_row].reshape(nb, t, 2, n_kv, hd)
                kv_w = proj[:, qw + 2 * kv_row:qw + 3 * kv_row].reshape(nb, t, 2, n_kv, hd)
                pj = cmp_proj[i].reshape(2 * n_kv, hd, hd)
                if sample:
                    pw_rows = jnp.repeat(cmp_pool[i].reshape(blk, 2 * n_kv), hd, axis=1)
                    pooled = _pool_pages(cache_cmp, i, page_table, pw_rows)
                    kv_all = jnp.concatenate([state_kv_win[i].reshape(nb, wb, kv_row).astype(F32),
                                              kv_w.reshape(nb, t, kv_row)], axis=1)
                    kv_all_p = jnp.pad(kv_all, ((0, 0), (0, (-kv_all.shape[1]) % SUBLANE), (0, 0)))
                    proj3 = proj.reshape(nb, t, -1)
                    ocw, selm = _attn_sample_a(proj3, pooled, pj, kv_all_p, slopes_row, past, n_h, n_kv, hd, blk, wb)
                    new_rows = jnp.pad(kv_s.reshape(nb, t, kv_row), ((0, 0), (0, (-t) % SUBLANE), (0, 0)))
                    o = _attn_sample_sel(cache_sel, i, page_table, proj3, selm, new_rows, ocw, slopes_row, past,
                                         n_h, n_kv, hd, blk).reshape(nb * t, qw)
                    kv_win_new = kv_all[:, t:].reshape(nb, wb, 2, n_kv, hd)
                else:
                    pw = jnp.broadcast_to(cmp_pool[i].reshape(blk, 2 * n_kv).T[:, :, None], (2 * n_kv, blk, hd))
                    cmp = _cmp_prompt(proj, nb, t, pw, pj, n_h)
                    gates = proj[:, qw + 3 * kv_row:].reshape(nb * t, 3, n_kv, n_h // n_kv)
                    gates = gates.transpose(2, 0, 1, 3).reshape(n_kv, nb * t, 3 * (n_h // n_kv))
                    gates = jnp.pad(gates, ((0, 0), (0, 0), (0, LANE - gates.shape[-1])))
                    o = _attn_prompt(proj, cmp, gates, slopes_kv, nb, t, n_h, n_kv, hd, blk)
                    kv_win_new = kv_w[:, t - min(WINDOW, t):]
                od.append((kv_c, kv_s, kv_win_new))
                x = _matmul_gated_residual(o, w_out_o[i], x, g1)
            x = _mlp(x, norm_g[l, 1], sc2, sh2, g2, w_mlp1[l], w_mlp2[l])
        y = _final_norm(x, final_g).reshape(nb, t, d)
        evs = [jnp.stack([s[j] for s in ev]) for j in range(3)]
        ods = [jnp.stack([s[j] for s in od]) for j in range(3)]
        return y, evs, ods

    y_p, (gdn_p, gconv_p, sconv_p), (kvc_p, kvs_p, kvw_p) = run(x_prompt, False)
    y_s, (gdn_s, gconv_s, sconv_s), (kvc_s, kvs_s, kvw_s) = run(x_sample, True)
    return (y_p, y_s, gdn_p, gdn_s, gconv_p, gconv_s, sconv_p, sconv_s, kvc_p, kvc_s, kvs_p, kvs_s, kvw_p, kvw_s)
```

```python
import functools

import jax
import jax.numpy as jnp
from jax import lax
from jax.experimental import pallas as pl
from jax.experimental.pallas import tpu as pltpu

F32 = jnp.float32
BF16 = jnp.bfloat16

EPS = 1e-6
NEG = -1e30
FORCE = 1e6
DEAD = -3e38
SEL_TOPK = 16
LOG2E = 1.4426950408889634
WINDOW = 512
GDN_CHUNK = 64
LANE = 128
SUBLANE = 8
VMEM_BUDGET = 56 * 1024 * 1024


def _params(sem):
    return pltpu.CompilerParams(dimension_semantics=sem, vmem_limit_bytes=VMEM_BUDGET)


def _pick(n, cands):
    for c in cands:
        if n % c == 0:
            return c
    return n


def _bdot(a, b):
    return jnp.dot(a.astype(BF16), b.astype(BF16), preferred_element_type=F32)


def _bdot_nt(a, b):
    return lax.dot_general(a.astype(BF16), b.astype(BF16), (((1,), (1,)), ((), ())),
                           preferred_element_type=F32)


def _silu(x):
    return x * jax.nn.sigmoid(x)


def _modnorm(x, g, scale, shift):
    y = x * lax.rsqrt(jnp.mean(x * x, axis=-1, keepdims=True) + EPS)
    return (y * g) * (1.0 + scale) + shift


def _masked_softmax(s, valid):
    s = jnp.where(valid, s, NEG)
    m = jnp.max(s, axis=-1, keepdims=True)
    p = jnp.where(valid, jnp.exp(s - m), 0.0)
    return p / jnp.maximum(jnp.sum(p, axis=-1, keepdims=True), 1e-30)


def _ada_kernel(c_ref, w_ref, b_ref, o_ref):
    o_ref[...] = _bdot(_silu(c_ref[...]), w_ref[...]) + b_ref[...]


def _ada_mod(c_all, w_ada, b_ada):
    depth, d, n6 = w_ada.shape
    rows = c_all.shape[0]
    tn = _pick(n6, (1024, 512, 256, 128))
    return pl.pallas_call(
        _ada_kernel,
        grid=(depth, n6 // tn),
        in_specs=[pl.BlockSpec((rows, d), lambda l, j: (0, 0)),
                  pl.BlockSpec((None, d, tn), lambda l, j: (l, 0, j)),
                  pl.BlockSpec((None, 1, tn), lambda l, j: (l, 0, j))],
        out_specs=pl.BlockSpec((None, rows, tn), lambda l, j: (l, 0, j)),
        out_shape=jax.ShapeDtypeStruct((depth, rows, n6), F32),
        compiler_params=_params(("arbitrary", "arbitrary")),
        name="ada_mod",
    )(c_all, w_ada, b_ada.reshape(depth, 1, n6))


def _mm1_kernel(x_ref, g_ref, sc_ref, sh_ref, w_ref, o_ref, hn_ref):
    @pl.when(pl.program_id(1) == 0)
    def _():
        hn_ref[...] = _modnorm(x_ref[...], g_ref[...], sc_ref[...], sh_ref[...]).astype(BF16)

    o_ref[...] = jnp.dot(hn_ref[...], w_ref[...].astype(BF16), preferred_element_type=F32)


def _mod_spec(mod, tm, rows_per_group, width, col):
    r = mod.shape[1]
    tiles = rows_per_group // tm
    if col:
        return pl.BlockSpec((None, r, width), lambda i, j: (i // tiles, 0, j))
    return pl.BlockSpec((None, r, width), lambda i, j: (i // tiles, 0, 0))


def _norm_mod_matmul(x, g, scale, shift, w):
    m, d = x.shape
    n = w.shape[1]
    groups = scale.shape[0]
    rpg = m // groups
    tm = _pick(rpg, (1024, 512, 256, 128, 64, 32, 16, 8))
    tn = 512 if n >= 512 else n
    return pl.pallas_call(
        _mm1_kernel,
        grid=(m // tm, pl.cdiv(n, tn)),
        in_specs=[pl.BlockSpec((tm, d), lambda i, j: (i, 0)),
                  pl.BlockSpec((1, d), lambda i, j: (0, 0)),
                  _mod_spec(scale, tm, rpg, d, False),
                  _mod_spec(shift, tm, rpg, d, False),
                  pl.BlockSpec((d, tn), lambda i, j: (0, j))],
        out_specs=pl.BlockSpec((tm, tn), lambda i, j: (i, j)),
        out_shape=jax.ShapeDtypeStruct((m, n), F32),
        scratch_shapes=[pltpu.VMEM((tm, d), BF16)],
        compiler_params=_params(("arbitrary", "arbitrary")),
        name="norm_mod_matmul",
    )(x, g.reshape(1, d), scale, shift, w)


def _mm2_kernel(a_ref, w_ref, x_ref, gate_ref, o_ref):
    y = jnp.dot(a_ref[...], w_ref[...].astype(BF16), preferred_element_type=F32)
    o_ref[...] = x_ref[...] + gate_ref[...] * y


def _matmul_gated_residual(a, w, x, gate):
    m, k = a.shape
    d = w.shape[1]
    groups = gate.shape[0]
    rpg = m // groups
    tm = _pick(rpg, (1024, 512, 256, 128, 64, 32, 16))
    tn = _pick(d, (512, 256, 128))
    return pl.pallas_call(
        _mm2_kernel,
        grid=(m // tm, d // tn),
        in_specs=[pl.BlockSpec((tm, k), lambda i, j: (i, 0)),
                  pl.BlockSpec((k, tn), lambda i, j: (0, j)),
                  pl.BlockSpec((tm, tn), lambda i, j: (i, j)),
                  _mod_spec(gate, tm, rpg, tn, True)],
        out_specs=pl.BlockSpec((tm, tn), lambda i, j: (i, j)),
        out_shape=jax.ShapeDtypeStruct((m, d), F32),
        compiler_params=_params(("arbitrary", "arbitrary")),
        name="matmul_gated_residual",
    )(a, w, x, gate)


def _mlp_kernel(x_ref, g_ref, sc_ref, sh_ref, gate_ref, w1_ref, w2_ref, o_ref, hn_ref):
    j = pl.program_id(1)

    @pl.when(j == 0)
    def _():
        hn_ref[...] = _modnorm(x_ref[...], g_ref[...], sc_ref[...], sh_ref[...]).astype(BF16)

    h = jnp.dot(hn_ref[...], w1_ref[...].astype(BF16), preferred_element_type=F32)
    h = jnp.square(jnp.maximum(h, 0.0))
    y = jnp.dot(h.astype(BF16), w2_ref[...].astype(BF16), preferred_element_type=F32)

    @pl.when(j == 0)
    def _():
        o_ref[...] = y

    @pl.when(j > 0)
    def _():
        o_ref[...] += y

    @pl.when(j == pl.num_programs(1) - 1)
    def _():
        o_ref[...] = x_ref[...] + gate_ref[...] * o_ref[...]


def _mlp(x, g, scale, shift, gate, w1, w2):
    m, d = x.shape
    f = w1.shape[1]
    groups = scale.shape[0]
    rpg = m // groups
    tm = _pick(rpg, (1024, 512, 256, 128, 64, 32, 16, 8))
    tf = _pick(f, (512, 256, 128))
    return pl.pallas_call(
        _mlp_kernel,
        grid=(m // tm, f // tf),
        in_specs=[pl.BlockSpec((tm, d), lambda i, j: (i, 0), pipeline_mode=pl.Buffered(1)),
                  pl.BlockSpec((1, d), lambda i, j: (0, 0)),
                  _mod_spec(scale, tm, rpg, d, False),
                  _mod_spec(shift, tm, rpg, d, False),
                  _mod_spec(gate, tm, rpg, d, False),
                  pl.BlockSpec((d, tf), lambda i, j: (0, j)),
                  pl.BlockSpec((tf, d), lambda i, j: (j, 0))],
        out_specs=pl.BlockSpec((tm, d), lambda i, j: (i, 0), pipeline_mode=pl.Buffered(1)),
        out_shape=jax.ShapeDtypeStruct((m, d), F32),
        scratch_shapes=[pltpu.VMEM((tm, d), BF16)],
        compiler_params=_params(("arbitrary", "arbitrary")),
        name="mlp",
    )(x, g.reshape(1, d), scale, shift, gate, w1, w2)


def _final_kernel(x_ref, g_ref, o_ref):
    x = x_ref[...]
    o_ref[...] = (x * lax.rsqrt(jnp.mean(x * x, axis=-1, keepdims=True) + EPS)) * g_ref[...]


def _final_norm(x, g):
    m, d = x.shape
    tm = _pick(m, (512, 256, 128, 64, 32, 16, 8))
    return pl.pallas_call(
        _final_kernel,
        grid=(m // tm,),
        in_specs=[pl.BlockSpec((tm, d), lambda i: (i, 0)), pl.BlockSpec((1, d), lambda i: (0, 0))],
        out_specs=pl.BlockSpec((tm, d), lambda i: (i, 0)),
        out_shape=jax.ShapeDtypeStruct((m, d), F32),
        compiler_params=_params(("arbitrary",)),
        name="final_norm",
    )(x, g.reshape(1, d))


def _split3(x):
    hi = x.astype(BF16)
    r = x - hi.astype(F32)
    mid = r.astype(BF16)
    lo = (r - mid.astype(F32)).astype(BF16)
    return hi, mid, lo


def _gdn_kernel(qkv_ref, z_ref, ab_ref, gb_ref, gcg_ref, hx_ref, cst_ref, scst_ref, cw_ref, scw_ref,
                alog_ref, dtb_ref, ng_ref, s0_ref,
                mix_ref, sout_ref, ulast_ref,
                xbuf, ubuf, s_ref, *, t_valid, n_kh, n_vh, dk, dv):
    c = GDN_CHUNK
    n = pl.program_id(1)
    nqk = n_kh * dk
    nv = n_vh * dv
    rep = n_vh // n_kh
    hpg = 4
    gw = hpg * c
    n_conv = cw_ref.shape[0]
    n_sc = scw_ref.shape[0]

    @pl.when(n == 0)
    def _():
        xbuf[0:SUBLANE, :] = cst_ref[...]
        ubuf[0:SUBLANE, :] = scst_ref[...]
        s_ref[...] = s0_ref[...]

    xbuf[SUBLANE:SUBLANE + c, :] = qkv_ref[...]
    cw = cw_ref[...]
    off = SUBLANE - (n_conv - 1)
    xc = cw[0:1, :] * xbuf[off:off + c, :]
    for j in range(1, n_conv):
        xc = xc + cw[j:j + 1, :] * xbuf[off + j:off + j + c, :]
    xbuf[0:SUBLANE, :] = xbuf[c:c + SUBLANE, :]
    xc = _silu(xc)

    u = gcg_ref[...] * hx_ref[...]
    ubuf[SUBLANE:SUBLANE + c, :] = u
    scw = scw_ref[...]
    offs = SUBLANE - (n_sc - 1)
    cu = scw[0:1, :] * ubuf[offs:offs + c, :]
    for j in range(1, n_sc):
        cu = cu + scw[j:j + 1, :] * ubuf[offs + j:offs + j + c, :]
    ubuf[0:SUBLANE, :] = ubuf[c:c + SUBLANE, :]
    ulast_ref[...] = u
    mix_ref[:, nv:] = (gb_ref[...] * cu).astype(mix_ref.dtype)

    ab = ab_ref[...]
    g_all = -jnp.exp(alog_ref[...]) * jax.nn.softplus(ab + dtb_ref[...])
    beta_all = jax.nn.sigmoid(ab)
    if t_valid < c:
        rowmask = lax.broadcasted_iota(jnp.int32, (c, 1), 0) < t_valid
        xc = jnp.where(rowmask, xc, 0.0)
        g_all = jnp.where(rowmask, g_all, 0.0)
        beta_all = jnp.where(rowmask, beta_all, 0.0)

    ri = lax.broadcasted_iota(jnp.int32, (c, c), 0)
    ci = lax.broadcasted_iota(jnp.int32, (c, c), 1)
    tril = jnp.where(ri >= ci, 1.0, 0.0).astype(BF16)
    ghi, gmid, glo = _split3(g_all)
    gcum = (jnp.dot(tril, ghi, preferred_element_type=F32)
            + jnp.dot(tril, gmid, preferred_element_type=F32)
            + jnp.dot(tril, glo, preferred_element_type=F32))
    glast = gcum[c - 1:c, :]

    def l2n(x):
        return x * lax.rsqrt(jnp.sum(x * x, axis=-1, keepdims=True) + EPS)

    gi = lax.broadcasted_iota(jnp.int32, (gw, gw), 0)
    gj = lax.broadcasted_iota(jnp.int32, (gw, gw), 1)
    same = (gi // c) == (gj // c)
    low_incl = same & (gi >= gj)
    low_strict = same & (gi > gj)
    lane_blk = lax.broadcasted_iota(jnp.int32, (dk, gw), 1) // c
    level_mask = []
    size = 1
    while size < c:
        level_mask.append(((gi // (2 * size)) == (gj // (2 * size))) & ((gi // size) != (gj // size)) & (gi > gj))
        size *= 2

    def stack(cols):
        return jnp.concatenate(cols, axis=0)

    for grp in range(n_vh // hpg):
        heads = [grp * hpg + j for j in range(hpg)]
        kheads = [h // rep for h in heads]
        qs = {kh: l2n(xc[:, kh * dk:(kh + 1) * dk]) * (dk ** -0.5) for kh in set(kheads)}
        ks = {kh: l2n(xc[:, nqk + kh * dk:nqk + (kh + 1) * dk]) for kh in set(kheads)}
        q_st = stack([qs[kh] for kh in kheads])
        k_st = stack([ks[kh] for kh in kheads])
        v_st = stack([xc[:, 2 * nqk + h * dv:2 * nqk + (h + 1) * dv] for h in heads])
        beta_st = stack([beta_all[:, n_vh + h:n_vh + h + 1] for h in heads])
        gc_st = stack([gcum[:, h:h + 1] for h in heads])
        gl_st = stack([jnp.broadcast_to(glast[:, h:h + 1], (c, 1)) for h in heads])
        gc_row = jnp.broadcast_to(gc_st, (gw, LANE)).T[0:1, :]
        diff = gc_st - gc_row
        decay = jnp.where(low_incl, jnp.exp(jnp.where(low_incl, diff, 0.0)), 0.0)
        egc = jnp.exp(gc_st)
        kb_st = k_st * beta_st

        nmat = _bdot_nt(kb_st, k_st) * jnp.where(low_strict, decay, 0.0)
        qk = _bdot_nt(q_st, k_st) * decay
        tm = -jnp.where(level_mask[0], nmat, 0.0)
        for lm in level_mask[1:]:
            ll = jnp.where(lm, nmat, 0.0)
            y = ll + _bdot(tm, ll)
            tm = tm - y - _bdot(y, tm)
        rhs = jnp.concatenate([v_st * beta_st, kb_st * egc], axis=1)
        sol = rhs + _bdot(tm, rhs)
        u_st = sol[:, :dv]
        w_st = sol[:, dv:]
        qg_st = q_st * egc
        kd_st = k_st * jnp.exp(gl_st - gc_st)

        vnew, qs_out = [], []
        for j, h in enumerate(heads):
            wq = jnp.concatenate([w_st[j * c:(j + 1) * c], qg_st[j * c:(j + 1) * c]], axis=0)
            r2 = _bdot(wq, s_ref[h])
            vnew.append(u_st[j * c:(j + 1) * c] - r2[:c])
            qs_out.append(r2[c:])
        vnew_st = stack(vnew)
        o_st = stack(qs_out) + _bdot(qk, vnew_st)
        kd_t = kd_st.T
        for j, h in enumerate(heads):
            upd = _bdot(jnp.where(lane_blk == j, kd_t, 0.0), vnew_st)
            s_ref[h] = s_ref[h] * jnp.exp(glast[:, h:h + 1]) + upd
            o_h = o_st[j * c:(j + 1) * c]
            on = (o_h * lax.rsqrt(jnp.mean(o_h * o_h, axis=-1, keepdims=True) + EPS)) * ng_ref[...]
            zz = z_ref[:, h * dv:(h + 1) * dv]
            mix_ref[:, h * dv:(h + 1) * dv] = (on * _silu(zz)).astype(mix_ref.dtype)

    sout_ref[...] = s_ref[...]


def _gdn_sconv(proj, conv_state, sc_state, s0, conv_w, sc_w, a_log, dt_bias, norm_g, t_valid):
    bsz, n_vh, dk, dv = s0.shape
    n_conv, conv_dim = conv_w.shape
    n_sc, sc = sc_w.shape
    nv = n_vh * dv
    n_kh = (conv_dim - nv) // (2 * dk)
    c = GDN_CHUNK
    rows = proj.shape[0]
    nchunk = rows // (bsz * c)
    assert conv_dim % nv == 0 and (conv_dim + nv) % sc == 0 and n_vh % 4 == 0 and 2 * n_vh <= LANE
    zb = conv_dim // nv
    gbb = (conv_dim + nv) // sc
    abb = (conv_dim + nv + 3 * sc) // LANE
    pad_lane = lambda v: jnp.pad(v.reshape(1, -1), ((0, 0), (0, LANE - v.shape[-1])))
    cst = jnp.pad(conv_state.astype(F32), ((0, 0), (SUBLANE - (n_conv - 1), 0), (0, 0)))
    scst = jnp.pad(sc_state.astype(F32), ((0, 0), (SUBLANE - (n_sc - 1), 0), (0, 0)))
    row = lambda b, n: b * nchunk + n
    kern = functools.partial(_gdn_kernel, t_valid=t_valid, n_kh=n_kh, n_vh=n_vh, dk=dk, dv=dv)
    return pl.pallas_call(
        kern,
        grid=(bsz, nchunk),
        in_specs=[pl.BlockSpec((c, conv_dim), lambda b, n: (row(b, n), 0)),
                  pl.BlockSpec((c, nv), lambda b, n: (row(b, n), zb)),
                  pl.BlockSpec((c, LANE), lambda b, n: (row(b, n), abb)),
                  pl.BlockSpec((c, sc), lambda b, n: (row(b, n), gbb)),
                  pl.BlockSpec((c, sc), lambda b, n: (row(b, n), gbb + 1)),
                  pl.BlockSpec((c, sc), lambda b, n: (row(b, n), gbb + 2)),
                  pl.BlockSpec((None, SUBLANE, conv_dim), lambda b, n: (b, 0, 0)),
                  pl.BlockSpec((None, SUBLANE, sc), lambda b, n: (b, 0, 0)),
                  pl.BlockSpec((n_conv, conv_dim), lambda b, n: (0, 0)),
                  pl.BlockSpec((n_sc, sc), lambda b, n: (0, 0)),
                  pl.BlockSpec((1, LANE), lambda b, n: (0, 0)),
                  pl.BlockSpec((1, LANE), lambda b, n: (0, 0)),
                  pl.BlockSpec((1, dv), lambda b, n: (0, 0)),
                  pl.BlockSpec((None, n_vh, dk, dv), lambda b, n: (b, 0, 0, 0))],
        out_specs=[pl.BlockSpec((c, nv + sc), lambda b, n: (row(b, n), 0)),
                   pl.BlockSpec((None, n_vh, dk, dv), lambda b, n: (b, 0, 0, 0)),
                   pl.BlockSpec((None, c, sc), lambda b, n: (b, 0, 0))],
        out_shape=[jax.ShapeDtypeStruct((rows, nv + sc), BF16),
                   jax.ShapeDtypeStruct((bsz, n_vh, dk, dv), F32),
                   jax.ShapeDtypeStruct((bsz, c, sc), F32)],
        scratch_shapes=[pltpu.VMEM((c + SUBLANE, conv_dim), F32),
                        pltpu.VMEM((c + SUBLANE, sc), F32),
                        pltpu.VMEM((n_vh, dk, dv), F32)],
        compiler_params=_params(("arbitrary", "arbitrary")),
        name="gdn_sconv",
    )(proj, proj, proj, proj, proj, proj, cst, scst, conv_w, sc_w,
      pad_lane(a_log), pad_lane(dt_bias), norm_g.reshape(1, dv), s0.astype(F32))


def _cmp_prompt_kernel(kv_ref, pw_ref, pj_ref, o_ref, *, blk):
    t, hd = kv_ref.shape
    x = kv_ref[...].reshape(t // blk, blk, hd) * pw_ref[...][None]
    o_ref[...] = _bdot(jnp.sum(x, axis=1), pj_ref[...])


def _cmp_prompt(proj, bsz, t, pw, pj, col0):
    nch, blk, hd = pw.shape
    return pl.pallas_call(
        functools.partial(_cmp_prompt_kernel, blk=blk),
        grid=(bsz, nch),
        in_specs=[pl.BlockSpec((t, hd), lambda b, ch: (b, col0 + ch)),
                  pl.BlockSpec((None, blk, hd), lambda b, ch: (ch, 0, 0)),
                  pl.BlockSpec((None, hd, hd), lambda b, ch: (ch, 0, 0))],
        out_specs=pl.BlockSpec((None, None, t // blk, hd), lambda b, ch: (b, ch, 0, 0)),
        out_shape=jax.ShapeDtypeStruct((bsz, nch, t // blk, hd), F32),
        compiler_params=_params(("arbitrary", "arbitrary")),
        name="cmp_prompt",
    )(proj, pw, pj)


_NT = (((1,), (1,)), ((), ()))


def _slope_features(sl2, lane):
    hi = sl2.astype(BF16).astype(F32)
    lo = sl2 - hi
    return jnp.where(lane == 0, 64.0 * hi, jnp.where(lane == 1, hi, jnp.where(lane == 2, 64.0 * lo,
                                                                             jnp.where(lane == 3, lo, 0.0))))


def _attn_prompt_kernel(q_ref, kc_ref, vc_ref, ks_ref, vs_ref, kw_ref, vw_ref, gt_ref, sl_ref, pos_ref, oh_ref,
                        o_ref, *, tq, tk, wk, blk, n_g, n_kv, hd):
    h = pl.program_id(1)
    qt = pl.program_id(2)
    q0 = qt * tq
    nc = kc_ref.shape[0]
    t_all = ks_ref.shape[0]
    sl = sl_ref[...]
    lane = lax.broadcasted_iota(jnp.int32, (1, LANE), 1)
    qs = [q_ref[:, g * hd:(g + 1) * hd] * (hd ** -0.5) for g in range(n_g)]
    qb = jnp.concatenate(qs, axis=0).astype(BF16)
    qaug = jnp.concatenate(
        [jnp.concatenate([(qs[g] * LOG2E).astype(BF16),
                          jnp.broadcast_to(_slope_features(sl[:, g:g + 1] * LOG2E, lane), (tq, LANE)).astype(BF16)],
                         axis=1) for g in range(n_g)], axis=0)
    qpos = q0 + lax.broadcasted_iota(jnp.int32, (tq, 1), 0)

    bidx = lax.broadcasted_iota(jnp.int32, (1, nc), 1)
    start = bidx * blk
    s_c = _bdot_nt(qb, kc_ref[...])
    dist_c = jnp.abs(qpos.astype(F32) - (start.astype(F32) + (blk - 1) / 2))
    valid_c = (start + (blk - 1)) <= qpos
    p_c = jnp.concatenate([_masked_softmax(s_c[g * tq:(g + 1) * tq] - sl[:, g:g + 1] * dist_c, valid_c)
                           for g in range(n_g)], axis=0)
    o_c = _bdot(p_c, vc_ref[...])
    imp = p_c[0:tq]
    for g in range(1, n_g):
        imp = imp + p_c[g * tq:(g + 1) * tq]

    imp_t = jnp.concatenate([imp, jnp.zeros((tq, LANE - nc), F32)], axis=1).T[0:nc, :]
    bcol = lax.broadcasted_iota(jnp.int32, (nc, 1), 0)
    cur = (q0 + lax.broadcasted_iota(jnp.int32, (1, tq), 1)) // blk
    score = jnp.where(bcol < cur, imp_t, NEG)
    score = jnp.where((bcol == 0) | (bcol == cur), FORCE, score)
    rank = jnp.zeros((nc, tq), F32)
    for i in range(nc):
        ri = score[i:i + 1, :]
        rank = rank + jnp.where((ri > score) | ((ri == score) & (bcol > i)), 1.0, 0.0)
    selneg_t = jnp.where((rank < min(SEL_TOPK, nc)) & (score > 0.5 * NEG), 0.0, NEG)
    selneg = jnp.concatenate([selneg_t, jnp.zeros((LANE - nc, tq), F32)], axis=0).T.astype(BF16)

    def flash_step(k_ref, v_ref, k0, size, bias, carry):
        m, l, acc = carry
        kaug = jnp.concatenate([k_ref[pl.ds(k0, size), :].astype(BF16), pos_ref[pl.ds(k0, size), :]], axis=1)
        s = lax.dot_general(qaug, kaug, _NT, preferred_element_type=F32)
        ps, m2, l2, accs = [], [], [], []
        for g in range(n_g):
            sg = s[g * tq:(g + 1) * tq] + bias
            mn = jnp.maximum(m[g], jnp.max(sg, axis=-1, keepdims=True))
            al = jnp.exp2(m[g] - mn)
            p = jnp.exp2(sg - mn)
            l2.append(al * l[g] + jnp.sum(p, axis=-1, keepdims=True))
            m2.append(mn)
            accs.append(al * acc[g * tq:(g + 1) * tq])
            ps.append(p.astype(BF16))
        pv = jnp.dot(jnp.concatenate(ps, axis=0), v_ref[pl.ds(k0, size), :].astype(BF16),
                     preferred_element_type=F32)
        return tuple(m2), tuple(l2), jnp.concatenate(accs, axis=0) + pv

    def finish(carry):
        _, l, acc = carry
        return [acc[g * tq:(g + 1) * tq] / jnp.maximum(l[g], 1e-30) for g in range(n_g)]

    init = (tuple(jnp.full((tq, 1), NEG, F32) for _ in range(n_g)),
            tuple(jnp.zeros((tq, 1), F32) for _ in range(n_g)), jnp.zeros((n_g * tq, hd), F32))

    def sel_bias(k0):
        return lax.dot_general(selneg, oh_ref[pl.ds(k0, tk), :], _NT, preferred_element_type=F32)

    def sel_body(ci, carry):
        k0 = pl.multiple_of(ci * tk, tk)
        return flash_step(ks_ref, vs_ref, k0, tk, sel_bias(k0), carry)

    n_full = q0 // tk
    carry = lax.fori_loop(0, n_full, sel_body, init)
    k0 = pl.multiple_of(n_full * tk, tk)
    kpos = k0 + lax.broadcasted_iota(jnp.int32, (1, tk), 1)
    o_s = finish(flash_step(ks_ref, vs_ref, k0, tk, sel_bias(k0) + jnp.where(kpos <= qpos, 0.0, NEG), carry))

    w0 = pl.multiple_of(jnp.clip(q0 + tq - wk, 0, t_all - wk), tq)
    dist = qpos - (w0 + lax.broadcasted_iota(jnp.int32, (1, wk), 1))
    o_w = finish(flash_step(kw_ref, vw_ref, w0, wk, jnp.where((dist >= 0) & (dist < WINDOW), 0.0, NEG), init))

    gt = jax.nn.sigmoid(pltpu.roll(gt_ref[...], (LANE - n_g * h) % LANE, 1))
    n_h = n_g * n_kv
    for g in range(n_g):
        r = slice(g * tq, (g + 1) * tq)
        o = (gt[:, g:g + 1] * o_c[r] + gt[:, n_h + g:n_h + g + 1] * o_s[g]
             + gt[:, 2 * n_h + g:2 * n_h + g + 1] * o_w[g])
        o_ref[:, g * hd:(g + 1) * hd] = o.astype(o_ref.dtype)


def _attn_prompt(proj, cmp, slopes, bsz, t, n_h, n_kv, hd, blk):
    n_g = n_h // n_kv
    tq = 128
    tk = _pick(t, (512, 256, 128))
    wk = WINDOW + tq
    assert t % tq == 0 and t >= wk and t // blk <= LANE and blk == 64 and n_h * 3 <= LANE
    nqt = t // tq
    nc = t // blk
    kvb = n_h
    pos = jnp.arange(t, dtype=jnp.int32)[:, None]
    lane = jnp.arange(LANE, dtype=jnp.int32)[None, :]
    pos_tab = jnp.where(lane < 4, jnp.where(lane % 2 == 0, pos // 64, pos % 64), 0).astype(BF16)
    onehot = (lane == pos // blk).astype(BF16)
    gate_blk = (n_h * hd + 6 * n_kv * hd) // LANE
    kern = functools.partial(_attn_prompt_kernel, tq=tq, tk=tk, wk=wk, blk=blk, n_g=n_g, n_kv=n_kv, hd=hd)
    kv_spec = lambda off: pl.BlockSpec((t, hd), lambda b, h, q: (b, kvb + off + h))
    return pl.pallas_call(
        kern,
        grid=(bsz, n_kv, nqt),
        in_specs=[pl.BlockSpec((tq, n_g * hd), lambda b, h, q: (b * nqt + q, h)),
                  pl.BlockSpec((None, None, nc, hd), lambda b, h, q: (b, h, 0, 0)),
                  pl.BlockSpec((None, None, nc, hd), lambda b, h, q: (b, n_kv + h, 0, 0)),
                  kv_spec(2 * n_kv), kv_spec(3 * n_kv), kv_spec(4 * n_kv), kv_spec(5 * n_kv),
                  pl.BlockSpec((tq, LANE), lambda b, h, q: (b * nqt + q, gate_blk)),
                  pl.BlockSpec((None, 1, LANE), lambda b, h, q: (h, 0, 0)),
                  pl.BlockSpec((t, LANE), lambda b, h, q: (0, 0)),
                  pl.BlockSpec((t, LANE), lambda b, h, q: (0, 0))],
        out_specs=pl.BlockSpec((tq, n_g * hd), lambda b, h, q: (b * nqt + q, h)),
        out_shape=jax.ShapeDtypeStruct((bsz * t, n_h * hd), BF16),
        compiler_params=_params(("arbitrary", "arbitrary", "arbitrary")),
        name="attn_prompt",
    )(proj, cmp, cmp, proj, proj, proj, proj, proj, slopes, pos_tab, onehot)


def _pool_pages_kernel(pt_ref, *refs, n_pp, rows_blk):
    pw = refs[n_pp][...]
    o_ref = refs[n_pp + 1]
    per_page = refs[0].shape[0] // rows_blk
    nch = o_ref.shape[0] // (n_pp * per_page)
    for k in range(n_pp):
        for hb in range(per_page):
            r = k * per_page + hb
            x = refs[k][hb * rows_blk:(hb + 1) * rows_blk, :] * pw
            o_ref[r * nch:(r + 1) * nch, :] = jnp.sum(x.reshape(rows_blk // nch, nch, x.shape[-1]), axis=0)


def _page_map(b, p, pt, *, layer, k, n_pp):
    return (layer, pt[b, p * n_pp + k], 0, 0)


def _pool_pages(cache, layer, page_table, pw_rows, nch):
    _, _, prow, hd = cache.shape
    bd, n_pages = page_table.shape
    rows_blk = pw_rows.shape[0]
    n_pp = _pick(n_pages, (8, 4, 2, 1))
    per_page = prow // rows_blk
    in_specs = [pl.BlockSpec((None, None, prow, hd),
                             functools.partial(_page_map, layer=layer, k=k, n_pp=n_pp)) for k in range(n_pp)]
    in_specs.append(pl.BlockSpec((rows_blk, hd), lambda b, p, pt: (0, 0)))
    return pl.pallas_call(
        functools.partial(_pool_pages_kernel, n_pp=n_pp, rows_blk=rows_blk),
        grid_spec=pltpu.PrefetchScalarGridSpec(
            num_scalar_prefetch=1, grid=(bd, n_pages // n_pp), in_specs=in_specs,
            out_specs=pl.BlockSpec((None, n_pp * per_page * nch, hd), lambda b, p, pt: (b, p, 0))),
        out_shape=jax.ShapeDtypeStruct((bd, n_pages * per_page * nch, hd), F32),
        compiler_params=_params(("arbitrary", "arbitrary")),
        name="pool_pages",
    )(page_table, *([cache] * n_pp), pw_rows)


def _attn_sample_a_kernel(q_ref, pooled_ref, pj_ref, kvw_ref, gt_ref, sl_ref, ocw_ref, selm_ref,
                          score_buf, *, past, n_q, n_kv, n_g, hd, blk, wb, nbp):
    nch = 2 * n_kv
    nc = pooled_ref.shape[0] // nch
    n_blocks = (past + n_q + blk - 1) // blk
    kvw_rows = kvw_ref.shape[0]
    half = n_kv * hd
    rep = lambda x: jnp.concatenate([x] * n_g, axis=0)
    sl = sl_ref[...]
    gt = jax.nn.sigmoid(gt_ref[...])
    qpos = past + lax.broadcasted_iota(jnp.int32, (n_q, 1), 0)
    qpos_r = rep(qpos)
    bidx_c = lax.broadcasted_iota(jnp.int32, (1, nc), 1)
    start = bidx_c * blk
    bidx = lax.broadcasted_iota(jnp.int32, (1, nbp), 1)
    cur = qpos // blk
    score_buf[...] = jnp.full(score_buf.shape, DEAD, F32)

    for h in range(n_kv):
        qb = jnp.concatenate([q_ref[:, (h * n_g + g) * hd:(h * n_g + g + 1) * hd] for g in range(n_g)], axis=0)
        qb = (qb * (hd ** -0.5)).astype(BF16)
        slope = jnp.concatenate(
            [jnp.broadcast_to(sl[:, h * n_g + g:h * n_g + g + 1], (n_q, 1)) for g in range(n_g)], axis=0)
        kc = _bdot(pooled_ref[pl.ds(h, nc, stride=nch), :], pj_ref[h])
        vc = _bdot(pooled_ref[pl.ds(n_kv + h, nc, stride=nch), :], pj_ref[n_kv + h])
        s = _bdot_nt(qb, kc)
        dist_c = qpos_r.astype(F32) - (start.astype(F32) + (blk - 1) / 2)
        s = s - slope * jnp.abs(dist_c)
        p_c = _masked_softmax(s, (start + (blk - 1)) <= qpos_r)
        o_c = _bdot(p_c, vc)
        imp = p_c[0:n_q]
        for g in range(1, n_g):
            imp = imp + p_c[g * n_q:(g + 1) * n_q]
        imp = jnp.concatenate([imp, jnp.zeros((n_q, nbp - nc), F32)], axis=1)
        score = jnp.where(bidx < cur, imp, NEG)
        score = jnp.where((bidx == 0) | (bidx == cur), FORCE, score)
        score = jnp.where(bidx < n_blocks, score, DEAD)
        score_buf[h * n_q:(h + 1) * n_q, :] = score

        kw = kvw_ref[:, h * hd:(h + 1) * hd]
        vw = kvw_ref[:, half + h * hd:half + (h + 1) * hd]
        kidx = lax.broadcasted_iota(jnp.int32, (1, kvw_rows), 1)
        dist = qpos_r - (past - wb + kidx)
        valid = (dist >= 0) & (dist < WINDOW) & (kidx < wb + n_q)
        sw = _bdot_nt(qb, kw) - slope * jnp.abs(dist).astype(F32)
        o_w = _bdot(_masked_softmax(sw, valid), vw)
        for g in range(n_g):
            hh = h * n_g + g
            r = slice(g * n_q, (g + 1) * n_q)
            ocw_ref[:, hh * hd:(hh + 1) * hd] = (gt[:, hh:hh + 1] * o_c[r]
                                                 + gt[:, 2 * n_kv * n_g + hh:2 * n_kv * n_g + hh + 1] * o_w[r])

    sc_all = score_buf[...]
    sc_t = sc_all.T
    ii = lax.broadcasted_iota(jnp.int32, (nbp, 1), 0)
    for h in range(n_kv):
        for t in range(n_q):
            r = h * n_q + t
            col = sc_t[:, r:r + 1]
            row = sc_all[r:r + 1, :]
            before = (col > row) | ((col == row) & (ii < bidx))
            rank = jnp.sum(jnp.where(before, 1.0, 0.0), axis=0, keepdims=True)
            sel = jnp.where((rank < min(SEL_TOPK, n_blocks)) & (row > 0.5 * NEG), 1.0, 0.0)
            for g in range(n_g):
                selm_ref[h, g * n_q + t:g * n_q + t + 1, :] = sel


def _attn_sample_a(proj3, pooled, pj, kv_all, slopes_row, past, n_h, n_kv, hd, blk, wb):
    bd, n_q, _ = proj3.shape
    n_g = n_h // n_kv
    n_blocks = (past + n_q + blk - 1) // blk
    nbp = pl.cdiv(n_blocks, LANE) * LANE
    kvw_rows = kv_all.shape[1]
    qw = n_h * hd
    assert n_kv * n_q <= LANE
    kern = functools.partial(_attn_sample_a_kernel, past=past, n_q=n_q, n_kv=n_kv, n_g=n_g, hd=hd, blk=blk,
                             wb=wb, nbp=nbp)
    return pl.pallas_call(
        kern,
        grid=(bd,),
        in_specs=[pl.BlockSpec((None, n_q, qw), lambda b: (b, 0, 0)),
                  pl.BlockSpec((None, pooled.shape[1], hd), lambda b: (b, 0, 0)),
                  pl.BlockSpec((2 * n_kv, hd, hd), lambda b: (0, 0, 0)),
                  pl.BlockSpec((None, kvw_rows, 2 * n_kv * hd), lambda b: (b, 0, 0)),
                  pl.BlockSpec((None, n_q, LANE), lambda b: (b, 0, (qw + 6 * n_kv * hd) // LANE)),
                  pl.BlockSpec((1, LANE), lambda b: (0, 0))],
        out_specs=[pl.BlockSpec((None, n_q, qw), lambda b: (b, 0, 0)),
                   pl.BlockSpec((None, n_kv, n_g * n_q, nbp), lambda b: (b, 0, 0, 0))],
        out_shape=[jax.ShapeDtypeStruct((bd, n_q, qw), F32),
                   jax.ShapeDtypeStruct((bd, n_kv, n_g * n_q, nbp), F32)],
        scratch_shapes=[pltpu.VMEM((LANE, nbp), F32)],
        compiler_params=_params(("arbitrary",)),
        name="attn_sample_a",
    )(proj3, pooled, pj, kv_all, proj3, slopes_row)


def _attn_sample_sel_kernel(pt_ref, *refs, n_pp, past, n_q, n_kv, n_g, hd, blk):
    pages = refs[:n_pp]
    q_ref, selm_ref, new_ref, ocw_ref, gt_ref, sl_ref, o_ref, m_ref, l_ref, acc_ref = refs[n_pp:]
    p = pl.program_id(1)
    nch = 2 * n_kv
    page = pages[0].shape[0] // nch
    keys = n_pp * page
    nbp = selm_ref.shape[-1]
    rows = n_g * n_q
    half = n_kv * hd
    rep = lambda x: jnp.concatenate([x] * n_g, axis=0)
    sl = sl_ref[...]
    qpos_r = rep(past + lax.broadcasted_iota(jnp.int32, (n_q, 1), 0))

    @pl.when(p == 0)
    def _():
        m_ref[...] = jnp.full(m_ref.shape, NEG, F32)
        l_ref[...] = jnp.zeros_like(l_ref)
        acc_ref[...] = jnp.zeros_like(acc_ref)

    def heads():
        for h in range(n_kv):
            qb = jnp.concatenate([q_ref[:, (h * n_g + g) * hd:(h * n_g + g + 1) * hd] for g in range(n_g)],
                                 axis=0) * (hd ** -0.5)
            slope = jnp.concatenate(
                [jnp.broadcast_to(sl[:, h * n_g + g:h * n_g + g + 1], (n_q, 1)) for g in range(n_g)], axis=0)
            yield h, qb, slope

    def update(h, sc, mask, pv_fn):
        sc = jnp.where(mask, sc, NEG)
        m = m_ref[h]
        m_new = jnp.maximum(m, jnp.max(sc, axis=-1, keepdims=True))
        alpha = jnp.exp(m - m_new)
        pr = jnp.where(mask, jnp.exp(sc - m_new), 0.0)
        l_ref[h] = alpha * l_ref[h] + jnp.sum(pr, axis=-1, keepdims=True)
        acc_ref[h] = alpha * acc_ref[h] + pv_fn(pr)
        m_ref[h] = m_new

    k0 = p * keys
    kk = lax.broadcasted_iota(jnp.int32, (nbp, keys), 1)
    nn = lax.broadcasted_iota(jnp.int32, (nbp, keys), 0)
    expand = jnp.where(nn == (k0 + kk) // blk, 1.0, 0.0).astype(BF16)
    dist = qpos_r - (k0 + lax.broadcasted_iota(jnp.int32, (1, keys), 1))
    for h, qb, slope in heads():
        mask = (jnp.dot(selm_ref[h].astype(BF16), expand, preferred_element_type=F32) > 0.5) & (dist >= 0)
        kx = jnp.concatenate([pages[k][pl.ds(h, page, stride=nch), :].astype(BF16) for k in range(n_pp)], axis=0)
        vx = jnp.concatenate([pages[k][pl.ds(n_kv + h, page, stride=nch), :].astype(BF16) for k in range(n_pp)],
                             axis=0)
        sc = _bdot_nt(qb, kx) - slope * dist.astype(F32)
        update(h, sc, mask, lambda pr, vx=vx: _bdot(pr, vx))

    @pl.when(p == pl.num_programs(1) - 1)
    def _():
        gt = jax.nn.sigmoid(gt_ref[...])
        for h, qb, slope in heads():
            nb0 = past // blk
            seln = selm_ref[h][:, nb0:nb0 + 1] > 0.5
            qr = qb.astype(BF16).astype(F32)
            for j in range(n_q):
                kj = new_ref[j:j + 1, h * hd:(h + 1) * hd].astype(BF16).astype(F32)
                vj = new_ref[j:j + 1, half + h * hd:half + (h + 1) * hd].astype(BF16).astype(F32)
                dj = qpos_r - (past + j)
                sc = jnp.sum(qr * kj, axis=-1, keepdims=True) - slope * dj.astype(F32)
                update(h, sc, seln & (dj >= 0), lambda pr, vj=vj: pr.astype(BF16).astype(F32) * vj)
            o_s = acc_ref[h] / jnp.maximum(l_ref[h], 1e-30)
            for g in range(n_g):
                hh = h * n_g + g
                r = slice(g * n_q, (g + 1) * n_q)
                gcol = n_kv * n_g + hh
                o = ocw_ref[:, hh * hd:(hh + 1) * hd] + gt[:, gcol:gcol + 1] * o_s[r]
                o_ref[:, hh * hd:(hh + 1) * hd] = o.astype(o_ref.dtype)


def _attn_sample_sel(cache, layer, page_table, proj3, selm, new_rows, ocw, slopes_row, past, n_h, n_kv, hd, blk):
    _, _, prow, _ = cache.shape
    bd, n_pages = page_table.shape
    n_q = proj3.shape[1]
    n_g = n_h // n_kv
    qw = n_h * hd
    nbp = selm.shape[-1]
    n_pp = _pick(n_pages, (8, 4, 2, 1))
    rows = n_g * n_q
    in_specs = [pl.BlockSpec((None, None, prow, hd),
                             functools.partial(_page_map, layer=layer, k=k, n_pp=n_pp)) for k in range(n_pp)]
    in_specs += [pl.BlockSpec((None, n_q, qw), lambda b, p, pt: (b, 0, 0)),
                 pl.BlockSpec((None, n_kv, rows, nbp), lambda b, p, pt: (b, 0, 0, 0)),
                 pl.BlockSpec((None, new_rows.shape[1], new_rows.shape[2]), lambda b, p, pt: (b, 0, 0)),
                 pl.BlockSpec((None, n_q, qw), lambda b, p, pt: (b, 0, 0)),
                 pl.BlockSpec((None, n_q, LANE), lambda b, p, pt: (b, 0, (qw + 6 * n_kv * hd) // LANE)),
                 pl.BlockSpec((1, LANE), lambda b, p, pt: (0, 0))]
    kern = functools.partial(_attn_sample_sel_kernel, n_pp=n_pp, past=past, n_q=n_q, n_kv=n_kv, n_g=n_g,
                             hd=hd, blk=blk)
    return pl.pallas_call(
        kern,
        grid_spec=pltpu.PrefetchScalarGridSpec(
            num_scalar_prefetch=1, grid=(bd, n_pages // n_pp), in_specs=in_specs,
            out_specs=pl.BlockSpec((None, n_q, qw), lambda b, p, pt: (b, 0, 0)),
            scratch_shapes=[pltpu.VMEM((n_kv, rows, 1), F32), pltpu.VMEM((n_kv, rows, 1), F32),
                            pltpu.VMEM((n_kv, rows, hd), F32)]),
        out_shape=jax.ShapeDtypeStruct((bd, n_q, qw), BF16),
        compiler_params=_params(("arbitrary", "arbitrary")),
        name="attn_sample_sel",
    )(page_table, *([cache] * n_pp), proj3, selm, new_rows, ocw, proj3, slopes_row)


def _alibi(n_h):
    h = jnp.arange(1, n_h + 1, dtype=F32)
    return jnp.exp2(-8.0 * h / n_h)


def kernel(x_prompt, x_sample, c_prompt, c_sample, state_gdn, state_gdn_conv, state_sconv, cache_kv_cmp,
           cache_kv_sel, state_kv_win, page_table, norm_g, w_ada, b_ada, w_in_e, conv_w_gdn, a_log, dt_bias,
           gdn_norm_g, conv_w_sc, w_out_e, w_in_o, cmp_pool, cmp_proj, w_out_o, w_mlp1, w_mlp2, final_g):
    bsz, seq, d = x_prompt.shape
    bd, n_q, _ = x_sample.shape
    depth = norm_g.shape[0]
    n_vh, dk, dv = state_gdn.shape[2:]
    conv_dim = conv_w_gdn.shape[2]
    sc_dim = conv_w_sc.shape[2]
    nv = n_vh * dv
    blk, _, n_kv = cmp_pool.shape[1:]
    hd = cmp_proj.shape[-1]
    n_h = w_out_o.shape[1] // hd
    kv_row = 2 * n_kv * hd
    qw = n_h * hd
    page = cache_kv_cmp.shape[2]
    past = page_table.shape[1] * page
    wb = state_kv_win.shape[2]
    c = GDN_CHUNK
    assert seq % c == 0 and seq % blk == 0 and seq >= WINDOW and n_q <= c and n_q < blk and past % blk == 0
    assert n_q >= conv_w_gdn.shape[1] - 1 and hd == LANE and dk == LANE and dv == LANE

    rows_c = -(-(bsz + bd) // SUBLANE) * SUBLANE
    c_all = jnp.pad(jnp.concatenate([c_prompt, c_sample], axis=0).astype(F32), ((0, rows_c - bsz - bd), (0, 0)))
    mod = _ada_mod(c_all, w_ada, b_ada).reshape(depth, rows_c, 6, d)

    def mods(l, sample):
        if sample:
            return [jnp.repeat(mod[l, bsz:bsz + bd, j], n_q, axis=0)[None] for j in range(6)]
        return [mod[l, :bsz, j][:, None, :] for j in range(6)]

    slopes = _alibi(n_h)
    slopes_kv = jnp.pad(slopes.reshape(n_kv, 1, n_h // n_kv), ((0, 0), (0, 0), (0, LANE - n_h // n_kv)))
    slopes_row = jnp.pad(slopes.reshape(1, n_h), ((0, 0), (0, LANE - n_h)))
    nch = 2 * n_kv
    cache_cmp = cache_kv_cmp.reshape(cache_kv_cmp.shape[:2] + (page * nch, hd))
    cache_sel = cache_kv_sel.reshape(cache_kv_sel.shape[:2] + (page * nch, hd))

    def even_weights(i):
        w = w_in_e[i]
        o_ab = conv_dim + nv
        o_rest = o_ab + 2 * n_vh
        return jnp.concatenate([w[:, :o_ab], w[:, o_rest:], w[:, o_ab:o_rest],
                                jnp.zeros((d, LANE - 2 * n_vh), w.dtype)], axis=1)

    def run(x3, sample):
        nb, t, _ = x3.shape
        x = x3.reshape(nb * t, d).astype(F32)
        ev, od = [], []
        for l in range(depth):
            i = l // 2
            sh1, sc1, g1, sh2, sc2, g2 = mods(l, sample)
            if l % 2 == 0:
                proj = _norm_mod_matmul(x, norm_g[l, 0], sc1, sh1, even_weights(i))
                qkv_rows = proj[:, :conv_dim].reshape(nb, t, conv_dim)
                if sample:
                    projp = jnp.pad(proj.reshape(nb, t, -1), ((0, 0), (0, c - t), (0, 0))).reshape(nb * c, -1)
                    cst, scst, s0, tv = state_gdn_conv[i], state_sconv[i], state_gdn[i], t
                else:
                    projp = proj
                    cst = jnp.zeros((nb, conv_w_gdn.shape[1] - 1, conv_dim), F32)
                    scst = jnp.zeros((nb, conv_w_sc.shape[1] - 1, sc_dim), F32)
                    s0, tv = jnp.zeros((nb, n_vh, dk, dv), F32), c
                mix, s_new, u_last = _gdn_sconv(projp, cst, scst, s0, conv_w_gdn[i], conv_w_sc[i], a_log[i],
                                                dt_bias[i], gdn_norm_g[i], tv)
                if sample:
                    mix = mix.reshape(nb, c, -1)[:, :t].reshape(nb * t, -1)
                conv_new = qkv_rows[:, t - cst.shape[1]:]
                sc_new = u_last[:, tv - scst.shape[1]:tv]
                ev.append((s_new, conv_new, sc_new))
                x = _matmul_gated_residual(mix, w_out_e[i], x, g1)
            else:
                proj = _norm_mod_matmul(x, norm_g[l, 0], sc1, sh1, w_in_o[i])
                kv_c = proj[:, qw:qw + kv_row].reshape(nb, t, 2, n_kv, hd)
                kv_s = proj[:, qw + kv_row:qw + 2 * kv_row].reshape(nb, t, 2, n_kv, hd)
                kv_w = proj[:, qw + 2 * kv_row:qw + 3 * kv_row].reshape(nb, t, 2, n_kv, hd)
                pj = cmp_proj[i].reshape(2 * n_kv, hd, hd)
                if sample:
                    pw_rows = jnp.broadcast_to(cmp_pool[i].reshape(blk * nch, 1), (blk * nch, hd))
                    pooled = _pool_pages(cache_cmp, i, page_table, pw_rows, nch)
                    kv_all = jnp.concatenate([state_kv_win[i].reshape(nb, wb, kv_row).astype(F32),
                                              kv_w.reshape(nb, t, kv_row)], axis=1)
                    kv_all_p = jnp.pad(kv_all, ((0, 0), (0, (-kv_all.shape[1]) % SUBLANE), (0, 0)))
                    proj3 = proj.reshape(nb, t, -1)
                    ocw, selm = _attn_sample_a(proj3, pooled, pj, kv_all_p, slopes_row, past, n_h, n_kv, hd, blk, wb)
                    new_rows = jnp.pad(kv_s.reshape(nb, t, kv_row), ((0, 0), (0, (-t) % SUBLANE), (0, 0)))
                    o = _attn_sample_sel(cache_sel, i, page_table, proj3, selm, new_rows, ocw, slopes_row, past,
                                         n_h, n_kv, hd, blk).reshape(nb * t, qw)
                    kv_win_new = kv_all[:, t:].reshape(nb, wb, 2, n_kv, hd)
                else:
                    pw = jnp.broadcast_to(cmp_pool[i].reshape(blk, 2 * n_kv).T[:, :, None], (2 * n_kv, blk, hd))
                    cmp = _cmp_prompt(proj, nb, t, pw, pj, n_h)
                    o = _attn_prompt(proj, cmp, slopes_kv, nb, t, n_h, n_kv, hd, blk)
                    kv_win_new = kv_w[:, t - min(WINDOW, t):]
                od.append((kv_c, kv_s, kv_win_new))
                x = _matmul_gated_residual(o, w_out_o[i], x, g1)
            x = _mlp(x, norm_g[l, 1], sc2, sh2, g2, w_mlp1[l], w_mlp2[l])
        y = _final_norm(x, final_g).reshape(nb, t, d)
        evs = [jnp.stack([s[j] for s in ev]) for j in range(3)]
        ods = [jnp.stack([s[j] for s in od]) for j in range(3)]
        return y, evs, ods

    y_p, (gdn_p, gconv_p, sconv_p), (kvc_p, kvs_p, kvw_p) = run(x_prompt, False)
    y_s, (gdn_s, gconv_s, sconv_s), (kvc_s, kvs_s, kvw_s) = run(x_sample, True)
    return (y_p, y_s, gdn_p, gdn_s, gconv_p, gconv_s, sconv_p, sconv_s, kvc_p, kvc_s, kvs_p, kvs_s, kvw_p, kvw_s)
```

```python
import functools

import jax
import jax.numpy as jnp
from jax import lax
from jax.experimental import pallas as pl
from jax.experimental.pallas import tpu as pltpu

F32 = jnp.float32
BF16 = jnp.bfloat16

EPS = 1e-6
NEG = -1e30
FORCE = 1e6
DEAD = -3e38
SEL_TOPK = 16
LOG2E = 1.4426950408889634
WINDOW = 512
GDN_CHUNK = 64
LANE = 128
SUBLANE = 8
VMEM_BUDGET = 56 * 1024 * 1024


def _params(sem):
    return pltpu.CompilerParams(dimension_semantics=sem, vmem_limit_bytes=VMEM_BUDGET)


def _pick(n, cands):
    for c in cands:
        if n % c == 0:
            return c
    return n


def _bdot(a, b):
    return jnp.dot(a.astype(BF16), b.astype(BF16), preferred_element_type=F32)


def _bdot_nt(a, b):
    return lax.dot_general(a.astype(BF16), b.astype(BF16), (((1,), (1,)), ((), ())),
                           preferred_element_type=F32)


def _silu(x):
    return x * jax.nn.sigmoid(x)


def _modnorm(x, g, scale, shift):
    y = x * lax.rsqrt(jnp.mean(x * x, axis=-1, keepdims=True) + EPS)
    return (y * g) * (1.0 + scale) + shift


def _masked_softmax(s, valid):
    s = jnp.where(valid, s, NEG)
    m = jnp.max(s, axis=-1, keepdims=True)
    p = jnp.where(valid, jnp.exp(s - m), 0.0)
    return p / jnp.maximum(jnp.sum(p, axis=-1, keepdims=True), 1e-30)


def _ada_kernel(c_ref, w_ref, b_ref, o_ref):
    o_ref[...] = _bdot(_silu(c_ref[...]), w_ref[...]) + b_ref[...]


def _ada_mod(c_all, w_ada, b_ada):
    depth, d, n6 = w_ada.shape
    rows = c_all.shape[0]
    tn = _pick(n6, (1024, 512, 256, 128))
    return pl.pallas_call(
        _ada_kernel,
        grid=(depth, n6 // tn),
        in_specs=[pl.BlockSpec((rows, d), lambda l, j: (0, 0)),
                  pl.BlockSpec((None, d, tn), lambda l, j: (l, 0, j)),
                  pl.BlockSpec((None, 1, tn), lambda l, j: (l, 0, j))],
        out_specs=pl.BlockSpec((None, rows, tn), lambda l, j: (l, 0, j)),
        out_shape=jax.ShapeDtypeStruct((depth, rows, n6), F32),
        compiler_params=_params(("arbitrary", "arbitrary")),
        name="ada_mod",
    )(c_all, w_ada, b_ada.reshape(depth, 1, n6))


def _mm1_kernel(x_ref, g_ref, sc_ref, sh_ref, *refs, n_main, n_w, rows_first, n_rows_out, nch):
    w_refs = refs[:n_w]
    o_ref = refs[n_w]
    row_refs = refs[n_w + 1:n_w + 1 + n_rows_out]
    hn_ref = refs[-1]
    j = pl.program_id(1)

    @pl.when(j == 0)
    def _():
        hn_ref[...] = _modnorm(x_ref[...], g_ref[...], sc_ref[...], sh_ref[...]).astype(BF16)

    def emit(wt_ref):
        y = lax.dot_general(hn_ref[...], wt_ref[...].astype(BF16), (((1,), (1,)), ((), ())),
                            preferred_element_type=F32)
        o_ref[...] = y
        return y

    if n_w == 1:
        y = emit(w_refs[0])
        tm, tn = o_ref.shape
        hd = row_refs[0].shape[-1] if n_rows_out else LANE
        per_tile = tn // hd
        tiles_per_out = nch // per_tile if n_rows_out else 1
        for r in range(n_rows_out):
            for part in range(tiles_per_out):
                @pl.when(j == rows_first + r * tiles_per_out + part)
                def _(r=r, part=part):
                    for cc in range(per_tile):
                        row_refs[r][pl.ds(part * per_tile + cc, tm, stride=nch), :] = y[:, cc * hd:(cc + 1) * hd]
    else:
        @pl.when(j < n_main)
        def _():
            emit(w_refs[0])

        @pl.when(j >= n_main)
        def _():
            emit(w_refs[1])


def _mod_spec(mod, tm, rows_per_group, width, col):
    r = mod.shape[1]
    tiles = rows_per_group // tm
    if col:
        return pl.BlockSpec((None, r, width), lambda i, j: (i // tiles, 0, j))
    return pl.BlockSpec((None, r, width), lambda i, j: (i // tiles, 0, 0))


def _norm_mod_matmul(x, g, scale, shift, w, layer, main_cols=None, w_tail=None, rows_out=None):
    m, d = x.shape
    tn = 512
    groups = scale.shape[0]
    rpg = m // groups
    if w_tail is None:
        n = w.shape[1]
        n_main = pl.cdiv(n, tn)
        weights = [w]
        w_specs = [pl.BlockSpec((None, tn, d), lambda i, j: (layer, j, 0))]
    else:
        assert main_cols % tn == 0
        n = main_cols + w_tail.shape[0]
        n_main = main_cols // tn
        weights = [w, w_tail]
        w_specs = [pl.BlockSpec((None, tn, d), lambda i, j: (layer, jnp.minimum(j, n_main - 1), 0)),
                   pl.BlockSpec((tn, d), lambda i, j: (jnp.maximum(j - n_main, 0), 0))]
    n_rows_out, rows_first, nch, hd = 0, 0, 1, LANE
    out_shape = [jax.ShapeDtypeStruct((m, n), F32)]
    if rows_out is not None:
        first_col, n_rows_out, nch, hd = rows_out
        assert first_col % tn == 0 and (nch * hd) % tn == 0 and w_tail is None
        rows_first = first_col // tn
    tm = _pick(rpg, (1024, 512, 256, 128, 64, 32, 16, 8))
    out_specs = [pl.BlockSpec((tm, tn), lambda i, j: (i, j))]
    for _ in range(n_rows_out):
        out_specs.append(pl.BlockSpec((tm * nch, hd), lambda i, j: (i, 0), pipeline_mode=pl.Buffered(1)))
        out_shape.append(jax.ShapeDtypeStruct((m * nch, hd), F32))
    kern = functools.partial(_mm1_kernel, n_main=n_main, n_w=len(weights), rows_first=rows_first,
                             n_rows_out=n_rows_out, nch=nch)
    outs = pl.pallas_call(
        kern,
        grid=(m // tm, pl.cdiv(n, tn)),
        in_specs=[pl.BlockSpec((tm, d), lambda i, j: (i, 0), pipeline_mode=pl.Buffered(1)),
                  pl.BlockSpec((1, d), lambda i, j: (0, 0)),
                  _mod_spec(scale, tm, rpg, d, False),
                  _mod_spec(shift, tm, rpg, d, False)] + w_specs,
        out_specs=out_specs,
        out_shape=out_shape,
        scratch_shapes=[pltpu.VMEM((tm, d), BF16)],
        compiler_params=_params(("arbitrary", "arbitrary")),
        name="norm_mod_matmul",
    )(x, g.reshape(1, d), scale, shift, *weights)
    return outs if n_rows_out else outs[0]


def _mm2_kernel(a_ref, w_ref, x_ref, gate_ref, o_ref):
    y = jnp.dot(a_ref[...], w_ref[...].astype(BF16), preferred_element_type=F32)
    o_ref[...] = x_ref[...] + gate_ref[...] * y


def _matmul_gated_residual(a, w, layer, x, gate):
    m, k = a.shape
    d = w.shape[2]
    groups = gate.shape[0]
    rpg = m // groups
    tm = _pick(rpg, (1024, 512, 256, 128, 64, 32, 16))
    tn = _pick(d, (512, 256, 128))
    return pl.pallas_call(
        _mm2_kernel,
        grid=(m // tm, d // tn),
        in_specs=[pl.BlockSpec((tm, k), lambda i, j: (i, 0)),
                  pl.BlockSpec((None, k, tn), lambda i, j: (layer, 0, j)),
                  pl.BlockSpec((tm, tn), lambda i, j: (i, j)),
                  _mod_spec(gate, tm, rpg, tn, True)],
        out_specs=pl.BlockSpec((tm, tn), lambda i, j: (i, j)),
        out_shape=jax.ShapeDtypeStruct((m, d), F32),
        compiler_params=_params(("arbitrary", "arbitrary")),
        name="matmul_gated_residual",
    )(a, w, x, gate)


def _mlp_kernel(x_ref, g_ref, sc_ref, sh_ref, gate_ref, w1_ref, w2_ref, o_ref, hn_ref):
    j = pl.program_id(1)

    @pl.when(j == 0)
    def _():
        hn_ref[...] = _modnorm(x_ref[...], g_ref[...], sc_ref[...], sh_ref[...]).astype(BF16)
        o_ref[...] = jnp.zeros_like(o_ref)

    h = jnp.dot(hn_ref[...], w1_ref[...].astype(BF16), preferred_element_type=F32)
    h = jnp.square(jnp.maximum(h, 0.0))
    o_ref[...] += jnp.dot(h.astype(BF16), w2_ref[...].astype(BF16), preferred_element_type=F32)

    @pl.when(j == pl.num_programs(1) - 1)
    def _():
        o_ref[...] = x_ref[...] + gate_ref[...] * o_ref[...]


def _mlp(x, g, scale, shift, gate, w1, w2, layer):
    m, d = x.shape
    f = w1.shape[2]
    groups = scale.shape[0]
    rpg = m // groups
    tm = _pick(rpg, (1024, 512, 256, 128, 64, 32, 16, 8))
    tf = _pick(f, (512, 256, 128))
    return pl.pallas_call(
        _mlp_kernel,
        grid=(m // tm, f // tf),
        in_specs=[pl.BlockSpec((tm, d), lambda i, j: (i, 0), pipeline_mode=pl.Buffered(1)),
                  pl.BlockSpec((1, d), lambda i, j: (0, 0)),
                  _mod_spec(scale, tm, rpg, d, False),
                  _mod_spec(shift, tm, rpg, d, False),
                  _mod_spec(gate, tm, rpg, d, False),
                  pl.BlockSpec((None, d, tf), lambda i, j: (layer, 0, j)),
                  pl.BlockSpec((None, tf, d), lambda i, j: (layer, j, 0))],
        out_specs=pl.BlockSpec((tm, d), lambda i, j: (i, 0), pipeline_mode=pl.Buffered(1)),
        out_shape=jax.ShapeDtypeStruct((m, d), F32),
        scratch_shapes=[pltpu.VMEM((tm, d), BF16)],
        compiler_params=_params(("arbitrary", "arbitrary")),
        name="mlp",
    )(x, g.reshape(1, d), scale, shift, gate, w1, w2)


def _final_kernel(x_ref, g_ref, o_ref):
    x = x_ref[...]
    o_ref[...] = (x * lax.rsqrt(jnp.mean(x * x, axis=-1, keepdims=True) + EPS)) * g_ref[...]


def _final_norm(x, g):
    m, d = x.shape
    tm = _pick(m, (512, 256, 128, 64, 32, 16, 8))
    return pl.pallas_call(
        _final_kernel,
        grid=(m // tm,),
        in_specs=[pl.BlockSpec((tm, d), lambda i: (i, 0)), pl.BlockSpec((1, d), lambda i: (0, 0))],
        out_specs=pl.BlockSpec((tm, d), lambda i: (i, 0)),
        out_shape=jax.ShapeDtypeStruct((m, d), F32),
        compiler_params=_params(("arbitrary",)),
        name="final_norm",
    )(x, g.reshape(1, d))


def _split3(x):
    hi = x.astype(BF16)
    r = x - hi.astype(F32)
    mid = r.astype(BF16)
    lo = (r - mid.astype(F32)).astype(BF16)
    return hi, mid, lo


def _gdn_kernel(qkv_ref, z_ref, ab_ref, gb_ref, gcg_ref, hx_ref, cst_ref, scst_ref, cw_ref, scw_ref,
                alog_ref, dtb_ref, ng_ref, s0_ref,
                mix_ref, sout_ref, ulast_ref,
                xbuf, ubuf, s_ref, *, t_valid, n_kh, n_vh, dk, dv):
    c = GDN_CHUNK
    n = pl.program_id(1)
    nqk = n_kh * dk
    nv = n_vh * dv
    rep = n_vh // n_kh
    hpg = 4
    gw = hpg * c
    n_conv = cw_ref.shape[0]
    n_sc = scw_ref.shape[0]

    @pl.when(n == 0)
    def _():
        xbuf[0:SUBLANE, :] = cst_ref[...]
        ubuf[0:SUBLANE, :] = scst_ref[...]
        s_ref[...] = s0_ref[...]

    xbuf[SUBLANE:SUBLANE + c, :] = qkv_ref[...]
    cw = cw_ref[...]
    off = SUBLANE - (n_conv - 1)
    xc = cw[0:1, :] * xbuf[off:off + c, :]
    for j in range(1, n_conv):
        xc = xc + cw[j:j + 1, :] * xbuf[off + j:off + j + c, :]
    xbuf[0:SUBLANE, :] = xbuf[c:c + SUBLANE, :]
    xc = _silu(xc)

    u = gcg_ref[...] * hx_ref[...]
    ubuf[SUBLANE:SUBLANE + c, :] = u
    scw = scw_ref[...]
    offs = SUBLANE - (n_sc - 1)
    cu = scw[0:1, :] * ubuf[offs:offs + c, :]
    for j in range(1, n_sc):
        cu = cu + scw[j:j + 1, :] * ubuf[offs + j:offs + j + c, :]
    ubuf[0:SUBLANE, :] = ubuf[c:c + SUBLANE, :]
    ulast_ref[...] = u
    mix_ref[:, nv:] = (gb_ref[...] * cu).astype(mix_ref.dtype)

    ab = ab_ref[...]
    g_all = -jnp.exp(alog_ref[...]) * jax.nn.softplus(ab + dtb_ref[...])
    beta_all = jax.nn.sigmoid(ab)
    if t_valid < c:
        rowmask = lax.broadcasted_iota(jnp.int32, (c, 1), 0) < t_valid
        xc = jnp.where(rowmask, xc, 0.0)
        g_all = jnp.where(rowmask, g_all, 0.0)
        beta_all = jnp.where(rowmask, beta_all, 0.0)

    ri = lax.broadcasted_iota(jnp.int32, (c, c), 0)
    ci = lax.broadcasted_iota(jnp.int32, (c, c), 1)
    tril = jnp.where(ri >= ci, 1.0, 0.0).astype(BF16)
    ghi, gmid, glo = _split3(g_all)
    gcum = (jnp.dot(tril, ghi, preferred_element_type=F32)
            + jnp.dot(tril, gmid, preferred_element_type=F32)
            + jnp.dot(tril, glo, preferred_element_type=F32))
    glast = gcum[c - 1:c, :]

    def l2n(x):
        return x * lax.rsqrt(jnp.sum(x * x, axis=-1, keepdims=True) + EPS)

    gi = lax.broadcasted_iota(jnp.int32, (gw, gw), 0)
    gj = lax.broadcasted_iota(jnp.int32, (gw, gw), 1)
    same = (gi // c) == (gj // c)
    low_incl = same & (gi >= gj)
    low_strict = same & (gi > gj)
    lane_blk = lax.broadcasted_iota(jnp.int32, (dk, gw), 1) // c
    level_mask = []
    size = 1
    while size < c:
        level_mask.append(((gi // (2 * size)) == (gj // (2 * size))) & ((gi // size) != (gj // size)) & (gi > gj))
        size *= 2

    def stack(cols):
        return jnp.concatenate(cols, axis=0)

    n_grp = n_vh // hpg
    pre = []
    for grp in range(n_grp):
        heads = [grp * hpg + j for j in range(hpg)]
        kheads = [h // rep for h in heads]
        qs = {kh: l2n(xc[:, kh * dk:(kh + 1) * dk]) * (dk ** -0.5) for kh in set(kheads)}
        ks = {kh: l2n(xc[:, nqk + kh * dk:nqk + (kh + 1) * dk]) for kh in set(kheads)}
        q_st = stack([qs[kh] for kh in kheads])
        k_st = stack([ks[kh] for kh in kheads])
        v_st = stack([xc[:, 2 * nqk + h * dv:2 * nqk + (h + 1) * dv] for h in heads])
        beta_st = stack([beta_all[:, n_vh + h:n_vh + h + 1] for h in heads])
        gc_st = stack([gcum[:, h:h + 1] for h in heads])
        gl_st = stack([jnp.broadcast_to(glast[:, h:h + 1], (c, 1)) for h in heads])
        gc_row = jnp.broadcast_to(gc_st, (gw, LANE)).T[0:1, :]
        diff = gc_st - gc_row
        decay = jnp.where(low_incl, jnp.exp(jnp.where(low_incl, diff, 0.0)), 0.0)
        egc = jnp.exp(gc_st)
        kb_st = k_st * beta_st

        nmat = _bdot_nt(kb_st, k_st) * jnp.where(low_strict, decay, 0.0)
        qk = _bdot_nt(q_st, k_st) * decay
        rhs = jnp.concatenate([v_st * beta_st, kb_st * egc], axis=1)
        pre.append((nmat, qk, rhs, q_st * egc, k_st * jnp.exp(gl_st - gc_st)))

    tms = [-jnp.where(level_mask[0], p[0], 0.0) for p in pre]
    for lm in level_mask[1:]:
        lls = [jnp.where(lm, p[0], 0.0) for p in pre]
        ys = [ll + _bdot(tm, ll) for tm, ll in zip(tms, lls)]
        tms = [tm - y - _bdot(y, tm) for tm, y in zip(tms, ys)]

    for grp in range(n_grp):
        heads = [grp * hpg + j for j in range(hpg)]
        nmat, qk, rhs, qg_st, kd_st = pre[grp]
        sol = rhs + _bdot(tms[grp], rhs)
        u_st = sol[:, :dv]
        w_st = sol[:, dv:]

        vnew, qs_out = [], []
        for j, h in enumerate(heads):
            wq = jnp.concatenate([w_st[j * c:(j + 1) * c], qg_st[j * c:(j + 1) * c]], axis=0)
            r2 = _bdot(wq, s_ref[h])
            vnew.append(u_st[j * c:(j + 1) * c] - r2[:c])
            qs_out.append(r2[c:])
        vnew_st = stack(vnew)
        o_st = stack(qs_out) + _bdot(qk, vnew_st)
        kd_t = kd_st.T
        for j, h in enumerate(heads):
            upd = _bdot(jnp.where(lane_blk == j, kd_t, 0.0), vnew_st)
            s_ref[h] = s_ref[h] * jnp.exp(glast[:, h:h + 1]) + upd
            o_h = o_st[j * c:(j + 1) * c]
            on = (o_h * lax.rsqrt(jnp.mean(o_h * o_h, axis=-1, keepdims=True) + EPS)) * ng_ref[...]
            zz = z_ref[:, h * dv:(h + 1) * dv]
            mix_ref[:, h * dv:(h + 1) * dv] = (on * _silu(zz)).astype(mix_ref.dtype)

    sout_ref[...] = s_ref[...]


def _gdn_sconv(proj, conv_state, sc_state, s0, conv_w, sc_w, a_log, dt_bias, norm_g, t_valid):
    bsz, n_vh, dk, dv = s0.shape
    n_conv, conv_dim = conv_w.shape
    n_sc, sc = sc_w.shape
    nv = n_vh * dv
    n_kh = (conv_dim - nv) // (2 * dk)
    c = GDN_CHUNK
    rows = proj.shape[0]
    nchunk = rows // (bsz * c)
    assert conv_dim % nv == 0 and (conv_dim + nv) % sc == 0 and n_vh % 4 == 0 and 2 * n_vh <= LANE
    zb = conv_dim // nv
    gbb = (conv_dim + nv) // sc
    abb = (conv_dim + nv + 3 * sc) // LANE
    pad_lane = lambda v: jnp.pad(v.reshape(1, -1), ((0, 0), (0, LANE - v.shape[-1])))
    cst = jnp.pad(conv_state.astype(F32), ((0, 0), (SUBLANE - (n_conv - 1), 0), (0, 0)))
    scst = jnp.pad(sc_state.astype(F32), ((0, 0), (SUBLANE - (n_sc - 1), 0), (0, 0)))
    row = lambda b, n: b * nchunk + n
    kern = functools.partial(_gdn_kernel, t_valid=t_valid, n_kh=n_kh, n_vh=n_vh, dk=dk, dv=dv)
    return pl.pallas_call(
        kern,
        grid=(bsz, nchunk),
        in_specs=[pl.BlockSpec((c, conv_dim), lambda b, n: (row(b, n), 0)),
                  pl.BlockSpec((c, nv), lambda b, n: (row(b, n), zb)),
                  pl.BlockSpec((c, LANE), lambda b, n: (row(b, n), abb)),
                  pl.BlockSpec((c, sc), lambda b, n: (row(b, n), gbb)),
                  pl.BlockSpec((c, sc), lambda b, n: (row(b, n), gbb + 1)),
                  pl.BlockSpec((c, sc), lambda b, n: (row(b, n), gbb + 2)),
                  pl.BlockSpec((None, SUBLANE, conv_dim), lambda b, n: (b, 0, 0)),
                  pl.BlockSpec((None, SUBLANE, sc), lambda b, n: (b, 0, 0)),
                  pl.BlockSpec((n_conv, conv_dim), lambda b, n: (0, 0)),
                  pl.BlockSpec((n_sc, sc), lambda b, n: (0, 0)),
                  pl.BlockSpec((1, LANE), lambda b, n: (0, 0)),
                  pl.BlockSpec((1, LANE), lambda b, n: (0, 0)),
                  pl.BlockSpec((1, dv), lambda b, n: (0, 0)),
                  pl.BlockSpec((None, n_vh, dk, dv), lambda b, n: (b, 0, 0, 0))],
        out_specs=[pl.BlockSpec((c, nv + sc), lambda b, n: (row(b, n), 0)),
                   pl.BlockSpec((None, n_vh, dk, dv), lambda b, n: (b, 0, 0, 0)),
                   pl.BlockSpec((None, c, sc), lambda b, n: (b, 0, 0))],
        out_shape=[jax.ShapeDtypeStruct((rows, nv + sc), BF16),
                   jax.ShapeDtypeStruct((bsz, n_vh, dk, dv), F32),
                   jax.ShapeDtypeStruct((bsz, c, sc), F32)],
        scratch_shapes=[pltpu.VMEM((c + SUBLANE, conv_dim), F32),
                        pltpu.VMEM((c + SUBLANE, sc), F32),
                        pltpu.VMEM((n_vh, dk, dv), F32)],
        compiler_params=_params(("arbitrary", "arbitrary")),
        name="gdn_sconv",
    )(proj, proj, proj, proj, proj, proj, cst, scst, conv_w, sc_w,
      pad_lane(a_log), pad_lane(dt_bias), norm_g.reshape(1, dv), s0.astype(F32))


def _cmp_prompt_kernel(kv_ref, pw_ref, pj_ref, o_ref, *, blk):
    t, hd = kv_ref.shape
    x = kv_ref[...].reshape(t // blk, blk, hd) * pw_ref[...][None]
    o_ref[...] = _bdot(jnp.sum(x, axis=1), pj_ref[...])


def _cmp_prompt(proj, bsz, t, pw, pj, col0):
    nch, blk, hd = pw.shape
    return pl.pallas_call(
        functools.partial(_cmp_prompt_kernel, blk=blk),
        grid=(bsz, nch),
        in_specs=[pl.BlockSpec((t, hd), lambda b, ch: (b, col0 + ch)),
                  pl.BlockSpec((None, blk, hd), lambda b, ch: (ch, 0, 0)),
                  pl.BlockSpec((None, hd, hd), lambda b, ch: (ch, 0, 0))],
        out_specs=pl.BlockSpec((None, None, t // blk, hd), lambda b, ch: (b, ch, 0, 0)),
        out_shape=jax.ShapeDtypeStruct((bsz, nch, t // blk, hd), F32),
        compiler_params=_params(("arbitrary", "arbitrary")),
        name="cmp_prompt",
    )(proj, pw, pj)


_NT = (((1,), (1,)), ((), ()))


def _slope_features(sl2, lane):
    hi = sl2.astype(BF16).astype(F32)
    lo = sl2 - hi
    return jnp.where(lane == 0, 64.0 * hi, jnp.where(lane == 1, hi, jnp.where(lane == 2, 64.0 * lo,
                                                                             jnp.where(lane == 3, lo, 0.0))))


def _attn_prompt_kernel(q_ref, kc_ref, vc_ref, ks_ref, vs_ref, kw_ref, vw_ref, gt_ref, sl_ref, pos_ref, oh_ref,
                        o_ref, *, tq, tk, wk, blk, n_g, n_kv, hd):
    h = pl.program_id(1)
    qt = pl.program_id(2)
    q0 = qt * tq
    nc = kc_ref.shape[0]
    t_all = ks_ref.shape[0]
    sl = sl_ref[...]
    lane = lax.broadcasted_iota(jnp.int32, (1, LANE), 1)
    qs = [q_ref[:, g * hd:(g + 1) * hd] * (hd ** -0.5) for g in range(n_g)]
    qb = jnp.concatenate(qs, axis=0).astype(BF16)
    qaug = jnp.concatenate(
        [jnp.concatenate([(qs[g] * LOG2E).astype(BF16),
                          jnp.broadcast_to(_slope_features(sl[:, g:g + 1] * LOG2E, lane), (tq, LANE)).astype(BF16)],
                         axis=1) for g in range(n_g)], axis=0)
    qpos = q0 + lax.broadcasted_iota(jnp.int32, (tq, 1), 0)

    bidx = lax.broadcasted_iota(jnp.int32, (1, nc), 1)
    start = bidx * blk
    s_c = _bdot_nt(qb, kc_ref[...])
    dist_c = jnp.abs(qpos.astype(F32) - (start.astype(F32) + (blk - 1) / 2))
    valid_c = (start + (blk - 1)) <= qpos
    p_c = jnp.concatenate([_masked_softmax(s_c[g * tq:(g + 1) * tq] - sl[:, g:g + 1] * dist_c, valid_c)
                           for g in range(n_g)], axis=0)
    o_c = _bdot(p_c, vc_ref[...])
    imp = p_c[0:tq]
    for g in range(1, n_g):
        imp = imp + p_c[g * tq:(g + 1) * tq]

    imp_t = jnp.concatenate([imp, jnp.zeros((tq, LANE - nc), F32)], axis=1).T[0:nc, :]
    bcol = lax.broadcasted_iota(jnp.int32, (nc, 1), 0)
    cur = (q0 + lax.broadcasted_iota(jnp.int32, (1, tq), 1)) // blk
    score = jnp.where(bcol < cur, imp_t, NEG)
    score = jnp.where((bcol == 0) | (bcol == cur), FORCE, score)
    rank = jnp.zeros((nc, tq), F32)
    for i in range(nc):
        ri = score[i:i + 1, :]
        rank = rank + jnp.where((ri > score) | ((ri == score) & (bcol > i)), 1.0, 0.0)
    selneg_t = jnp.where((rank < min(SEL_TOPK, nc)) & (score > 0.5 * NEG), 0.0, NEG)
    selneg = jnp.concatenate([selneg_t, jnp.zeros((LANE - nc, tq), F32)], axis=0).T.astype(BF16)

    def flash_step(k_ref, v_ref, k0, size, bias, carry):
        m, l, acc = carry
        kaug = jnp.concatenate([k_ref[pl.ds(k0, size), :].astype(BF16), pos_ref[pl.ds(k0, size), :]], axis=1)
        s = lax.dot_general(qaug, kaug, _NT, preferred_element_type=F32)
        gs = range(n_g)
        sgs = [s[g * tq:(g + 1) * tq] + bias for g in gs]
        m2 = [jnp.maximum(m[g], jnp.max(sgs[g], axis=-1, keepdims=True)) for g in gs]
        als = [jnp.exp2(m[g] - m2[g]) for g in gs]
        pfs = [jnp.exp2(sgs[g] - m2[g]) for g in gs]
        l2 = [als[g] * l[g] + jnp.sum(pfs[g], axis=-1, keepdims=True) for g in gs]
        ps = [p.astype(BF16) for p in pfs]
        accs = [als[g] * acc[g * tq:(g + 1) * tq] for g in gs]
        pv = jnp.dot(jnp.concatenate(ps, axis=0), v_ref[pl.ds(k0, size), :].astype(BF16),
                     preferred_element_type=F32)
        return tuple(m2), tuple(l2), jnp.concatenate(accs, axis=0) + pv

    def finish(carry):
        _, l, acc = carry
        return [acc[g * tq:(g + 1) * tq] / jnp.maximum(l[g], 1e-30) for g in range(n_g)]

    init = (tuple(jnp.full((tq, 1), NEG, F32) for _ in range(n_g)),
            tuple(jnp.zeros((tq, 1), F32) for _ in range(n_g)), jnp.zeros((n_g * tq, hd), F32))

    def sel_bias(k0):
        return lax.dot_general(selneg, oh_ref[pl.ds(k0, tk), :], _NT, preferred_element_type=F32)

    def sel_body(ci, carry):
        k0 = pl.multiple_of(ci * tk, tk)
        return flash_step(ks_ref, vs_ref, k0, tk, sel_bias(k0), carry)

    n_full = q0 // tk
    carry = lax.fori_loop(0, n_full, sel_body, init)
    k0 = pl.multiple_of(n_full * tk, tk)
    kpos = k0 + lax.broadcasted_iota(jnp.int32, (1, tk), 1)
    o_s = finish(flash_step(ks_ref, vs_ref, k0, tk, sel_bias(k0) + jnp.where(kpos <= qpos, 0.0, NEG), carry))

    w0 = pl.multiple_of(jnp.clip(q0 + tq - wk, 0, t_all - wk), tq)
    dist = qpos - (w0 + lax.broadcasted_iota(jnp.int32, (1, wk), 1))
    o_w = finish(flash_step(kw_ref, vw_ref, w0, wk, jnp.where((dist >= 0) & (dist < WINDOW), 0.0, NEG), init))

    gt = jax.nn.sigmoid(pltpu.roll(gt_ref[...], (LANE - n_g * h) % LANE, 1))
    n_h = n_g * n_kv
    for g in range(n_g):
        r = slice(g * tq, (g + 1) * tq)
        o = (gt[:, g:g + 1] * o_c[r] + gt[:, n_h + g:n_h + g + 1] * o_s[g]
             + gt[:, 2 * n_h + g:2 * n_h + g + 1] * o_w[g])
        o_ref[:, g * hd:(g + 1) * hd] = o.astype(o_ref.dtype)


def _attn_prompt(proj, cmp, slopes, bsz, t, n_h, n_kv, hd, blk):
    n_g = n_h // n_kv
    tq = 128
    tk = _pick(t, (512, 256, 128))
    wk = WINDOW + tq
    assert t % tq == 0 and t >= wk and t // blk <= LANE and blk == 64 and n_h * 3 <= LANE
    nqt = t // tq
    nc = t // blk
    kvb = n_h
    pos = jnp.arange(t, dtype=jnp.int32)[:, None]
    lane = jnp.arange(LANE, dtype=jnp.int32)[None, :]
    pos_tab = jnp.where(lane < 4, jnp.where(lane % 2 == 0, pos // 64, pos % 64), 0).astype(BF16)
    onehot = (lane == pos // blk).astype(BF16)
    gate_blk = (n_h * hd + 6 * n_kv * hd) // LANE
    kern = functools.partial(_attn_prompt_kernel, tq=tq, tk=tk, wk=wk, blk=blk, n_g=n_g, n_kv=n_kv, hd=hd)
    kv_spec = lambda off: pl.BlockSpec((t, hd), lambda b, h, q: (b, kvb + off + h))
    return pl.pallas_call(
        kern,
        grid=(bsz, n_kv, nqt),
        in_specs=[pl.BlockSpec((tq, n_g * hd), lambda b, h, q: (b * nqt + q, h)),
                  pl.BlockSpec((None, None, nc, hd), lambda b, h, q: (b, h, 0, 0)),
                  pl.BlockSpec((None, None, nc, hd), lambda b, h, q: (b, n_kv + h, 0, 0)),
                  kv_spec(2 * n_kv), kv_spec(3 * n_kv), kv_spec(4 * n_kv), kv_spec(5 * n_kv),
                  pl.BlockSpec((tq, LANE), lambda b, h, q: (b * nqt + q, gate_blk)),
                  pl.BlockSpec((None, 1, LANE), lambda b, h, q: (h, 0, 0)),
                  pl.BlockSpec((t, LANE), lambda b, h, q: (0, 0)),
                  pl.BlockSpec((t, LANE), lambda b, h, q: (0, 0))],
        out_specs=pl.BlockSpec((tq, n_g * hd), lambda b, h, q: (b * nqt + q, h)),
        out_shape=jax.ShapeDtypeStruct((bsz * t, n_h * hd), BF16),
        compiler_params=_params(("arbitrary", "arbitrary", "arbitrary")),
        name="attn_prompt",
    )(proj, cmp, cmp, proj, proj, proj, proj, proj, slopes, pos_tab, onehot)


def _pool_pages_kernel(pt_ref, *refs, n_pp, rows_blk):
    pw = refs[n_pp][...]
    o_ref = refs[n_pp + 1]
    per_page = refs[0].shape[0] // rows_blk
    nch = o_ref.shape[0] // (n_pp * per_page)
    for k in range(n_pp):
        for hb in range(per_page):
            r = k * per_page + hb
            x = refs[k][hb * rows_blk:(hb + 1) * rows_blk, :] * pw
            o_ref[r * nch:(r + 1) * nch, :] = jnp.sum(x.reshape(rows_blk // nch, nch, x.shape[-1]), axis=0)


def _page_map(b, p, pt, *, layer, k, n_pp):
    return (layer, pt[b, p * n_pp + k], 0, 0)


def _pool_pages(cache, layer, page_table, pw_rows, nch):
    _, _, prow, hd = cache.shape
    bd, n_pages = page_table.shape
    rows_blk = pw_rows.shape[0]
    n_pp = _pick(n_pages, (8, 4, 2, 1))
    per_page = prow // rows_blk
    in_specs = [pl.BlockSpec((None, None, prow, hd),
                             functools.partial(_page_map, layer=layer, k=k, n_pp=n_pp)) for k in range(n_pp)]
    in_specs.append(pl.BlockSpec((rows_blk, hd), lambda b, p, pt: (0, 0)))
    return pl.pallas_call(
        functools.partial(_pool_pages_kernel, n_pp=n_pp, rows_blk=rows_blk),
        grid_spec=pltpu.PrefetchScalarGridSpec(
            num_scalar_prefetch=1, grid=(bd, n_pages // n_pp), in_specs=in_specs,
            out_specs=pl.BlockSpec((None, n_pp * per_page * nch, hd), lambda b, p, pt: (b, p, 0))),
        out_shape=jax.ShapeDtypeStruct((bd, n_pages * per_page * nch, hd), F32),
        compiler_params=_params(("arbitrary", "arbitrary")),
        name="pool_pages",
    )(page_table, *([cache] * n_pp), pw_rows)


def _attn_sample_a_kernel(q_ref, pooled_ref, pj_ref, kvw_ref, gt_ref, sl_ref, ocw_ref, selm_ref,
                          score_buf, *, past, n_q, n_kv, n_g, hd, blk, wb, nbp):
    nch = 2 * n_kv
    nc = pooled_ref.shape[0] // nch
    n_blocks = (past + n_q + blk - 1) // blk
    kvw_rows = kvw_ref.shape[0]
    half = n_kv * hd
    rep = lambda x: jnp.concatenate([x] * n_g, axis=0)
    sl = sl_ref[...]
    gt = jax.nn.sigmoid(gt_ref[...])
    qpos = past + lax.broadcasted_iota(jnp.int32, (n_q, 1), 0)
    qpos_r = rep(qpos)
    bidx_c = lax.broadcasted_iota(jnp.int32, (1, nc), 1)
    start = bidx_c * blk
    bidx = lax.broadcasted_iota(jnp.int32, (1, nbp), 1)
    cur = qpos // blk
    score_buf[...] = jnp.full(score_buf.shape, DEAD, F32)

    for h in range(n_kv):
        qb = jnp.concatenate([q_ref[:, (h * n_g + g) * hd:(h * n_g + g + 1) * hd] for g in range(n_g)], axis=0)
        qb = (qb * (hd ** -0.5)).astype(BF16)
        slope = jnp.concatenate(
            [jnp.broadcast_to(sl[:, h * n_g + g:h * n_g + g + 1], (n_q, 1)) for g in range(n_g)], axis=0)
        kc = _bdot(pooled_ref[pl.ds(h, nc, stride=nch), :], pj_ref[h])
        vc = _bdot(pooled_ref[pl.ds(n_kv + h, nc, stride=nch), :], pj_ref[n_kv + h])
        s = _bdot_nt(qb, kc)
        dist_c = qpos_r.astype(F32) - (start.astype(F32) + (blk - 1) / 2)
        s = s - slope * jnp.abs(dist_c)
        p_c = _masked_softmax(s, (start + (blk - 1)) <= qpos_r)
        o_c = _bdot(p_c, vc)
        imp = p_c[0:n_q]
        for g in range(1, n_g):
            imp = imp + p_c[g * n_q:(g + 1) * n_q]
        imp = jnp.concatenate([imp, jnp.zeros((n_q, nbp - nc), F32)], axis=1)
        score = jnp.where(bidx < cur, imp, NEG)
        score = jnp.where((bidx == 0) | (bidx == cur), FORCE, score)
        score = jnp.where(bidx < n_blocks, score, DEAD)
        score_buf[h * n_q:(h + 1) * n_q, :] = score

        kw = kvw_ref[:, h * hd:(h + 1) * hd]
        vw = kvw_ref[:, half + h * hd:half + (h + 1) * hd]
        kidx = lax.broadcasted_iota(jnp.int32, (1, kvw_rows), 1)
        dist = qpos_r - (past - wb + kidx)
        valid = (dist >= 0) & (dist < WINDOW) & (kidx < wb + n_q)
        sw = _bdot_nt(qb, kw) - slope * jnp.abs(dist).astype(F32)
        o_w = _bdot(_masked_softmax(sw, valid), vw)
        for g in range(n_g):
            hh = h * n_g + g
            r = slice(g * n_q, (g + 1) * n_q)
            ocw_ref[:, hh * hd:(hh + 1) * hd] = (gt[:, hh:hh + 1] * o_c[r]
                                                 + gt[:, 2 * n_kv * n_g + hh:2 * n_kv * n_g + hh + 1] * o_w[r])

    sc_all = score_buf[...]
    sc_t = sc_all.T
    ii = lax.broadcasted_iota(jnp.int32, (nbp, 1), 0)
    for h in range(n_kv):
        for t in range(n_q):
            r = h * n_q + t
            col = sc_t[:, r:r + 1]
            row = sc_all[r:r + 1, :]
            before = (col > row) | ((col == row) & (ii < bidx))
            rank = jnp.sum(jnp.where(before, 1.0, 0.0), axis=0, keepdims=True)
            sel = jnp.where((rank < min(SEL_TOPK, n_blocks)) & (row > 0.5 * NEG), 1.0, 0.0)
            for g in range(n_g):
                selm_ref[h, g * n_q + t:g * n_q + t + 1, :] = sel


def _attn_sample_a(proj3, pooled, pj, kv_all, slopes_row, past, n_h, n_kv, hd, blk, wb):
    bd, n_q, _ = proj3.shape
    n_g = n_h // n_kv
    n_blocks = (past + n_q + blk - 1) // blk
    nbp = pl.cdiv(n_blocks, LANE) * LANE
    kvw_rows = kv_all.shape[1]
    qw = n_h * hd
    assert n_kv * n_q <= LANE
    kern = functools.partial(_attn_sample_a_kernel, past=past, n_q=n_q, n_kv=n_kv, n_g=n_g, hd=hd, blk=blk,
                             wb=wb, nbp=nbp)
    return pl.pallas_call(
        kern,
        grid=(bd,),
        in_specs=[pl.BlockSpec((None, n_q, qw), lambda b: (b, 0, 0)),
                  pl.BlockSpec((None, pooled.shape[1], hd), lambda b: (b, 0, 0)),
                  pl.BlockSpec((2 * n_kv, hd, hd), lambda b: (0, 0, 0)),
                  pl.BlockSpec((None, kvw_rows, 2 * n_kv * hd), lambda b: (b, 0, 0)),
                  pl.BlockSpec((None, n_q, LANE), lambda b: (b, 0, (qw + 6 * n_kv * hd) // LANE)),
                  pl.BlockSpec((1, LANE), lambda b: (0, 0))],
        out_specs=[pl.BlockSpec((None, n_q, qw), lambda b: (b, 0, 0)),
                   pl.BlockSpec((None, n_kv, n_g * n_q, nbp), lambda b: (b, 0, 0, 0))],
        out_shape=[jax.ShapeDtypeStruct((bd, n_q, qw), F32),
                   jax.ShapeDtypeStruct((bd, n_kv, n_g * n_q, nbp), F32)],
        scratch_shapes=[pltpu.VMEM((LANE, nbp), F32)],
        compiler_params=_params(("arbitrary",)),
        name="attn_sample_a",
    )(proj3, pooled, pj, kv_all, proj3, slopes_row)


def _attn_sample_sel_kernel(pt_ref, *refs, n_pp, past, n_q, n_kv, n_g, hd, blk):
    pages = refs[:n_pp]
    q_ref, selm_ref, new_ref, ocw_ref, gt_ref, sl_ref, o_ref, m_ref, l_ref, acc_ref = refs[n_pp:]
    p = pl.program_id(1)
    nch = 2 * n_kv
    page = pages[0].shape[0] // nch
    keys = n_pp * page
    nbp = selm_ref.shape[-1]
    rows = n_g * n_q
    half = n_kv * hd
    rep = lambda x: jnp.concatenate([x] * n_g, axis=0)
    sl = sl_ref[...]
    qpos_r = rep(past + lax.broadcasted_iota(jnp.int32, (n_q, 1), 0))

    @pl.when(p == 0)
    def _():
        m_ref[...] = jnp.full(m_ref.shape, NEG, F32)
        l_ref[...] = jnp.zeros_like(l_ref)
        acc_ref[...] = jnp.zeros_like(acc_ref)

    def heads():
        for h in range(n_kv):
            qb = jnp.concatenate([q_ref[:, (h * n_g + g) * hd:(h * n_g + g + 1) * hd] for g in range(n_g)],
                                 axis=0) * (hd ** -0.5)
            slope = jnp.concatenate(
                [jnp.broadcast_to(sl[:, h * n_g + g:h * n_g + g + 1], (n_q, 1)) for g in range(n_g)], axis=0)
            yield h, qb, slope

    def update(h, sc, mask, pv_fn):
        sc = jnp.where(mask, sc, NEG)
        m = m_ref[h]
        m_new = jnp.maximum(m, jnp.max(sc, axis=-1, keepdims=True))
        alpha = jnp.exp(m - m_new)
        pr = jnp.where(mask, jnp.exp(sc - m_new), 0.0)
        l_ref[h] = alpha * l_ref[h] + jnp.sum(pr, axis=-1, keepdims=True)
        acc_ref[h] = alpha * acc_ref[h] + pv_fn(pr)
        m_ref[h] = m_new

    k0 = p * keys
    kk = lax.broadcasted_iota(jnp.int32, (nbp, keys), 1)
    nn = lax.broadcasted_iota(jnp.int32, (nbp, keys), 0)
    expand = jnp.where(nn == (k0 + kk) // blk, 1.0, 0.0).astype(BF16)
    dist = qpos_r - (k0 + lax.broadcasted_iota(jnp.int32, (1, keys), 1))
    staged = []
    for h, qb, slope in heads():
        mask = (jnp.dot(selm_ref[h].astype(BF16), expand, preferred_element_type=F32) > 0.5) & (dist >= 0)
        kx = jnp.concatenate([pages[k][pl.ds(h, page, stride=nch), :].astype(BF16) for k in range(n_pp)], axis=0)
        sc = jnp.where(mask, _bdot_nt(qb, kx) - slope * dist.astype(F32), NEG)
        staged.append((h, mask, sc))
    probs = []
    for h, mask, sc in staged:
        m = m_ref[h]
        m_new = jnp.maximum(m, jnp.max(sc, axis=-1, keepdims=True))
        alpha = jnp.exp(m - m_new)
        pr = jnp.where(mask, jnp.exp(sc - m_new), 0.0)
        l_ref[h] = alpha * l_ref[h] + jnp.sum(pr, axis=-1, keepdims=True)
        m_ref[h] = m_new
        probs.append((h, alpha, pr))
    for h, alpha, pr in probs:
        vx = jnp.concatenate([pages[k][pl.ds(n_kv + h, page, stride=nch), :].astype(BF16) for k in range(n_pp)],
                             axis=0)
        acc_ref[h] = alpha * acc_ref[h] + _bdot(pr, vx)

    @pl.when(p == pl.num_programs(1) - 1)
    def _():
        gt = jax.nn.sigmoid(gt_ref[...])
        for h, qb, slope in heads():
            nb0 = past // blk
            seln = selm_ref[h][:, nb0:nb0 + 1] > 0.5
            qr = qb.astype(BF16).astype(F32)
            for j in range(n_q):
                kj = new_ref[j:j + 1, h * hd:(h + 1) * hd].astype(BF16).astype(F32)
                vj = new_ref[j:j + 1, half + h * hd:half + (h + 1) * hd].astype(BF16).astype(F32)
                dj = qpos_r - (past + j)
                sc = jnp.sum(qr * kj, axis=-1, keepdims=True) - slope * dj.astype(F32)
                update(h, sc, seln & (dj >= 0), lambda pr, vj=vj: pr.astype(BF16).astype(F32) * vj)
            o_s = acc_ref[h] / jnp.maximum(l_ref[h], 1e-30)
            for g in range(n_g):
                hh = h * n_g + g
                r = slice(g * n_q, (g + 1) * n_q)
                gcol = n_kv * n_g + hh
                o = ocw_ref[:, hh * hd:(hh + 1) * hd] + gt[:, gcol:gcol + 1] * o_s[r]
                o_ref[:, hh * hd:(hh + 1) * hd] = o.astype(o_ref.dtype)


def _attn_sample_sel(cache, layer, page_table, proj3, selm, new_rows, ocw, slopes_row, past, n_h, n_kv, hd, blk):
    _, _, prow, _ = cache.shape
    bd, n_pages = page_table.shape
    n_q = proj3.shape[1]
    n_g = n_h // n_kv
    qw = n_h * hd
    nbp = selm.shape[-1]
    n_pp = _pick(n_pages, (8, 4, 2, 1))
    rows = n_g * n_q
    in_specs = [pl.BlockSpec((None, None, prow, hd),
                             functools.partial(_page_map, layer=layer, k=k, n_pp=n_pp)) for k in range(n_pp)]
    in_specs += [pl.BlockSpec((None, n_q, qw), lambda b, p, pt: (b, 0, 0)),
                 pl.BlockSpec((None, n_kv, rows, nbp), lambda b, p, pt: (b, 0, 0, 0)),
                 pl.BlockSpec((None, new_rows.shape[1], new_rows.shape[2]), lambda b, p, pt: (b, 0, 0)),
                 pl.BlockSpec((None, n_q, qw), lambda b, p, pt: (b, 0, 0)),
                 pl.BlockSpec((None, n_q, LANE), lambda b, p, pt: (b, 0, (qw + 6 * n_kv * hd) // LANE)),
                 pl.BlockSpec((1, LANE), lambda b, p, pt: (0, 0))]
    kern = functools.partial(_attn_sample_sel_kernel, n_pp=n_pp, past=past, n_q=n_q, n_kv=n_kv, n_g=n_g,
                             hd=hd, blk=blk)
    return pl.pallas_call(
        kern,
        grid_spec=pltpu.PrefetchScalarGridSpec(
            num_scalar_prefetch=1, grid=(bd, n_pages // n_pp), in_specs=in_specs,
            out_specs=pl.BlockSpec((None, n_q, qw), lambda b, p, pt: (b, 0, 0)),
            scratch_shapes=[pltpu.VMEM((n_kv, rows, 1), F32), pltpu.VMEM((n_kv, rows, 1), F32),
                            pltpu.VMEM((n_kv, rows, hd), F32)]),
        out_shape=jax.ShapeDtypeStruct((bd, n_q, qw), BF16),
        compiler_params=_params(("arbitrary", "arbitrary")),
        name="attn_sample_sel",
    )(page_table, *([cache] * n_pp), proj3, selm, new_rows, ocw, proj3, slopes_row)


def _alibi(n_h):
    h = jnp.arange(1, n_h + 1, dtype=F32)
    return jnp.exp2(-8.0 * h / n_h)


def kernel(x_prompt, x_sample, c_prompt, c_sample, state_gdn, state_gdn_conv, state_sconv, cache_kv_cmp,
           cache_kv_sel, state_kv_win, page_table, norm_g, w_ada, b_ada, w_in_e, conv_w_gdn, a_log, dt_bias,
           gdn_norm_g, conv_w_sc, w_out_e, w_in_o, cmp_pool, cmp_proj, w_out_o, w_mlp1, w_mlp2, final_g):
    bsz, seq, d = x_prompt.shape
    bd, n_q, _ = x_sample.shape
    depth = norm_g.shape[0]
    n_vh, dk, dv = state_gdn.shape[2:]
    conv_dim = conv_w_gdn.shape[2]
    sc_dim = conv_w_sc.shape[2]
    nv = n_vh * dv
    blk, _, n_kv = cmp_pool.shape[1:]
    hd = cmp_proj.shape[-1]
    n_h = w_out_o.shape[1] // hd
    kv_row = 2 * n_kv * hd
    qw = n_h * hd
    page = cache_kv_cmp.shape[2]
    past = page_table.shape[1] * page
    wb = state_kv_win.shape[2]
    c = GDN_CHUNK
    assert seq % c == 0 and seq % blk == 0 and seq >= WINDOW and n_q <= c and n_q < blk and past % blk == 0
    assert n_q >= conv_w_gdn.shape[1] - 1 and hd == LANE and dk == LANE and dv == LANE

    rows_c = -(-(bsz + bd) // SUBLANE) * SUBLANE
    c_all = jnp.pad(jnp.concatenate([c_prompt, c_sample], axis=0).astype(F32), ((0, rows_c - bsz - bd), (0, 0)))
    mod = _ada_mod(c_all, w_ada, b_ada).reshape(depth, rows_c, 6, d)

    def mods(l, sample):
        if sample:
            return [jnp.repeat(mod[l, bsz:bsz + bd, j], n_q, axis=0)[None] for j in range(6)]
        return [mod[l, :bsz, j][:, None, :] for j in range(6)]

    slopes = _alibi(n_h)
    slopes_kv = jnp.pad(slopes.reshape(n_kv, 1, n_h // n_kv), ((0, 0), (0, 0), (0, LANE - n_h // n_kv)))
    slopes_row = jnp.pad(slopes.reshape(1, n_h), ((0, 0), (0, LANE - n_h)))
    nch = 2 * n_kv
    cache_cmp = cache_kv_cmp.reshape(cache_kv_cmp.shape[:2] + (page * nch, hd))
    cache_sel = cache_kv_sel.reshape(cache_kv_sel.shape[:2] + (page * nch, hd))

    w_in_e_t = jnp.swapaxes(w_in_e, 1, 2)
    w_in_o_t = jnp.swapaxes(w_in_o, 1, 2)

    def even_tail(i):
        o_ab = conv_dim + nv
        o_rest = o_ab + 2 * n_vh
        return jnp.concatenate([w_in_e_t[i, o_rest:], w_in_e_t[i, o_ab:o_rest],
                                jnp.zeros((LANE - 2 * n_vh, d), w_in_e.dtype)], axis=0)

    def run(x3, sample):
        nb, t, _ = x3.shape
        x = x3.reshape(nb * t, d).astype(F32)
        ev, od = [], []
        for l in range(depth):
            i = l // 2
            sh1, sc1, g1, sh2, sc2, g2 = mods(l, sample)
            if l % 2 == 0:
                proj = _norm_mod_matmul(x, norm_g[l, 0], sc1, sh1, w_in_e_t, i, main_cols=conv_dim + nv,
                                        w_tail=even_tail(i))
                n_cst = conv_w_gdn.shape[1] - 1
                conv_new = proj.reshape(nb, t, -1)[:, t - n_cst:, :conv_dim]
                if sample:
                    projp = jnp.pad(proj.reshape(nb, t, -1), ((0, 0), (0, c - t), (0, 0))).reshape(nb * c, -1)
                    cst, scst, s0, tv = state_gdn_conv[i], state_sconv[i], state_gdn[i], t
                else:
                    projp = proj
                    cst = jnp.zeros((nb, conv_w_gdn.shape[1] - 1, conv_dim), F32)
                    scst = jnp.zeros((nb, conv_w_sc.shape[1] - 1, sc_dim), F32)
                    s0, tv = jnp.zeros((nb, n_vh, dk, dv), F32), c
                mix, s_new, u_last = _gdn_sconv(projp, cst, scst, s0, conv_w_gdn[i], conv_w_sc[i], a_log[i],
                                                dt_bias[i], gdn_norm_g[i], tv)
                if sample:
                    mix = mix.reshape(nb, c, -1)[:, :t].reshape(nb * t, -1)
                sc_new = u_last[:, tv - scst.shape[1]:tv]
                ev.append((s_new, conv_new, sc_new))
                x = _matmul_gated_residual(mix, w_out_e, i, x, g1)
            else:
                proj, kvc8, kvs8, kvw8 = _norm_mod_matmul(x, norm_g[l, 0], sc1, sh1, w_in_o_t, i,
                                                          rows_out=(qw, 3, nch, hd))
                kv_c, kv_s, kv_w = [a.reshape(nb, t, 2, n_kv, hd) for a in (kvc8, kvs8, kvw8)]
                pj = cmp_proj[i].reshape(2 * n_kv, hd, hd)
                if sample:
                    pw_rows = jnp.broadcast_to(cmp_pool[i].reshape(blk * nch, 1), (blk * nch, hd))
                    pooled = _pool_pages(cache_cmp, i, page_table, pw_rows, nch)
                    proj3 = proj.reshape(nb, t, -1)
                    kv_all = jnp.concatenate([state_kv_win[i].reshape(nb, wb, kv_row).astype(F32),
                                              proj3[:, :, qw + 2 * kv_row:qw + 3 * kv_row]], axis=1)
                    kv_all_p = jnp.pad(kv_all, ((0, 0), (0, (-kv_all.shape[1]) % SUBLANE), (0, 0)))
                    ocw, selm = _attn_sample_a(proj3, pooled, pj, kv_all_p, slopes_row, past, n_h, n_kv, hd, blk, wb)
                    new_rows = jnp.pad(proj3[:, :, qw + kv_row:qw + 2 * kv_row], ((0, 0), (0, (-t) % SUBLANE), (0, 0)))
                    o = _attn_sample_sel(cache_sel, i, page_table, proj3, selm, new_rows, ocw, slopes_row, past,
                                         n_h, n_kv, hd, blk).reshape(nb * t, qw)
                    kv_win_new = kv_all[:, t:].reshape(nb, wb, 2, n_kv, hd)
                else:
                    pw = jnp.broadcast_to(cmp_pool[i].reshape(blk, 2 * n_kv).T[:, :, None], (2 * n_kv, blk, hd))
                    cmp = _cmp_prompt(proj, nb, t, pw, pj, n_h)
                    o = _attn_prompt(proj, cmp, slopes_kv, nb, t, n_h, n_kv, hd, blk)
                    kv_win_new = kv_w[:, t - min(WINDOW, t):]
                od.append((kv_c, kv_s, kv_win_new))
                x = _matmul_gated_residual(o, w_out_o, i, x, g1)
            x = _mlp(x, norm_g[l, 1], sc2, sh2, g2, w_mlp1, w_mlp2, l)
        y = _final_norm(x, final_g).reshape(nb, t, d)
        evs = [jnp.stack([s[j] for s in ev]) for j in range(3)]
        ods = [jnp.stack([s[j] for s in od]) for j in range(3)]
        return y, evs, ods

    y_p, (gdn_p, gconv_p, sconv_p), (kvc_p, kvs_p, kvw_p) = run(x_prompt, False)
    y_s, (gdn_s, gconv_s, sconv_s), (kvc_s, kvs_s, kvw_s) = run(x_sample, True)
    return (y_p, y_s, gdn_p, gdn_s, gconv_p, gconv_s, sconv_p, sconv_s, kvc_p, kvc_s, kvs_p, kvs_s, kvw_p, kvw_s)
```

```python
import functools

import jax
import jax.numpy as jnp
from jax import lax
from jax.experimental import pallas as pl
from jax.experimental.pallas import tpu as pltpu

F32 = jnp.float32
BF16 = jnp.bfloat16

EPS = 1e-6
NEG = -1e30
FORCE = 1e6
DEAD = -3e38
SEL_TOPK = 16
LOG2E = 1.4426950408889634
WINDOW = 512
GDN_CHUNK = 64
LANE = 128
SUBLANE = 8
VMEM_BUDGET = 56 * 1024 * 1024


def _params(sem):
    return pltpu.CompilerParams(dimension_semantics=sem, vmem_limit_bytes=VMEM_BUDGET)


def _pick(n, cands):
    for c in cands:
        if n % c == 0:
            return c
    return n


def _bdot(a, b):
    return jnp.dot(a.astype(BF16), b.astype(BF16), preferred_element_type=F32)


def _bdot_nt(a, b):
    return lax.dot_general(a.astype(BF16), b.astype(BF16), (((1,), (1,)), ((), ())),
                           preferred_element_type=F32)


def _silu(x):
    return x * jax.nn.sigmoid(x)


def _modnorm(x, g, scale, shift):
    y = x * lax.rsqrt(jnp.mean(x * x, axis=-1, keepdims=True) + EPS)
    return (y * g) * (1.0 + scale) + shift


def _modnorm_rows(x_ref, g_ref, sc_ref, sh_ref, hn_ref):
    tm = x_ref.shape[0]
    ch = min(tm, 256)

    def body(r, carry):
        rows = pl.ds(pl.multiple_of(r * ch, ch), ch)
        sc = sc_ref[...] if sc_ref.shape[0] == 1 else sc_ref[rows, :]
        sh = sh_ref[...] if sh_ref.shape[0] == 1 else sh_ref[rows, :]
        hn_ref[rows, :] = _modnorm(x_ref[rows, :], g_ref[...], sc, sh).astype(BF16)
        return carry

    lax.fori_loop(0, tm // ch, body, 0)


def _masked_softmax(s, valid):
    s = jnp.where(valid, s, NEG)
    m = jnp.max(s, axis=-1, keepdims=True)
    p = jnp.where(valid, jnp.exp(s - m), 0.0)
    return p / jnp.maximum(jnp.sum(p, axis=-1, keepdims=True), 1e-30)


def _ada_kernel(c_ref, w_ref, b_ref, o_ref):
    o_ref[...] = _bdot(_silu(c_ref[...]), w_ref[...]) + b_ref[...]


def _ada_mod(c_all, w_ada, b_ada):
    depth, d, n6 = w_ada.shape
    rows = c_all.shape[0]
    tn = _pick(n6, (1024, 512, 256, 128))
    return pl.pallas_call(
        _ada_kernel,
        grid=(depth, n6 // tn),
        in_specs=[pl.BlockSpec((rows, d), lambda l, j: (0, 0)),
                  pl.BlockSpec((None, d, tn), lambda l, j: (l, 0, j)),
                  pl.BlockSpec((None, 1, tn), lambda l, j: (l, 0, j))],
        out_specs=pl.BlockSpec((None, rows, tn), lambda l, j: (l, 0, j)),
        out_shape=jax.ShapeDtypeStruct((depth, rows, n6), F32),
        compiler_params=_params(("arbitrary", "arbitrary")),
        name="ada_mod",
    )(c_all, w_ada, b_ada.reshape(depth, 1, n6))


def _mm1_kernel(x_ref, g_ref, sc_ref, sh_ref, *refs, n_main, n_w, rows_first, n_rows_out, nch):
    w_refs = refs[:n_w]
    o_ref = refs[n_w]
    row_refs = refs[n_w + 1:n_w + 1 + n_rows_out]
    hn_ref = refs[-1]
    j = pl.program_id(1)

    @pl.when(j == 0)
    def _():
        _modnorm_rows(x_ref, g_ref, sc_ref, sh_ref, hn_ref)

    def emit(wt_ref):
        y = lax.dot_general(hn_ref[...], wt_ref[...].astype(BF16), (((1,), (1,)), ((), ())),
                            preferred_element_type=F32)
        o_ref[...] = y
        return y

    if n_w == 1:
        y = emit(w_refs[0])
        tm, tn = o_ref.shape
        hd = row_refs[0].shape[-1] if n_rows_out else LANE
        per_tile = tn // hd
        tiles_per_out = nch // per_tile if n_rows_out else 1
        for r in range(n_rows_out):
            for part in range(tiles_per_out):
                @pl.when(j == rows_first + r * tiles_per_out + part)
                def _(r=r, part=part):
                    for cc in range(per_tile):
                        row_refs[r][pl.ds(part * per_tile + cc, tm, stride=nch), :] = y[:, cc * hd:(cc + 1) * hd]
    else:
        @pl.when(j < n_main)
        def _():
            emit(w_refs[0])

        @pl.when(j >= n_main)
        def _():
            emit(w_refs[1])


def _mod_spec(mod, tm, rows_per_group, width, col):
    r = mod.shape[1]
    tiles = rows_per_group // tm
    if col:
        return pl.BlockSpec((None, r, width), lambda i, j: (i // tiles, 0, j))
    return pl.BlockSpec((None, r, width), lambda i, j: (i // tiles, 0, 0))


def _norm_mod_matmul(x, g, scale, shift, w, layer, main_cols=None, w_tail=None, rows_out=None):
    m, d = x.shape
    tn = 512
    groups = scale.shape[0]
    rpg = m // groups
    if w_tail is None:
        n = w.shape[1]
        n_main = pl.cdiv(n, tn)
        weights = [w]
        w_specs = [pl.BlockSpec((None, tn, d), lambda i, j: (layer, j, 0))]
    else:
        assert main_cols % tn == 0
        n = main_cols + w_tail.shape[0]
        n_main = main_cols // tn
        weights = [w, w_tail]
        w_specs = [pl.BlockSpec((None, tn, d), lambda i, j: (layer, jnp.minimum(j, n_main - 1), 0)),
                   pl.BlockSpec((tn, d), lambda i, j: (jnp.maximum(j - n_main, 0), 0))]
    n_rows_out, rows_first, nch, hd = 0, 0, 1, LANE
    out_shape = [jax.ShapeDtypeStruct((m, n), F32)]
    if rows_out is not None:
        first_col, n_rows_out, nch, hd = rows_out
        assert first_col % tn == 0 and (nch * hd) % tn == 0 and w_tail is None
        rows_first = first_col // tn
    big = n_rows_out == 0 and w.dtype == BF16
    tm = _pick(rpg, ((2048,) if big else ()) + (1024, 512, 256, 128, 64, 32, 16, 8))
    out_specs = [pl.BlockSpec((tm, tn), lambda i, j: (i, j))]
    for _ in range(n_rows_out):
        out_specs.append(pl.BlockSpec((tm * nch, hd), lambda i, j: (i, 0)))
        out_shape.append(jax.ShapeDtypeStruct((m * nch, hd), F32))
    kern = functools.partial(_mm1_kernel, n_main=n_main, n_w=len(weights), rows_first=rows_first,
                             n_rows_out=n_rows_out, nch=nch)
    outs = pl.pallas_call(
        kern,
        grid=(m // tm, pl.cdiv(n, tn)),
        in_specs=[pl.BlockSpec((tm, d), lambda i, j: (i, 0), pipeline_mode=pl.Buffered(1)),
                  pl.BlockSpec((1, d), lambda i, j: (0, 0)),
                  _mod_spec(scale, tm, rpg, d, False),
                  _mod_spec(shift, tm, rpg, d, False)] + w_specs,
        out_specs=out_specs,
        out_shape=out_shape,
        scratch_shapes=[pltpu.VMEM((tm, d), BF16)],
        compiler_params=_params(("arbitrary", "arbitrary")),
        name="norm_mod_matmul",
    )(x, g.reshape(1, d), scale, shift, *weights)
    return outs if n_rows_out else outs[0]


def _mm2_kernel(a_ref, w_ref, x_ref, gate_ref, o_ref):
    y = jnp.dot(a_ref[...], w_ref[...].astype(BF16), preferred_element_type=F32)
    o_ref[...] = x_ref[...] + gate_ref[...] * y


def _matmul_gated_residual(a, w, layer, x, gate):
    m, k = a.shape
    d = w.shape[2]
    groups = gate.shape[0]
    rpg = m // groups
    tm = _pick(rpg, (1024, 512, 256, 128, 64, 32, 16))
    tn = _pick(d, (512, 256, 128))
    return pl.pallas_call(
        _mm2_kernel,
        grid=(m // tm, d // tn),
        in_specs=[pl.BlockSpec((tm, k), lambda i, j: (i, 0)),
                  pl.BlockSpec((None, k, tn), lambda i, j: (layer, 0, j)),
                  pl.BlockSpec((tm, tn), lambda i, j: (i, j)),
                  _mod_spec(gate, tm, rpg, tn, True)],
        out_specs=pl.BlockSpec((tm, tn), lambda i, j: (i, j)),
        out_shape=jax.ShapeDtypeStruct((m, d), F32),
        compiler_params=_params(("arbitrary", "arbitrary")),
        name="matmul_gated_residual",
    )(a, w, x, gate)


def _mlp_kernel(x_ref, g_ref, sc_ref, sh_ref, gate_ref, w1_ref, w2_ref, fg_ref, o_ref, hn_ref, *, final):
    j = pl.program_id(1)

    @pl.when(j == 0)
    def _():
        _modnorm_rows(x_ref, g_ref, sc_ref, sh_ref, hn_ref)
        o_ref[...] = jnp.zeros_like(o_ref)

    h = jnp.dot(hn_ref[...], w1_ref[...].astype(BF16), preferred_element_type=F32)
    h = jnp.square(jnp.maximum(h, 0.0))
    o_ref[...] += jnp.dot(h.astype(BF16), w2_ref[...].astype(BF16), preferred_element_type=F32)

    @pl.when(j == pl.num_programs(1) - 1)
    def _():
        y = x_ref[...] + gate_ref[...] * o_ref[...]
        if final:
            y = (y * lax.rsqrt(jnp.mean(y * y, axis=-1, keepdims=True) + EPS)) * fg_ref[...]
        o_ref[...] = y


def _mlp(x, g, scale, shift, gate, w1, w2, layer, final_g=None):
    m, d = x.shape
    final = final_g is not None
    fg = (final_g if final else g).reshape(1, d)
    f = w1.shape[2]
    groups = scale.shape[0]
    rpg = m // groups
    tm = _pick(rpg, (1024, 512, 256, 128, 64, 32, 16, 8))
    tf = _pick(f, (512, 256, 128))
    return pl.pallas_call(
        functools.partial(_mlp_kernel, final=final),
        grid=(m // tm, f // tf),
        in_specs=[pl.BlockSpec((tm, d), lambda i, j: (i, 0)),
                  pl.BlockSpec((1, d), lambda i, j: (0, 0)),
                  _mod_spec(scale, tm, rpg, d, False),
                  _mod_spec(shift, tm, rpg, d, False),
                  _mod_spec(gate, tm, rpg, d, False),
                  pl.BlockSpec((None, d, tf), lambda i, j: (layer, 0, j)),
                  pl.BlockSpec((None, tf, d), lambda i, j: (layer, j, 0)),
                  pl.BlockSpec((1, d), lambda i, j: (0, 0))],
        out_specs=pl.BlockSpec((tm, d), lambda i, j: (i, 0), pipeline_mode=pl.Buffered(1)),
        out_shape=jax.ShapeDtypeStruct((m, d), F32),
        scratch_shapes=[pltpu.VMEM((tm, d), BF16)],
        compiler_params=_params(("arbitrary", "arbitrary")),
        name="mlp",
    )(x, g.reshape(1, d), scale, shift, gate, w1, w2, fg)


def _split3(x):
    hi = x.astype(BF16)
    r = x - hi.astype(F32)
    mid = r.astype(BF16)
    lo = (r - mid.astype(F32)).astype(BF16)
    return hi, mid, lo


def _gdn_kernel(qkv_ref, z_ref, ab_ref, gb_ref, gcg_ref, hx_ref, cst_ref, scst_ref, cw_ref, scw_ref,
                alog_ref, dtb_ref, ng_ref, s0_ref,
                mix_ref, sout_ref, ulast_ref,
                xbuf, ubuf, s_ref, *, t_valid, n_kh, n_vh, dk, dv):
    c = GDN_CHUNK
    n = pl.program_id(1)
    nqk = n_kh * dk
    nv = n_vh * dv
    rep = n_vh // n_kh
    hpg = 4
    gw = hpg * c
    n_conv = cw_ref.shape[0]
    n_sc = scw_ref.shape[0]

    @pl.when(n == 0)
    def _():
        xbuf[0:SUBLANE, :] = cst_ref[...]
        ubuf[0:SUBLANE, :] = scst_ref[...]
        s_ref[...] = s0_ref[...]

    xbuf[SUBLANE:SUBLANE + c, :] = qkv_ref[...]
    cw = cw_ref[...]
    off = SUBLANE - (n_conv - 1)
    xc = cw[0:1, :] * xbuf[off:off + c, :]
    for j in range(1, n_conv):
        xc = xc + cw[j:j + 1, :] * xbuf[off + j:off + j + c, :]
    xbuf[0:SUBLANE, :] = xbuf[c:c + SUBLANE, :]
    xc = _silu(xc)

    u = gcg_ref[...] * hx_ref[...]
    ubuf[SUBLANE:SUBLANE + c, :] = u
    scw = scw_ref[...]
    offs = SUBLANE - (n_sc - 1)
    cu = scw[0:1, :] * ubuf[offs:offs + c, :]
    for j in range(1, n_sc):
        cu = cu + scw[j:j + 1, :] * ubuf[offs + j:offs + j + c, :]
    ubuf[0:SUBLANE, :] = ubuf[c:c + SUBLANE, :]
    ulast_ref[...] = u
    mix_ref[:, nv:] = (gb_ref[...] * cu).astype(mix_ref.dtype)

    ab = ab_ref[...]
    g_all = -jnp.exp(alog_ref[...]) * jax.nn.softplus(ab + dtb_ref[...])
    beta_all = jax.nn.sigmoid(ab)
    if t_valid < c:
        rowmask = lax.broadcasted_iota(jnp.int32, (c, 1), 0) < t_valid
        xc = jnp.where(rowmask, xc, 0.0)
        g_all = jnp.where(rowmask, g_all, 0.0)
        beta_all = jnp.where(rowmask, beta_all, 0.0)

    ri = lax.broadcasted_iota(jnp.int32, (c, c), 0)
    ci = lax.broadcasted_iota(jnp.int32, (c, c), 1)
    tril = jnp.where(ri >= ci, 1.0, 0.0).astype(BF16)
    ghi, gmid, glo = _split3(g_all)
    gcum = (jnp.dot(tril, ghi, preferred_element_type=F32)
            + jnp.dot(tril, gmid, preferred_element_type=F32)
            + jnp.dot(tril, glo, preferred_element_type=F32))
    glast = gcum[c - 1:c, :]

    def l2n(x):
        return x * lax.rsqrt(jnp.sum(x * x, axis=-1, keepdims=True) + EPS)

    gi = lax.broadcasted_iota(jnp.int32, (gw, gw), 0)
    gj = lax.broadcasted_iota(jnp.int32, (gw, gw), 1)
    same = (gi // c) == (gj // c)
    low_incl = same & (gi >= gj)
    low_strict = same & (gi > gj)
    lane_blk = lax.broadcasted_iota(jnp.int32, (dk, gw), 1) // c
    level_mask = []
    size = 1
    while size < c:
        level_mask.append(((gi // (2 * size)) == (gj // (2 * size))) & ((gi // size) != (gj // size)) & (gi > gj))
        size *= 2

    def stack(cols):
        return jnp.concatenate(cols, axis=0)

    n_grp = n_vh // hpg
    pre = []
    for grp in range(n_grp):
        heads = [grp * hpg + j for j in range(hpg)]
        kheads = [h // rep for h in heads]
        qs = {kh: l2n(xc[:, kh * dk:(kh + 1) * dk]) * (dk ** -0.5) for kh in set(kheads)}
        ks = {kh: l2n(xc[:, nqk + kh * dk:nqk + (kh + 1) * dk]) for kh in set(kheads)}
        q_st = stack([qs[kh] for kh in kheads])
        k_st = stack([ks[kh] for kh in kheads])
        v_st = stack([xc[:, 2 * nqk + h * dv:2 * nqk + (h + 1) * dv] for h in heads])
        beta_st = stack([beta_all[:, n_vh + h:n_vh + h + 1] for h in heads])
        gc_st = stack([gcum[:, h:h + 1] for h in heads])
        gl_st = stack([jnp.broadcast_to(glast[:, h:h + 1], (c, 1)) for h in heads])
        gc_row = jnp.broadcast_to(gc_st, (gw, LANE)).T[0:1, :]
        diff = gc_st - gc_row
        decay = jnp.where(low_incl, jnp.exp(jnp.where(low_incl, diff, 0.0)), 0.0)
        egc = jnp.exp(gc_st)
        kb_st = k_st * beta_st

        nmat = _bdot_nt(kb_st, k_st) * jnp.where(low_strict, decay, 0.0)
        qk = _bdot_nt(q_st, k_st) * decay
        rhs = jnp.concatenate([v_st * beta_st, kb_st * egc], axis=1)
        pre.append((nmat, qk, rhs, q_st * egc, k_st * jnp.exp(gl_st - gc_st)))

    tms = [-jnp.where(level_mask[0], p[0], 0.0) for p in pre]
    for lm in level_mask[1:]:
        lls = [jnp.where(lm, p[0], 0.0) for p in pre]
        ys = [ll + _bdot(tm, ll) for tm, ll in zip(tms, lls)]
        tms = [tm - y - _bdot(y, tm) for tm, y in zip(tms, ys)]

    for grp in range(n_grp):
        heads = [grp * hpg + j for j in range(hpg)]
        nmat, qk, rhs, qg_st, kd_st = pre[grp]
        sol = rhs + _bdot(tms[grp], rhs)
        u_st = sol[:, :dv]
        w_st = sol[:, dv:]

        vnew, qs_out = [], []
        for j, h in enumerate(heads):
            wq = jnp.concatenate([w_st[j * c:(j + 1) * c], qg_st[j * c:(j + 1) * c]], axis=0)
            r2 = _bdot(wq, s_ref[h])
            vnew.append(u_st[j * c:(j + 1) * c] - r2[:c])
            qs_out.append(r2[c:])
        vnew_st = stack(vnew)
        o_st = stack(qs_out) + _bdot(qk, vnew_st)
        kd_t = kd_st.T
        for j, h in enumerate(heads):
            upd = _bdot(jnp.where(lane_blk == j, kd_t, 0.0), vnew_st)
            s_ref[h] = s_ref[h] * jnp.exp(glast[:, h:h + 1]) + upd
            o_h = o_st[j * c:(j + 1) * c]
            on = (o_h * lax.rsqrt(jnp.mean(o_h * o_h, axis=-1, keepdims=True) + EPS)) * ng_ref[...]
            zz = z_ref[:, h * dv:(h + 1) * dv]
            mix_ref[:, h * dv:(h + 1) * dv] = (on * _silu(zz)).astype(mix_ref.dtype)

    sout_ref[...] = s_ref[...]


def _gdn_sconv(proj, conv_state, sc_state, s0, conv_w, sc_w, a_log, dt_bias, norm_g, t_valid):
    bsz, n_vh, dk, dv = s0.shape
    n_conv, conv_dim = conv_w.shape
    n_sc, sc = sc_w.shape
    nv = n_vh * dv
    n_kh = (conv_dim - nv) // (2 * dk)
    c = GDN_CHUNK
    rows = proj.shape[0]
    nchunk = rows // (bsz * c)
    assert conv_dim % nv == 0 and (conv_dim + nv) % sc == 0 and n_vh % 4 == 0 and 2 * n_vh <= LANE
    zb = conv_dim // nv
    gbb = (conv_dim + nv) // sc
    abb = (conv_dim + nv + 3 * sc) // LANE
    pad_lane = lambda v: jnp.pad(v.reshape(1, -1), ((0, 0), (0, LANE - v.shape[-1])))
    cst = jnp.pad(conv_state.astype(F32), ((0, 0), (SUBLANE - (n_conv - 1), 0), (0, 0)))
    scst = jnp.pad(sc_state.astype(F32), ((0, 0), (SUBLANE - (n_sc - 1), 0), (0, 0)))
    row = lambda b, n: b * nchunk + n
    kern = functools.partial(_gdn_kernel, t_valid=t_valid, n_kh=n_kh, n_vh=n_vh, dk=dk, dv=dv)
    return pl.pallas_call(
        kern,
        grid=(bsz, nchunk),
        in_specs=[pl.BlockSpec((c, conv_dim), lambda b, n: (row(b, n), 0)),
                  pl.BlockSpec((c, nv), lambda b, n: (row(b, n), zb)),
                  pl.BlockSpec((c, LANE), lambda b, n: (row(b, n), abb)),
                  pl.BlockSpec((c, sc), lambda b, n: (row(b, n), gbb)),
                  pl.BlockSpec((c, sc), lambda b, n: (row(b, n), gbb + 1)),
                  pl.BlockSpec((c, sc), lambda b, n: (row(b, n), gbb + 2)),
                  pl.BlockSpec((None, SUBLANE, conv_dim), lambda b, n: (b, 0, 0)),
                  pl.BlockSpec((None, SUBLANE, sc), lambda b, n: (b, 0, 0)),
                  pl.BlockSpec((n_conv, conv_dim), lambda b, n: (0, 0)),
                  pl.BlockSpec((n_sc, sc), lambda b, n: (0, 0)),
                  pl.BlockSpec((1, LANE), lambda b, n: (0, 0)),
                  pl.BlockSpec((1, LANE), lambda b, n: (0, 0)),
                  pl.BlockSpec((1, dv), lambda b, n: (0, 0)),
                  pl.BlockSpec((None, n_vh, dk, dv), lambda b, n: (b, 0, 0, 0))],
        out_specs=[pl.BlockSpec((c, nv + sc), lambda b, n: (row(b, n), 0)),
                   pl.BlockSpec((None, n_vh, dk, dv), lambda b, n: (b, 0, 0, 0)),
                   pl.BlockSpec((None, c, sc), lambda b, n: (b, 0, 0))],
        out_shape=[jax.ShapeDtypeStruct((rows, nv + sc), BF16),
                   jax.ShapeDtypeStruct((bsz, n_vh, dk, dv), F32),
                   jax.ShapeDtypeStruct((bsz, c, sc), F32)],
        scratch_shapes=[pltpu.VMEM((c + SUBLANE, conv_dim), F32),
                        pltpu.VMEM((c + SUBLANE, sc), F32),
                        pltpu.VMEM((n_vh, dk, dv), F32)],
        compiler_params=_params(("arbitrary", "arbitrary")),
        name="gdn_sconv",
    )(proj, proj, proj, proj, proj, proj, cst, scst, conv_w, sc_w,
      pad_lane(a_log), pad_lane(dt_bias), norm_g.reshape(1, dv), s0.astype(F32))


def _cmp_prompt_kernel(kv_ref, pw_ref, pj_ref, o_ref, *, blk):
    t, hd = kv_ref.shape
    x = kv_ref[...].reshape(t // blk, blk, hd) * pw_ref[...][None]
    o_ref[...] = _bdot(jnp.sum(x, axis=1), pj_ref[...])


def _cmp_prompt(proj, bsz, t, pw, pj, col0):
    nch, blk, hd = pw.shape
    return pl.pallas_call(
        functools.partial(_cmp_prompt_kernel, blk=blk),
        grid=(bsz, nch),
        in_specs=[pl.BlockSpec((t, hd), lambda b, ch: (b, col0 + ch)),
                  pl.BlockSpec((None, blk, hd), lambda b, ch: (ch, 0, 0)),
                  pl.BlockSpec((None, hd, hd), lambda b, ch: (ch, 0, 0))],
        out_specs=pl.BlockSpec((None, None, t // blk, hd), lambda b, ch: (b, ch, 0, 0)),
        out_shape=jax.ShapeDtypeStruct((bsz, nch, t // blk, hd), F32),
        compiler_params=_params(("arbitrary", "arbitrary")),
        name="cmp_prompt",
    )(proj, pw, pj)


_NT = (((1,), (1,)), ((), ()))


def _slope_features(sl2, lane):
    hi = sl2.astype(BF16).astype(F32)
    lo = sl2 - hi
    return jnp.where(lane == 0, 64.0 * hi, jnp.where(lane == 1, hi, jnp.where(lane == 2, 64.0 * lo,
                                                                             jnp.where(lane == 3, lo, 0.0))))


def _attn_prompt_kernel(q_ref, kc_ref, vc_ref, ks_ref, vs_ref, kw_ref, vw_ref, gt_ref, sl_ref, pos_ref, oh_ref,
                        o_ref, *, tq, tk, wk, blk, n_g, n_kv, hd):
    h = pl.program_id(1)
    qt = pl.program_id(2)
    q0 = qt * tq
    nc = kc_ref.shape[0]
    t_all = ks_ref.shape[0]
    sl = sl_ref[...]
    lane = lax.broadcasted_iota(jnp.int32, (1, LANE), 1)
    qs = [q_ref[:, g * hd:(g + 1) * hd] * (hd ** -0.5) for g in range(n_g)]
    qb = jnp.concatenate(qs, axis=0).astype(BF16)
    qaug = jnp.concatenate(
        [jnp.concatenate([(qs[g] * LOG2E).astype(BF16),
                          jnp.broadcast_to(_slope_features(sl[:, g:g + 1] * LOG2E, lane), (tq, LANE)).astype(BF16)],
                         axis=1) for g in range(n_g)], axis=0)
    qpos = q0 + lax.broadcasted_iota(jnp.int32, (tq, 1), 0)

    bidx = lax.broadcasted_iota(jnp.int32, (1, nc), 1)
    start = bidx * blk
    s_c = _bdot_nt(qb, kc_ref[...])
    dist_c = jnp.abs(qpos.astype(F32) - (start.astype(F32) + (blk - 1) / 2))
    valid_c = (start + (blk - 1)) <= qpos
    p_c = jnp.concatenate([_masked_softmax(s_c[g * tq:(g + 1) * tq] - sl[:, g:g + 1] * dist_c, valid_c)
                           for g in range(n_g)], axis=0)
    o_c = _bdot(p_c, vc_ref[...])
    imp = p_c[0:tq]
    for g in range(1, n_g):
        imp = imp + p_c[g * tq:(g + 1) * tq]

    imp_t = jnp.concatenate([imp, jnp.zeros((tq, LANE - nc), F32)], axis=1).T[0:nc, :]
    bcol = lax.broadcasted_iota(jnp.int32, (nc, 1), 0)
    cur = (q0 + lax.broadcasted_iota(jnp.int32, (1, tq), 1)) // blk
    score = jnp.where(bcol < cur, imp_t, NEG)
    score = jnp.where((bcol == 0) | (bcol == cur), FORCE, score)
    rank = jnp.zeros((nc, tq), F32)
    for i in range(nc):
        ri = score[i:i + 1, :]
        rank = rank + jnp.where((ri > score) | ((ri == score) & (bcol > i)), 1.0, 0.0)
    selneg_t = jnp.where((rank < min(SEL_TOPK, nc)) & (score > 0.5 * NEG), 0.0, NEG)
    selneg = jnp.concatenate([selneg_t, jnp.zeros((LANE - nc, tq), F32)], axis=0).T.astype(BF16)

    def flash_step(k_ref, v_ref, k0, size, bias, carry):
        m, l, acc = carry
        kaug = jnp.concatenate([k_ref[pl.ds(k0, size), :].astype(BF16), pos_ref[pl.ds(k0, size), :]], axis=1)
        s = lax.dot_general(qaug, kaug, _NT, preferred_element_type=F32)
        gs = range(n_g)
        sgs = [s[g * tq:(g + 1) * tq] + bias for g in gs]
        m2 = [jnp.maximum(m[g], jnp.max(sgs[g], axis=-1, keepdims=True)) for g in gs]
        als = [jnp.exp2(m[g] - m2[g]) for g in gs]
        pfs = [jnp.exp2(sgs[g] - m2[g]) for g in gs]
        l2 = [als[g] * l[g] + jnp.sum(pfs[g], axis=-1, keepdims=True) for g in gs]
        ps = [p.astype(BF16) for p in pfs]
        accs = [als[g] * acc[g * tq:(g + 1) * tq] for g in gs]
        pv = jnp.dot(jnp.concatenate(ps, axis=0), v_ref[pl.ds(k0, size), :].astype(BF16),
                     preferred_element_type=F32)
        return tuple(m2), tuple(l2), jnp.concatenate(accs, axis=0) + pv

    def finish(carry):
        _, l, acc = carry
        return [acc[g * tq:(g + 1) * tq] / jnp.maximum(l[g], 1e-30) for g in range(n_g)]

    init = (tuple(jnp.full((tq, 1), NEG, F32) for _ in range(n_g)),
            tuple(jnp.zeros((tq, 1), F32) for _ in range(n_g)), jnp.zeros((n_g * tq, hd), F32))

    def sel_bias(k0):
        return lax.dot_general(selneg, oh_ref[pl.ds(k0, tk), :], _NT, preferred_element_type=F32)

    def sel_chunk(ci, carry):
        k0 = pl.multiple_of(ci * tk, tk)
        return flash_step(ks_ref, vs_ref, k0, tk, sel_bias(k0), carry)

    n_full = q0 // tk
    carry = lax.fori_loop(0, n_full, sel_chunk, init)
    k0 = pl.multiple_of(n_full * tk, tk)
    kpos = k0 + lax.broadcasted_iota(jnp.int32, (1, tk), 1)
    o_s = finish(flash_step(ks_ref, vs_ref, k0, tk, sel_bias(k0) + jnp.where(kpos <= qpos, 0.0, NEG), carry))

    w0 = pl.multiple_of(jnp.clip(q0 + tq - wk, 0, t_all - wk), tq)
    dist = qpos - (w0 + lax.broadcasted_iota(jnp.int32, (1, wk), 1))
    o_w = finish(flash_step(kw_ref, vw_ref, w0, wk, jnp.where((dist >= 0) & (dist < WINDOW), 0.0, NEG), init))

    gt = jax.nn.sigmoid(pltpu.roll(gt_ref[...], (LANE - n_g * h) % LANE, 1))
    n_h = n_g * n_kv
    for g in range(n_g):
        r = slice(g * tq, (g + 1) * tq)
        o = (gt[:, g:g + 1] * o_c[r] + gt[:, n_h + g:n_h + g + 1] * o_s[g]
             + gt[:, 2 * n_h + g:2 * n_h + g + 1] * o_w[g])
        o_ref[:, g * hd:(g + 1) * hd] = o.astype(o_ref.dtype)


def _attn_prompt(proj, cmp, slopes, bsz, t, n_h, n_kv, hd, blk):
    n_g = n_h // n_kv
    tq = 128
    tk = _pick(t, (512, 256, 128))
    wk = WINDOW + tq
    assert t % tq == 0 and t >= wk and t // blk <= LANE and blk == 64 and n_h * 3 <= LANE
    nqt = t // tq
    nc = t // blk
    kvb = n_h
    pos = jnp.arange(t, dtype=jnp.int32)[:, None]
    lane = jnp.arange(LANE, dtype=jnp.int32)[None, :]
    pos_tab = jnp.where(lane < 4, jnp.where(lane % 2 == 0, pos // 64, pos % 64), 0).astype(BF16)
    onehot = (lane == pos // blk).astype(BF16)
    gate_blk = (n_h * hd + 6 * n_kv * hd) // LANE
    kern = functools.partial(_attn_prompt_kernel, tq=tq, tk=tk, wk=wk, blk=blk, n_g=n_g, n_kv=n_kv, hd=hd)
    kv_spec = lambda off: pl.BlockSpec((t, hd), lambda b, h, q: (b, kvb + off + h))
    return pl.pallas_call(
        kern,
        grid=(bsz, n_kv, nqt),
        in_specs=[pl.BlockSpec((tq, n_g * hd), lambda b, h, q: (b * nqt + q, h)),
                  pl.BlockSpec((None, None, nc, hd), lambda b, h, q: (b, h, 0, 0)),
                  pl.BlockSpec((None, None, nc, hd), lambda b, h, q: (b, n_kv + h, 0, 0)),
                  kv_spec(2 * n_kv), kv_spec(3 * n_kv), kv_spec(4 * n_kv), kv_spec(5 * n_kv),
                  pl.BlockSpec((tq, LANE), lambda b, h, q: (b * nqt + q, gate_blk)),
                  pl.BlockSpec((None, 1, LANE), lambda b, h, q: (h, 0, 0)),
                  pl.BlockSpec((t, LANE), lambda b, h, q: (0, 0)),
                  pl.BlockSpec((t, LANE), lambda b, h, q: (0, 0))],
        out_specs=pl.BlockSpec((tq, n_g * hd), lambda b, h, q: (b * nqt + q, h)),
        out_shape=jax.ShapeDtypeStruct((bsz * t, n_h * hd), BF16),
        compiler_params=_params(("arbitrary", "arbitrary", "arbitrary")),
        name="attn_prompt",
    )(proj, cmp, cmp, proj, proj, proj, proj, proj, slopes, pos_tab, onehot)


def _pool_pages_kernel(pt_ref, *refs, n_pp, rows_blk):
    pw = refs[n_pp][...]
    o_ref = refs[n_pp + 1]
    per_page = refs[0].shape[0] // rows_blk
    nch = o_ref.shape[0] // (n_pp * per_page)
    for k in range(n_pp):
        for hb in range(per_page):
            r = k * per_page + hb
            x = refs[k][hb * rows_blk:(hb + 1) * rows_blk, :] * pw
            o_ref[r * nch:(r + 1) * nch, :] = jnp.sum(x.reshape(rows_blk // nch, nch, x.shape[-1]), axis=0)


def _page_map(b, p, pt, *, layer, k, n_pp):
    return (layer, pt[b, p * n_pp + k], 0, 0)


def _pool_pages(cache, layer, page_table, pw_rows, nch):
    _, _, prow, hd = cache.shape
    bd, n_pages = page_table.shape
    rows_blk = pw_rows.shape[0]
    n_pp = _pick(n_pages, (8, 4, 2, 1))
    per_page = prow // rows_blk
    in_specs = [pl.BlockSpec((None, None, prow, hd),
                             functools.partial(_page_map, layer=layer, k=k, n_pp=n_pp)) for k in range(n_pp)]
    in_specs.append(pl.BlockSpec((rows_blk, hd), lambda b, p, pt: (0, 0)))
    return pl.pallas_call(
        functools.partial(_pool_pages_kernel, n_pp=n_pp, rows_blk=rows_blk),
        grid_spec=pltpu.PrefetchScalarGridSpec(
            num_scalar_prefetch=1, grid=(bd, n_pages // n_pp), in_specs=in_specs,
            out_specs=pl.BlockSpec((None, n_pp * per_page * nch, hd), lambda b, p, pt: (b, p, 0))),
        out_shape=jax.ShapeDtypeStruct((bd, n_pages * per_page * nch, hd), F32),
        compiler_params=_params(("arbitrary", "arbitrary")),
        name="pool_pages",
    )(page_table, *([cache] * n_pp), pw_rows)


def _attn_sample_a_kernel(q_ref, pooled_ref, pj_ref, kvw_ref, gt_ref, sl_ref, ocw_ref, selm_ref,
                          score_buf, *, past, n_q, n_kv, n_g, hd, blk, wb, nbp):
    nch = 2 * n_kv
    nc = pooled_ref.shape[0] // nch
    n_blocks = (past + n_q + blk - 1) // blk
    kvw_rows = kvw_ref.shape[0]
    half = n_kv * hd
    rep = lambda x: jnp.concatenate([x] * n_g, axis=0)
    sl = sl_ref[...]
    gt = jax.nn.sigmoid(gt_ref[...])
    qpos = past + lax.broadcasted_iota(jnp.int32, (n_q, 1), 0)
    qpos_r = rep(qpos)
    bidx_c = lax.broadcasted_iota(jnp.int32, (1, nc), 1)
    start = bidx_c * blk
    bidx = lax.broadcasted_iota(jnp.int32, (1, nbp), 1)
    cur = qpos // blk
    score_buf[...] = jnp.full(score_buf.shape, DEAD, F32)

    for h in range(n_kv):
        qb = jnp.concatenate([q_ref[:, (h * n_g + g) * hd:(h * n_g + g + 1) * hd] for g in range(n_g)], axis=0)
        qb = (qb * (hd ** -0.5)).astype(BF16)
        slope = jnp.concatenate(
            [jnp.broadcast_to(sl[:, h * n_g + g:h * n_g + g + 1], (n_q, 1)) for g in range(n_g)], axis=0)
        kc = _bdot(pooled_ref[pl.ds(h, nc, stride=nch), :], pj_ref[h])
        vc = _bdot(pooled_ref[pl.ds(n_kv + h, nc, stride=nch), :], pj_ref[n_kv + h])
        s = _bdot_nt(qb, kc)
        dist_c = qpos_r.astype(F32) - (start.astype(F32) + (blk - 1) / 2)
        s = s - slope * jnp.abs(dist_c)
        p_c = _masked_softmax(s, (start + (blk - 1)) <= qpos_r)
        o_c = _bdot(p_c, vc)
        imp = p_c[0:n_q]
        for g in range(1, n_g):
            imp = imp + p_c[g * n_q:(g + 1) * n_q]
        imp = jnp.concatenate([imp, jnp.zeros((n_q, nbp - nc), F32)], axis=1)
        score = jnp.where(bidx < cur, imp, NEG)
        score = jnp.where((bidx == 0) | (bidx == cur), FORCE, score)
        score = jnp.where(bidx < n_blocks, score, DEAD)
        score_buf[h * n_q:(h + 1) * n_q, :] = score

        kw = kvw_ref[:, h * hd:(h + 1) * hd]
        vw = kvw_ref[:, half + h * hd:half + (h + 1) * hd]
        kidx = lax.broadcasted_iota(jnp.int32, (1, kvw_rows), 1)
        dist = qpos_r - (past - wb + kidx)
        valid = (dist >= 0) & (dist < WINDOW) & (kidx < wb + n_q)
        sw = _bdot_nt(qb, kw) - slope * jnp.abs(dist).astype(F32)
        o_w = _bdot(_masked_softmax(sw, valid), vw)
        for g in range(n_g):
            hh = h * n_g + g
            r = slice(g * n_q, (g + 1) * n_q)
            ocw_ref[:, hh * hd:(hh + 1) * hd] = (gt[:, hh:hh + 1] * o_c[r]
                                                 + gt[:, 2 * n_kv * n_g + hh:2 * n_kv * n_g + hh + 1] * o_w[r])

    sc_all = score_buf[...]
    sc_t = sc_all.T
    ii = lax.broadcasted_iota(jnp.int32, (nbp, 1), 0)
    for h in range(n_kv):
        for t in range(n_q):
            r = h * n_q + t
            col = sc_t[:, r:r + 1]
            row = sc_all[r:r + 1, :]
            before = (col > row) | ((col == row) & (ii < bidx))
            rank = jnp.sum(jnp.where(before, 1.0, 0.0), axis=0, keepdims=True)
            sel = jnp.where((rank < min(SEL_TOPK, n_blocks)) & (row > 0.5 * NEG), 1.0, 0.0)
            for g in range(n_g):
                selm_ref[h, g * n_q + t:g * n_q + t + 1, :] = sel


def _attn_sample_a(proj3, pooled, pj, kv_all, slopes_row, past, n_h, n_kv, hd, blk, wb):
    bd, n_q, _ = proj3.shape
    n_g = n_h // n_kv
    n_blocks = (past + n_q + blk - 1) // blk
    nbp = pl.cdiv(n_blocks, LANE) * LANE
    kvw_rows = kv_all.shape[1]
    qw = n_h * hd
    assert n_kv * n_q <= LANE
    kern = functools.partial(_attn_sample_a_kernel, past=past, n_q=n_q, n_kv=n_kv, n_g=n_g, hd=hd, blk=blk,
                             wb=wb, nbp=nbp)
    return pl.pallas_call(
        kern,
        grid=(bd,),
        in_specs=[pl.BlockSpec((None, n_q, qw), lambda b: (b, 0, 0)),
                  pl.BlockSpec((None, pooled.shape[1], hd), lambda b: (b, 0, 0)),
                  pl.BlockSpec((2 * n_kv, hd, hd), lambda b: (0, 0, 0)),
                  pl.BlockSpec((None, kvw_rows, 2 * n_kv * hd), lambda b: (b, 0, 0)),
                  pl.BlockSpec((None, n_q, LANE), lambda b: (b, 0, (qw + 6 * n_kv * hd) // LANE)),
                  pl.BlockSpec((1, LANE), lambda b: (0, 0))],
        out_specs=[pl.BlockSpec((None, n_q, qw), lambda b: (b, 0, 0)),
                   pl.BlockSpec((None, n_kv, n_g * n_q, nbp), lambda b: (b, 0, 0, 0))],
        out_shape=[jax.ShapeDtypeStruct((bd, n_q, qw), F32),
                   jax.ShapeDtypeStruct((bd, n_kv, n_g * n_q, nbp), F32)],
        scratch_shapes=[pltpu.VMEM((LANE, nbp), F32)],
        compiler_params=_params(("arbitrary",)),
        name="attn_sample_a",
    )(proj3, pooled, pj, kv_all, proj3, slopes_row)


def _attn_sample_sel_kernel(pt_ref, *refs, n_pp, past, n_q, n_kv, n_g, hd, blk):
    pages = refs[:n_pp]
    q_ref, selm_ref, new_ref, ocw_ref, gt_ref, sl_ref, o_ref, m_ref, l_ref, acc_ref = refs[n_pp:]
    p = pl.program_id(1)
    nch = 2 * n_kv
    page = pages[0].shape[0] // nch
    keys = n_pp * page
    nbp = selm_ref.shape[-1]
    rows = n_g * n_q
    half = n_kv * hd
    rep = lambda x: jnp.concatenate([x] * n_g, axis=0)
    sl = sl_ref[...]
    qpos_r = rep(past + lax.broadcasted_iota(jnp.int32, (n_q, 1), 0))

    @pl.when(p == 0)
    def _():
        m_ref[...] = jnp.full(m_ref.shape, NEG, F32)
        l_ref[...] = jnp.zeros_like(l_ref)
        acc_ref[...] = jnp.zeros_like(acc_ref)

    def heads():
        for h in range(n_kv):
            qb = jnp.concatenate([q_ref[:, (h * n_g + g) * hd:(h * n_g + g + 1) * hd] for g in range(n_g)],
                                 axis=0) * (hd ** -0.5)
            slope = jnp.concatenate(
                [jnp.broadcast_to(sl[:, h * n_g + g:h * n_g + g + 1], (n_q, 1)) for g in range(n_g)], axis=0)
            yield h, qb, slope

    def update(h, sc, mask, pv_fn):
        sc = jnp.where(mask, sc, NEG)
        m = m_ref[h]
        m_new = jnp.maximum(m, jnp.max(sc, axis=-1, keepdims=True))
        alpha = jnp.exp(m - m_new)
        pr = jnp.where(mask, jnp.exp(sc - m_new), 0.0)
        l_ref[h] = alpha * l_ref[h] + jnp.sum(pr, axis=-1, keepdims=True)
        acc_ref[h] = alpha * acc_ref[h] + pv_fn(pr)
        m_ref[h] = m_new

    k0 = p * keys
    kk = lax.broadcasted_iota(jnp.int32, (nbp, keys), 1)
    nn = lax.broadcasted_iota(jnp.int32, (nbp, keys), 0)
    expand = jnp.where(nn == (k0 + kk) // blk, 1.0, 0.0).astype(BF16)
    dist = qpos_r - (k0 + lax.broadcasted_iota(jnp.int32, (1, keys), 1))
    staged = []
    for h, qb, slope in heads():
        mask = (jnp.dot(selm_ref[h].astype(BF16), expand, preferred_element_type=F32) > 0.5) & (dist >= 0)
        kx = jnp.concatenate([pages[k][pl.ds(h, page, stride=nch), :].astype(BF16) for k in range(n_pp)], axis=0)
        sc = jnp.where(mask, _bdot_nt(qb, kx) - slope * dist.astype(F32), NEG)
        staged.append((h, mask, sc))
    probs = []
    for h, mask, sc in staged:
        m = m_ref[h]
        m_new = jnp.maximum(m, jnp.max(sc, axis=-1, keepdims=True))
        alpha = jnp.exp(m - m_new)
        pr = jnp.where(mask, jnp.exp(sc - m_new), 0.0)
        l_ref[h] = alpha * l_ref[h] + jnp.sum(pr, axis=-1, keepdims=True)
        m_ref[h] = m_new
        probs.append((h, alpha, pr))
    for h, alpha, pr in probs:
        vx = jnp.concatenate([pages[k][pl.ds(n_kv + h, page, stride=nch), :].astype(BF16) for k in range(n_pp)],
                             axis=0)
        acc_ref[h] = alpha * acc_ref[h] + _bdot(pr, vx)

    @pl.when(p == pl.num_programs(1) - 1)
    def _():
        gt = jax.nn.sigmoid(gt_ref[...])
        for h, qb, slope in heads():
            nb0 = past // blk
            seln = selm_ref[h][:, nb0:nb0 + 1] > 0.5
            qr = qb.astype(BF16).astype(F32)
            for j in range(n_q):
                kj = new_ref[j:j + 1, h * hd:(h + 1) * hd].astype(BF16).astype(F32)
                vj = new_ref[j:j + 1, half + h * hd:half + (h + 1) * hd].astype(BF16).astype(F32)
                dj = qpos_r - (past + j)
                sc = jnp.sum(qr * kj, axis=-1, keepdims=True) - slope * dj.astype(F32)
                update(h, sc, seln & (dj >= 0), lambda pr, vj=vj: pr.astype(BF16).astype(F32) * vj)
            o_s = acc_ref[h] / jnp.maximum(l_ref[h], 1e-30)
            for g in range(n_g):
                hh = h * n_g + g
                r = slice(g * n_q, (g + 1) * n_q)
                gcol = n_kv * n_g + hh
                o = ocw_ref[:, hh * hd:(hh + 1) * hd] + gt[:, gcol:gcol + 1] * o_s[r]
                o_ref[:, hh * hd:(hh + 1) * hd] = o.astype(o_ref.dtype)


def _attn_sample_sel(cache, layer, page_table, proj3, selm, new_rows, ocw, slopes_row, past, n_h, n_kv, hd, blk):
    _, _, prow, _ = cache.shape
    bd, n_pages = page_table.shape
    n_q = proj3.shape[1]
    n_g = n_h // n_kv
    qw = n_h * hd
    nbp = selm.shape[-1]
    n_pp = _pick(n_pages, (8, 4, 2, 1))
    rows = n_g * n_q
    in_specs = [pl.BlockSpec((None, None, prow, hd),
                             functools.partial(_page_map, layer=layer, k=k, n_pp=n_pp)) for k in range(n_pp)]
    in_specs += [pl.BlockSpec((None, n_q, qw), lambda b, p, pt: (b, 0, 0)),
                 pl.BlockSpec((None, n_kv, rows, nbp), lambda b, p, pt: (b, 0, 0, 0)),
                 pl.BlockSpec((None, new_rows.shape[1], new_rows.shape[2]), lambda b, p, pt: (b, 0, 0)),
                 pl.BlockSpec((None, n_q, qw), lambda b, p, pt: (b, 0, 0)),
                 pl.BlockSpec((None, n_q, LANE), lambda b, p, pt: (b, 0, (qw + 6 * n_kv * hd) // LANE)),
                 pl.BlockSpec((1, LANE), lambda b, p, pt: (0, 0))]
    kern = functools.partial(_attn_sample_sel_kernel, n_pp=n_pp, past=past, n_q=n_q, n_kv=n_kv, n_g=n_g,
                             hd=hd, blk=blk)
    return pl.pallas_call(
        kern,
        grid_spec=pltpu.PrefetchScalarGridSpec(
            num_scalar_prefetch=1, grid=(bd, n_pages // n_pp), in_specs=in_specs,
            out_specs=pl.BlockSpec((None, n_q, qw), lambda b, p, pt: (b, 0, 0)),
            scratch_shapes=[pltpu.VMEM((n_kv, rows, 1), F32), pltpu.VMEM((n_kv, rows, 1), F32),
                            pltpu.VMEM((n_kv, rows, hd), F32)]),
        out_shape=jax.ShapeDtypeStruct((bd, n_q, qw), BF16),
        compiler_params=_params(("arbitrary", "arbitrary")),
        name="attn_sample_sel",
    )(page_table, *([cache] * n_pp), proj3, selm, new_rows, ocw, proj3, slopes_row)


def _alibi(n_h):
    h = jnp.arange(1, n_h + 1, dtype=F32)
    return jnp.exp2(-8.0 * h / n_h)


def kernel(x_prompt, x_sample, c_prompt, c_sample, state_gdn, state_gdn_conv, state_sconv, cache_kv_cmp,
           cache_kv_sel, state_kv_win, page_table, norm_g, w_ada, b_ada, w_in_e, conv_w_gdn, a_log, dt_bias,
           gdn_norm_g, conv_w_sc, w_out_e, w_in_o, cmp_pool, cmp_proj, w_out_o, w_mlp1, w_mlp2, final_g):
    bsz, seq, d = x_prompt.shape
    bd, n_q, _ = x_sample.shape
    depth = norm_g.shape[0]
    n_vh, dk, dv = state_gdn.shape[2:]
    conv_dim = conv_w_gdn.shape[2]
    sc_dim = conv_w_sc.shape[2]
    nv = n_vh * dv
    blk, _, n_kv = cmp_pool.shape[1:]
    hd = cmp_proj.shape[-1]
    n_h = w_out_o.shape[1] // hd
    kv_row = 2 * n_kv * hd
    qw = n_h * hd
    page = cache_kv_cmp.shape[2]
    past = page_table.shape[1] * page
    wb = state_kv_win.shape[2]
    c = GDN_CHUNK
    assert seq % c == 0 and seq % blk == 0 and seq >= WINDOW and n_q <= c and n_q < blk and past % blk == 0
    assert n_q >= conv_w_gdn.shape[1] - 1 and hd == LANE and dk == LANE and dv == LANE

    rows_c = -(-(bsz + bd) // SUBLANE) * SUBLANE
    c_all = jnp.pad(jnp.concatenate([c_prompt, c_sample], axis=0).astype(F32), ((0, rows_c - bsz - bd), (0, 0)))
    mod = _ada_mod(c_all, w_ada, b_ada).reshape(depth, rows_c, 6, d)

    def mods(l, sample):
        if sample:
            return [jnp.repeat(mod[l, bsz:bsz + bd, j], n_q, axis=0)[None] for j in range(6)]
        return [mod[l, :bsz, j][:, None, :] for j in range(6)]

    slopes = _alibi(n_h)
    slopes_kv = jnp.pad(slopes.reshape(n_kv, 1, n_h // n_kv), ((0, 0), (0, 0), (0, LANE - n_h // n_kv)))
    slopes_row = jnp.pad(slopes.reshape(1, n_h), ((0, 0), (0, LANE - n_h)))
    nch = 2 * n_kv
    cache_cmp = cache_kv_cmp.reshape(cache_kv_cmp.shape[:2] + (page * nch, hd))
    cache_sel = cache_kv_sel.reshape(cache_kv_sel.shape[:2] + (page * nch, hd))

    w_in_e_t = jnp.swapaxes(w_in_e, 1, 2).astype(BF16)
    w_in_o_t = jnp.swapaxes(w_in_o, 1, 2).astype(BF16)

    def even_tail(i):
        o_ab = conv_dim + nv
        o_rest = o_ab + 2 * n_vh
        return jnp.concatenate([w_in_e_t[i, o_rest:], w_in_e_t[i, o_ab:o_rest],
                                jnp.zeros((LANE - 2 * n_vh, d), w_in_e_t.dtype)], axis=0)

    def run(x3, sample):
        nb, t, _ = x3.shape
        x = x3.reshape(nb * t, d).astype(F32)
        ev, od = [], []
        for l in range(depth):
            i = l // 2
            sh1, sc1, g1, sh2, sc2, g2 = mods(l, sample)
            if l % 2 == 0:
                proj = _norm_mod_matmul(x, norm_g[l, 0], sc1, sh1, w_in_e_t, i, main_cols=conv_dim + nv,
                                        w_tail=even_tail(i))
                n_cst = conv_w_gdn.shape[1] - 1
                conv_new = proj.reshape(nb, t, -1)[:, t - n_cst:, :conv_dim]
                if sample:
                    projp = jnp.pad(proj.reshape(nb, t, -1), ((0, 0), (0, c - t), (0, 0))).reshape(nb * c, -1)
                    cst, scst, s0, tv = state_gdn_conv[i], state_sconv[i], state_gdn[i], t
                else:
                    projp = proj
                    cst = jnp.zeros((nb, conv_w_gdn.shape[1] - 1, conv_dim), F32)
                    scst = jnp.zeros((nb, conv_w_sc.shape[1] - 1, sc_dim), F32)
                    s0, tv = jnp.zeros((nb, n_vh, dk, dv), F32), c
                mix, s_new, u_last = _gdn_sconv(projp, cst, scst, s0, conv_w_gdn[i], conv_w_sc[i], a_log[i],
                                                dt_bias[i], gdn_norm_g[i], tv)
                if sample:
                    mix = mix.reshape(nb, c, -1)[:, :t].reshape(nb * t, -1)
                sc_new = u_last[:, tv - scst.shape[1]:tv]
                ev.append((s_new, conv_new, sc_new))
                x = _matmul_gated_residual(mix, w_out_e, i, x, g1)
            else:
                proj, kvc8, kvs8, kvw8 = _norm_mod_matmul(x, norm_g[l, 0], sc1, sh1, w_in_o_t, i,
                                                          rows_out=(qw, 3, nch, hd))
                kv_c, kv_s, kv_w = [a.reshape(nb, t, 2, n_kv, hd) for a in (kvc8, kvs8, kvw8)]
                pj = cmp_proj[i].reshape(2 * n_kv, hd, hd)
                if sample:
                    pw_rows = jnp.broadcast_to(cmp_pool[i].reshape(blk * nch, 1), (blk * nch, hd))
                    pooled = _pool_pages(cache_cmp, i, page_table, pw_rows, nch)
                    proj3 = proj.reshape(nb, t, -1)
                    kv_all = jnp.concatenate([state_kv_win[i].reshape(nb, wb, kv_row).astype(F32),
                                              proj3[:, :, qw + 2 * kv_row:qw + 3 * kv_row]], axis=1)
                    kv_all_p = jnp.pad(kv_all, ((0, 0), (0, (-kv_all.shape[1]) % SUBLANE), (0, 0)))
                    ocw, selm = _attn_sample_a(proj3, pooled, pj, kv_all_p, slopes_row, past, n_h, n_kv, hd, blk, wb)
                    new_rows = jnp.pad(proj3[:, :, qw + kv_row:qw + 2 * kv_row], ((0, 0), (0, (-t) % SUBLANE), (0, 0)))
                    o = _attn_sample_sel(cache_sel, i, page_table, proj3, selm, new_rows, ocw, slopes_row, past,
                                         n_h, n_kv, hd, blk).reshape(nb * t, qw)
                    kv_win_new = kv_all[:, t:].reshape(nb, wb, 2, n_kv, hd)
                else:
                    pw = jnp.broadcast_to(cmp_pool[i].reshape(blk, 2 * n_kv).T[:, :, None], (2 * n_kv, blk, hd))
                    cmp = _cmp_prompt(proj, nb, t, pw, pj, n_h)
                    o = _attn_prompt(proj, cmp, slopes_kv, nb, t, n_h, n_kv, hd, blk)
                    kv_win_new = kv_w[:, t - min(WINDOW, t):]
                od.append((kv_c, kv_s, kv_win_new))
                x = _matmul_gated_residual(o, w_out_o, i, x, g1)
            x = _mlp(x, norm_g[l, 1], sc2, sh2, g2, w_mlp1, w_mlp2, l, final_g if l == depth - 1 else None)
        y = x.reshape(nb, t, d)
        evs = [jnp.stack([s[j] for s in ev]) for j in range(3)]
        ods = [jnp.stack([s[j] for s in od]) for j in range(3)]
        return y, evs, ods

    y_p, (gdn_p, gconv_p, sconv_p), (kvc_p, kvs_p, kvw_p) = run(x_prompt, False)
    y_s, (gdn_s, gconv_s, sconv_s), (kvc_s, kvs_s, kvw_s) = run(x_sample, True)
    return (y_p, y_s, gdn_p, gdn_s, gconv_p, gconv_s, sconv_p, sconv_s, kvc_p, kvc_s, kvs_p, kvs_s, kvw_p, kvw_s)
```

```python
import functools

import jax
import jax.numpy as jnp
from jax import lax
from jax.experimental import pallas as pl
from jax.experimental.pallas import tpu as pltpu

F32 = jnp.float32
BF16 = jnp.bfloat16

EPS = 1e-6
NEG = -1e30
FORCE = 1e6
DEAD = -3e38
SEL_TOPK = 16
LOG2E = 1.4426950408889634
WINDOW = 512
GDN_CHUNK = 64
LANE = 128
SUBLANE = 8
VMEM_BUDGET = 56 * 1024 * 1024


def _params(sem):
    return pltpu.CompilerParams(dimension_semantics=sem, vmem_limit_bytes=VMEM_BUDGET)


def _pick(n, cands):
    for c in cands:
        if n % c == 0:
            return c
    return n


def _bdot(a, b):
    return jnp.dot(a.astype(BF16), b.astype(BF16), preferred_element_type=F32)


def _bdot_nt(a, b):
    return lax.dot_general(a.astype(BF16), b.astype(BF16), (((1,), (1,)), ((), ())),
                           preferred_element_type=F32)


def _silu(x):
    return x * jax.nn.sigmoid(x)


def _modnorm(x, g, scale, shift):
    y = x * lax.rsqrt(jnp.mean(x * x, axis=-1, keepdims=True) + EPS)
    return (y * g) * (1.0 + scale) + shift


def _modnorm_rows(x_ref, g_ref, sc_ref, sh_ref, hn_ref):
    tm = x_ref.shape[0]
    ch = min(tm, 256)

    def body(r, carry):
        rows = pl.ds(pl.multiple_of(r * ch, ch), ch)
        sc = sc_ref[...] if sc_ref.shape[0] == 1 else sc_ref[rows, :]
        sh = sh_ref[...] if sh_ref.shape[0] == 1 else sh_ref[rows, :]
        hn_ref[rows, :] = _modnorm(x_ref[rows, :], g_ref[...], sc, sh).astype(BF16)
        return carry

    lax.fori_loop(0, tm // ch, body, 0)


def _masked_softmax(s, valid):
    s = jnp.where(valid, s, NEG)
    m = jnp.max(s, axis=-1, keepdims=True)
    p = jnp.where(valid, jnp.exp(s - m), 0.0)
    return p / jnp.maximum(jnp.sum(p, axis=-1, keepdims=True), 1e-30)


def _ada_kernel(c_ref, w_ref, b_ref, o_ref):
    o_ref[...] = _bdot(_silu(c_ref[...]), w_ref[...]) + b_ref[...]


def _ada_mod(c_all, w_ada, b_ada):
    depth, d, n6 = w_ada.shape
    rows = c_all.shape[0]
    tn = _pick(n6, (1024, 512, 256, 128))
    return pl.pallas_call(
        _ada_kernel,
        grid=(depth, n6 // tn),
        in_specs=[pl.BlockSpec((rows, d), lambda l, j: (0, 0)),
                  pl.BlockSpec((None, d, tn), lambda l, j: (l, 0, j)),
                  pl.BlockSpec((None, 1, tn), lambda l, j: (l, 0, j))],
        out_specs=pl.BlockSpec((None, rows, tn), lambda l, j: (l, 0, j)),
        out_shape=jax.ShapeDtypeStruct((depth, rows, n6), F32),
        compiler_params=_params(("arbitrary", "arbitrary")),
        name="ada_mod",
    )(c_all, w_ada, b_ada.reshape(depth, 1, n6))


def _mm1_kernel(x_ref, g_ref, sc_ref, sh_ref, *refs, n_main, n_w, rows_first, n_rows_out, nch):
    w_refs = refs[:n_w]
    o_ref = refs[n_w]
    row_refs = refs[n_w + 1:n_w + 1 + n_rows_out]
    hn_ref = refs[-1]
    j = pl.program_id(1)

    @pl.when(j == 0)
    def _():
        _modnorm_rows(x_ref, g_ref, sc_ref, sh_ref, hn_ref)

    def emit(wt_ref):
        y = lax.dot_general(hn_ref[...], wt_ref[...].astype(BF16), (((1,), (1,)), ((), ())),
                            preferred_element_type=F32)
        o_ref[...] = y
        return y

    if n_w == 1:
        y = emit(w_refs[0])
        tm, tn = o_ref.shape
        hd = row_refs[0].shape[-1] if n_rows_out else LANE
        per_tile = tn // hd
        tiles_per_out = nch // per_tile if n_rows_out else 1
        for r in range(n_rows_out):
            for part in range(tiles_per_out):
                @pl.when(j == rows_first + r * tiles_per_out + part)
                def _(r=r, part=part):
                    for cc in range(per_tile):
                        row_refs[r][pl.ds(part * per_tile + cc, tm, stride=nch), :] = y[:, cc * hd:(cc + 1) * hd]
    else:
        @pl.when(j < n_main)
        def _():
            emit(w_refs[0])

        @pl.when(j >= n_main)
        def _():
            emit(w_refs[1])


def _mod_spec(mod, tm, rows_per_group, width, col):
    r = mod.shape[1]
    tiles = rows_per_group // tm
    if col:
        return pl.BlockSpec((None, r, width), lambda i, j: (i // tiles, 0, j))
    return pl.BlockSpec((None, r, width), lambda i, j: (i // tiles, 0, 0))


def _norm_mod_matmul(x, g, scale, shift, w, layer, main_cols=None, w_tail=None, rows_out=None):
    m, d = x.shape
    tn = 512
    groups = scale.shape[0]
    rpg = m // groups
    if w_tail is None:
        n = w.shape[1]
        n_main = pl.cdiv(n, tn)
        weights = [w]
        w_specs = [pl.BlockSpec((None, tn, d), lambda i, j: (layer, j, 0))]
    else:
        assert main_cols % tn == 0
        n = main_cols + w_tail.shape[0]
        n_main = main_cols // tn
        weights = [w, w_tail]
        w_specs = [pl.BlockSpec((None, tn, d), lambda i, j: (layer, jnp.minimum(j, n_main - 1), 0)),
                   pl.BlockSpec((tn, d), lambda i, j: (jnp.maximum(j - n_main, 0), 0))]
    n_rows_out, rows_first, nch, hd = 0, 0, 1, LANE
    out_shape = [jax.ShapeDtypeStruct((m, n), F32)]
    if rows_out is not None:
        first_col, n_rows_out, nch, hd = rows_out
        assert first_col % tn == 0 and (nch * hd) % tn == 0 and w_tail is None
        rows_first = first_col // tn
    big = n_rows_out == 0 and w.dtype == BF16
    tm = _pick(rpg, ((2048,) if big else ()) + (1024, 512, 256, 128, 64, 32, 16, 8))
    out_specs = [pl.BlockSpec((tm, tn), lambda i, j: (i, j))]
    for _ in range(n_rows_out):
        out_specs.append(pl.BlockSpec((tm * nch, hd), lambda i, j: (i, 0)))
        out_shape.append(jax.ShapeDtypeStruct((m * nch, hd), F32))
    kern = functools.partial(_mm1_kernel, n_main=n_main, n_w=len(weights), rows_first=rows_first,
                             n_rows_out=n_rows_out, nch=nch)
    outs = pl.pallas_call(
        kern,
        grid=(m // tm, pl.cdiv(n, tn)),
        in_specs=[pl.BlockSpec((tm, d), lambda i, j: (i, 0), pipeline_mode=pl.Buffered(1)),
                  pl.BlockSpec((1, d), lambda i, j: (0, 0)),
                  _mod_spec(scale, tm, rpg, d, False),
                  _mod_spec(shift, tm, rpg, d, False)] + w_specs,
        out_specs=out_specs,
        out_shape=out_shape,
        scratch_shapes=[pltpu.VMEM((tm, d), BF16)],
        compiler_params=_params(("arbitrary", "arbitrary")),
        name="norm_mod_matmul",
    )(x, g.reshape(1, d), scale, shift, *weights)
    return outs if n_rows_out else outs[0]


def _mm2_kernel(a_ref, w_ref, x_ref, gate_ref, o_ref):
    y = jnp.dot(a_ref[...], w_ref[...].astype(BF16), preferred_element_type=F32)
    o_ref[...] = x_ref[...] + gate_ref[...] * y


def _matmul_gated_residual(a, w, layer, x, gate):
    m, k = a.shape
    d = w.shape[2]
    groups = gate.shape[0]
    rpg = m // groups
    tm = _pick(rpg, (1024, 512, 256, 128, 64, 32, 16))
    tn = _pick(d, (512, 256, 128))
    return pl.pallas_call(
        _mm2_kernel,
        grid=(m // tm, d // tn),
        in_specs=[pl.BlockSpec((tm, k), lambda i, j: (i, 0)),
                  pl.BlockSpec((None, k, tn), lambda i, j: (layer, 0, j)),
                  pl.BlockSpec((tm, tn), lambda i, j: (i, j)),
                  _mod_spec(gate, tm, rpg, tn, True)],
        out_specs=pl.BlockSpec((tm, tn), lambda i, j: (i, j)),
        out_shape=jax.ShapeDtypeStruct((m, d), F32),
        compiler_params=_params(("arbitrary", "arbitrary")),
        name="matmul_gated_residual",
    )(a, w, x, gate)


def _mlp_kernel(x_ref, g_ref, sc_ref, sh_ref, gate_ref, w1_ref, w2_ref, fg_ref, o_ref, hn_ref, *, final):
    j = pl.program_id(1)

    @pl.when(j == 0)
    def _():
        _modnorm_rows(x_ref, g_ref, sc_ref, sh_ref, hn_ref)
        o_ref[...] = jnp.zeros_like(o_ref)

    h = jnp.dot(hn_ref[...], w1_ref[...].astype(BF16), preferred_element_type=F32)
    h = jnp.square(jnp.maximum(h, 0.0))
    o_ref[...] += jnp.dot(h.astype(BF16), w2_ref[...].astype(BF16), preferred_element_type=F32)

    @pl.when(j == pl.num_programs(1) - 1)
    def _():
        y = x_ref[...] + gate_ref[...] * o_ref[...]
        if final:
            y = (y * lax.rsqrt(jnp.mean(y * y, axis=-1, keepdims=True) + EPS)) * fg_ref[...]
        o_ref[...] = y


def _mlp(x, g, scale, shift, gate, w1, w2, layer, final_g=None):
    m, d = x.shape
    final = final_g is not None
    fg = (final_g if final else g).reshape(1, d)
    f = w1.shape[2]
    groups = scale.shape[0]
    rpg = m // groups
    tm = _pick(rpg, (1024, 512, 256, 128, 64, 32, 16, 8))
    tf = _pick(f, (512, 256, 128))
    return pl.pallas_call(
        functools.partial(_mlp_kernel, final=final),
        grid=(m // tm, f // tf),
        in_specs=[pl.BlockSpec((tm, d), lambda i, j: (i, 0)),
                  pl.BlockSpec((1, d), lambda i, j: (0, 0)),
                  _mod_spec(scale, tm, rpg, d, False),
                  _mod_spec(shift, tm, rpg, d, False),
                  _mod_spec(gate, tm, rpg, d, False),
                  pl.BlockSpec((None, d, tf), lambda i, j: (layer, 0, j)),
                  pl.BlockSpec((None, tf, d), lambda i, j: (layer, j, 0)),
                  pl.BlockSpec((1, d), lambda i, j: (0, 0))],
        out_specs=pl.BlockSpec((tm, d), lambda i, j: (i, 0), pipeline_mode=pl.Buffered(1)),
        out_shape=jax.ShapeDtypeStruct((m, d), F32),
        scratch_shapes=[pltpu.VMEM((tm, d), BF16)],
        compiler_params=_params(("arbitrary", "arbitrary")),
        name="mlp",
    )(x, g.reshape(1, d), scale, shift, gate, w1, w2, fg)


def _split3(x):
    hi = x.astype(BF16)
    r = x - hi.astype(F32)
    mid = r.astype(BF16)
    lo = (r - mid.astype(F32)).astype(BF16)
    return hi, mid, lo


def _gdn_kernel(qkv_ref, z_ref, ab_ref, gb_ref, gcg_ref, hx_ref, cst_ref, scst_ref, cw_ref, scw_ref,
                alog_ref, dtb_ref, ng_ref, s0_ref,
                mix_ref, sout_ref, ulast_ref,
                xbuf, ubuf, s_ref, *, t_valid, n_kh, n_vh, dk, dv):
    c = GDN_CHUNK
    n = pl.program_id(1)
    nqk = n_kh * dk
    nv = n_vh * dv
    rep = n_vh // n_kh
    hpg = 4
    gw = hpg * c
    n_conv = cw_ref.shape[0]
    n_sc = scw_ref.shape[0]

    @pl.when(n == 0)
    def _():
        xbuf[0:SUBLANE, :] = cst_ref[...]
        ubuf[0:SUBLANE, :] = scst_ref[...]
        s_ref[...] = s0_ref[...]

    xbuf[SUBLANE:SUBLANE + c, :] = qkv_ref[...]
    cw = cw_ref[...]
    off = SUBLANE - (n_conv - 1)
    xc = cw[0:1, :] * xbuf[off:off + c, :]
    for j in range(1, n_conv):
        xc = xc + cw[j:j + 1, :] * xbuf[off + j:off + j + c, :]
    xbuf[0:SUBLANE, :] = xbuf[c:c + SUBLANE, :]
    xc = _silu(xc)

    u = gcg_ref[...] * hx_ref[...]
    ubuf[SUBLANE:SUBLANE + c, :] = u
    scw = scw_ref[...]
    offs = SUBLANE - (n_sc - 1)
    cu = scw[0:1, :] * ubuf[offs:offs + c, :]
    for j in range(1, n_sc):
        cu = cu + scw[j:j + 1, :] * ubuf[offs + j:offs + j + c, :]
    ubuf[0:SUBLANE, :] = ubuf[c:c + SUBLANE, :]
    ulast_ref[...] = u
    mix_ref[:, nv:] = (gb_ref[...] * cu).astype(mix_ref.dtype)

    ab = ab_ref[...]
    g_all = -jnp.exp(alog_ref[...]) * jax.nn.softplus(ab + dtb_ref[...])
    beta_all = jax.nn.sigmoid(ab)
    if t_valid < c:
        rowmask = lax.broadcasted_iota(jnp.int32, (c, 1), 0) < t_valid
        xc = jnp.where(rowmask, xc, 0.0)
        g_all = jnp.where(rowmask, g_all, 0.0)
        beta_all = jnp.where(rowmask, beta_all, 0.0)

    ri = lax.broadcasted_iota(jnp.int32, (c, c), 0)
    ci = lax.broadcasted_iota(jnp.int32, (c, c), 1)
    tril = jnp.where(ri >= ci, 1.0, 0.0).astype(BF16)
    ghi, gmid, glo = _split3(g_all)
    gcum = (jnp.dot(tril, ghi, preferred_element_type=F32)
            + jnp.dot(tril, gmid, preferred_element_type=F32)
            + jnp.dot(tril, glo, preferred_element_type=F32))
    glast = gcum[c - 1:c, :]

    def l2n(x):
        return x * lax.rsqrt(jnp.sum(x * x, axis=-1, keepdims=True) + EPS)

    gi = lax.broadcasted_iota(jnp.int32, (gw, gw), 0)
    gj = lax.broadcasted_iota(jnp.int32, (gw, gw), 1)
    same = (gi // c) == (gj // c)
    low_incl = same & (gi >= gj)
    low_strict = same & (gi > gj)
    lane_blk = lax.broadcasted_iota(jnp.int32, (dk, gw), 1) // c
    level_mask = []
    size = 1
    while size < c:
        level_mask.append(((gi // (2 * size)) == (gj // (2 * size))) & ((gi // size) != (gj // size)) & (gi > gj))
        size *= 2

    def stack(cols):
        return jnp.concatenate(cols, axis=0)

    n_grp = n_vh // hpg
    pre = []
    for grp in range(n_grp):
        heads = [grp * hpg + j for j in range(hpg)]
        kheads = [h // rep for h in heads]
        qs = {kh: l2n(xc[:, kh * dk:(kh + 1) * dk]) * (dk ** -0.5) for kh in set(kheads)}
        ks = {kh: l2n(xc[:, nqk + kh * dk:nqk + (kh + 1) * dk]) for kh in set(kheads)}
        q_st = stack([qs[kh] for kh in kheads])
        k_st = stack([ks[kh] for kh in kheads])
        v_st = stack([xc[:, 2 * nqk + h * dv:2 * nqk + (h + 1) * dv] for h in heads])
        beta_st = stack([beta_all[:, n_vh + h:n_vh + h + 1] for h in heads])
        gc_st = stack([gcum[:, h:h + 1] for h in heads])
        gl_st = stack([jnp.broadcast_to(glast[:, h:h + 1], (c, 1)) for h in heads])
        gc_row = jnp.broadcast_to(gc_st, (gw, LANE)).T[0:1, :]
        diff = gc_st - gc_row
        decay = jnp.where(low_incl, jnp.exp(jnp.where(low_incl, diff, 0.0)), 0.0)
        egc = jnp.exp(gc_st)
        kb_st = k_st * beta_st

        nmat = _bdot_nt(kb_st, k_st) * jnp.where(low_strict, decay, 0.0)
        qk = _bdot_nt(q_st, k_st) * decay
        rhs = jnp.concatenate([v_st * beta_st, kb_st * egc], axis=1)
        pre.append((nmat, qk, rhs, q_st * egc, k_st * jnp.exp(gl_st - gc_st)))

    tms = [-jnp.where(level_mask[0], p[0], 0.0) for p in pre]
    for lm in level_mask[1:]:
        lls = [jnp.where(lm, p[0], 0.0) for p in pre]
        ys = [ll + _bdot(tm, ll) for tm, ll in zip(tms, lls)]
        tms = [tm - y - _bdot(y, tm) for tm, y in zip(tms, ys)]

    for grp in range(n_grp):
        heads = [grp * hpg + j for j in range(hpg)]
        nmat, qk, rhs, qg_st, kd_st = pre[grp]
        sol = rhs + _bdot(tms[grp], rhs)
        u_st = sol[:, :dv]
        w_st = sol[:, dv:]

        vnew, qs_out = [], []
        for j, h in enumerate(heads):
            wq = jnp.concatenate([w_st[j * c:(j + 1) * c], qg_st[j * c:(j + 1) * c]], axis=0)
            r2 = _bdot(wq, s_ref[h])
            vnew.append(u_st[j * c:(j + 1) * c] - r2[:c])
            qs_out.append(r2[c:])
        vnew_st = stack(vnew)
        o_st = stack(qs_out) + _bdot(qk, vnew_st)
        kd_t = kd_st.T
        for j, h in enumerate(heads):
            upd = _bdot(jnp.where(lane_blk == j, kd_t, 0.0), vnew_st)
            s_ref[h] = s_ref[h] * jnp.exp(glast[:, h:h + 1]) + upd
            o_h = o_st[j * c:(j + 1) * c]
            on = (o_h * lax.rsqrt(jnp.mean(o_h * o_h, axis=-1, keepdims=True) + EPS)) * ng_ref[...]
            zz = z_ref[:, h * dv:(h + 1) * dv]
            mix_ref[:, h * dv:(h + 1) * dv] = (on * _silu(zz)).astype(mix_ref.dtype)

    sout_ref[...] = s_ref[...]


def _gdn_sconv(proj, conv_state, sc_state, s0, conv_w, sc_w, a_log, dt_bias, norm_g, t_valid):
    bsz, n_vh, dk, dv = s0.shape
    n_conv, conv_dim = conv_w.shape
    n_sc, sc = sc_w.shape
    nv = n_vh * dv
    n_kh = (conv_dim - nv) // (2 * dk)
    c = GDN_CHUNK
    rows = proj.shape[0]
    nchunk = rows // (bsz * c)
    assert conv_dim % nv == 0 and (conv_dim + nv) % sc == 0 and n_vh % 4 == 0 and 2 * n_vh <= LANE
    zb = conv_dim // nv
    gbb = (conv_dim + nv) // sc
    abb = (conv_dim + nv + 3 * sc) // LANE
    pad_lane = lambda v: jnp.pad(v.reshape(1, -1), ((0, 0), (0, LANE - v.shape[-1])))
    cst = jnp.pad(conv_state.astype(F32), ((0, 0), (SUBLANE - (n_conv - 1), 0), (0, 0)))
    scst = jnp.pad(sc_state.astype(F32), ((0, 0), (SUBLANE - (n_sc - 1), 0), (0, 0)))
    row = lambda b, n: b * nchunk + n
    kern = functools.partial(_gdn_kernel, t_valid=t_valid, n_kh=n_kh, n_vh=n_vh, dk=dk, dv=dv)
    return pl.pallas_call(
        kern,
        grid=(bsz, nchunk),
        in_specs=[pl.BlockSpec((c, conv_dim), lambda b, n: (row(b, n), 0)),
                  pl.BlockSpec((c, nv), lambda b, n: (row(b, n), zb)),
                  pl.BlockSpec((c, LANE), lambda b, n: (row(b, n), abb)),
                  pl.BlockSpec((c, sc), lambda b, n: (row(b, n), gbb)),
                  pl.BlockSpec((c, sc), lambda b, n: (row(b, n), gbb + 1)),
                  pl.BlockSpec((c, sc), lambda b, n: (row(b, n), gbb + 2)),
                  pl.BlockSpec((None, SUBLANE, conv_dim), lambda b, n: (b, 0, 0)),
                  pl.BlockSpec((None, SUBLANE, sc), lambda b, n: (b, 0, 0)),
                  pl.BlockSpec((n_conv, conv_dim), lambda b, n: (0, 0)),
                  pl.BlockSpec((n_sc, sc), lambda b, n: (0, 0)),
                  pl.BlockSpec((1, LANE), lambda b, n: (0, 0)),
                  pl.BlockSpec((1, LANE), lambda b, n: (0, 0)),
                  pl.BlockSpec((1, dv), lambda b, n: (0, 0)),
                  pl.BlockSpec((None, n_vh, dk, dv), lambda b, n: (b, 0, 0, 0))],
        out_specs=[pl.BlockSpec((c, nv + sc), lambda b, n: (row(b, n), 0)),
                   pl.BlockSpec((None, n_vh, dk, dv), lambda b, n: (b, 0, 0, 0)),
                   pl.BlockSpec((None, c, sc), lambda b, n: (b, 0, 0))],
        out_shape=[jax.ShapeDtypeStruct((rows, nv + sc), BF16),
                   jax.ShapeDtypeStruct((bsz, n_vh, dk, dv), F32),
                   jax.ShapeDtypeStruct((bsz, c, sc), F32)],
        scratch_shapes=[pltpu.VMEM((c + SUBLANE, conv_dim), F32),
                        pltpu.VMEM((c + SUBLANE, sc), F32),
                        pltpu.VMEM((n_vh, dk, dv), F32)],
        compiler_params=_params(("arbitrary", "arbitrary")),
        name="gdn_sconv",
    )(proj, proj, proj, proj, proj, proj, cst, scst, conv_w, sc_w,
      pad_lane(a_log), pad_lane(dt_bias), norm_g.reshape(1, dv), s0.astype(F32))


def _cmp_prompt_kernel(kv_ref, pw_ref, pj_ref, o_ref, *, blk):
    t, hd = kv_ref.shape
    x = kv_ref[...].reshape(t // blk, blk, hd) * pw_ref[...][None]
    o_ref[...] = _bdot(jnp.sum(x, axis=1), pj_ref[...])


def _cmp_prompt(proj, bsz, t, pw, pj, col0):
    nch, blk, hd = pw.shape
    return pl.pallas_call(
        functools.partial(_cmp_prompt_kernel, blk=blk),
        grid=(bsz, nch),
        in_specs=[pl.BlockSpec((t, hd), lambda b, ch: (b, col0 + ch)),
                  pl.BlockSpec((None, blk, hd), lambda b, ch: (ch, 0, 0)),
                  pl.BlockSpec((None, hd, hd), lambda b, ch: (ch, 0, 0))],
        out_specs=pl.BlockSpec((None, None, t // blk, hd), lambda b, ch: (b, ch, 0, 0)),
        out_shape=jax.ShapeDtypeStruct((bsz, nch, t // blk, hd), F32),
        compiler_params=_params(("arbitrary", "arbitrary")),
        name="cmp_prompt",
    )(proj, pw, pj)


_NT = (((1,), (1,)), ((), ()))


def _slope_features(sl2, lane):
    hi = sl2.astype(BF16).astype(F32)
    lo = sl2 - hi
    return jnp.where(lane == 0, 64.0 * hi, jnp.where(lane == 1, hi, jnp.where(lane == 2, 64.0 * lo,
                                                                             jnp.where(lane == 3, lo, 0.0))))


def _pool_blocks(page_refs, pw_ref, o_ref):
    pw = pw_ref[...]
    rows_blk = pw.shape[0]
    per_page = page_refs[0].shape[0] // rows_blk
    nch = o_ref.shape[0] // (len(page_refs) * per_page)
    for k, page_ref in enumerate(page_refs):
        for hb in range(per_page):
            r = k * per_page + hb
            x = page_ref[hb * rows_blk:(hb + 1) * rows_blk, :] * pw
            o_ref[r * nch:(r + 1) * nch, :] = jnp.sum(x.reshape(rows_blk // nch, nch, x.shape[-1]), axis=0)


def _attn_prompt_kernel(*refs, tq, tk, wk, blk, n_g, n_kv, hd, n_pool):
    if n_pool:
        refs = refs[1:]
    q_ref, kc_ref, vc_ref, ks_ref, vs_ref, kw_ref, vw_ref, gt_ref, sl_ref, pos_ref, oh_ref = refs[:11]
    o_ref = refs[11 + n_pool + (1 if n_pool else 0)]
    if n_pool:
        _pool_blocks(refs[11:11 + n_pool], refs[11 + n_pool], refs[-1])
    h = pl.program_id(1)
    qt = pl.program_id(2)
    q0 = qt * tq
    nc = kc_ref.shape[0]
    t_all = ks_ref.shape[0]
    sl = sl_ref[...]
    lane = lax.broadcasted_iota(jnp.int32, (1, LANE), 1)
    qs = [q_ref[:, g * hd:(g + 1) * hd] * (hd ** -0.5) for g in range(n_g)]
    qb = jnp.concatenate(qs, axis=0).astype(BF16)
    qaug = jnp.concatenate(
        [jnp.concatenate([(qs[g] * LOG2E).astype(BF16),
                          jnp.broadcast_to(_slope_features(sl[:, g:g + 1] * LOG2E, lane), (tq, LANE)).astype(BF16)],
                         axis=1) for g in range(n_g)], axis=0)
    qpos = q0 + lax.broadcasted_iota(jnp.int32, (tq, 1), 0)

    bidx = lax.broadcasted_iota(jnp.int32, (1, nc), 1)
    start = bidx * blk
    s_c = _bdot_nt(qb, kc_ref[...])
    dist_c = jnp.abs(qpos.astype(F32) - (start.astype(F32) + (blk - 1) / 2))
    valid_c = (start + (blk - 1)) <= qpos
    p_c = jnp.concatenate([_masked_softmax(s_c[g * tq:(g + 1) * tq] - sl[:, g:g + 1] * dist_c, valid_c)
                           for g in range(n_g)], axis=0)
    o_c = _bdot(p_c, vc_ref[...])
    imp = p_c[0:tq]
    for g in range(1, n_g):
        imp = imp + p_c[g * tq:(g + 1) * tq]

    imp_t = jnp.concatenate([imp, jnp.zeros((tq, LANE - nc), F32)], axis=1).T[0:nc, :]
    bcol = lax.broadcasted_iota(jnp.int32, (nc, 1), 0)
    cur = (q0 + lax.broadcasted_iota(jnp.int32, (1, tq), 1)) // blk
    score = jnp.where(bcol < cur, imp_t, NEG)
    score = jnp.where((bcol == 0) | (bcol == cur), FORCE, score)
    rank = jnp.zeros((nc, tq), F32)
    for i in range(nc):
        ri = score[i:i + 1, :]
        rank = rank + jnp.where((ri > score) | ((ri == score) & (bcol > i)), 1.0, 0.0)
    selneg_t = jnp.where((rank < min(SEL_TOPK, nc)) & (score > 0.5 * NEG), 0.0, NEG)
    selneg = jnp.concatenate([selneg_t, jnp.zeros((LANE - nc, tq), F32)], axis=0).T.astype(BF16)

    def flash_step(k_ref, v_ref, k0, size, bias, carry):
        m, l, acc = carry
        kaug = jnp.concatenate([k_ref[pl.ds(k0, size), :].astype(BF16), pos_ref[pl.ds(k0, size), :]], axis=1)
        s = lax.dot_general(qaug, kaug, _NT, preferred_element_type=F32)
        gs = range(n_g)
        sgs = [s[g * tq:(g + 1) * tq] + bias for g in gs]
        m2 = [jnp.maximum(m[g], jnp.max(sgs[g], axis=-1, keepdims=True)) for g in gs]
        als = [jnp.exp2(m[g] - m2[g]) for g in gs]
        pfs = [jnp.exp2(sgs[g] - m2[g]) for g in gs]
        l2 = [als[g] * l[g] + jnp.sum(pfs[g], axis=-1, keepdims=True) for g in gs]
        ps = [p.astype(BF16) for p in pfs]
        accs = [als[g] * acc[g * tq:(g + 1) * tq] for g in gs]
        pv = jnp.dot(jnp.concatenate(ps, axis=0), v_ref[pl.ds(k0, size), :].astype(BF16),
                     preferred_element_type=F32)
        return tuple(m2), tuple(l2), jnp.concatenate(accs, axis=0) + pv

    def finish(carry):
        _, l, acc = carry
        return [acc[g * tq:(g + 1) * tq] / jnp.maximum(l[g], 1e-30) for g in range(n_g)]

    init = (tuple(jnp.full((tq, 1), NEG, F32) for _ in range(n_g)),
            tuple(jnp.zeros((tq, 1), F32) for _ in range(n_g)), jnp.zeros((n_g * tq, hd), F32))

    def sel_bias(k0):
        return lax.dot_general(selneg, oh_ref[pl.ds(k0, tk), :], _NT, preferred_element_type=F32)

    def sel_chunk(ci, carry):
        k0 = pl.multiple_of(ci * tk, tk)
        return flash_step(ks_ref, vs_ref, k0, tk, sel_bias(k0), carry)

    n_full = q0 // tk
    carry = lax.fori_loop(0, n_full, sel_chunk, init)
    k0 = pl.multiple_of(n_full * tk, tk)
    kpos = k0 + lax.broadcasted_iota(jnp.int32, (1, tk), 1)
    o_s = finish(flash_step(ks_ref, vs_ref, k0, tk, sel_bias(k0) + jnp.where(kpos <= qpos, 0.0, NEG), carry))

    w0 = pl.multiple_of(jnp.clip(q0 + tq - wk, 0, t_all - wk), tq)
    dist = qpos - (w0 + lax.broadcasted_iota(jnp.int32, (1, wk), 1))
    o_w = finish(flash_step(kw_ref, vw_ref, w0, wk, jnp.where((dist >= 0) & (dist < WINDOW), 0.0, NEG), init))

    gt = jax.nn.sigmoid(pltpu.roll(gt_ref[...], (LANE - n_g * h) % LANE, 1))
    n_h = n_g * n_kv
    for g in range(n_g):
        r = slice(g * tq, (g + 1) * tq)
        o = (gt[:, g:g + 1] * o_c[r] + gt[:, n_h + g:n_h + g + 1] * o_s[g]
             + gt[:, 2 * n_h + g:2 * n_h + g + 1] * o_w[g])
        o_ref[:, g * hd:(g + 1) * hd] = o.astype(o_ref.dtype)


def _attn_prompt_tiles(t):
    return 128, t // 128


def _attn_prompt(proj, cmp, slopes, bsz, t, n_h, n_kv, hd, blk, pool=None):
    n_g = n_h // n_kv
    tq, nqt = _attn_prompt_tiles(t)
    tk = _pick(t, (512, 256, 128))
    wk = WINDOW + tq
    assert t % tq == 0 and t >= wk and t // blk <= LANE and blk == 64 and n_h * 3 <= LANE
    nc = t // blk
    kvb = n_h
    pos = jnp.arange(t, dtype=jnp.int32)[:, None]
    lane = jnp.arange(LANE, dtype=jnp.int32)[None, :]
    pos_tab = jnp.where(lane < 4, jnp.where(lane % 2 == 0, pos // 64, pos % 64), 0).astype(BF16)
    onehot = (lane == pos // blk).astype(BF16)
    gate_blk = (n_h * hd + 6 * n_kv * hd) // LANE
    n_pool = pool[5] if pool else 0
    kern = functools.partial(_attn_prompt_kernel, tq=tq, tk=tk, wk=wk, blk=blk, n_g=n_g, n_kv=n_kv, hd=hd,
                             n_pool=n_pool)
    kv_spec = lambda off: pl.BlockSpec((t, hd), lambda b, h, q, *_: (b, kvb + off + h))
    in_specs = [pl.BlockSpec((tq, n_g * hd), lambda b, h, q, *_: (b * nqt + q, h)),
                pl.BlockSpec((None, None, nc, hd), lambda b, h, q, *_: (b, h, 0, 0)),
                pl.BlockSpec((None, None, nc, hd), lambda b, h, q, *_: (b, n_kv + h, 0, 0)),
                kv_spec(2 * n_kv), kv_spec(3 * n_kv), kv_spec(4 * n_kv), kv_spec(5 * n_kv),
                pl.BlockSpec((tq, LANE), lambda b, h, q, *_: (b * nqt + q, gate_blk)),
                pl.BlockSpec((None, 1, LANE), lambda b, h, q, *_: (h, 0, 0)),
                pl.BlockSpec((t, LANE), lambda b, h, q, *_: (0, 0)),
                pl.BlockSpec((t, LANE), lambda b, h, q, *_: (0, 0))]
    out_specs = [pl.BlockSpec((tq, n_g * hd), lambda b, h, q, *_: (b * nqt + q, h))]
    out_shape = [jax.ShapeDtypeStruct((bsz * t, n_h * hd), BF16)]
    operands = [proj, cmp, cmp, proj, proj, proj, proj, proj, slopes, pos_tab, onehot]
    prefetch = []
    if pool:
        cache, layer, page_table, pw_rows, nch, _ = pool
        _, _, prow, _ = cache.shape
        bd, n_pages = page_table.shape
        rows_blk = pw_rows.shape[0]
        assert bsz * n_kv == bd and nqt * n_pool == n_pages
        out_rows = n_pool * (prow // rows_blk) * nch
        for k in range(n_pool):
            in_specs.append(pl.BlockSpec((None, None, prow, hd),
                                         lambda b, h, q, pt, k=k: (layer, pt[b * n_kv + h, q * n_pool + k], 0, 0)))
        in_specs.append(pl.BlockSpec((rows_blk, hd), lambda b, h, q, pt: (0, 0)))
        out_specs.append(pl.BlockSpec((None, out_rows, hd), lambda b, h, q, pt: (b * n_kv + h, q, 0)))
        out_shape.append(jax.ShapeDtypeStruct((bd, nqt * out_rows, hd), F32))
        operands += [cache] * n_pool + [pw_rows]
        prefetch = [page_table]
    outs = pl.pallas_call(
        kern,
        grid_spec=pltpu.PrefetchScalarGridSpec(num_scalar_prefetch=len(prefetch), grid=(bsz, n_kv, nqt),
                                               in_specs=in_specs, out_specs=out_specs),
        out_shape=out_shape,
        compiler_params=_params(("arbitrary", "arbitrary", "arbitrary")),
        name="attn_prompt",
    )(*prefetch, *operands)
    return outs if pool else outs[0]


def _pool_pages_kernel(pt_ref, *refs, n_pp, rows_blk):
    _pool_blocks(refs[:n_pp], refs[n_pp], refs[n_pp + 1])


def _page_map(b, p, pt, *, layer, k, n_pp):
    return (layer, pt[b, p * n_pp + k], 0, 0)


def _pool_pages(cache, layer, page_table, pw_rows, nch):
    _, _, prow, hd = cache.shape
    bd, n_pages = page_table.shape
    rows_blk = pw_rows.shape[0]
    n_pp = _pick(n_pages, (8, 4, 2, 1))
    per_page = prow // rows_blk
    in_specs = [pl.BlockSpec((None, None, prow, hd),
                             functools.partial(_page_map, layer=layer, k=k, n_pp=n_pp)) for k in range(n_pp)]
    in_specs.append(pl.BlockSpec((rows_blk, hd), lambda b, p, pt: (0, 0)))
    return pl.pallas_call(
        functools.partial(_pool_pages_kernel, n_pp=n_pp, rows_blk=rows_blk),
        grid_spec=pltpu.PrefetchScalarGridSpec(
            num_scalar_prefetch=1, grid=(bd, n_pages // n_pp), in_specs=in_specs,
            out_specs=pl.BlockSpec((None, n_pp * per_page * nch, hd), lambda b, p, pt: (b, p, 0))),
        out_shape=jax.ShapeDtypeStruct((bd, n_pages * per_page * nch, hd), F32),
        compiler_params=_params(("arbitrary", "arbitrary")),
        name="pool_pages",
    )(page_table, *([cache] * n_pp), pw_rows)


def _attn_sample_a_kernel(q_ref, pooled_ref, pj_ref, kvw_ref, gt_ref, sl_ref, ocw_ref, selm_ref,
                          score_buf, *, past, n_q, n_kv, n_g, hd, blk, wb, nbp):
    nch = 2 * n_kv
    nc = pooled_ref.shape[0] // nch
    n_blocks = (past + n_q + blk - 1) // blk
    kvw_rows = kvw_ref.shape[0]
    half = n_kv * hd
    rep = lambda x: jnp.concatenate([x] * n_g, axis=0)
    sl = sl_ref[...]
    gt = jax.nn.sigmoid(gt_ref[...])
    qpos = past + lax.broadcasted_iota(jnp.int32, (n_q, 1), 0)
    qpos_r = rep(qpos)
    bidx_c = lax.broadcasted_iota(jnp.int32, (1, nc), 1)
    start = bidx_c * blk
    bidx = lax.broadcasted_iota(jnp.int32, (1, nbp), 1)
    cur = qpos // blk
    score_buf[...] = jnp.full(score_buf.shape, DEAD, F32)

    for h in range(n_kv):
        qb = jnp.concatenate([q_ref[:, (h * n_g + g) * hd:(h * n_g + g + 1) * hd] for g in range(n_g)], axis=0)
        qb = (qb * (hd ** -0.5)).astype(BF16)
        slope = jnp.concatenate(
            [jnp.broadcast_to(sl[:, h * n_g + g:h * n_g + g + 1], (n_q, 1)) for g in range(n_g)], axis=0)
        kc = _bdot(pooled_ref[pl.ds(h, nc, stride=nch), :], pj_ref[h])
        vc = _bdot(pooled_ref[pl.ds(n_kv + h, nc, stride=nch), :], pj_ref[n_kv + h])
        s = _bdot_nt(qb, kc)
        dist_c = qpos_r.astype(F32) - (start.astype(F32) + (blk - 1) / 2)
        s = s - slope * jnp.abs(dist_c)
        p_c = _masked_softmax(s, (start + (blk - 1)) <= qpos_r)
        o_c = _bdot(p_c, vc)
        imp = p_c[0:n_q]
        for g in range(1, n_g):
            imp = imp + p_c[g * n_q:(g + 1) * n_q]
        imp = jnp.concatenate([imp, jnp.zeros((n_q, nbp - nc), F32)], axis=1)
        score = jnp.where(bidx < cur, imp, NEG)
        score = jnp.where((bidx == 0) | (bidx == cur), FORCE, score)
        score = jnp.where(bidx < n_blocks, score, DEAD)
        score_buf[h * n_q:(h + 1) * n_q, :] = score

        kw = kvw_ref[:, h * hd:(h + 1) * hd]
        vw = kvw_ref[:, half + h * hd:half + (h + 1) * hd]
        kidx = lax.broadcasted_iota(jnp.int32, (1, kvw_rows), 1)
        dist = qpos_r - (past - wb + kidx)
        valid = (dist >= 0) & (dist < WINDOW) & (kidx < wb + n_q)
        sw = _bdot_nt(qb, kw) - slope * jnp.abs(dist).astype(F32)
        o_w = _bdot(_masked_softmax(sw, valid), vw)
        for g in range(n_g):
            hh = h * n_g + g
            r = slice(g * n_q, (g + 1) * n_q)
            ocw_ref[:, hh * hd:(hh + 1) * hd] = (gt[:, hh:hh + 1] * o_c[r]
                                                 + gt[:, 2 * n_kv * n_g + hh:2 * n_kv * n_g + hh + 1] * o_w[r])

    sc_all = score_buf[...]
    sc_t = sc_all.T
    ii = lax.broadcasted_iota(jnp.int32, (nbp, 1), 0)
    for h in range(n_kv):
        for t in range(n_q):
            r = h * n_q + t
            col = sc_t[:, r:r + 1]
            row = sc_all[r:r + 1, :]
            before = (col > row) | ((col == row) & (ii < bidx))
            rank = jnp.sum(jnp.where(before, 1.0, 0.0), axis=0, keepdims=True)
            sel = jnp.where((rank < min(SEL_TOPK, n_blocks)) & (row > 0.5 * NEG), 1.0, 0.0)
            for g in range(n_g):
                selm_ref[h, g * n_q + t:g * n_q + t + 1, :] = sel


def _attn_sample_a(proj3, pooled, pj, kv_all, slopes_row, past, n_h, n_kv, hd, blk, wb):
    bd, n_q, _ = proj3.shape
    n_g = n_h // n_kv
    n_blocks = (past + n_q + blk - 1) // blk
    nbp = pl.cdiv(n_blocks, LANE) * LANE
    kvw_rows = kv_all.shape[1]
    qw = n_h * hd
    assert n_kv * n_q <= LANE
    kern = functools.partial(_attn_sample_a_kernel, past=past, n_q=n_q, n_kv=n_kv, n_g=n_g, hd=hd, blk=blk,
                             wb=wb, nbp=nbp)
    return pl.pallas_call(
        kern,
        grid=(bd,),
        in_specs=[pl.BlockSpec((None, n_q, qw), lambda b: (b, 0, 0)),
                  pl.BlockSpec((None, pooled.shape[1], hd), lambda b: (b, 0, 0)),
                  pl.BlockSpec((2 * n_kv, hd, hd), lambda b: (0, 0, 0)),
                  pl.BlockSpec((None, kvw_rows, 2 * n_kv * hd), lambda b: (b, 0, 0)),
                  pl.BlockSpec((None, n_q, LANE), lambda b: (b, 0, (qw + 6 * n_kv * hd) // LANE)),
                  pl.BlockSpec((1, LANE), lambda b: (0, 0))],
        out_specs=[pl.BlockSpec((None, n_q, qw), lambda b: (b, 0, 0)),
                   pl.BlockSpec((None, n_kv, n_g * n_q, nbp), lambda b: (b, 0, 0, 0))],
        out_shape=[jax.ShapeDtypeStruct((bd, n_q, qw), F32),
                   jax.ShapeDtypeStruct((bd, n_kv, n_g * n_q, nbp), F32)],
        scratch_shapes=[pltpu.VMEM((LANE, nbp), F32)],
        compiler_params=_params(("arbitrary",)),
        name="attn_sample_a",
    )(proj3, pooled, pj, kv_all, proj3, slopes_row)


def _attn_sample_sel_kernel(pt_ref, *refs, n_pp, past, n_q, n_kv, n_g, hd, blk):
    pages = refs[:n_pp]
    q_ref, selm_ref, new_ref, ocw_ref, gt_ref, sl_ref, o_ref, m_ref, l_ref, acc_ref = refs[n_pp:]
    p = pl.program_id(1)
    nch = 2 * n_kv
    page = pages[0].shape[0] // nch
    keys = n_pp * page
    nbp = selm_ref.shape[-1]
    rows = n_g * n_q
    half = n_kv * hd
    rep = lambda x: jnp.concatenate([x] * n_g, axis=0)
    sl = sl_ref[...]
    qpos_r = rep(past + lax.broadcasted_iota(jnp.int32, (n_q, 1), 0))

    @pl.when(p == 0)
    def _():
        m_ref[...] = jnp.full(m_ref.shape, NEG, F32)
        l_ref[...] = jnp.zeros_like(l_ref)
        acc_ref[...] = jnp.zeros_like(acc_ref)

    def heads():
        for h in range(n_kv):
            qb = jnp.concatenate([q_ref[:, (h * n_g + g) * hd:(h * n_g + g + 1) * hd] for g in range(n_g)],
                                 axis=0) * (hd ** -0.5)
            slope = jnp.concatenate(
                [jnp.broadcast_to(sl[:, h * n_g + g:h * n_g + g + 1], (n_q, 1)) for g in range(n_g)], axis=0)
            yield h, qb, slope

    def update(h, sc, mask, pv_fn):
        sc = jnp.where(mask, sc, NEG)
        m = m_ref[h]
        m_new = jnp.maximum(m, jnp.max(sc, axis=-1, keepdims=True))
        alpha = jnp.exp(m - m_new)
        pr = jnp.where(mask, jnp.exp(sc - m_new), 0.0)
        l_ref[h] = alpha * l_ref[h] + jnp.sum(pr, axis=-1, keepdims=True)
        acc_ref[h] = alpha * acc_ref[h] + pv_fn(pr)
        m_ref[h] = m_new

    k0 = p * keys
    kk = lax.broadcasted_iota(jnp.int32, (nbp, keys), 1)
    nn = lax.broadcasted_iota(jnp.int32, (nbp, keys), 0)
    expand = jnp.where(nn == (k0 + kk) // blk, 1.0, 0.0).astype(BF16)
    dist = qpos_r - (k0 + lax.broadcasted_iota(jnp.int32, (1, keys), 1))
    staged = []
    for h, qb, slope in heads():
        mask = (jnp.dot(selm_ref[h].astype(BF16), expand, preferred_element_type=F32) > 0.5) & (dist >= 0)
        kx = jnp.concatenate([pages[k][pl.ds(h, page, stride=nch), :].astype(BF16) for k in range(n_pp)], axis=0)
        sc = jnp.where(mask, _bdot_nt(qb, kx) - slope * dist.astype(F32), NEG)
        staged.append((h, mask, sc))
    probs = []
    for h, mask, sc in staged:
        m = m_ref[h]
        m_new = jnp.maximum(m, jnp.max(sc, axis=-1, keepdims=True))
        alpha = jnp.exp(m - m_new)
        pr = jnp.where(mask, jnp.exp(sc - m_new), 0.0)
        l_ref[h] = alpha * l_ref[h] + jnp.sum(pr, axis=-1, keepdims=True)
        m_ref[h] = m_new
        probs.append((h, alpha, pr))
    for h, alpha, pr in probs:
        vx = jnp.concatenate([pages[k][pl.ds(n_kv + h, page, stride=nch), :].astype(BF16) for k in range(n_pp)],
                             axis=0)
        acc_ref[h] = alpha * acc_ref[h] + _bdot(pr, vx)

    @pl.when(p == pl.num_programs(1) - 1)
    def _():
        gt = jax.nn.sigmoid(gt_ref[...])
        for h, qb, slope in heads():
            nb0 = past // blk
            seln = selm_ref[h][:, nb0:nb0 + 1] > 0.5
            qr = qb.astype(BF16).astype(F32)
            for j in range(n_q):
                kj = new_ref[j:j + 1, h * hd:(h + 1) * hd].astype(BF16).astype(F32)
                vj = new_ref[j:j + 1, half + h * hd:half + (h + 1) * hd].astype(BF16).astype(F32)
                dj = qpos_r - (past + j)
                sc = jnp.sum(qr * kj, axis=-1, keepdims=True) - slope * dj.astype(F32)
                update(h, sc, seln & (dj >= 0), lambda pr, vj=vj: pr.astype(BF16).astype(F32) * vj)
            o_s = acc_ref[h] / jnp.maximum(l_ref[h], 1e-30)
            for g in range(n_g):
                hh = h * n_g + g
                r = slice(g * n_q, (g + 1) * n_q)
                gcol = n_kv * n_g + hh
                o = ocw_ref[:, hh * hd:(hh + 1) * hd] + gt[:, gcol:gcol + 1] * o_s[r]
                o_ref[:, hh * hd:(hh + 1) * hd] = o.astype(o_ref.dtype)


def _attn_sample_sel(cache, layer, page_table, proj3, selm, new_rows, ocw, slopes_row, past, n_h, n_kv, hd, blk):
    _, _, prow, _ = cache.shape
    bd, n_pages = page_table.shape
    n_q = proj3.shape[1]
    n_g = n_h // n_kv
    qw = n_h * hd
    nbp = selm.shape[-1]
    n_pp = _pick(n_pages, (8, 4, 2, 1))
    rows = n_g * n_q
    in_specs = [pl.BlockSpec((None, None, prow, hd),
                             functools.partial(_page_map, layer=layer, k=k, n_pp=n_pp)) for k in range(n_pp)]
    in_specs += [pl.BlockSpec((None, n_q, qw), lambda b, p, pt: (b, 0, 0)),
                 pl.BlockSpec((None, n_kv, rows, nbp), lambda b, p, pt: (b, 0, 0, 0)),
                 pl.BlockSpec((None, new_rows.shape[1], new_rows.shape[2]), lambda b, p, pt: (b, 0, 0)),
                 pl.BlockSpec((None, n_q, qw), lambda b, p, pt: (b, 0, 0)),
                 pl.BlockSpec((None, n_q, LANE), lambda b, p, pt: (b, 0, (qw + 6 * n_kv * hd) // LANE)),
                 pl.BlockSpec((1, LANE), lambda b, p, pt: (0, 0))]
    kern = functools.partial(_attn_sample_sel_kernel, n_pp=n_pp, past=past, n_q=n_q, n_kv=n_kv, n_g=n_g,
                             hd=hd, blk=blk)
    return pl.pallas_call(
        kern,
        grid_spec=pltpu.PrefetchScalarGridSpec(
            num_scalar_prefetch=1, grid=(bd, n_pages // n_pp), in_specs=in_specs,
            out_specs=pl.BlockSpec((None, n_q, qw), lambda b, p, pt: (b, 0, 0)),
            scratch_shapes=[pltpu.VMEM((n_kv, rows, 1), F32), pltpu.VMEM((n_kv, rows, 1), F32),
                            pltpu.VMEM((n_kv, rows, hd), F32)]),
        out_shape=jax.ShapeDtypeStruct((bd, n_q, qw), BF16),
        compiler_params=_params(("arbitrary", "arbitrary")),
        name="attn_sample_sel",
    )(page_table, *([cache] * n_pp), proj3, selm, new_rows, ocw, proj3, slopes_row)


def _alibi(n_h):
    h = jnp.arange(1, n_h + 1, dtype=F32)
    return jnp.exp2(-8.0 * h / n_h)


def kernel(x_prompt, x_sample, c_prompt, c_sample, state_gdn, state_gdn_conv, state_sconv, cache_kv_cmp,
           cache_kv_sel, state_kv_win, page_table, norm_g, w_ada, b_ada, w_in_e, conv_w_gdn, a_log, dt_bias,
           gdn_norm_g, conv_w_sc, w_out_e, w_in_o, cmp_pool, cmp_proj, w_out_o, w_mlp1, w_mlp2, final_g):
    bsz, seq, d = x_prompt.shape
    bd, n_q, _ = x_sample.shape
    depth = norm_g.shape[0]
    n_vh, dk, dv = state_gdn.shape[2:]
    conv_dim = conv_w_gdn.shape[2]
    sc_dim = conv_w_sc.shape[2]
    nv = n_vh * dv
    blk, _, n_kv = cmp_pool.shape[1:]
    hd = cmp_proj.shape[-1]
    n_h = w_out_o.shape[1] // hd
    kv_row = 2 * n_kv * hd
    qw = n_h * hd
    page = cache_kv_cmp.shape[2]
    past = page_table.shape[1] * page
    wb = state_kv_win.shape[2]
    c = GDN_CHUNK
    assert seq % c == 0 and seq % blk == 0 and seq >= WINDOW and n_q <= c and n_q < blk and past % blk == 0
    assert n_q >= conv_w_gdn.shape[1] - 1 and hd == LANE and dk == LANE and dv == LANE

    rows_c = -(-(bsz + bd) // SUBLANE) * SUBLANE
    c_all = jnp.pad(jnp.concatenate([c_prompt, c_sample], axis=0).astype(F32), ((0, rows_c - bsz - bd), (0, 0)))
    mod = _ada_mod(c_all, w_ada, b_ada).reshape(depth, rows_c, 6, d)

    def mods(l, sample):
        if sample:
            return [jnp.repeat(mod[l, bsz:bsz + bd, j], n_q, axis=0)[None] for j in range(6)]
        return [mod[l, :bsz, j][:, None, :] for j in range(6)]

    slopes = _alibi(n_h)
    slopes_kv = jnp.pad(slopes.reshape(n_kv, 1, n_h // n_kv), ((0, 0), (0, 0), (0, LANE - n_h // n_kv)))
    slopes_row = jnp.pad(slopes.reshape(1, n_h), ((0, 0), (0, LANE - n_h)))
    nch = 2 * n_kv
    cache_cmp = cache_kv_cmp.reshape(cache_kv_cmp.shape[:2] + (page * nch, hd))
    cache_sel = cache_kv_sel.reshape(cache_kv_sel.shape[:2] + (page * nch, hd))

    w_in_e_t = jnp.swapaxes(w_in_e, 1, 2).astype(BF16)
    w_in_o_t = jnp.swapaxes(w_in_o, 1, 2).astype(BF16)

    def even_tail(i):
        o_ab = conv_dim + nv
        o_rest = o_ab + 2 * n_vh
        return jnp.concatenate([w_in_e_t[i, o_rest:], w_in_e_t[i, o_ab:o_rest],
                                jnp.zeros((LANE - 2 * n_vh, d), w_in_e_t.dtype)], axis=0)

    pooled_by_layer = {}

    def run(x3, sample):
        nb, t, _ = x3.shape
        x = x3.reshape(nb * t, d).astype(F32)
        ev, od = [], []
        for l in range(depth):
            i = l // 2
            sh1, sc1, g1, sh2, sc2, g2 = mods(l, sample)
            if l % 2 == 0:
                proj = _norm_mod_matmul(x, norm_g[l, 0], sc1, sh1, w_in_e_t, i, main_cols=conv_dim + nv,
                                        w_tail=even_tail(i))
                n_cst = conv_w_gdn.shape[1] - 1
                conv_new = proj.reshape(nb, t, -1)[:, t - n_cst:, :conv_dim]
                if sample:
                    projp = jnp.pad(proj.reshape(nb, t, -1), ((0, 0), (0, c - t), (0, 0))).reshape(nb * c, -1)
                    cst, scst, s0, tv = state_gdn_conv[i], state_sconv[i], state_gdn[i], t
                else:
                    projp = proj
                    cst = jnp.zeros((nb, conv_w_gdn.shape[1] - 1, conv_dim), F32)
                    scst = jnp.zeros((nb, conv_w_sc.shape[1] - 1, sc_dim), F32)
                    s0, tv = jnp.zeros((nb, n_vh, dk, dv), F32), c
                mix, s_new, u_last = _gdn_sconv(projp, cst, scst, s0, conv_w_gdn[i], conv_w_sc[i], a_log[i],
                                                dt_bias[i], gdn_norm_g[i], tv)
                if sample:
                    mix = mix.reshape(nb, c, -1)[:, :t].reshape(nb * t, -1)
                sc_new = u_last[:, tv - scst.shape[1]:tv]
                ev.append((s_new, conv_new, sc_new))
                x = _matmul_gated_residual(mix, w_out_e, i, x, g1)
            else:
                proj, kvc8, kvs8, kvw8 = _norm_mod_matmul(x, norm_g[l, 0], sc1, sh1, w_in_o_t, i,
                                                          rows_out=(qw, 3, nch, hd))
                kv_c, kv_s, kv_w = [a.reshape(nb, t, 2, n_kv, hd) for a in (kvc8, kvs8, kvw8)]
                pj = cmp_proj[i].reshape(2 * n_kv, hd, hd)
                pw_rows = jnp.broadcast_to(cmp_pool[i].reshape(blk * nch, 1), (blk * nch, hd))
                if sample:
                    pooled = pooled_by_layer.get(i)
                    if pooled is None:
                        pooled = _pool_pages(cache_cmp, i, page_table, pw_rows, nch)
                    proj3 = proj.reshape(nb, t, -1)
                    kv_all = jnp.concatenate([state_kv_win[i].reshape(nb, wb, kv_row).astype(F32),
                                              proj3[:, :, qw + 2 * kv_row:qw + 3 * kv_row]], axis=1)
                    kv_all_p = jnp.pad(kv_all, ((0, 0), (0, (-kv_all.shape[1]) % SUBLANE), (0, 0)))
                    ocw, selm = _attn_sample_a(proj3, pooled, pj, kv_all_p, slopes_row, past, n_h, n_kv, hd, blk, wb)
                    new_rows = jnp.pad(proj3[:, :, qw + kv_row:qw + 2 * kv_row], ((0, 0), (0, (-t) % SUBLANE), (0, 0)))
                    o = _attn_sample_sel(cache_sel, i, page_table, proj3, selm, new_rows, ocw, slopes_row, past,
                                         n_h, n_kv, hd, blk).reshape(nb * t, qw)
                    kv_win_new = kv_all[:, t:].reshape(nb, wb, 2, n_kv, hd)
                else:
                    pw = jnp.broadcast_to(cmp_pool[i].reshape(blk, 2 * n_kv).T[:, :, None], (2 * n_kv, blk, hd))
                    cmp = _cmp_prompt(proj, nb, t, pw, pj, n_h)
                    n_steps = nb * n_kv * _attn_prompt_tiles(t)[1]
                    n_pool = (bd * page_table.shape[1]) // n_steps
                    if nb * n_kv == bd and n_pool * n_steps == bd * page_table.shape[1] and 1 <= n_pool <= 8:
                        o, pooled_by_layer[i] = _attn_prompt(proj, cmp, slopes_kv, nb, t, n_h, n_kv, hd, blk,
                                                             pool=(cache_cmp, i, page_table, pw_rows, nch, n_pool))
                    else:
                        o = _attn_prompt(proj, cmp, slopes_kv, nb, t, n_h, n_kv, hd, blk)
                    kv_win_new = kv_w[:, t - min(WINDOW, t):]
                od.append((kv_c, kv_s, kv_win_new))
                x = _matmul_gated_residual(o, w_out_o, i, x, g1)
            x = _mlp(x, norm_g[l, 1], sc2, sh2, g2, w_mlp1, w_mlp2, l, final_g if l == depth - 1 else None)
        y = x.reshape(nb, t, d)
        evs = [jnp.stack([s[j] for s in ev]) for j in range(3)]
        ods = [jnp.stack([s[j] for s in od]) for j in range(3)]
        return y, evs, ods

    y_p, (gdn_p, gconv_p, sconv_p), (kvc_p, kvs_p, kvw_p) = run(x_prompt, False)
    y_s, (gdn_s, gconv_s, sconv_s), (kvc_s, kvs_s, kvw_s) = run(x_sample, True)
    return (y_p, y_s, gdn_p, gdn_s, gconv_p, gconv_s, sconv_p, sconv_s, kvc_p, kvc_s, kvs_p, kvs_s, kvw_p, kvw_s)
```

```python
import functools

import jax
import jax.numpy as jnp
from jax import lax
from jax.experimental import pallas as pl
from jax.experimental.pallas import tpu as pltpu

F32 = jnp.float32
BF16 = jnp.bfloat16

EPS = 1e-6
NEG = -1e30
FORCE = 1e6
DEAD = -3e38
SEL_TOPK = 16
LOG2E = 1.4426950408889634
WINDOW = 512
GDN_CHUNK = 64
LANE = 128
SUBLANE = 8
VMEM_BUDGET = 56 * 1024 * 1024


def _params(sem):
    return pltpu.CompilerParams(dimension_semantics=sem, vmem_limit_bytes=VMEM_BUDGET)


def _pick(n, cands):
    for c in cands:
        if n % c == 0:
            return c
    return n


def _bdot(a, b):
    return jnp.dot(a.astype(BF16), b.astype(BF16), preferred_element_type=F32)


def _bdot_nt(a, b):
    return lax.dot_general(a.astype(BF16), b.astype(BF16), (((1,), (1,)), ((), ())),
                           preferred_element_type=F32)


def _silu(x):
    return x * jax.nn.sigmoid(x)


def _modnorm(x, g, scale, shift):
    y = x * lax.rsqrt(jnp.mean(x * x, axis=-1, keepdims=True) + EPS)
    return (y * g) * (1.0 + scale) + shift


def _modnorm_rows(x_ref, g_ref, sc_ref, sh_ref, hn_ref):
    tm = x_ref.shape[0]
    ch = min(tm, 256)

    def body(r, carry):
        rows = pl.ds(pl.multiple_of(r * ch, ch), ch)
        sc = sc_ref[...] if sc_ref.shape[0] == 1 else sc_ref[rows, :]
        sh = sh_ref[...] if sh_ref.shape[0] == 1 else sh_ref[rows, :]
        hn_ref[rows, :] = _modnorm(x_ref[rows, :], g_ref[...], sc, sh).astype(BF16)
        return carry

    lax.fori_loop(0, tm // ch, body, 0)


def _masked_softmax(s, valid):
    s = jnp.where(valid, s, NEG)
    m = jnp.max(s, axis=-1, keepdims=True)
    p = jnp.where(valid, jnp.exp(s - m), 0.0)
    return p / jnp.maximum(jnp.sum(p, axis=-1, keepdims=True), 1e-30)


def _ada_kernel(c_ref, w_ref, b_ref, o_ref):
    o_ref[...] = _bdot(_silu(c_ref[...]), w_ref[...]) + b_ref[...]


def _ada_mod(c_all, w_ada, b_ada):
    depth, d, n6 = w_ada.shape
    rows = c_all.shape[0]
    tn = _pick(n6, (1024, 512, 256, 128))
    return pl.pallas_call(
        _ada_kernel,
        grid=(depth, n6 // tn),
        in_specs=[pl.BlockSpec((rows, d), lambda l, j: (0, 0)),
                  pl.BlockSpec((None, d, tn), lambda l, j: (l, 0, j)),
                  pl.BlockSpec((None, 1, tn), lambda l, j: (l, 0, j))],
        out_specs=pl.BlockSpec((None, rows, tn), lambda l, j: (l, 0, j)),
        out_shape=jax.ShapeDtypeStruct((depth, rows, n6), F32),
        compiler_params=_params(("arbitrary", "arbitrary")),
        name="ada_mod",
    )(c_all, w_ada, b_ada.reshape(depth, 1, n6))


def _mm1_kernel(x_ref, g_ref, sc_ref, sh_ref, *refs, n_main, n_w, rows_first, n_rows_out, nch):
    w_refs = refs[:n_w]
    o_ref = refs[n_w]
    row_refs = refs[n_w + 1:n_w + 1 + n_rows_out]
    hn_ref = refs[-1]
    j = pl.program_id(1)

    @pl.when(j == 0)
    def _():
        _modnorm_rows(x_ref, g_ref, sc_ref, sh_ref, hn_ref)

    def emit(wt_ref):
        y = lax.dot_general(hn_ref[...], wt_ref[...].astype(BF16), (((1,), (1,)), ((), ())),
                            preferred_element_type=F32)
        o_ref[...] = y
        return y

    if n_w == 1:
        y = emit(w_refs[0])
        tm, tn = o_ref.shape
        hd = row_refs[0].shape[-1] if n_rows_out else LANE
        per_tile = tn // hd
        tiles_per_out = nch // per_tile if n_rows_out else 1
        for r in range(n_rows_out):
            for part in range(tiles_per_out):
                @pl.when(j == rows_first + r * tiles_per_out + part)
                def _(r=r, part=part):
                    for cc in range(per_tile):
                        row_refs[r][pl.ds(part * per_tile + cc, tm, stride=nch), :] = y[:, cc * hd:(cc + 1) * hd]
    else:
        @pl.when(j < n_main)
        def _():
            emit(w_refs[0])

        @pl.when(j >= n_main)
        def _():
            emit(w_refs[1])


def _mod_spec(mod, tm, rows_per_group, width, col):
    r = mod.shape[1]
    tiles = rows_per_group // tm
    if col:
        return pl.BlockSpec((None, r, width), lambda i, j: (i // tiles, 0, j))
    return pl.BlockSpec((None, r, width), lambda i, j: (i // tiles, 0, 0))


def _norm_mod_matmul(x, g, scale, shift, w, layer, main_cols=None, w_tail=None, rows_out=None):
    m, d = x.shape
    tn = 512
    groups = scale.shape[0]
    rpg = m // groups
    if w_tail is None:
        n = w.shape[1]
        n_main = pl.cdiv(n, tn)
        weights = [w]
        w_specs = [pl.BlockSpec((None, tn, d), lambda i, j: (layer, j, 0))]
    else:
        assert main_cols % tn == 0
        n = main_cols + w_tail.shape[0]
        n_main = main_cols // tn
        weights = [w, w_tail]
        w_specs = [pl.BlockSpec((None, tn, d), lambda i, j: (layer, jnp.minimum(j, n_main - 1), 0)),
                   pl.BlockSpec((tn, d), lambda i, j: (jnp.maximum(j - n_main, 0), 0))]
    n_rows_out, rows_first, nch, hd = 0, 0, 1, LANE
    out_shape = [jax.ShapeDtypeStruct((m, n), F32)]
    if rows_out is not None:
        first_col, n_rows_out, nch, hd = rows_out
        assert first_col % tn == 0 and (nch * hd) % tn == 0 and w_tail is None
        rows_first = first_col // tn
    big = n_rows_out == 0 and w.dtype == BF16
    tm = _pick(rpg, ((2048,) if big else ()) + (1024, 512, 256, 128, 64, 32, 16, 8))
    out_specs = [pl.BlockSpec((tm, tn), lambda i, j: (i, j))]
    for _ in range(n_rows_out):
        out_specs.append(pl.BlockSpec((tm * nch, hd), lambda i, j: (i, 0)))
        out_shape.append(jax.ShapeDtypeStruct((m * nch, hd), F32))
    kern = functools.partial(_mm1_kernel, n_main=n_main, n_w=len(weights), rows_first=rows_first,
                             n_rows_out=n_rows_out, nch=nch)
    outs = pl.pallas_call(
        kern,
        grid=(m // tm, pl.cdiv(n, tn)),
        in_specs=[pl.BlockSpec((tm, d), lambda i, j: (i, 0), pipeline_mode=pl.Buffered(1)),
                  pl.BlockSpec((1, d), lambda i, j: (0, 0)),
                  _mod_spec(scale, tm, rpg, d, False),
                  _mod_spec(shift, tm, rpg, d, False)] + w_specs,
        out_specs=out_specs,
        out_shape=out_shape,
        scratch_shapes=[pltpu.VMEM((tm, d), BF16)],
        compiler_params=_params(("arbitrary", "arbitrary")),
        name="norm_mod_matmul",
    )(x, g.reshape(1, d), scale, shift, *weights)
    return outs if n_rows_out else outs[0]


def _mm2_kernel(a_ref, w_ref, x_ref, gate_ref, o_ref):
    y = jnp.dot(a_ref[...], w_ref[...].astype(BF16), preferred_element_type=F32)
    o_ref[...] = x_ref[...] + gate_ref[...] * y


def _matmul_gated_residual(a, w, layer, x, gate):
    m, k = a.shape
    d = w.shape[2]
    groups = gate.shape[0]
    rpg = m // groups
    tm = _pick(rpg, (1024, 512, 256, 128, 64, 32, 16))
    tn = _pick(d, (512, 256, 128))
    return pl.pallas_call(
        _mm2_kernel,
        grid=(m // tm, d // tn),
        in_specs=[pl.BlockSpec((tm, k), lambda i, j: (i, 0)),
                  pl.BlockSpec((None, k, tn), lambda i, j: (layer, 0, j)),
                  pl.BlockSpec((tm, tn), lambda i, j: (i, j)),
                  _mod_spec(gate, tm, rpg, tn, True)],
        out_specs=pl.BlockSpec((tm, tn), lambda i, j: (i, j)),
        out_shape=jax.ShapeDtypeStruct((m, d), F32),
        compiler_params=_params(("arbitrary", "arbitrary")),
        name="matmul_gated_residual",
    )(a, w, x, gate)


def _mlp_kernel(x_ref, g_ref, sc_ref, sh_ref, gate_ref, w1_ref, w2_ref, fg_ref, o_ref, hn_ref, *, final):
    j = pl.program_id(1)

    @pl.when(j == 0)
    def _():
        _modnorm_rows(x_ref, g_ref, sc_ref, sh_ref, hn_ref)
        o_ref[...] = jnp.zeros_like(o_ref)

    h = jnp.dot(hn_ref[...], w1_ref[...].astype(BF16), preferred_element_type=F32)
    h = jnp.square(jnp.maximum(h, 0.0))
    o_ref[...] += jnp.dot(h.astype(BF16), w2_ref[...].astype(BF16), preferred_element_type=F32)

    @pl.when(j == pl.num_programs(1) - 1)
    def _():
        y = x_ref[...] + gate_ref[...] * o_ref[...]
        if final:
            y = (y * lax.rsqrt(jnp.mean(y * y, axis=-1, keepdims=True) + EPS)) * fg_ref[...]
        o_ref[...] = y


def _mlp(x, g, scale, shift, gate, w1, w2, layer, final_g=None):
    m, d = x.shape
    final = final_g is not None
    fg = (final_g if final else g).reshape(1, d)
    f = w1.shape[2]
    groups = scale.shape[0]
    rpg = m // groups
    tm = _pick(rpg, (1024, 512, 256, 128, 64, 32, 16, 8))
    tf = _pick(f, (512, 256, 128))
    return pl.pallas_call(
        functools.partial(_mlp_kernel, final=final),
        grid=(m // tm, f // tf),
        in_specs=[pl.BlockSpec((tm, d), lambda i, j: (i, 0)),
                  pl.BlockSpec((1, d), lambda i, j: (0, 0)),
                  _mod_spec(scale, tm, rpg, d, False),
                  _mod_spec(shift, tm, rpg, d, False),
                  _mod_spec(gate, tm, rpg, d, False),
                  pl.BlockSpec((None, d, tf), lambda i, j: (layer, 0, j)),
                  pl.BlockSpec((None, tf, d), lambda i, j: (layer, j, 0)),
                  pl.BlockSpec((1, d), lambda i, j: (0, 0))],
        out_specs=pl.BlockSpec((tm, d), lambda i, j: (i, 0), pipeline_mode=pl.Buffered(1)),
        out_shape=jax.ShapeDtypeStruct((m, d), F32),
        scratch_shapes=[pltpu.VMEM((tm, d), BF16)],
        compiler_params=_params(("arbitrary", "arbitrary")),
        name="mlp",
    )(x, g.reshape(1, d), scale, shift, gate, w1, w2, fg)


def _split3(x):
    hi = x.astype(BF16)
    r = x - hi.astype(F32)
    mid = r.astype(BF16)
    lo = (r - mid.astype(F32)).astype(BF16)
    return hi, mid, lo


def _gdn_kernel(qkv_ref, z_ref, ab_ref, gb_ref, gcg_ref, hx_ref, cst_ref, scst_ref, cw_ref, scw_ref,
                alog_ref, dtb_ref, ng_ref, s0_ref,
                mix_ref, sout_ref, ulast_ref,
                xbuf, ubuf, s_ref, *, t_valid, n_kh, n_vh, dk, dv, n_sub):
    c = GDN_CHUNK
    rows = n_sub * c
    n = pl.program_id(1)
    nqk = n_kh * dk
    nv = n_vh * dv
    rep = n_vh // n_kh
    hpg = 2
    gw = hpg * c
    n_conv = cw_ref.shape[0]
    n_sc = scw_ref.shape[0]

    @pl.when(n == 0)
    def _():
        xbuf[0:SUBLANE, :] = cst_ref[...]
        ubuf[0:SUBLANE, :] = scst_ref[...]
        s_ref[...] = s0_ref[...]

    xbuf[SUBLANE:SUBLANE + rows, :] = qkv_ref[...]
    cw = cw_ref[...]
    off = SUBLANE - (n_conv - 1)
    xc_all = cw[0:1, :] * xbuf[off:off + rows, :]
    for j in range(1, n_conv):
        xc_all = xc_all + cw[j:j + 1, :] * xbuf[off + j:off + j + rows, :]
    xbuf[0:SUBLANE, :] = xbuf[rows:rows + SUBLANE, :]
    xc_all = _silu(xc_all)

    u = gcg_ref[...] * hx_ref[...]
    ubuf[SUBLANE:SUBLANE + rows, :] = u
    scw = scw_ref[...]
    offs = SUBLANE - (n_sc - 1)
    cu = scw[0:1, :] * ubuf[offs:offs + rows, :]
    for j in range(1, n_sc):
        cu = cu + scw[j:j + 1, :] * ubuf[offs + j:offs + j + rows, :]
    ubuf[0:SUBLANE, :] = ubuf[rows:rows + SUBLANE, :]
    ulast_ref[...] = u[rows - c:, :]
    mix_ref[:, nv:] = (gb_ref[...] * cu).astype(mix_ref.dtype)

    ab = ab_ref[...]
    g_rows = -jnp.exp(alog_ref[...]) * jax.nn.softplus(ab + dtb_ref[...])
    beta_rows = jax.nn.sigmoid(ab)
    if t_valid < rows:
        rowmask = lax.broadcasted_iota(jnp.int32, (rows, 1), 0) < t_valid
        xc_all = jnp.where(rowmask, xc_all, 0.0)
        g_rows = jnp.where(rowmask, g_rows, 0.0)
        beta_rows = jnp.where(rowmask, beta_rows, 0.0)

    ri = lax.broadcasted_iota(jnp.int32, (c, c), 0)
    ci = lax.broadcasted_iota(jnp.int32, (c, c), 1)
    tril = jnp.where(ri >= ci, 1.0, 0.0).astype(BF16)

    def l2n(x):
        return x * lax.rsqrt(jnp.sum(x * x, axis=-1, keepdims=True) + EPS)

    gi = lax.broadcasted_iota(jnp.int32, (gw, gw), 0)
    gj = lax.broadcasted_iota(jnp.int32, (gw, gw), 1)
    same = (gi // c) == (gj // c)
    low_incl = same & (gi >= gj)
    low_strict = same & (gi > gj)
    lane_blk = lax.broadcasted_iota(jnp.int32, (dk, gw), 1) // c
    level_mask = []
    size = 1
    while size < c:
        level_mask.append(((gi // (2 * size)) == (gj // (2 * size))) & ((gi // size) != (gj // size)) & (gi > gj))
        size *= 2

    def stack(cols):
        return jnp.concatenate(cols, axis=0)

    n_grp = n_vh // hpg
    pre, glasts = [], []
    for sub in range(n_sub):
        r0 = sub * c
        xc = xc_all[r0:r0 + c]
        beta_all = beta_rows[r0:r0 + c]
        ghi, gmid, glo = _split3(g_rows[r0:r0 + c])
        gcum = (jnp.dot(tril, ghi, preferred_element_type=F32)
                + jnp.dot(tril, gmid, preferred_element_type=F32)
                + jnp.dot(tril, glo, preferred_element_type=F32))
        glast = gcum[c - 1:c, :]
        glasts.append(glast)
        for grp in range(n_grp):
            heads = [grp * hpg + j for j in range(hpg)]
            kheads = [h // rep for h in heads]
            qs = {kh: l2n(xc[:, kh * dk:(kh + 1) * dk]) * (dk ** -0.5) for kh in set(kheads)}
            ks = {kh: l2n(xc[:, nqk + kh * dk:nqk + (kh + 1) * dk]) for kh in set(kheads)}
            q_st = stack([qs[kh] for kh in kheads])
            k_st = stack([ks[kh] for kh in kheads])
            v_st = stack([xc[:, 2 * nqk + h * dv:2 * nqk + (h + 1) * dv] for h in heads])
            beta_st = stack([beta_all[:, n_vh + h:n_vh + h + 1] for h in heads])
            gc_st = stack([gcum[:, h:h + 1] for h in heads])
            gl_st = stack([jnp.broadcast_to(glast[:, h:h + 1], (c, 1)) for h in heads])
            gc_row = jnp.broadcast_to(gc_st, (gw, LANE)).T[0:1, :]
            diff = gc_st - gc_row
            decay = jnp.where(low_incl, jnp.exp(jnp.where(low_incl, diff, 0.0)), 0.0)
            egc = jnp.exp(gc_st)
            kb_st = k_st * beta_st

            nmat = _bdot_nt(kb_st, k_st) * jnp.where(low_strict, decay, 0.0)
            qk = _bdot_nt(q_st, k_st) * decay
            rhs = jnp.concatenate([v_st * beta_st, kb_st * egc], axis=1)
            pre.append((nmat, qk, rhs, q_st * egc, k_st * jnp.exp(gl_st - gc_st)))

    tms = [-jnp.where(level_mask[0], p[0], 0.0) for p in pre]
    for lm in level_mask[1:]:
        lls = [jnp.where(lm, p[0], 0.0) for p in pre]
        ys = [ll + _bdot(tm, ll) for tm, ll in zip(tms, lls)]
        tms = [tm - y - _bdot(y, tm) for tm, y in zip(tms, ys)]
    sols = [p[2] + _bdot(tm, p[2]) for p, tm in zip(pre, tms)]

    for sub in range(n_sub):
        r0 = sub * c
        glast = glasts[sub]
        for grp in range(n_grp):
            heads = [grp * hpg + j for j in range(hpg)]
            nmat, qk, rhs, qg_st, kd_st = pre[sub * n_grp + grp]
            sol = sols[sub * n_grp + grp]
            u_st = sol[:, :dv]
            w_st = sol[:, dv:]

            vnew, qs_out = [], []
            for j, h in enumerate(heads):
                wq = jnp.concatenate([w_st[j * c:(j + 1) * c], qg_st[j * c:(j + 1) * c]], axis=0)
                r2 = _bdot(wq, s_ref[h])
                vnew.append(u_st[j * c:(j + 1) * c] - r2[:c])
                qs_out.append(r2[c:])
            vnew_st = stack(vnew)
            o_st = stack(qs_out) + _bdot(qk, vnew_st)
            kd_t = kd_st.T
            for j, h in enumerate(heads):
                upd = _bdot(jnp.where(lane_blk == j, kd_t, 0.0), vnew_st)
                s_ref[h] = s_ref[h] * jnp.exp(glast[:, h:h + 1]) + upd
                o_h = o_st[j * c:(j + 1) * c]
                on = (o_h * lax.rsqrt(jnp.mean(o_h * o_h, axis=-1, keepdims=True) + EPS)) * ng_ref[...]
                zz = z_ref[r0:r0 + c, h * dv:(h + 1) * dv]
                mix_ref[r0:r0 + c, h * dv:(h + 1) * dv] = (on * _silu(zz)).astype(mix_ref.dtype)

    sout_ref[...] = s_ref[...]


def _gdn_sconv(proj, conv_state, sc_state, s0, conv_w, sc_w, a_log, dt_bias, norm_g, t_valid):
    bsz, n_vh, dk, dv = s0.shape
    n_conv, conv_dim = conv_w.shape
    n_sc, sc = sc_w.shape
    nv = n_vh * dv
    n_kh = (conv_dim - nv) // (2 * dk)
    c = GDN_CHUNK
    rows = proj.shape[0]
    nchunk = rows // (bsz * c)
    assert conv_dim % nv == 0 and (conv_dim + nv) % sc == 0 and n_vh % 2 == 0 and 2 * n_vh <= LANE
    n_sub = 2 if (nchunk % 2 == 0 and t_valid == c) else 1
    assert t_valid == c or nchunk == 1
    nstep = nchunk // n_sub
    rs = n_sub * c
    zb = conv_dim // nv
    gbb = (conv_dim + nv) // sc
    abb = (conv_dim + nv + 3 * sc) // LANE
    pad_lane = lambda v: jnp.pad(v.reshape(1, -1), ((0, 0), (0, LANE - v.shape[-1])))
    cst = jnp.pad(conv_state.astype(F32), ((0, 0), (SUBLANE - (n_conv - 1), 0), (0, 0)))
    scst = jnp.pad(sc_state.astype(F32), ((0, 0), (SUBLANE - (n_sc - 1), 0), (0, 0)))
    row = lambda b, n: b * nstep + n
    kern = functools.partial(_gdn_kernel, t_valid=t_valid if n_sub == 1 else rs, n_kh=n_kh, n_vh=n_vh, dk=dk,
                             dv=dv, n_sub=n_sub)
    return pl.pallas_call(
        kern,
        grid=(bsz, nstep),
        in_specs=[pl.BlockSpec((rs, conv_dim), lambda b, n: (row(b, n), 0)),
                  pl.BlockSpec((rs, nv), lambda b, n: (row(b, n), zb)),
                  pl.BlockSpec((rs, LANE), lambda b, n: (row(b, n), abb)),
                  pl.BlockSpec((rs, sc), lambda b, n: (row(b, n), gbb)),
                  pl.BlockSpec((rs, sc), lambda b, n: (row(b, n), gbb + 1)),
                  pl.BlockSpec((rs, sc), lambda b, n: (row(b, n), gbb + 2)),
                  pl.BlockSpec((None, SUBLANE, conv_dim), lambda b, n: (b, 0, 0)),
                  pl.BlockSpec((None, SUBLANE, sc), lambda b, n: (b, 0, 0)),
                  pl.BlockSpec((n_conv, conv_dim), lambda b, n: (0, 0)),
                  pl.BlockSpec((n_sc, sc), lambda b, n: (0, 0)),
                  pl.BlockSpec((1, LANE), lambda b, n: (0, 0)),
                  pl.BlockSpec((1, LANE), lambda b, n: (0, 0)),
                  pl.BlockSpec((1, dv), lambda b, n: (0, 0)),
                  pl.BlockSpec((None, n_vh, dk, dv), lambda b, n: (b, 0, 0, 0))],
        out_specs=[pl.BlockSpec((rs, nv + sc), lambda b, n: (row(b, n), 0)),
                   pl.BlockSpec((None, n_vh, dk, dv), lambda b, n: (b, 0, 0, 0)),
                   pl.BlockSpec((None, c, sc), lambda b, n: (b, 0, 0))],
        out_shape=[jax.ShapeDtypeStruct((rows, nv + sc), BF16),
                   jax.ShapeDtypeStruct((bsz, n_vh, dk, dv), F32),
                   jax.ShapeDtypeStruct((bsz, c, sc), F32)],
        scratch_shapes=[pltpu.VMEM((rs + SUBLANE, conv_dim), F32),
                        pltpu.VMEM((rs + SUBLANE, sc), F32),
                        pltpu.VMEM((n_vh, dk, dv), F32)],
        compiler_params=_params(("arbitrary", "arbitrary")),
        name="gdn_sconv",
    )(proj, proj, proj, proj, proj, proj, cst, scst, conv_w, sc_w,
      pad_lane(a_log), pad_lane(dt_bias), norm_g.reshape(1, dv), s0.astype(F32))


def _cmp_prompt_kernel(kv_ref, pw_ref, pj_ref, o_ref, *, blk):
    t, hd = kv_ref.shape
    x = kv_ref[...].reshape(t // blk, blk, hd) * pw_ref[...][None]
    o_ref[...] = _bdot(jnp.sum(x, axis=1), pj_ref[...])


def _cmp_prompt(proj, bsz, t, pw, pj, col0):
    nch, blk, hd = pw.shape
    return pl.pallas_call(
        functools.partial(_cmp_prompt_kernel, blk=blk),
        grid=(bsz, nch),
        in_specs=[pl.BlockSpec((t, hd), lambda b, ch: (b, col0 + ch)),
                  pl.BlockSpec((None, blk, hd), lambda b, ch: (ch, 0, 0)),
                  pl.BlockSpec((None, hd, hd), lambda b, ch: (ch, 0, 0))],
        out_specs=pl.BlockSpec((None, None, t // blk, hd), lambda b, ch: (b, ch, 0, 0)),
        out_shape=jax.ShapeDtypeStruct((bsz, nch, t // blk, hd), F32),
        compiler_params=_params(("arbitrary", "arbitrary")),
        name="cmp_prompt",
    )(proj, pw, pj)


_NT = (((1,), (1,)), ((), ()))


def _slope_features(sl2, lane):
    hi = sl2.astype(BF16).astype(F32)
    lo = sl2 - hi
    return jnp.where(lane == 0, 64.0 * hi, jnp.where(lane == 1, hi, jnp.where(lane == 2, 64.0 * lo,
                                                                             jnp.where(lane == 3, lo, 0.0))))


def _pool_blocks(page_refs, pw_ref, o_ref):
    pw = pw_ref[...]
    rows_blk = pw.shape[0]
    per_page = page_refs[0].shape[0] // rows_blk
    nch = o_ref.shape[0] // (len(page_refs) * per_page)
    for k, page_ref in enumerate(page_refs):
        for hb in range(per_page):
            r = k * per_page + hb
            x = page_ref[hb * rows_blk:(hb + 1) * rows_blk, :] * pw
            o_ref[r * nch:(r + 1) * nch, :] = jnp.sum(x.reshape(rows_blk // nch, nch, x.shape[-1]), axis=0)


def _attn_prompt_kernel(*refs, tq, tk, wk, blk, n_g, n_kv, hd, n_pool):
    if n_pool:
        refs = refs[1:]
    q_ref, kc_ref, vc_ref, ks_ref, vs_ref, kw_ref, vw_ref, gt_ref, sl_ref, pos_ref, oh_ref = refs[:11]
    o_ref = refs[11 + n_pool + (1 if n_pool else 0)]
    if n_pool:
        _pool_blocks(refs[11:11 + n_pool], refs[11 + n_pool], refs[-1])
    h = pl.program_id(1)
    qt = pl.program_id(2)
    q0 = qt * tq
    nc = kc_ref.shape[0]
    t_all = ks_ref.shape[0]
    sl = sl_ref[...]
    lane = lax.broadcasted_iota(jnp.int32, (1, LANE), 1)
    qs = [q_ref[:, g * hd:(g + 1) * hd] * (hd ** -0.5) for g in range(n_g)]
    qb = jnp.concatenate(qs, axis=0).astype(BF16)
    qaug = jnp.concatenate(
        [jnp.concatenate([(qs[g] * LOG2E).astype(BF16),
                          jnp.broadcast_to(_slope_features(sl[:, g:g + 1] * LOG2E, lane), (tq, LANE)).astype(BF16)],
                         axis=1) for g in range(n_g)], axis=0)
    qpos = q0 + lax.broadcasted_iota(jnp.int32, (tq, 1), 0)

    bidx = lax.broadcasted_iota(jnp.int32, (1, nc), 1)
    start = bidx * blk
    s_c = _bdot_nt(qb, kc_ref[...])
    dist_c = jnp.abs(qpos.astype(F32) - (start.astype(F32) + (blk - 1) / 2))
    valid_c = (start + (blk - 1)) <= qpos
    p_c = jnp.concatenate([_masked_softmax(s_c[g * tq:(g + 1) * tq] - sl[:, g:g + 1] * dist_c, valid_c)
                           for g in range(n_g)], axis=0)
    o_c = _bdot(p_c, vc_ref[...])
    imp = p_c[0:tq]
    for g in range(1, n_g):
        imp = imp + p_c[g * tq:(g + 1) * tq]

    imp_t = jnp.concatenate([imp, jnp.zeros((tq, LANE - nc), F32)], axis=1).T[0:nc, :]
    bcol = lax.broadcasted_iota(jnp.int32, (nc, 1), 0)
    cur = (q0 + lax.broadcasted_iota(jnp.int32, (1, tq), 1)) // blk
    score = jnp.where(bcol < cur, imp_t, NEG)
    score = jnp.where((bcol == 0) | (bcol == cur), FORCE, score)
    rank = jnp.zeros((nc, tq), F32)
    for i in range(nc):
        ri = score[i:i + 1, :]
        rank = rank + jnp.where((ri > score) | ((ri == score) & (bcol > i)), 1.0, 0.0)
    selneg_t = jnp.where((rank < min(SEL_TOPK, nc)) & (score > 0.5 * NEG), 0.0, NEG)
    selneg = jnp.concatenate([selneg_t, jnp.zeros((LANE - nc, tq), F32)], axis=0).T.astype(BF16)

    def flash_step(k_ref, v_ref, k0, size, bias, carry):
        m, l, acc = carry
        kaug = jnp.concatenate([k_ref[pl.ds(k0, size), :].astype(BF16), pos_ref[pl.ds(k0, size), :]], axis=1)
        s = lax.dot_general(qaug, kaug, _NT, preferred_element_type=F32)
        gs = range(n_g)
        sgs = [s[g * tq:(g + 1) * tq] + bias for g in gs]
        m2 = [jnp.maximum(m[g], jnp.max(sgs[g], axis=-1, keepdims=True)) for g in gs]
        als = [jnp.exp2(m[g] - m2[g]) for g in gs]
        pfs = [jnp.exp2(sgs[g] - m2[g]) for g in gs]
        l2 = [als[g] * l[g] + jnp.sum(pfs[g], axis=-1, keepdims=True) for g in gs]
        ps = [p.astype(BF16) for p in pfs]
        accs = [als[g] * acc[g * tq:(g + 1) * tq] for g in gs]
        pv = jnp.dot(jnp.concatenate(ps, axis=0), v_ref[pl.ds(k0, size), :].astype(BF16),
                     preferred_element_type=F32)
        return tuple(m2), tuple(l2), jnp.concatenate(accs, axis=0) + pv

    def finish(carry):
        _, l, acc = carry
        return [acc[g * tq:(g + 1) * tq] / jnp.maximum(l[g], 1e-30) for g in range(n_g)]

    init = (tuple(jnp.full((tq, 1), NEG, F32) for _ in range(n_g)),
            tuple(jnp.zeros((tq, 1), F32) for _ in range(n_g)), jnp.zeros((n_g * tq, hd), F32))

    def sel_bias(k0):
        return lax.dot_general(selneg, oh_ref[pl.ds(k0, tk), :], _NT, preferred_element_type=F32)

    def sel_chunk(ci, carry):
        k0 = pl.multiple_of(ci * tk, tk)
        return flash_step(ks_ref, vs_ref, k0, tk, sel_bias(k0), carry)

    n_full = q0 // tk
    carry = lax.fori_loop(0, n_full, sel_chunk, init)
    k0 = pl.multiple_of(n_full * tk, tk)
    kpos = k0 + lax.broadcasted_iota(jnp.int32, (1, tk), 1)
    o_s = finish(flash_step(ks_ref, vs_ref, k0, tk, sel_bias(k0) + jnp.where(kpos <= qpos, 0.0, NEG), carry))

    w0 = pl.multiple_of(jnp.clip(q0 + tq - wk, 0, t_all - wk), tq)
    dist = qpos - (w0 + lax.broadcasted_iota(jnp.int32, (1, wk), 1))
    o_w = finish(flash_step(kw_ref, vw_ref, w0, wk, jnp.where((dist >= 0) & (dist < WINDOW), 0.0, NEG), init))

    gt = jax.nn.sigmoid(pltpu.roll(gt_ref[...], (LANE - n_g * h) % LANE, 1))
    n_h = n_g * n_kv
    for g in range(n_g):
        r = slice(g * tq, (g + 1) * tq)
        o = (gt[:, g:g + 1] * o_c[r] + gt[:, n_h + g:n_h + g + 1] * o_s[g]
             + gt[:, 2 * n_h + g:2 * n_h + g + 1] * o_w[g])
        o_ref[:, g * hd:(g + 1) * hd] = o.astype(o_ref.dtype)


def _attn_prompt_tiles(t):
    return 128, t // 128


def _attn_prompt(proj, cmp, slopes, bsz, t, n_h, n_kv, hd, blk, pool=None):
    n_g = n_h // n_kv
    tq, nqt = _attn_prompt_tiles(t)
    tk = _pick(t, (512, 256, 128))
    wk = WINDOW + tq
    assert t % tq == 0 and t >= wk and t // blk <= LANE and blk == 64 and n_h * 3 <= LANE
    nc = t // blk
    kvb = n_h
    pos = jnp.arange(t, dtype=jnp.int32)[:, None]
    lane = jnp.arange(LANE, dtype=jnp.int32)[None, :]
    pos_tab = jnp.where(lane < 4, jnp.where(lane % 2 == 0, pos // 64, pos % 64), 0).astype(BF16)
    onehot = (lane == pos // blk).astype(BF16)
    gate_blk = (n_h * hd + 6 * n_kv * hd) // LANE
    n_pool = pool[5] if pool else 0
    kern = functools.partial(_attn_prompt_kernel, tq=tq, tk=tk, wk=wk, blk=blk, n_g=n_g, n_kv=n_kv, hd=hd,
                             n_pool=n_pool)
    kv_spec = lambda off: pl.BlockSpec((t, hd), lambda b, h, q, *_: (b, kvb + off + h))
    in_specs = [pl.BlockSpec((tq, n_g * hd), lambda b, h, q, *_: (b * nqt + q, h)),
                pl.BlockSpec((None, None, nc, hd), lambda b, h, q, *_: (b, h, 0, 0)),
                pl.BlockSpec((None, None, nc, hd), lambda b, h, q, *_: (b, n_kv + h, 0, 0)),
                kv_spec(2 * n_kv), kv_spec(3 * n_kv), kv_spec(4 * n_kv), kv_spec(5 * n_kv),
                pl.BlockSpec((tq, LANE), lambda b, h, q, *_: (b * nqt + q, gate_blk)),
                pl.BlockSpec((None, 1, LANE), lambda b, h, q, *_: (h, 0, 0)),
                pl.BlockSpec((t, LANE), lambda b, h, q, *_: (0, 0)),
                pl.BlockSpec((t, LANE), lambda b, h, q, *_: (0, 0))]
    out_specs = [pl.BlockSpec((tq, n_g * hd), lambda b, h, q, *_: (b * nqt + q, h))]
    out_shape = [jax.ShapeDtypeStruct((bsz * t, n_h * hd), BF16)]
    operands = [proj, cmp, cmp, proj, proj, proj, proj, proj, slopes, pos_tab, onehot]
    prefetch = []
    if pool:
        cache, layer, page_table, pw_rows, nch, _ = pool
        _, _, prow, _ = cache.shape
        bd, n_pages = page_table.shape
        rows_blk = pw_rows.shape[0]
        assert bsz * n_kv == bd and nqt * n_pool == n_pages
        out_rows = n_pool * (prow // rows_blk) * nch
        for k in range(n_pool):
            in_specs.append(pl.BlockSpec((None, None, prow, hd),
                                         lambda b, h, q, pt, k=k: (layer, pt[b * n_kv + h, q * n_pool + k], 0, 0)))
        in_specs.append(pl.BlockSpec((rows_blk, hd), lambda b, h, q, pt: (0, 0)))
        out_specs.append(pl.BlockSpec((None, out_rows, hd), lambda b, h, q, pt: (b * n_kv + h, q, 0)))
        out_shape.append(jax.ShapeDtypeStruct((bd, nqt * out_rows, hd), F32))
        operands += [cache] * n_pool + [pw_rows]
        prefetch = [page_table]
    outs = pl.pallas_call(
        kern,
        grid_spec=pltpu.PrefetchScalarGridSpec(num_scalar_prefetch=len(prefetch), grid=(bsz, n_kv, nqt),
                                               in_specs=in_specs, out_specs=out_specs),
        out_shape=out_shape,
        compiler_params=_params(("arbitrary", "arbitrary", "arbitrary")),
        name="attn_prompt",
    )(*prefetch, *operands)
    return outs if pool else outs[0]


def _pool_pages_kernel(pt_ref, *refs, n_pp, rows_blk):
    _pool_blocks(refs[:n_pp], refs[n_pp], refs[n_pp + 1])


def _page_map(b, p, pt, *, layer, k, n_pp):
    return (layer, pt[b, p * n_pp + k], 0, 0)


def _pool_pages(cache, layer, page_table, pw_rows, nch):
    _, _, prow, hd = cache.shape
    bd, n_pages = page_table.shape
    rows_blk = pw_rows.shape[0]
    n_pp = _pick(n_pages, (8, 4, 2, 1))
    per_page = prow // rows_blk
    in_specs = [pl.BlockSpec((None, None, prow, hd),
                             functools.partial(_page_map, layer=layer, k=k, n_pp=n_pp)) for k in range(n_pp)]
    in_specs.append(pl.BlockSpec((rows_blk, hd), lambda b, p, pt: (0, 0)))
    return pl.pallas_call(
        functools.partial(_pool_pages_kernel, n_pp=n_pp, rows_blk=rows_blk),
        grid_spec=pltpu.PrefetchScalarGridSpec(
            num_scalar_prefetch=1, grid=(bd, n_pages // n_pp), in_specs=in_specs,
            out_specs=pl.BlockSpec((None, n_pp * per_page * nch, hd), lambda b, p, pt: (b, p, 0))),
        out_shape=jax.ShapeDtypeStruct((bd, n_pages * per_page * nch, hd), F32),
        compiler_params=_params(("arbitrary", "arbitrary")),
        name="pool_pages",
    )(page_table, *([cache] * n_pp), pw_rows)


def _attn_sample_a_kernel(q_ref, pooled_ref, pj_ref, kvw_ref, gt_ref, sl_ref, ocw_ref, selm_ref,
                          score_buf, *, past, n_q, n_kv, n_g, hd, blk, wb, nbp):
    nch = 2 * n_kv
    nc = pooled_ref.shape[0] // nch
    n_blocks = (past + n_q + blk - 1) // blk
    kvw_rows = kvw_ref.shape[0]
    half = n_kv * hd
    rep = lambda x: jnp.concatenate([x] * n_g, axis=0)
    sl = sl_ref[...]
    gt = jax.nn.sigmoid(gt_ref[...])
    qpos = past + lax.broadcasted_iota(jnp.int32, (n_q, 1), 0)
    qpos_r = rep(qpos)
    bidx_c = lax.broadcasted_iota(jnp.int32, (1, nc), 1)
    start = bidx_c * blk
    bidx = lax.broadcasted_iota(jnp.int32, (1, nbp), 1)
    cur = qpos // blk
    score_buf[...] = jnp.full(score_buf.shape, DEAD, F32)

    for h in range(n_kv):
        qb = jnp.concatenate([q_ref[:, (h * n_g + g) * hd:(h * n_g + g + 1) * hd] for g in range(n_g)], axis=0)
        qb = (qb * (hd ** -0.5)).astype(BF16)
        slope = jnp.concatenate(
            [jnp.broadcast_to(sl[:, h * n_g + g:h * n_g + g + 1], (n_q, 1)) for g in range(n_g)], axis=0)
        kc = _bdot(pooled_ref[pl.ds(h, nc, stride=nch), :], pj_ref[h])
        vc = _bdot(pooled_ref[pl.ds(n_kv + h, nc, stride=nch), :], pj_ref[n_kv + h])
        s = _bdot_nt(qb, kc)
        dist_c = qpos_r.astype(F32) - (start.astype(F32) + (blk - 1) / 2)
        s = s - slope * jnp.abs(dist_c)
        p_c = _masked_softmax(s, (start + (blk - 1)) <= qpos_r)
        o_c = _bdot(p_c, vc)
        imp = p_c[0:n_q]
        for g in range(1, n_g):
            imp = imp + p_c[g * n_q:(g + 1) * n_q]
        imp = jnp.concatenate([imp, jnp.zeros((n_q, nbp - nc), F32)], axis=1)
        score = jnp.where(bidx < cur, imp, NEG)
        score = jnp.where((bidx == 0) | (bidx == cur), FORCE, score)
        score = jnp.where(bidx < n_blocks, score, DEAD)
        score_buf[h * n_q:(h + 1) * n_q, :] = score

        kw = kvw_ref[:, h * hd:(h + 1) * hd]
        vw = kvw_ref[:, half + h * hd:half + (h + 1) * hd]
        kidx = lax.broadcasted_iota(jnp.int32, (1, kvw_rows), 1)
        dist = qpos_r - (past - wb + kidx)
        valid = (dist >= 0) & (dist < WINDOW) & (kidx < wb + n_q)
        sw = _bdot_nt(qb, kw) - slope * jnp.abs(dist).astype(F32)
        o_w = _bdot(_masked_softmax(sw, valid), vw)
        for g in range(n_g):
            hh = h * n_g + g
            r = slice(g * n_q, (g + 1) * n_q)
            ocw_ref[:, hh * hd:(hh + 1) * hd] = (gt[:, hh:hh + 1] * o_c[r]
                                                 + gt[:, 2 * n_kv * n_g + hh:2 * n_kv * n_g + hh + 1] * o_w[r])

    sc_all = score_buf[...]
    sc_t = sc_all.T
    ii = lax.broadcasted_iota(jnp.int32, (nbp, 1), 0)
    for h in range(n_kv):
        for t in range(n_q):
            r = h * n_q + t
            col = sc_t[:, r:r + 1]
            row = sc_all[r:r + 1, :]
            before = (col > row) | ((col == row) & (ii < bidx))
            rank = jnp.sum(jnp.where(before, 1.0, 0.0), axis=0, keepdims=True)
            sel = jnp.where((rank < min(SEL_TOPK, n_blocks)) & (row > 0.5 * NEG), 1.0, 0.0)
            for g in range(n_g):
                selm_ref[h, g * n_q + t:g * n_q + t + 1, :] = sel


def _attn_sample_a(proj3, pooled, pj, kv_all, slopes_row, past, n_h, n_kv, hd, blk, wb):
    bd, n_q, _ = proj3.shape
    n_g = n_h // n_kv
    n_blocks = (past + n_q + blk - 1) // blk
    nbp = pl.cdiv(n_blocks, LANE) * LANE
    kvw_rows = kv_all.shape[1]
    qw = n_h * hd
    assert n_kv * n_q <= LANE
    kern = functools.partial(_attn_sample_a_kernel, past=past, n_q=n_q, n_kv=n_kv, n_g=n_g, hd=hd, blk=blk,
                             wb=wb, nbp=nbp)
    return pl.pallas_call(
        kern,
        grid=(bd,),
        in_specs=[pl.BlockSpec((None, n_q, qw), lambda b: (b, 0, 0)),
                  pl.BlockSpec((None, pooled.shape[1], hd), lambda b: (b, 0, 0)),
                  pl.BlockSpec((2 * n_kv, hd, hd), lambda b: (0, 0, 0)),
                  pl.BlockSpec((None, kvw_rows, 2 * n_kv * hd), lambda b: (b, 0, 0)),
                  pl.BlockSpec((None, n_q, LANE), lambda b: (b, 0, (qw + 6 * n_kv * hd) // LANE)),
                  pl.BlockSpec((1, LANE), lambda b: (0, 0))],
        out_specs=[pl.BlockSpec((None, n_q, qw), lambda b: (b, 0, 0)),
                   pl.BlockSpec((None, n_kv, n_g * n_q, nbp), lambda b: (b, 0, 0, 0))],
        out_shape=[jax.ShapeDtypeStruct((bd, n_q, qw), F32),
                   jax.ShapeDtypeStruct((bd, n_kv, n_g * n_q, nbp), F32)],
        scratch_shapes=[pltpu.VMEM((LANE, nbp), F32)],
        compiler_params=_params(("arbitrary",)),
        name="attn_sample_a",
    )(proj3, pooled, pj, kv_all, proj3, slopes_row)


def _attn_sample_sel_kernel(pt_ref, *refs, n_pp, past, n_q, n_kv, n_g, hd, blk):
    pages = refs[:n_pp]
    q_ref, selm_ref, new_ref, ocw_ref, gt_ref, sl_ref, o_ref, m_ref, l_ref, acc_ref = refs[n_pp:]
    p = pl.program_id(1)
    nch = 2 * n_kv
    page = pages[0].shape[0] // nch
    keys = n_pp * page
    nbp = selm_ref.shape[-1]
    rows = n_g * n_q
    half = n_kv * hd
    rep = lambda x: jnp.concatenate([x] * n_g, axis=0)
    sl = sl_ref[...]
    qpos_r = rep(past + lax.broadcasted_iota(jnp.int32, (n_q, 1), 0))

    @pl.when(p == 0)
    def _():
        m_ref[...] = jnp.full(m_ref.shape, NEG, F32)
        l_ref[...] = jnp.zeros_like(l_ref)
        acc_ref[...] = jnp.zeros_like(acc_ref)

    def heads():
        for h in range(n_kv):
            qb = jnp.concatenate([q_ref[:, (h * n_g + g) * hd:(h * n_g + g + 1) * hd] for g in range(n_g)],
                                 axis=0) * (hd ** -0.5)
            slope = jnp.concatenate(
                [jnp.broadcast_to(sl[:, h * n_g + g:h * n_g + g + 1], (n_q, 1)) for g in range(n_g)], axis=0)
            yield h, qb, slope

    def update(h, sc, mask, pv_fn):
        sc = jnp.where(mask, sc, NEG)
        m = m_ref[h]
        m_new = jnp.maximum(m, jnp.max(sc, axis=-1, keepdims=True))
        alpha = jnp.exp(m - m_new)
        pr = jnp.where(mask, jnp.exp(sc - m_new), 0.0)
        l_ref[h] = alpha * l_ref[h] + jnp.sum(pr, axis=-1, keepdims=True)
        acc_ref[h] = alpha * acc_ref[h] + pv_fn(pr)
        m_ref[h] = m_new

    k0 = p * keys
    kk = lax.broadcasted_iota(jnp.int32, (nbp, keys), 1)
    nn = lax.broadcasted_iota(jnp.int32, (nbp, keys), 0)
    expand = jnp.where(nn == (k0 + kk) // blk, 1.0, 0.0).astype(BF16)
    dist = qpos_r - (k0 + lax.broadcasted_iota(jnp.int32, (1, keys), 1))
    staged = []
    for h, qb, slope in heads():
        mask = (jnp.dot(selm_ref[h].astype(BF16), expand, preferred_element_type=F32) > 0.5) & (dist >= 0)
        kx = jnp.concatenate([pages[k][pl.ds(h, page, stride=nch), :].astype(BF16) for k in range(n_pp)], axis=0)
        sc = jnp.where(mask, _bdot_nt(qb, kx) - slope * dist.astype(F32), NEG)
        staged.append((h, mask, sc))
    probs = []
    for h, mask, sc in staged:
        m = m_ref[h]
        m_new = jnp.maximum(m, jnp.max(sc, axis=-1, keepdims=True))
        alpha = jnp.exp(m - m_new)
        pr = jnp.where(mask, jnp.exp(sc - m_new), 0.0)
        l_ref[h] = alpha * l_ref[h] + jnp.sum(pr, axis=-1, keepdims=True)
        m_ref[h] = m_new
        probs.append((h, alpha, pr))
    for h, alpha, pr in probs:
        vx = jnp.concatenate([pages[k][pl.ds(n_kv + h, page, stride=nch), :].astype(BF16) for k in range(n_pp)],
                             axis=0)
        acc_ref[h] = alpha * acc_ref[h] + _bdot(pr, vx)

    @pl.when(p == pl.num_programs(1) - 1)
    def _():
        gt = jax.nn.sigmoid(gt_ref[...])
        for h, qb, slope in heads():
            nb0 = past // blk
            seln = selm_ref[h][:, nb0:nb0 + 1] > 0.5
            qr = qb.astype(BF16).astype(F32)
            for j in range(n_q):
                kj = new_ref[j:j + 1, h * hd:(h + 1) * hd].astype(BF16).astype(F32)
                vj = new_ref[j:j + 1, half + h * hd:half + (h + 1) * hd].astype(BF16).astype(F32)
                dj = qpos_r - (past + j)
                sc = jnp.sum(qr * kj, axis=-1, keepdims=True) - slope * dj.astype(F32)
                update(h, sc, seln & (dj >= 0), lambda pr, vj=vj: pr.astype(BF16).astype(F32) * vj)
            o_s = acc_ref[h] / jnp.maximum(l_ref[h], 1e-30)
            for g in range(n_g):
                hh = h * n_g + g
                r = slice(g * n_q, (g + 1) * n_q)
                gcol = n_kv * n_g + hh
                o = ocw_ref[:, hh * hd:(hh + 1) * hd] + gt[:, gcol:gcol + 1] * o_s[r]
                o_ref[:, hh * hd:(hh + 1) * hd] = o.astype(o_ref.dtype)


def _attn_sample_sel(cache, layer, page_table, proj3, selm, new_rows, ocw, slopes_row, past, n_h, n_kv, hd, blk):
    _, _, prow, _ = cache.shape
    bd, n_pages = page_table.shape
    n_q = proj3.shape[1]
    n_g = n_h // n_kv
    qw = n_h * hd
    nbp = selm.shape[-1]
    n_pp = _pick(n_pages, (16, 8, 4, 2, 1))
    rows = n_g * n_q
    in_specs = [pl.BlockSpec((None, None, prow, hd),
                             functools.partial(_page_map, layer=layer, k=k, n_pp=n_pp)) for k in range(n_pp)]
    in_specs += [pl.BlockSpec((None, n_q, qw), lambda b, p, pt: (b, 0, 0)),
                 pl.BlockSpec((None, n_kv, rows, nbp), lambda b, p, pt: (b, 0, 0, 0)),
                 pl.BlockSpec((None, new_rows.shape[1], new_rows.shape[2]), lambda b, p, pt: (b, 0, 0)),
                 pl.BlockSpec((None, n_q, qw), lambda b, p, pt: (b, 0, 0)),
                 pl.BlockSpec((None, n_q, LANE), lambda b, p, pt: (b, 0, (qw + 6 * n_kv * hd) // LANE)),
                 pl.BlockSpec((1, LANE), lambda b, p, pt: (0, 0))]
    kern = functools.partial(_attn_sample_sel_kernel, n_pp=n_pp, past=past, n_q=n_q, n_kv=n_kv, n_g=n_g,
                             hd=hd, blk=blk)
    return pl.pallas_call(
        kern,
        grid_spec=pltpu.PrefetchScalarGridSpec(
            num_scalar_prefetch=1, grid=(bd, n_pages // n_pp), in_specs=in_specs,
            out_specs=pl.BlockSpec((None, n_q, qw), lambda b, p, pt: (b, 0, 0)),
            scratch_shapes=[pltpu.VMEM((n_kv, rows, 1), F32), pltpu.VMEM((n_kv, rows, 1), F32),
                            pltpu.VMEM((n_kv, rows, hd), F32)]),
        out_shape=jax.ShapeDtypeStruct((bd, n_q, qw), BF16),
        compiler_params=_params(("arbitrary", "arbitrary")),
        name="attn_sample_sel",
    )(page_table, *([cache] * n_pp), proj3, selm, new_rows, ocw, proj3, slopes_row)


def _alibi(n_h):
    h = jnp.arange(1, n_h + 1, dtype=F32)
    return jnp.exp2(-8.0 * h / n_h)


def kernel(x_prompt, x_sample, c_prompt, c_sample, state_gdn, state_gdn_conv, state_sconv, cache_kv_cmp,
           cache_kv_sel, state_kv_win, page_table, norm_g, w_ada, b_ada, w_in_e, conv_w_gdn, a_log, dt_bias,
           gdn_norm_g, conv_w_sc, w_out_e, w_in_o, cmp_pool, cmp_proj, w_out_o, w_mlp1, w_mlp2, final_g):
    bsz, seq, d = x_prompt.shape
    bd, n_q, _ = x_sample.shape
    depth = norm_g.shape[0]
    n_vh, dk, dv = state_gdn.shape[2:]
    conv_dim = conv_w_gdn.shape[2]
    sc_dim = conv_w_sc.shape[2]
    nv = n_vh * dv
    blk, _, n_kv = cmp_pool.shape[1:]
    hd = cmp_proj.shape[-1]
    n_h = w_out_o.shape[1] // hd
    kv_row = 2 * n_kv * hd
    qw = n_h * hd
    page = cache_kv_cmp.shape[2]
    past = page_table.shape[1] * page
    wb = state_kv_win.shape[2]
    c = GDN_CHUNK
    assert seq % c == 0 and seq % blk == 0 and seq >= WINDOW and n_q <= c and n_q < blk and past % blk == 0
    assert n_q >= conv_w_gdn.shape[1] - 1 and hd == LANE and dk == LANE and dv == LANE

    rows_c = -(-(bsz + bd) // SUBLANE) * SUBLANE
    c_all = jnp.pad(jnp.concatenate([c_prompt, c_sample], axis=0).astype(F32), ((0, rows_c - bsz - bd), (0, 0)))
    mod = _ada_mod(c_all, w_ada, b_ada).reshape(depth, rows_c, 6, d)

    def mods(l, sample):
        if sample:
            return [jnp.repeat(mod[l, bsz:bsz + bd, j], n_q, axis=0)[None] for j in range(6)]
        return [mod[l, :bsz, j][:, None, :] for j in range(6)]

    slopes = _alibi(n_h)
    slopes_kv = jnp.pad(slopes.reshape(n_kv, 1, n_h // n_kv), ((0, 0), (0, 0), (0, LANE - n_h // n_kv)))
    slopes_row = jnp.pad(slopes.reshape(1, n_h), ((0, 0), (0, LANE - n_h)))
    nch = 2 * n_kv
    cache_cmp = cache_kv_cmp.reshape(cache_kv_cmp.shape[:2] + (page * nch, hd))
    cache_sel = cache_kv_sel.reshape(cache_kv_sel.shape[:2] + (page * nch, hd))

    w_in_e_t = jnp.swapaxes(w_in_e, 1, 2).astype(BF16)
    w_in_o_t = jnp.swapaxes(w_in_o, 1, 2).astype(BF16)

    def even_tail(i):
        o_ab = conv_dim + nv
        o_rest = o_ab + 2 * n_vh
        return jnp.concatenate([w_in_e_t[i, o_rest:], w_in_e_t[i, o_ab:o_rest],
                                jnp.zeros((LANE - 2 * n_vh, d), w_in_e_t.dtype)], axis=0)

    pooled_by_layer = {}

    def run(x3, sample):
        nb, t, _ = x3.shape
        x = x3.reshape(nb * t, d).astype(F32)
        ev, od = [], []
        for l in range(depth):
            i = l // 2
            sh1, sc1, g1, sh2, sc2, g2 = mods(l, sample)
            if l % 2 == 0:
                proj = _norm_mod_matmul(x, norm_g[l, 0], sc1, sh1, w_in_e_t, i, main_cols=conv_dim + nv,
                                        w_tail=even_tail(i))
                n_cst = conv_w_gdn.shape[1] - 1
                conv_new = proj.reshape(nb, t, -1)[:, t - n_cst:, :conv_dim]
                if sample:
                    projp = jnp.pad(proj.reshape(nb, t, -1), ((0, 0), (0, c - t), (0, 0))).reshape(nb * c, -1)
                    cst, scst, s0, tv = state_gdn_conv[i], state_sconv[i], state_gdn[i], t
                else:
                    projp = proj
                    cst = jnp.zeros((nb, conv_w_gdn.shape[1] - 1, conv_dim), F32)
                    scst = jnp.zeros((nb, conv_w_sc.shape[1] - 1, sc_dim), F32)
                    s0, tv = jnp.zeros((nb, n_vh, dk, dv), F32), c
                mix, s_new, u_last = _gdn_sconv(projp, cst, scst, s0, conv_w_gdn[i], conv_w_sc[i], a_log[i],
                                                dt_bias[i], gdn_norm_g[i], tv)
                if sample:
                    mix = mix.reshape(nb, c, -1)[:, :t].reshape(nb * t, -1)
                sc_new = u_last[:, tv - scst.shape[1]:tv]
                ev.append((s_new, conv_new, sc_new))
                x = _matmul_gated_residual(mix, w_out_e, i, x, g1)
            else:
                proj, kvc8, kvs8, kvw8 = _norm_mod_matmul(x, norm_g[l, 0], sc1, sh1, w_in_o_t, i,
                                                          rows_out=(qw, 3, nch, hd))
                kv_c, kv_s, kv_w = [a.reshape(nb, t, 2, n_kv, hd) for a in (kvc8, kvs8, kvw8)]
                pj = cmp_proj[i].reshape(2 * n_kv, hd, hd)
                pw_rows = jnp.broadcast_to(cmp_pool[i].reshape(blk * nch, 1), (blk * nch, hd))
                if sample:
                    pooled = pooled_by_layer.get(i)
                    if pooled is None:
                        pooled = _pool_pages(cache_cmp, i, page_table, pw_rows, nch)
                    proj3 = proj.reshape(nb, t, -1)
                    kv_all = jnp.concatenate([state_kv_win[i].reshape(nb, wb, kv_row).astype(F32),
                                              proj3[:, :, qw + 2 * kv_row:qw + 3 * kv_row]], axis=1)
                    kv_all_p = jnp.pad(kv_all, ((0, 0), (0, (-kv_all.shape[1]) % SUBLANE), (0, 0)))
                    ocw, selm = _attn_sample_a(proj3, pooled, pj, kv_all_p, slopes_row, past, n_h, n_kv, hd, blk, wb)
                    new_rows = jnp.pad(proj3[:, :, qw + kv_row:qw + 2 * kv_row], ((0, 0), (0, (-t) % SUBLANE), (0, 0)))
                    o = _attn_sample_sel(cache_sel, i, page_table, proj3, selm, new_rows, ocw, slopes_row, past,
                                         n_h, n_kv, hd, blk).reshape(nb * t, qw)
                    kv_win_new = kv_all[:, t:].reshape(nb, wb, 2, n_kv, hd)
                else:
                    pw = jnp.broadcast_to(cmp_pool[i].reshape(blk, 2 * n_kv).T[:, :, None], (2 * n_kv, blk, hd))
                    cmp = _cmp_prompt(proj, nb, t, pw, pj, n_h)
                    n_steps = nb * n_kv * _attn_prompt_tiles(t)[1]
                    n_pool = (bd * page_table.shape[1]) // n_steps
                    if nb * n_kv == bd and n_pool * n_steps == bd * page_table.shape[1] and 1 <= n_pool <= 8:
                        o, pooled_by_layer[i] = _attn_prompt(proj, cmp, slopes_kv, nb, t, n_h, n_kv, hd, blk,
                                                             pool=(cache_cmp, i, page_table, pw_rows, nch, n_pool))
                    else:
                        o = _attn_prompt(proj, cmp, slopes_kv, nb, t, n_h, n_kv, hd, blk)
                    kv_win_new = kv_w[:, t - min(WINDOW, t):]
                od.append((kv_c, kv_s, kv_win_new))
                x = _matmul_gated_residual(o, w_out_o, i, x, g1)
            x = _mlp(x, norm_g[l, 1], sc2, sh2, g2, w_mlp1, w_mlp2, l, final_g if l == depth - 1 else None)
        y = x.reshape(nb, t, d)
        evs = [jnp.stack([s[j] for s in ev]) for j in range(3)]
        ods = [jnp.stack([s[j] for s in od]) for j in range(3)]
        return y, evs, ods

    y_p, (gdn_p, gconv_p, sconv_p), (kvc_p, kvs_p, kvw_p) = run(x_prompt, False)
    y_s, (gdn_s, gconv_s, sconv_s), (kvc_s, kvs_s, kvw_s) = run(x_sample, True)
    return (y_p, y_s, gdn_p, gdn_s, gconv_p, gconv_s, sconv_p, sconv_s, kvc_p, kvc_s, kvs_p, kvs_s, kvw_p, kvw_s)
```

```python
import functools

import jax
import jax.numpy as jnp
from jax import lax
from jax.experimental import pallas as pl
from jax.experimental.pallas import tpu as pltpu

F32 = jnp.float32
BF16 = jnp.bfloat16

EPS = 1e-6
NEG = -1e30
FORCE = 1e6
DEAD = -3e38
SEL_TOPK = 16
LOG2E = 1.4426950408889634
WINDOW = 512
GDN_CHUNK = 64
LANE = 128
SUBLANE = 8
VMEM_BUDGET = 56 * 1024 * 1024


def _params(sem):
    return pltpu.CompilerParams(dimension_semantics=sem, vmem_limit_bytes=VMEM_BUDGET)


def _pick(n, cands):
    for c in cands:
        if n % c == 0:
            return c
    return n


def _bdot(a, b):
    return jnp.dot(a.astype(BF16), b.astype(BF16), preferred_element_type=F32)


def _bdot_nt(a, b):
    return lax.dot_general(a.astype(BF16), b.astype(BF16), (((1,), (1,)), ((), ())),
                           preferred_element_type=F32)


def _silu(x):
    return x * jax.nn.sigmoid(x)


def _modnorm(x, g, scale, shift):
    y = x * lax.rsqrt(jnp.mean(x * x, axis=-1, keepdims=True) + EPS)
    return (y * g) * (1.0 + scale) + shift


def _modnorm_rows(x_ref, g_ref, sc_ref, sh_ref, hn_ref):
    tm = x_ref.shape[0]
    ch = min(tm, 256)

    def body(r, carry):
        rows = pl.ds(pl.multiple_of(r * ch, ch), ch)
        sc = sc_ref[...] if sc_ref.shape[0] == 1 else sc_ref[rows, :]
        sh = sh_ref[...] if sh_ref.shape[0] == 1 else sh_ref[rows, :]
        hn_ref[rows, :] = _modnorm(x_ref[rows, :], g_ref[...], sc, sh).astype(BF16)
        return carry

    lax.fori_loop(0, tm // ch, body, 0)


def _masked_softmax(s, valid):
    s = jnp.where(valid, s, NEG)
    m = jnp.max(s, axis=-1, keepdims=True)
    p = jnp.where(valid, jnp.exp(s - m), 0.0)
    return p / jnp.maximum(jnp.sum(p, axis=-1, keepdims=True), 1e-30)


def _ada_kernel(c_ref, w_ref, b_ref, o_ref):
    o_ref[...] = _bdot(_silu(c_ref[...]), w_ref[...]) + b_ref[...]


def _ada_mod(c_all, w_ada, b_ada):
    depth, d, n6 = w_ada.shape
    rows = c_all.shape[0]
    tn = _pick(n6, (1024, 512, 256, 128))
    return pl.pallas_call(
        _ada_kernel,
        grid=(depth, n6 // tn),
        in_specs=[pl.BlockSpec((rows, d), lambda l, j: (0, 0)),
                  pl.BlockSpec((None, d, tn), lambda l, j: (l, 0, j)),
                  pl.BlockSpec((None, 1, tn), lambda l, j: (l, 0, j))],
        out_specs=pl.BlockSpec((None, rows, tn), lambda l, j: (l, 0, j)),
        out_shape=jax.ShapeDtypeStruct((depth, rows, n6), F32),
        compiler_params=_params(("arbitrary", "arbitrary")),
        name="ada_mod",
    )(c_all, w_ada, b_ada.reshape(depth, 1, n6))


def _mm1_kernel(x_ref, g_ref, sc_ref, sh_ref, *refs, n_main, n_w, rows_first, n_rows_out, nch):
    w_refs = refs[:n_w]
    o_ref = refs[n_w]
    row_refs = refs[n_w + 1:n_w + 1 + n_rows_out]
    hn_ref = refs[-1]
    j = pl.program_id(1)

    @pl.when(j == 0)
    def _():
        _modnorm_rows(x_ref, g_ref, sc_ref, sh_ref, hn_ref)

    def emit(wt_ref):
        y = lax.dot_general(hn_ref[...], wt_ref[...].astype(BF16), (((1,), (1,)), ((), ())),
                            preferred_element_type=F32)
        o_ref[...] = y
        return y

    if n_w == 1:
        y = emit(w_refs[0])
        tm, tn = o_ref.shape
        hd = row_refs[0].shape[-1] if n_rows_out else LANE
        per_tile = tn // hd
        tiles_per_out = nch // per_tile if n_rows_out else 1
        for r in range(n_rows_out):
            for part in range(tiles_per_out):
                @pl.when(j == rows_first + r * tiles_per_out + part)
                def _(r=r, part=part):
                    for cc in range(per_tile):
                        row_refs[r][pl.ds(part * per_tile + cc, tm, stride=nch), :] = y[:, cc * hd:(cc + 1) * hd]
    else:
        @pl.when(j < n_main)
        def _():
            emit(w_refs[0])

        @pl.when(j >= n_main)
        def _():
            emit(w_refs[1])


def _mod_spec(mod, tm, rows_per_group, width, col):
    r = mod.shape[1]
    tiles = rows_per_group // tm
    if col:
        return pl.BlockSpec((None, r, width), lambda i, j: (i // tiles, 0, j))
    return pl.BlockSpec((None, r, width), lambda i, j: (i // tiles, 0, 0))


def _norm_mod_matmul(x, g, scale, shift, w, layer, main_cols=None, w_tail=None, rows_out=None):
    m, d = x.shape
    tn = 512
    groups = scale.shape[0]
    rpg = m // groups
    if w_tail is None:
        n = w.shape[1]
        n_main = pl.cdiv(n, tn)
        weights = [w]
        w_specs = [pl.BlockSpec((None, tn, d), lambda i, j: (layer, j, 0))]
    else:
        assert main_cols % tn == 0
        n = main_cols + w_tail.shape[0]
        n_main = main_cols // tn
        weights = [w, w_tail]
        w_specs = [pl.BlockSpec((None, tn, d), lambda i, j: (layer, jnp.minimum(j, n_main - 1), 0)),
                   pl.BlockSpec((tn, d), lambda i, j: (jnp.maximum(j - n_main, 0), 0))]
    n_rows_out, rows_first, nch, hd = 0, 0, 1, LANE
    out_shape = [jax.ShapeDtypeStruct((m, n), F32)]
    if rows_out is not None:
        first_col, n_rows_out, nch, hd = rows_out
        assert first_col % tn == 0 and (nch * hd) % tn == 0 and w_tail is None
        rows_first = first_col // tn
    big = n_rows_out == 0 and w.dtype == BF16
    tm = _pick(rpg, ((2048,) if big else ()) + (1024, 512, 256, 128, 64, 32, 16, 8))
    out_specs = [pl.BlockSpec((tm, tn), lambda i, j: (i, j))]
    for _ in range(n_rows_out):
        out_specs.append(pl.BlockSpec((tm * nch, hd), lambda i, j: (i, 0)))
        out_shape.append(jax.ShapeDtypeStruct((m * nch, hd), F32))
    kern = functools.partial(_mm1_kernel, n_main=n_main, n_w=len(weights), rows_first=rows_first,
                             n_rows_out=n_rows_out, nch=nch)
    outs = pl.pallas_call(
        kern,
        grid=(m // tm, pl.cdiv(n, tn)),
        in_specs=[pl.BlockSpec((tm, d), lambda i, j: (i, 0), pipeline_mode=pl.Buffered(1)),
                  pl.BlockSpec((1, d), lambda i, j: (0, 0)),
                  _mod_spec(scale, tm, rpg, d, False),
                  _mod_spec(shift, tm, rpg, d, False)] + w_specs,
        out_specs=out_specs,
        out_shape=out_shape,
        scratch_shapes=[pltpu.VMEM((tm, d), BF16)],
        compiler_params=_params(("arbitrary", "arbitrary")),
        name="norm_mod_matmul",
    )(x, g.reshape(1, d), scale, shift, *weights)
    return outs if n_rows_out else outs[0]


def _mm2_kernel(a_ref, w_ref, x_ref, gate_ref, o_ref):
    y = jnp.dot(a_ref[...], w_ref[...].astype(BF16), preferred_element_type=F32)
    o_ref[...] = x_ref[...] + gate_ref[...] * y


def _matmul_gated_residual(a, w, layer, x, gate):
    m, k = a.shape
    d = w.shape[2]
    groups = gate.shape[0]
    rpg = m // groups
    tm = _pick(rpg, (1024, 512, 256, 128, 64, 32, 16))
    tn = _pick(d, (512, 256, 128))
    return pl.pallas_call(
        _mm2_kernel,
        grid=(m // tm, d // tn),
        in_specs=[pl.BlockSpec((tm, k), lambda i, j: (i, 0)),
                  pl.BlockSpec((None, k, tn), lambda i, j: (layer, 0, j)),
                  pl.BlockSpec((tm, tn), lambda i, j: (i, j)),
                  _mod_spec(gate, tm, rpg, tn, True)],
        out_specs=pl.BlockSpec((tm, tn), lambda i, j: (i, j)),
        out_shape=jax.ShapeDtypeStruct((m, d), F32),
        compiler_params=_params(("arbitrary", "arbitrary")),
        name="matmul_gated_residual",
    )(a, w, x, gate)


def _mlp_kernel(x_ref, g_ref, sc_ref, sh_ref, gate_ref, w1_ref, w2_ref, fg_ref, *refs, final, side):
    if side:
        xs_ref, scs_ref, shs_ref, gates_ref, o_ref, os_ref, hn_ref, hns_ref = refs
    else:
        o_ref, hn_ref = refs
    i = pl.program_id(0)
    j = pl.program_id(1)
    last = pl.num_programs(1) - 1

    def mlp_tile(hn):
        h = jnp.dot(hn, w1_ref[...].astype(BF16), preferred_element_type=F32)
        h = jnp.square(jnp.maximum(h, 0.0))
        return jnp.dot(h.astype(BF16), w2_ref[...].astype(BF16), preferred_element_type=F32)

    def finish(x, gate, acc):
        y = x + gate * acc
        if final:
            y = (y * lax.rsqrt(jnp.mean(y * y, axis=-1, keepdims=True) + EPS)) * fg_ref[...]
        return y

    @pl.when(j == 0)
    def _():
        _modnorm_rows(x_ref, g_ref, sc_ref, sh_ref, hn_ref)
        o_ref[...] = jnp.zeros_like(o_ref)

    o_ref[...] += mlp_tile(hn_ref[...])

    @pl.when(j == last)
    def _():
        o_ref[...] = finish(x_ref[...], gate_ref[...], o_ref[...])

    if side:
        @pl.when((i == 0) & (j == 0))
        def _():
            _modnorm_rows(xs_ref, g_ref, scs_ref, shs_ref, hns_ref)
            os_ref[...] = jnp.zeros_like(os_ref)

        @pl.when(i == 0)
        def _():
            os_ref[...] += mlp_tile(hns_ref[...])

        @pl.when((i == 0) & (j == last))
        def _():
            os_ref[...] = finish(xs_ref[...], gates_ref[...], os_ref[...])


def _mlp(x, g, scale, shift, gate, w1, w2, layer, final_g=None, side=None):
    m, d = x.shape
    final = final_g is not None
    fg = (final_g if final else g).reshape(1, d)
    f = w1.shape[2]
    groups = scale.shape[0]
    rpg = m // groups
    tm = _pick(rpg, (1024, 512, 256, 128, 64, 32, 16, 8))
    tf = _pick(f, (512, 256, 128))
    in_specs = [pl.BlockSpec((tm, d), lambda i, j: (i, 0)),
                pl.BlockSpec((1, d), lambda i, j: (0, 0)),
                _mod_spec(scale, tm, rpg, d, False),
                _mod_spec(shift, tm, rpg, d, False),
                _mod_spec(gate, tm, rpg, d, False),
                pl.BlockSpec((None, d, tf), lambda i, j: (layer, 0, j)),
                pl.BlockSpec((None, tf, d), lambda i, j: (layer, j, 0)),
                pl.BlockSpec((1, d), lambda i, j: (0, 0))]
    out_specs = [pl.BlockSpec((tm, d), lambda i, j: (i, 0), pipeline_mode=pl.Buffered(1))]
    out_shape = [jax.ShapeDtypeStruct((m, d), F32)]
    scratch = [pltpu.VMEM((tm, d), BF16)]
    operands = [x, g.reshape(1, d), scale, shift, gate, w1, w2, fg]
    if side is not None:
        xs = side[0]
        ms = xs.shape[0]
        in_specs.append(pl.BlockSpec((ms, d), lambda i, j: (0, 0)))
        in_specs += [pl.BlockSpec((None, ms, d), lambda i, j: (0, 0, 0))] * 3
        out_specs.append(pl.BlockSpec((ms, d), lambda i, j: (0, 0)))
        out_shape.append(jax.ShapeDtypeStruct((ms, d), F32))
        scratch.append(pltpu.VMEM((ms, d), BF16))
        operands += list(side)
    outs = pl.pallas_call(
        functools.partial(_mlp_kernel, final=final, side=side is not None),
        grid=(m // tm, f // tf),
        in_specs=in_specs,
        out_specs=out_specs,
        out_shape=out_shape,
        scratch_shapes=scratch,
        compiler_params=_params(("arbitrary", "arbitrary")),
        name="mlp",
    )(*operands)
    return outs if side is not None else outs[0]


def _split3(x):
    hi = x.astype(BF16)
    r = x - hi.astype(F32)
    mid = r.astype(BF16)
    lo = (r - mid.astype(F32)).astype(BF16)
    return hi, mid, lo


def _gdn_kernel(qkv_ref, z_ref, ab_ref, gb_ref, gcg_ref, hx_ref, cst_ref, scst_ref, cw_ref, scw_ref,
                alog_ref, dtb_ref, ng_ref, s0_ref,
                mix_ref, sout_ref, ulast_ref,
                xbuf, ubuf, s_ref, *, t_valid, n_kh, n_vh, dk, dv, n_sub):
    c = GDN_CHUNK
    rows = n_sub * c
    n = pl.program_id(1)
    nqk = n_kh * dk
    nv = n_vh * dv
    rep = n_vh // n_kh
    hpg = 2
    gw = hpg * c
    n_conv = cw_ref.shape[0]
    n_sc = scw_ref.shape[0]

    @pl.when(n == 0)
    def _():
        xbuf[0:SUBLANE, :] = cst_ref[...]
        ubuf[0:SUBLANE, :] = scst_ref[...]
        s_ref[...] = s0_ref[...]

    xbuf[SUBLANE:SUBLANE + rows, :] = qkv_ref[...]
    cw = cw_ref[...]
    off = SUBLANE - (n_conv - 1)
    xc_all = cw[0:1, :] * xbuf[off:off + rows, :]
    for j in range(1, n_conv):
        xc_all = xc_all + cw[j:j + 1, :] * xbuf[off + j:off + j + rows, :]
    xbuf[0:SUBLANE, :] = xbuf[rows:rows + SUBLANE, :]
    xc_all = _silu(xc_all)

    u = gcg_ref[...] * hx_ref[...]
    ubuf[SUBLANE:SUBLANE + rows, :] = u
    scw = scw_ref[...]
    offs = SUBLANE - (n_sc - 1)
    cu = scw[0:1, :] * ubuf[offs:offs + rows, :]
    for j in range(1, n_sc):
        cu = cu + scw[j:j + 1, :] * ubuf[offs + j:offs + j + rows, :]
    ubuf[0:SUBLANE, :] = ubuf[rows:rows + SUBLANE, :]
    ulast_ref[...] = u[rows - c:, :]
    mix_ref[:, nv:] = (gb_ref[...] * cu).astype(mix_ref.dtype)

    ab = ab_ref[...]
    g_rows = -jnp.exp(alog_ref[...]) * jax.nn.softplus(ab + dtb_ref[...])
    beta_rows = jax.nn.sigmoid(ab)
    if t_valid < rows:
        rowmask = lax.broadcasted_iota(jnp.int32, (rows, 1), 0) < t_valid
        xc_all = jnp.where(rowmask, xc_all, 0.0)
        g_rows = jnp.where(rowmask, g_rows, 0.0)
        beta_rows = jnp.where(rowmask, beta_rows, 0.0)

    ri = lax.broadcasted_iota(jnp.int32, (c, c), 0)
    ci = lax.broadcasted_iota(jnp.int32, (c, c), 1)
    tril = jnp.where(ri >= ci, 1.0, 0.0).astype(BF16)

    def l2n(x):
        return x * lax.rsqrt(jnp.sum(x * x, axis=-1, keepdims=True) + EPS)

    gi = lax.broadcasted_iota(jnp.int32, (gw, gw), 0)
    gj = lax.broadcasted_iota(jnp.int32, (gw, gw), 1)
    same = (gi // c) == (gj // c)
    low_incl = same & (gi >= gj)
    low_strict = same & (gi > gj)
    lane_blk = lax.broadcasted_iota(jnp.int32, (dk, gw), 1) // c
    level_mask = []
    size = 1
    while size < c:
        level_mask.append(((gi // (2 * size)) == (gj // (2 * size))) & ((gi // size) != (gj // size)) & (gi > gj))
        size *= 2

    def stack(cols):
        return jnp.concatenate(cols, axis=0)

    n_grp = n_vh // hpg
    pre, glasts = [], []
    for sub in range(n_sub):
        r0 = sub * c
        xc = xc_all[r0:r0 + c]
        beta_all = beta_rows[r0:r0 + c]
        ghi, gmid, glo = _split3(g_rows[r0:r0 + c])
        gcum = (jnp.dot(tril, ghi, preferred_element_type=F32)
                + jnp.dot(tril, gmid, preferred_element_type=F32)
                + jnp.dot(tril, glo, preferred_element_type=F32))
        glast = gcum[c - 1:c, :]
        glasts.append(glast)
        for grp in range(n_grp):
            heads = [grp * hpg + j for j in range(hpg)]
            kheads = [h // rep for h in heads]
            qs = {kh: l2n(xc[:, kh * dk:(kh + 1) * dk]) * (dk ** -0.5) for kh in set(kheads)}
            ks = {kh: l2n(xc[:, nqk + kh * dk:nqk + (kh + 1) * dk]) for kh in set(kheads)}
            q_st = stack([qs[kh] for kh in kheads])
            k_st = stack([ks[kh] for kh in kheads])
            v_st = stack([xc[:, 2 * nqk + h * dv:2 * nqk + (h + 1) * dv] for h in heads])
            beta_st = stack([beta_all[:, n_vh + h:n_vh + h + 1] for h in heads])
            gc_st = stack([gcum[:, h:h + 1] for h in heads])
            gl_st = stack([jnp.broadcast_to(glast[:, h:h + 1], (c, 1)) for h in heads])
            gc_row = jnp.broadcast_to(gc_st, (gw, LANE)).T[0:1, :]
            diff = gc_st - gc_row
            decay = jnp.where(low_incl, jnp.exp(jnp.where(low_incl, diff, 0.0)), 0.0)
            egc = jnp.exp(gc_st)
            kb_st = k_st * beta_st

            nmat = _bdot_nt(kb_st, k_st) * jnp.where(low_strict, decay, 0.0)
            qk = _bdot_nt(q_st, k_st) * decay
            rhs = jnp.concatenate([v_st * beta_st, kb_st * egc], axis=1)
            pre.append((nmat, qk, rhs, q_st * egc, k_st * jnp.exp(gl_st - gc_st)))

    tms = [-jnp.where(level_mask[0], p[0], 0.0) for p in pre]
    for lm in level_mask[1:]:
        lls = [jnp.where(lm, p[0], 0.0) for p in pre]
        ys = [ll + _bdot(tm, ll) for tm, ll in zip(tms, lls)]
        tms = [tm - y - _bdot(y, tm) for tm, y in zip(tms, ys)]
    sols = [p[2] + _bdot(tm, p[2]) for p, tm in zip(pre, tms)]

    for sub in range(n_sub):
        r0 = sub * c
        glast = glasts[sub]
        for grp in range(n_grp):
            heads = [grp * hpg + j for j in range(hpg)]
            nmat, qk, rhs, qg_st, kd_st = pre[sub * n_grp + grp]
            sol = sols[sub * n_grp + grp]
            u_st = sol[:, :dv]
            w_st = sol[:, dv:]

            vnew, qs_out = [], []
            for j, h in enumerate(heads):
                wq = jnp.concatenate([w_st[j * c:(j + 1) * c], qg_st[j * c:(j + 1) * c]], axis=0)
                r2 = _bdot(wq, s_ref[h])
                vnew.append(u_st[j * c:(j + 1) * c] - r2[:c])
                qs_out.append(r2[c:])
            vnew_st = stack(vnew)
            o_st = stack(qs_out) + _bdot(qk, vnew_st)
            kd_t = kd_st.T
            for j, h in enumerate(heads):
                upd = _bdot(jnp.where(lane_blk == j, kd_t, 0.0), vnew_st)
                s_ref[h] = s_ref[h] * jnp.exp(glast[:, h:h + 1]) + upd
                o_h = o_st[j * c:(j + 1) * c]
                on = (o_h * lax.rsqrt(jnp.mean(o_h * o_h, axis=-1, keepdims=True) + EPS)) * ng_ref[...]
                zz = z_ref[r0:r0 + c, h * dv:(h + 1) * dv]
                mix_ref[r0:r0 + c, h * dv:(h + 1) * dv] = (on * _silu(zz)).astype(mix_ref.dtype)

    sout_ref[...] = s_ref[...]


def _gdn_sconv(proj, conv_state, sc_state, s0, conv_w, sc_w, a_log, dt_bias, norm_g, t_valid):
    bsz, n_vh, dk, dv = s0.shape
    n_conv, conv_dim = conv_w.shape
    n_sc, sc = sc_w.shape
    nv = n_vh * dv
    n_kh = (conv_dim - nv) // (2 * dk)
    c = GDN_CHUNK
    rows = proj.shape[0]
    nchunk = rows // (bsz * c)
    assert conv_dim % nv == 0 and (conv_dim + nv) % sc == 0 and n_vh % 2 == 0 and 2 * n_vh <= LANE
    n_sub = 2 if (nchunk % 2 == 0 and t_valid == c) else 1
    assert t_valid == c or nchunk == 1
    nstep = nchunk // n_sub
    rs = n_sub * c
    zb = conv_dim // nv
    gbb = (conv_dim + nv) // sc
    abb = (conv_dim + nv + 3 * sc) // LANE
    pad_lane = lambda v: jnp.pad(v.reshape(1, -1), ((0, 0), (0, LANE - v.shape[-1])))
    cst = jnp.pad(conv_state.astype(F32), ((0, 0), (SUBLANE - (n_conv - 1), 0), (0, 0)))
    scst = jnp.pad(sc_state.astype(F32), ((0, 0), (SUBLANE - (n_sc - 1), 0), (0, 0)))
    row = lambda b, n: b * nstep + n
    kern = functools.partial(_gdn_kernel, t_valid=t_valid if n_sub == 1 else rs, n_kh=n_kh, n_vh=n_vh, dk=dk,
                             dv=dv, n_sub=n_sub)
    return pl.pallas_call(
        kern,
        grid=(bsz, nstep),
        in_specs=[pl.BlockSpec((rs, conv_dim), lambda b, n: (row(b, n), 0)),
                  pl.BlockSpec((rs, nv), lambda b, n: (row(b, n), zb)),
                  pl.BlockSpec((rs, LANE), lambda b, n: (row(b, n), abb)),
                  pl.BlockSpec((rs, sc), lambda b, n: (row(b, n), gbb)),
                  pl.BlockSpec((rs, sc), lambda b, n: (row(b, n), gbb + 1)),
                  pl.BlockSpec((rs, sc), lambda b, n: (row(b, n), gbb + 2)),
                  pl.BlockSpec((None, SUBLANE, conv_dim), lambda b, n: (b, 0, 0)),
                  pl.BlockSpec((None, SUBLANE, sc), lambda b, n: (b, 0, 0)),
                  pl.BlockSpec((n_conv, conv_dim), lambda b, n: (0, 0)),
                  pl.BlockSpec((n_sc, sc), lambda b, n: (0, 0)),
                  pl.BlockSpec((1, LANE), lambda b, n: (0, 0)),
                  pl.BlockSpec((1, LANE), lambda b, n: (0, 0)),
                  pl.BlockSpec((1, dv), lambda b, n: (0, 0)),
                  pl.BlockSpec((None, n_vh, dk, dv), lambda b, n: (b, 0, 0, 0))],
        out_specs=[pl.BlockSpec((rs, nv + sc), lambda b, n: (row(b, n), 0)),
                   pl.BlockSpec((None, n_vh, dk, dv), lambda b, n: (b, 0, 0, 0)),
                   pl.BlockSpec((None, c, sc), lambda b, n: (b, 0, 0))],
        out_shape=[jax.ShapeDtypeStruct((rows, nv + sc), BF16),
                   jax.ShapeDtypeStruct((bsz, n_vh, dk, dv), F32),
                   jax.ShapeDtypeStruct((bsz, c, sc), F32)],
        scratch_shapes=[pltpu.VMEM((rs + SUBLANE, conv_dim), F32),
                        pltpu.VMEM((rs + SUBLANE, sc), F32),
                        pltpu.VMEM((n_vh, dk, dv), F32)],
        compiler_params=_params(("arbitrary", "arbitrary")),
        name="gdn_sconv",
    )(proj, proj, proj, proj, proj, proj, cst, scst, conv_w, sc_w,
      pad_lane(a_log), pad_lane(dt_bias), norm_g.reshape(1, dv), s0.astype(F32))


def _cmp_prompt_kernel(kv_ref, pw_ref, pj_ref, o_ref, *, blk):
    t, hd = kv_ref.shape
    x = kv_ref[...].reshape(t // blk, blk, hd) * pw_ref[...][None]
    o_ref[...] = _bdot(jnp.sum(x, axis=1), pj_ref[...])


def _cmp_prompt(proj, bsz, t, pw, pj, col0):
    nch, blk, hd = pw.shape
    return pl.pallas_call(
        functools.partial(_cmp_prompt_kernel, blk=blk),
        grid=(bsz, nch),
        in_specs=[pl.BlockSpec((t, hd), lambda b, ch: (b, col0 + ch)),
                  pl.BlockSpec((None, blk, hd), lambda b, ch: (ch, 0, 0)),
                  pl.BlockSpec((None, hd, hd), lambda b, ch: (ch, 0, 0))],
        out_specs=pl.BlockSpec((None, None, t // blk, hd), lambda b, ch: (b, ch, 0, 0)),
        out_shape=jax.ShapeDtypeStruct((bsz, nch, t // blk, hd), F32),
        compiler_params=_params(("arbitrary", "arbitrary")),
        name="cmp_prompt",
    )(proj, pw, pj)


_NT = (((1,), (1,)), ((), ()))


def _slope_features(sl2, lane):
    hi = sl2.astype(BF16).astype(F32)
    lo = sl2 - hi
    return jnp.where(lane == 0, 64.0 * hi, jnp.where(lane == 1, hi, jnp.where(lane == 2, 64.0 * lo,
                                                                             jnp.where(lane == 3, lo, 0.0))))


def _pool_blocks(page_refs, pw_ref, o_ref):
    pw = pw_ref[...]
    rows_blk = pw.shape[0]
    per_page = page_refs[0].shape[0] // rows_blk
    nch = o_ref.shape[0] // (len(page_refs) * per_page)
    for k, page_ref in enumerate(page_refs):
        for hb in range(per_page):
            r = k * per_page + hb
            x = page_ref[hb * rows_blk:(hb + 1) * rows_blk, :] * pw
            o_ref[r * nch:(r + 1) * nch, :] = jnp.sum(x.reshape(rows_blk // nch, nch, x.shape[-1]), axis=0)


def _attn_prompt_kernel(*refs, tq, tk, wk, blk, n_g, n_kv, hd, n_pool):
    if n_pool:
        refs = refs[1:]
    q_ref, kc_ref, vc_ref, ks_ref, vs_ref, kw_ref, vw_ref, gt_ref, sl_ref, pos_ref, oh_ref = refs[:11]
    o_ref = refs[11 + n_pool + (1 if n_pool else 0)]
    if n_pool:
        _pool_blocks(refs[11:11 + n_pool], refs[11 + n_pool], refs[-1])
    h = pl.program_id(1)
    qt = pl.program_id(2)
    q0 = qt * tq
    nc = kc_ref.shape[0]
    t_all = ks_ref.shape[0]
    sl = sl_ref[...]
    lane = lax.broadcasted_iota(jnp.int32, (1, LANE), 1)
    qs = [q_ref[:, g * hd:(g + 1) * hd] * (hd ** -0.5) for g in range(n_g)]
    qb = jnp.concatenate(qs, axis=0).astype(BF16)
    qaug = jnp.concatenate(
        [jnp.concatenate([(qs[g] * LOG2E).astype(BF16),
                          jnp.broadcast_to(_slope_features(sl[:, g:g + 1] * LOG2E, lane), (tq, LANE)).astype(BF16)],
                         axis=1) for g in range(n_g)], axis=0)
    qpos = q0 + lax.broadcasted_iota(jnp.int32, (tq, 1), 0)

    bidx = lax.broadcasted_iota(jnp.int32, (1, nc), 1)
    start = bidx * blk
    s_c = _bdot_nt(qb, kc_ref[...])
    dist_c = jnp.abs(qpos.astype(F32) - (start.astype(F32) + (blk - 1) / 2))
    valid_c = (start + (blk - 1)) <= qpos
    p_c = jnp.concatenate([_masked_softmax(s_c[g * tq:(g + 1) * tq] - sl[:, g:g + 1] * dist_c, valid_c)
                           for g in range(n_g)], axis=0)
    o_c = _bdot(p_c, vc_ref[...])
    imp = p_c[0:tq]
    for g in range(1, n_g):
        imp = imp + p_c[g * tq:(g + 1) * tq]

    imp_t = jnp.concatenate([imp, jnp.zeros((tq, LANE - nc), F32)], axis=1).T[0:nc, :]
    bcol = lax.broadcasted_iota(jnp.int32, (nc, 1), 0)
    cur = (q0 + lax.broadcasted_iota(jnp.int32, (1, tq), 1)) // blk
    score = jnp.where(bcol < cur, imp_t, NEG)
    score = jnp.where((bcol == 0) | (bcol == cur), FORCE, score)
    rank = jnp.zeros((nc, tq), F32)
    for i in range(nc):
        ri = score[i:i + 1, :]
        rank = rank + jnp.where((ri > score) | ((ri == score) & (bcol > i)), 1.0, 0.0)
    selneg_t = jnp.where((rank < min(SEL_TOPK, nc)) & (score > 0.5 * NEG), 0.0, NEG)
    selneg = jnp.concatenate([selneg_t, jnp.zeros((LANE - nc, tq), F32)], axis=0).T.astype(BF16)

    def flash_step(k_ref, v_ref, k0, size, bias, carry):
        m, l, acc = carry
        kaug = jnp.concatenate([k_ref[pl.ds(k0, size), :].astype(BF16), pos_ref[pl.ds(k0, size), :]], axis=1)
        s = lax.dot_general(qaug, kaug, _NT, preferred_element_type=F32)
        gs = range(n_g)
        sgs = [s[g * tq:(g + 1) * tq] + bias for g in gs]
        m2 = [jnp.maximum(m[g], jnp.max(sgs[g], axis=-1, keepdims=True)) for g in gs]
        als = [jnp.exp2(m[g] - m2[g]) for g in gs]
        pfs = [jnp.exp2(sgs[g] - m2[g]) for g in gs]
        l2 = [als[g] * l[g] + jnp.sum(pfs[g], axis=-1, keepdims=True) for g in gs]
        ps = [p.astype(BF16) for p in pfs]
        accs = [als[g] * acc[g * tq:(g + 1) * tq] for g in gs]
        pv = jnp.dot(jnp.concatenate(ps, axis=0), v_ref[pl.ds(k0, size), :].astype(BF16),
                     preferred_element_type=F32)
        return tuple(m2), tuple(l2), jnp.concatenate(accs, axis=0) + pv

    def finish(carry):
        _, l, acc = carry
        return [acc[g * tq:(g + 1) * tq] / jnp.maximum(l[g], 1e-30) for g in range(n_g)]

    init = (tuple(jnp.full((tq, 1), NEG, F32) for _ in range(n_g)),
            tuple(jnp.zeros((tq, 1), F32) for _ in range(n_g)), jnp.zeros((n_g * tq, hd), F32))

    def sel_bias(k0):
        return lax.dot_general(selneg, oh_ref[pl.ds(k0, tk), :], _NT, preferred_element_type=F32)

    def sel_chunk(ci, carry):
        k0 = pl.multiple_of(ci * tk, tk)
        return flash_step(ks_ref, vs_ref, k0, tk, sel_bias(k0), carry)

    n_full = q0 // tk
    carry = lax.fori_loop(0, n_full, sel_chunk, init)
    k0 = pl.multiple_of(n_full * tk, tk)
    kpos = k0 + lax.broadcasted_iota(jnp.int32, (1, tk), 1)
    o_s = finish(flash_step(ks_ref, vs_ref, k0, tk, sel_bias(k0) + jnp.where(kpos <= qpos, 0.0, NEG), carry))

    w0 = pl.multiple_of(jnp.clip(q0 + tq - wk, 0, t_all - wk), tq)
    dist = qpos - (w0 + lax.broadcasted_iota(jnp.int32, (1, wk), 1))
    o_w = finish(flash_step(kw_ref, vw_ref, w0, wk, jnp.where((dist >= 0) & (dist < WINDOW), 0.0, NEG), init))

    gt = jax.nn.sigmoid(pltpu.roll(gt_ref[...], (LANE - n_g * h) % LANE, 1))
    n_h = n_g * n_kv
    for g in range(n_g):
        r = slice(g * tq, (g + 1) * tq)
        o = (gt[:, g:g + 1] * o_c[r] + gt[:, n_h + g:n_h + g + 1] * o_s[g]
             + gt[:, 2 * n_h + g:2 * n_h + g + 1] * o_w[g])
        o_ref[:, g * hd:(g + 1) * hd] = o.astype(o_ref.dtype)


def _attn_prompt_tiles(t):
    return 128, t // 128


def _attn_prompt(proj, cmp, slopes, bsz, t, n_h, n_kv, hd, blk, pool=None):
    n_g = n_h // n_kv
    tq, nqt = _attn_prompt_tiles(t)
    tk = _pick(t, (512, 256, 128))
    wk = WINDOW + tq
    assert t % tq == 0 and t >= wk and t // blk <= LANE and blk == 64 and n_h * 3 <= LANE
    nc = t // blk
    kvb = n_h
    pos = jnp.arange(t, dtype=jnp.int32)[:, None]
    lane = jnp.arange(LANE, dtype=jnp.int32)[None, :]
    pos_tab = jnp.where(lane < 4, jnp.where(lane % 2 == 0, pos // 64, pos % 64), 0).astype(BF16)
    onehot = (lane == pos // blk).astype(BF16)
    gate_blk = (n_h * hd + 6 * n_kv * hd) // LANE
    n_pool = pool[5] if pool else 0
    kern = functools.partial(_attn_prompt_kernel, tq=tq, tk=tk, wk=wk, blk=blk, n_g=n_g, n_kv=n_kv, hd=hd,
                             n_pool=n_pool)
    kv_spec = lambda off: pl.BlockSpec((t, hd), lambda b, h, q, *_: (b, kvb + off + h))
    in_specs = [pl.BlockSpec((tq, n_g * hd), lambda b, h, q, *_: (b * nqt + q, h)),
                pl.BlockSpec((None, None, nc, hd), lambda b, h, q, *_: (b, h, 0, 0)),
                pl.BlockSpec((None, None, nc, hd), lambda b, h, q, *_: (b, n_kv + h, 0, 0)),
                kv_spec(2 * n_kv), kv_spec(3 * n_kv), kv_spec(4 * n_kv), kv_spec(5 * n_kv),
                pl.BlockSpec((tq, LANE), lambda b, h, q, *_: (b * nqt + q, gate_blk)),
                pl.BlockSpec((None, 1, LANE), lambda b, h, q, *_: (h, 0, 0)),
                pl.BlockSpec((t, LANE), lambda b, h, q, *_: (0, 0)),
                pl.BlockSpec((t, LANE), lambda b, h, q, *_: (0, 0))]
    out_specs = [pl.BlockSpec((tq, n_g * hd), lambda b, h, q, *_: (b * nqt + q, h))]
    out_shape = [jax.ShapeDtypeStruct((bsz * t, n_h * hd), BF16)]
    operands = [proj, cmp, cmp, proj, proj, proj, proj, proj, slopes, pos_tab, onehot]
    prefetch = []
    if pool:
        cache, layer, page_table, pw_rows, nch, _ = pool
        _, _, prow, _ = cache.shape
        bd, n_pages = page_table.shape
        rows_blk = pw_rows.shape[0]
        assert bsz * n_kv == bd and nqt * n_pool == n_pages
        out_rows = n_pool * (prow // rows_blk) * nch
        for k in range(n_pool):
            in_specs.append(pl.BlockSpec((None, None, prow, hd),
                                         lambda b, h, q, pt, k=k: (layer, pt[b * n_kv + h, q * n_pool + k], 0, 0)))
        in_specs.append(pl.BlockSpec((rows_blk, hd), lambda b, h, q, pt: (0, 0)))
        out_specs.append(pl.BlockSpec((None, out_rows, hd), lambda b, h, q, pt: (b * n_kv + h, q, 0)))
        out_shape.append(jax.ShapeDtypeStruct((bd, nqt * out_rows, hd), F32))
        operands += [cache] * n_pool + [pw_rows]
        prefetch = [page_table]
    outs = pl.pallas_call(
        kern,
        grid_spec=pltpu.PrefetchScalarGridSpec(num_scalar_prefetch=len(prefetch), grid=(bsz, n_kv, nqt),
                                               in_specs=in_specs, out_specs=out_specs),
        out_shape=out_shape,
        compiler_params=_params(("arbitrary", "arbitrary", "arbitrary")),
        name="attn_prompt",
    )(*prefetch, *operands)
    return outs if pool else outs[0]


def _pool_pages_kernel(pt_ref, *refs, n_pp, rows_blk):
    _pool_blocks(refs[:n_pp], refs[n_pp], refs[n_pp + 1])


def _page_map(b, p, pt, *, layer, k, n_pp):
    return (layer, pt[b, p * n_pp + k], 0, 0)


def _pool_pages(cache, layer, page_table, pw_rows, nch):
    _, _, prow, hd = cache.shape
    bd, n_pages = page_table.shape
    rows_blk = pw_rows.shape[0]
    n_pp = _pick(n_pages, (8, 4, 2, 1))
    per_page = prow // rows_blk
    in_specs = [pl.BlockSpec((None, None, prow, hd),
                             functools.partial(_page_map, layer=layer, k=k, n_pp=n_pp)) for k in range(n_pp)]
    in_specs.append(pl.BlockSpec((rows_blk, hd), lambda b, p, pt: (0, 0)))
    return pl.pallas_call(
        functools.partial(_pool_pages_kernel, n_pp=n_pp, rows_blk=rows_blk),
        grid_spec=pltpu.PrefetchScalarGridSpec(
            num_scalar_prefetch=1, grid=(bd, n_pages // n_pp), in_specs=in_specs,
            out_specs=pl.BlockSpec((None, n_pp * per_page * nch, hd), lambda b, p, pt: (b, p, 0))),
        out_shape=jax.ShapeDtypeStruct((bd, n_pages * per_page * nch, hd), F32),
        compiler_params=_params(("arbitrary", "arbitrary")),
        name="pool_pages",
    )(page_table, *([cache] * n_pp), pw_rows)


def _attn_sample_a_kernel(q_ref, pooled_ref, pj_ref, kvw_ref, gt_ref, sl_ref, ocw_ref, selm_ref,
                          score_buf, *, past, n_q, n_kv, n_g, hd, blk, wb, nbp):
    nch = 2 * n_kv
    nc = pooled_ref.shape[0] // nch
    n_blocks = (past + n_q + blk - 1) // blk
    kvw_rows = kvw_ref.shape[0]
    half = n_kv * hd
    rep = lambda x: jnp.concatenate([x] * n_g, axis=0)
    sl = sl_ref[...]
    gt = jax.nn.sigmoid(gt_ref[...])
    qpos = past + lax.broadcasted_iota(jnp.int32, (n_q, 1), 0)
    qpos_r = rep(qpos)
    bidx_c = lax.broadcasted_iota(jnp.int32, (1, nc), 1)
    start = bidx_c * blk
    bidx = lax.broadcasted_iota(jnp.int32, (1, nbp), 1)
    cur = qpos // blk
    score_buf[...] = jnp.full(score_buf.shape, DEAD, F32)

    for h in range(n_kv):
        qb = jnp.concatenate([q_ref[:, (h * n_g + g) * hd:(h * n_g + g + 1) * hd] for g in range(n_g)], axis=0)
        qb = (qb * (hd ** -0.5)).astype(BF16)
        slope = jnp.concatenate(
            [jnp.broadcast_to(sl[:, h * n_g + g:h * n_g + g + 1], (n_q, 1)) for g in range(n_g)], axis=0)
        kc = _bdot(pooled_ref[pl.ds(h, nc, stride=nch), :], pj_ref[h])
        vc = _bdot(pooled_ref[pl.ds(n_kv + h, nc, stride=nch), :], pj_ref[n_kv + h])
        s = _bdot_nt(qb, kc)
        dist_c = qpos_r.astype(F32) - (start.astype(F32) + (blk - 1) / 2)
        s = s - slope * jnp.abs(dist_c)
        p_c = _masked_softmax(s, (start + (blk - 1)) <= qpos_r)
        o_c = _bdot(p_c, vc)
        imp = p_c[0:n_q]
        for g in range(1, n_g):
            imp = imp + p_c[g * n_q:(g + 1) * n_q]
        imp = jnp.concatenate([imp, jnp.zeros((n_q, nbp - nc), F32)], axis=1)
        score = jnp.where(bidx < cur, imp, NEG)
        score = jnp.where((bidx == 0) | (bidx == cur), FORCE, score)
        score = jnp.where(bidx < n_blocks, score, DEAD)
        score_buf[h * n_q:(h + 1) * n_q, :] = score

        kw = kvw_ref[:, h * hd:(h + 1) * hd]
        vw = kvw_ref[:, half + h * hd:half + (h + 1) * hd]
        kidx = lax.broadcasted_iota(jnp.int32, (1, kvw_rows), 1)
        dist = qpos_r - (past - wb + kidx)
        valid = (dist >= 0) & (dist < WINDOW) & (kidx < wb + n_q)
        sw = _bdot_nt(qb, kw) - slope * jnp.abs(dist).astype(F32)
        o_w = _bdot(_masked_softmax(sw, valid), vw)
        for g in range(n_g):
            hh = h * n_g + g
            r = slice(g * n_q, (g + 1) * n_q)
            ocw_ref[:, hh * hd:(hh + 1) * hd] = (gt[:, hh:hh + 1] * o_c[r]
                                                 + gt[:, 2 * n_kv * n_g + hh:2 * n_kv * n_g + hh + 1] * o_w[r])

    sc_all = score_buf[...]
    sc_t = sc_all.T
    ii = lax.broadcasted_iota(jnp.int32, (nbp, 1), 0)
    for h in range(n_kv):
        for t in range(n_q):
            r = h * n_q + t
            col = sc_t[:, r:r + 1]
            row = sc_all[r:r + 1, :]
            before = (col > row) | ((col == row) & (ii < bidx))
            rank = jnp.sum(jnp.where(before, 1.0, 0.0), axis=0, keepdims=True)
            sel = jnp.where((rank < min(SEL_TOPK, n_blocks)) & (row > 0.5 * NEG), 1.0, 0.0)
            for g in range(n_g):
                selm_ref[h, g * n_q + t:g * n_q + t + 1, :] = sel


def _attn_sample_a(proj3, pooled, pj, kv_all, slopes_row, past, n_h, n_kv, hd, blk, wb):
    bd, n_q, _ = proj3.shape
    n_g = n_h // n_kv
    n_blocks = (past + n_q + blk - 1) // blk
    nbp = pl.cdiv(n_blocks, LANE) * LANE
    kvw_rows = kv_all.shape[1]
    qw = n_h * hd
    assert n_kv * n_q <= LANE
    kern = functools.partial(_attn_sample_a_kernel, past=past, n_q=n_q, n_kv=n_kv, n_g=n_g, hd=hd, blk=blk,
                             wb=wb, nbp=nbp)
    return pl.pallas_call(
        kern,
        grid=(bd,),
        in_specs=[pl.BlockSpec((None, n_q, qw), lambda b: (b, 0, 0)),
                  pl.BlockSpec((None, pooled.shape[1], hd), lambda b: (b, 0, 0)),
                  pl.BlockSpec((2 * n_kv, hd, hd), lambda b: (0, 0, 0)),
                  pl.BlockSpec((None, kvw_rows, 2 * n_kv * hd), lambda b: (b, 0, 0)),
                  pl.BlockSpec((None, n_q, LANE), lambda b: (b, 0, (qw + 6 * n_kv * hd) // LANE)),
                  pl.BlockSpec((1, LANE), lambda b: (0, 0))],
        out_specs=[pl.BlockSpec((None, n_q, qw), lambda b: (b, 0, 0)),
                   pl.BlockSpec((None, n_kv, n_g * n_q, nbp), lambda b: (b, 0, 0, 0))],
        out_shape=[jax.ShapeDtypeStruct((bd, n_q, qw), F32),
                   jax.ShapeDtypeStruct((bd, n_kv, n_g * n_q, nbp), F32)],
        scratch_shapes=[pltpu.VMEM((LANE, nbp), F32)],
        compiler_params=_params(("arbitrary",)),
        name="attn_sample_a",
    )(proj3, pooled, pj, kv_all, proj3, slopes_row)


def _attn_sample_sel_kernel(pt_ref, *refs, n_pp, past, n_q, n_kv, n_g, hd, blk):
    pages = refs[:n_pp]
    q_ref, selm_ref, new_ref, ocw_ref, gt_ref, sl_ref, o_ref, m_ref, l_ref, acc_ref = refs[n_pp:]
    p = pl.program_id(1)
    nch = 2 * n_kv
    page = pages[0].shape[0] // nch
    keys = n_pp * page
    nbp = selm_ref.shape[-1]
    rows = n_g * n_q
    half = n_kv * hd
    rep = lambda x: jnp.concatenate([x] * n_g, axis=0)
    sl = sl_ref[...]
    qpos_r = rep(past + lax.broadcasted_iota(jnp.int32, (n_q, 1), 0))

    @pl.when(p == 0)
    def _():
        m_ref[...] = jnp.full(m_ref.shape, NEG, F32)
        l_ref[...] = jnp.zeros_like(l_ref)
        acc_ref[...] = jnp.zeros_like(acc_ref)

    def heads():
        for h in range(n_kv):
            qb = jnp.concatenate([q_ref[:, (h * n_g + g) * hd:(h * n_g + g + 1) * hd] for g in range(n_g)],
                                 axis=0) * (hd ** -0.5)
            slope = jnp.concatenate(
                [jnp.broadcast_to(sl[:, h * n_g + g:h * n_g + g + 1], (n_q, 1)) for g in range(n_g)], axis=0)
            yield h, qb, slope

    def update(h, sc, mask, pv_fn):
        sc = jnp.where(mask, sc, NEG)
        m = m_ref[h]
        m_new = jnp.maximum(m, jnp.max(sc, axis=-1, keepdims=True))
        alpha = jnp.exp(m - m_new)
        pr = jnp.where(mask, jnp.exp(sc - m_new), 0.0)
        l_ref[h] = alpha * l_ref[h] + jnp.sum(pr, axis=-1, keepdims=True)
        acc_ref[h] = alpha * acc_ref[h] + pv_fn(pr)
        m_ref[h] = m_new

    k0 = p * keys
    kk = lax.broadcasted_iota(jnp.int32, (nbp, keys), 1)
    nn = lax.broadcasted_iota(jnp.int32, (nbp, keys), 0)
    expand = jnp.where(nn == (k0 + kk) // blk, 1.0, 0.0).astype(BF16)
    dist = qpos_r - (k0 + lax.broadcasted_iota(jnp.int32, (1, keys), 1))
    staged = []
    for h, qb, slope in heads():
        mask = (jnp.dot(selm_ref[h].astype(BF16), expand, preferred_element_type=F32) > 0.5) & (dist >= 0)
        kx = jnp.concatenate([pages[k][pl.ds(h, page, stride=nch), :].astype(BF16) for k in range(n_pp)], axis=0)
        sc = jnp.where(mask, _bdot_nt(qb, kx) - slope * dist.astype(F32), NEG)
        staged.append((h, mask, sc))
    probs = []
    for h, mask, sc in staged:
        m = m_ref[h]
        m_new = jnp.maximum(m, jnp.max(sc, axis=-1, keepdims=True))
        alpha = jnp.exp(m - m_new)
        pr = jnp.where(mask, jnp.exp(sc - m_new), 0.0)
        l_ref[h] = alpha * l_ref[h] + jnp.sum(pr, axis=-1, keepdims=True)
        m_ref[h] = m_new
        probs.append((h, alpha, pr))
    for h, alpha, pr in probs:
        vx = jnp.concatenate([pages[k][pl.ds(n_kv + h, page, stride=nch), :].astype(BF16) for k in range(n_pp)],
                             axis=0)
        acc_ref[h] = alpha * acc_ref[h] + _bdot(pr, vx)

    @pl.when(p == pl.num_programs(1) - 1)
    def _():
        gt = jax.nn.sigmoid(gt_ref[...])
        for h, qb, slope in heads():
            nb0 = past // blk
            seln = selm_ref[h][:, nb0:nb0 + 1] > 0.5
            qr = qb.astype(BF16).astype(F32)
            for j in range(n_q):
                kj = new_ref[j:j + 1, h * hd:(h + 1) * hd].astype(BF16).astype(F32)
                vj = new_ref[j:j + 1, half + h * hd:half + (h + 1) * hd].astype(BF16).astype(F32)
                dj = qpos_r - (past + j)
                sc = jnp.sum(qr * kj, axis=-1, keepdims=True) - slope * dj.astype(F32)
                update(h, sc, seln & (dj >= 0), lambda pr, vj=vj: pr.astype(BF16).astype(F32) * vj)
            o_s = acc_ref[h] / jnp.maximum(l_ref[h], 1e-30)
            for g in range(n_g):
                hh = h * n_g + g
                r = slice(g * n_q, (g + 1) * n_q)
                gcol = n_kv * n_g + hh
                o = ocw_ref[:, hh * hd:(hh + 1) * hd] + gt[:, gcol:gcol + 1] * o_s[r]
                o_ref[:, hh * hd:(hh + 1) * hd] = o.astype(o_ref.dtype)


def _attn_sample_sel(cache, layer, page_table, proj3, selm, new_rows, ocw, slopes_row, past, n_h, n_kv, hd, blk):
    _, _, prow, _ = cache.shape
    bd, n_pages = page_table.shape
    n_q = proj3.shape[1]
    n_g = n_h // n_kv
    qw = n_h * hd
    nbp = selm.shape[-1]
    n_pp = _pick(n_pages, (16, 8, 4, 2, 1))
    rows = n_g * n_q
    in_specs = [pl.BlockSpec((None, None, prow, hd),
                             functools.partial(_page_map, layer=layer, k=k, n_pp=n_pp)) for k in range(n_pp)]
    in_specs += [pl.BlockSpec((None, n_q, qw), lambda b, p, pt: (b, 0, 0)),
                 pl.BlockSpec((None, n_kv, rows, nbp), lambda b, p, pt: (b, 0, 0, 0)),
                 pl.BlockSpec((None, new_rows.shape[1], new_rows.shape[2]), lambda b, p, pt: (b, 0, 0)),
                 pl.BlockSpec((None, n_q, qw), lambda b, p, pt: (b, 0, 0)),
                 pl.BlockSpec((None, n_q, LANE), lambda b, p, pt: (b, 0, (qw + 6 * n_kv * hd) // LANE)),
                 pl.BlockSpec((1, LANE), lambda b, p, pt: (0, 0))]
    kern = functools.partial(_attn_sample_sel_kernel, n_pp=n_pp, past=past, n_q=n_q, n_kv=n_kv, n_g=n_g,
                             hd=hd, blk=blk)
    return pl.pallas_call(
        kern,
        grid_spec=pltpu.PrefetchScalarGridSpec(
            num_scalar_prefetch=1, grid=(bd, n_pages // n_pp), in_specs=in_specs,
            out_specs=pl.BlockSpec((None, n_q, qw), lambda b, p, pt: (b, 0, 0)),
            scratch_shapes=[pltpu.VMEM((n_kv, rows, 1), F32), pltpu.VMEM((n_kv, rows, 1), F32),
                            pltpu.VMEM((n_kv, rows, hd), F32)]),
        out_shape=jax.ShapeDtypeStruct((bd, n_q, qw), BF16),
        compiler_params=_params(("arbitrary", "arbitrary")),
        name="attn_sample_sel",
    )(page_table, *([cache] * n_pp), proj3, selm, new_rows, ocw, proj3, slopes_row)


def _alibi(n_h):
    h = jnp.arange(1, n_h + 1, dtype=F32)
    return jnp.exp2(-8.0 * h / n_h)


def kernel(x_prompt, x_sample, c_prompt, c_sample, state_gdn, state_gdn_conv, state_sconv, cache_kv_cmp,
           cache_kv_sel, state_kv_win, page_table, norm_g, w_ada, b_ada, w_in_e, conv_w_gdn, a_log, dt_bias,
           gdn_norm_g, conv_w_sc, w_out_e, w_in_o, cmp_pool, cmp_proj, w_out_o, w_mlp1, w_mlp2, final_g):
    bsz, seq, d = x_prompt.shape
    bd, n_q, _ = x_sample.shape
    depth = norm_g.shape[0]
    n_vh, dk, dv = state_gdn.shape[2:]
    conv_dim = conv_w_gdn.shape[2]
    sc_dim = conv_w_sc.shape[2]
    nv = n_vh * dv
    blk, _, n_kv = cmp_pool.shape[1:]
    hd = cmp_proj.shape[-1]
    n_h = w_out_o.shape[1] // hd
    kv_row = 2 * n_kv * hd
    qw = n_h * hd
    page = cache_kv_cmp.shape[2]
    past = page_table.shape[1] * page
    wb = state_kv_win.shape[2]
    c = GDN_CHUNK
    assert seq % c == 0 and seq % blk == 0 and seq >= WINDOW and n_q <= c and n_q < blk and past % blk == 0
    assert n_q >= conv_w_gdn.shape[1] - 1 and hd == LANE and dk == LANE and dv == LANE

    rows_c = -(-(bsz + bd) // SUBLANE) * SUBLANE
    c_all = jnp.pad(jnp.concatenate([c_prompt, c_sample], axis=0).astype(F32), ((0, rows_c - bsz - bd), (0, 0)))
    mod = _ada_mod(c_all, w_ada, b_ada).reshape(depth, rows_c, 6, d)

    def mods(l, sample):
        if sample:
            return [jnp.repeat(mod[l, bsz:bsz + bd, j], n_q, axis=0)[None] for j in range(6)]
        return [mod[l, :bsz, j][:, None, :] for j in range(6)]

    slopes = _alibi(n_h)
    slopes_kv = jnp.pad(slopes.reshape(n_kv, 1, n_h // n_kv), ((0, 0), (0, 0), (0, LANE - n_h // n_kv)))
    slopes_row = jnp.pad(slopes.reshape(1, n_h), ((0, 0), (0, LANE - n_h)))
    nch = 2 * n_kv
    cache_cmp = cache_kv_cmp.reshape(cache_kv_cmp.shape[:2] + (page * nch, hd))
    cache_sel = cache_kv_sel.reshape(cache_kv_sel.shape[:2] + (page * nch, hd))

    w_in_e_t = jnp.swapaxes(w_in_e, 1, 2).astype(BF16)
    w_in_o_t = jnp.swapaxes(w_in_o, 1, 2).astype(BF16)

    def even_tail(i):
        o_ab = conv_dim + nv
        o_rest = o_ab + 2 * n_vh
        return jnp.concatenate([w_in_e_t[i, o_rest:], w_in_e_t[i, o_ab:o_rest],
                                jnp.zeros((LANE - 2 * n_vh, d), w_in_e_t.dtype)], axis=0)

    pooled_by_layer = {}

    def mixer(x, l, sample, ev, od):
        nb, t = (bd, n_q) if sample else (bsz, seq)
        i = l // 2
        sh1, sc1, g1 = mods(l, sample)[:3]
        if l % 2 == 0:
            proj = _norm_mod_matmul(x, norm_g[l, 0], sc1, sh1, w_in_e_t, i, main_cols=conv_dim + nv,
                                    w_tail=even_tail(i))
            n_cst = conv_w_gdn.shape[1] - 1
            conv_new = proj.reshape(nb, t, -1)[:, t - n_cst:, :conv_dim]
            if sample:
                projp = jnp.pad(proj.reshape(nb, t, -1), ((0, 0), (0, c - t), (0, 0))).reshape(nb * c, -1)
                cst, scst, s0, tv = state_gdn_conv[i], state_sconv[i], state_gdn[i], t
            else:
                projp = proj
                cst = jnp.zeros((nb, conv_w_gdn.shape[1] - 1, conv_dim), F32)
                scst = jnp.zeros((nb, conv_w_sc.shape[1] - 1, sc_dim), F32)
                s0, tv = jnp.zeros((nb, n_vh, dk, dv), F32), c
            mix, s_new, u_last = _gdn_sconv(projp, cst, scst, s0, conv_w_gdn[i], conv_w_sc[i], a_log[i],
                                            dt_bias[i], gdn_norm_g[i], tv)
            if sample:
                mix = mix.reshape(nb, c, -1)[:, :t].reshape(nb * t, -1)
            sc_new = u_last[:, tv - scst.shape[1]:tv]
            ev.append((s_new, conv_new, sc_new))
            return _matmul_gated_residual(mix, w_out_e, i, x, g1)

        proj, kvc8, kvs8, kvw8 = _norm_mod_matmul(x, norm_g[l, 0], sc1, sh1, w_in_o_t, i,
                                                  rows_out=(qw, 3, nch, hd))
        kv_c, kv_s, kv_w = [a.reshape(nb, t, 2, n_kv, hd) for a in (kvc8, kvs8, kvw8)]
        pj = cmp_proj[i].reshape(2 * n_kv, hd, hd)
        pw_rows = jnp.broadcast_to(cmp_pool[i].reshape(blk * nch, 1), (blk * nch, hd))
        if sample:
            pooled = pooled_by_layer.get(i)
            if pooled is None:
                pooled = _pool_pages(cache_cmp, i, page_table, pw_rows, nch)
            proj3 = proj.reshape(nb, t, -1)
            kv_all = jnp.concatenate([state_kv_win[i].reshape(nb, wb, kv_row).astype(F32),
                                      proj3[:, :, qw + 2 * kv_row:qw + 3 * kv_row]], axis=1)
            kv_all_p = jnp.pad(kv_all, ((0, 0), (0, (-kv_all.shape[1]) % SUBLANE), (0, 0)))
            ocw, selm = _attn_sample_a(proj3, pooled, pj, kv_all_p, slopes_row, past, n_h, n_kv, hd, blk, wb)
            new_rows = jnp.pad(proj3[:, :, qw + kv_row:qw + 2 * kv_row], ((0, 0), (0, (-t) % SUBLANE), (0, 0)))
            o = _attn_sample_sel(cache_sel, i, page_table, proj3, selm, new_rows, ocw, slopes_row, past,
                                 n_h, n_kv, hd, blk).reshape(nb * t, qw)
            kv_win_new = kv_all[:, t:].reshape(nb, wb, 2, n_kv, hd)
        else:
            pw = jnp.broadcast_to(cmp_pool[i].reshape(blk, 2 * n_kv).T[:, :, None], (2 * n_kv, blk, hd))
            cmp = _cmp_prompt(proj, nb, t, pw, pj, n_h)
            n_steps = nb * n_kv * _attn_prompt_tiles(t)[1]
            n_pool = (bd * page_table.shape[1]) // n_steps
            if nb * n_kv == bd and n_pool * n_steps == bd * page_table.shape[1] and 1 <= n_pool <= 8:
                o, pooled_by_layer[i] = _attn_prompt(proj, cmp, slopes_kv, nb, t, n_h, n_kv, hd, blk,
                                                     pool=(cache_cmp, i, page_table, pw_rows, nch, n_pool))
            else:
                o = _attn_prompt(proj, cmp, slopes_kv, nb, t, n_h, n_kv, hd, blk)
            kv_win_new = kv_w[:, t - min(WINDOW, t):]
        od.append((kv_c, kv_s, kv_win_new))
        return _matmul_gated_residual(o, w_out_o, i, x, g1)

    xp = x_prompt.reshape(bsz * seq, d).astype(F32)
    xs = x_sample.reshape(bd * n_q, d).astype(F32)
    ev_p, od_p, ev_s, od_s = [], [], [], []
    for l in range(depth):
        xp = mixer(xp, l, False, ev_p, od_p)
        xs = mixer(xs, l, True, ev_s, od_s)
        _, _, _, sh2, sc2, g2 = mods(l, False)
        _, _, _, sh2s, sc2s, g2s = mods(l, True)
        xp, xs = _mlp(xp, norm_g[l, 1], sc2, sh2, g2, w_mlp1, w_mlp2, l, final_g if l == depth - 1 else None,
                      side=(xs, sc2s, sh2s, g2s))
    stack3 = lambda items: [jnp.stack([s[j] for s in items]) for j in range(3)]
    gdn_p, gconv_p, sconv_p = stack3(ev_p)
    gdn_s, gconv_s, sconv_s = stack3(ev_s)
    kvc_p, kvs_p, kvw_p = stack3(od_p)
    kvc_s, kvs_s, kvw_s = stack3(od_s)
    y_p = xp.reshape(bsz, seq, d)
    y_s = xs.reshape(bd, n_q, d)
    return (y_p, y_s, gdn_p, gdn_s, gconv_p, gconv_s, sconv_p, sconv_s, kvc_p, kvc_s, kvs_p, kvs_s, kvw_p, kvw_s)
```

```python
import functools

import jax
import jax.numpy as jnp
from jax import lax
from jax.experimental import pallas as pl
from jax.experimental.pallas import tpu as pltpu

F32 = jnp.float32
BF16 = jnp.bfloat16

EPS = 1e-6
NEG = -1e30
FORCE = 1e6
DEAD = -3e38
SEL_TOPK = 16
LOG2E = 1.4426950408889634
WINDOW = 512
GDN_CHUNK = 64
LANE = 128
SUBLANE = 8
VMEM_BUDGET = 56 * 1024 * 1024


def _params(sem):
    return pltpu.CompilerParams(dimension_semantics=sem, vmem_limit_bytes=VMEM_BUDGET)


def _pick(n, cands):
    for c in cands:
        if n % c == 0:
            return c
    return n


def _bdot(a, b):
    return jnp.dot(a.astype(BF16), b.astype(BF16), preferred_element_type=F32)


def _bdot_nt(a, b):
    return lax.dot_general(a.astype(BF16), b.astype(BF16), (((1,), (1,)), ((), ())),
                           preferred_element_type=F32)


def _silu(x):
    return x * jax.nn.sigmoid(x)


def _modnorm(x, g, scale, shift):
    y = x * lax.rsqrt(jnp.mean(x * x, axis=-1, keepdims=True) + EPS)
    return (y * g) * (1.0 + scale) + shift


def _modnorm_rows(x_ref, g_ref, sc_ref, sh_ref, hn_ref):
    tm = x_ref.shape[0]
    ch = min(tm, 256)

    def body(r, carry):
        rows = pl.ds(pl.multiple_of(r * ch, ch), ch)
        sc = sc_ref[...] if sc_ref.shape[0] == 1 else sc_ref[rows, :]
        sh = sh_ref[...] if sh_ref.shape[0] == 1 else sh_ref[rows, :]
        hn_ref[rows, :] = _modnorm(x_ref[rows, :], g_ref[...], sc, sh).astype(BF16)
        return carry

    lax.fori_loop(0, tm // ch, body, 0)


def _masked_softmax(s, valid):
    s = jnp.where(valid, s, NEG)
    m = jnp.max(s, axis=-1, keepdims=True)
    p = jnp.where(valid, jnp.exp(s - m), 0.0)
    return p / jnp.maximum(jnp.sum(p, axis=-1, keepdims=True), 1e-30)


def _ada_kernel(c_ref, w_ref, b_ref, o_ref):
    o_ref[...] = _bdot(_silu(c_ref[...]), w_ref[...]) + b_ref[...]


def _ada_mod(c_all, w_ada, b_ada):
    depth, d, n6 = w_ada.shape
    rows = c_all.shape[0]
    tn = _pick(n6, (1024, 512, 256, 128))
    return pl.pallas_call(
        _ada_kernel,
        grid=(depth, n6 // tn),
        in_specs=[pl.BlockSpec((rows, d), lambda l, j: (0, 0)),
                  pl.BlockSpec((None, d, tn), lambda l, j: (l, 0, j)),
                  pl.BlockSpec((None, 1, tn), lambda l, j: (l, 0, j))],
        out_specs=pl.BlockSpec((None, rows, tn), lambda l, j: (l, 0, j)),
        out_shape=jax.ShapeDtypeStruct((depth, rows, n6), F32),
        compiler_params=_params(("arbitrary", "arbitrary")),
        name="ada_mod",
    )(c_all, w_ada, b_ada.reshape(depth, 1, n6))


def _mm1_kernel(x_ref, g_ref, sc_ref, sh_ref, *refs, n_main, n_w, rows_first, n_rows_out, nch):
    w_refs = refs[:n_w]
    o_ref = refs[n_w]
    row_refs = refs[n_w + 1:n_w + 1 + n_rows_out]
    hn_ref = refs[-1]
    j = pl.program_id(1)

    @pl.when(j == 0)
    def _():
        _modnorm_rows(x_ref, g_ref, sc_ref, sh_ref, hn_ref)

    def emit(wt_ref):
        y = lax.dot_general(hn_ref[...], wt_ref[...].astype(BF16), (((1,), (1,)), ((), ())),
                            preferred_element_type=F32)
        o_ref[...] = y
        return y

    if n_w == 1:
        y = emit(w_refs[0])
        tm, tn = o_ref.shape
        hd = row_refs[0].shape[-1] if n_rows_out else LANE
        per_tile = tn // hd
        tiles_per_out = nch // per_tile if n_rows_out else 1
        for r in range(n_rows_out):
            for part in range(tiles_per_out):
                @pl.when(j == rows_first + r * tiles_per_out + part)
                def _(r=r, part=part):
                    for cc in range(per_tile):
                        row_refs[r][pl.ds(part * per_tile + cc, tm, stride=nch), :] = y[:, cc * hd:(cc + 1) * hd]
    else:
        @pl.when(j < n_main)
        def _():
            emit(w_refs[0])

        @pl.when(j >= n_main)
        def _():
            emit(w_refs[1])


def _mod_spec(mod, tm, rows_per_group, width, col):
    r = mod.shape[1]
    tiles = rows_per_group // tm
    if col:
        return pl.BlockSpec((None, r, width), lambda i, j: (i // tiles, 0, j))
    return pl.BlockSpec((None, r, width), lambda i, j: (i // tiles, 0, 0))


def _norm_mod_matmul(x, g, scale, shift, w, layer, main_cols=None, w_tail=None, rows_out=None):
    m, d = x.shape
    tn = 512
    groups = scale.shape[0]
    rpg = m // groups
    if w_tail is None:
        n = w.shape[1]
        n_main = pl.cdiv(n, tn)
        weights = [w]
        w_specs = [pl.BlockSpec((None, tn, d), lambda i, j: (layer, j, 0))]
    else:
        assert main_cols % tn == 0
        n = main_cols + w_tail.shape[0]
        n_main = main_cols // tn
        weights = [w, w_tail]
        w_specs = [pl.BlockSpec((None, tn, d), lambda i, j: (layer, jnp.minimum(j, n_main - 1), 0)),
                   pl.BlockSpec((tn, d), lambda i, j: (jnp.maximum(j - n_main, 0), 0))]
    n_rows_out, rows_first, nch, hd = 0, 0, 1, LANE
    out_shape = [jax.ShapeDtypeStruct((m, n), F32)]
    if rows_out is not None:
        first_col, n_rows_out, nch, hd = rows_out
        assert first_col % tn == 0 and (nch * hd) % tn == 0 and w_tail is None
        rows_first = first_col // tn
    big = n_rows_out == 0 and w.dtype == BF16
    tm = _pick(rpg, ((2048,) if big else ()) + (1024, 512, 256, 128, 64, 32, 16, 8))
    out_specs = [pl.BlockSpec((tm, tn), lambda i, j: (i, j))]
    for _ in range(n_rows_out):
        out_specs.append(pl.BlockSpec((tm * nch, hd), lambda i, j: (i, 0)))
        out_shape.append(jax.ShapeDtypeStruct((m * nch, hd), F32))
    kern = functools.partial(_mm1_kernel, n_main=n_main, n_w=len(weights), rows_first=rows_first,
                             n_rows_out=n_rows_out, nch=nch)
    outs = pl.pallas_call(
        kern,
        grid=(m // tm, pl.cdiv(n, tn)),
        in_specs=[pl.BlockSpec((tm, d), lambda i, j: (i, 0), pipeline_mode=pl.Buffered(1)),
                  pl.BlockSpec((1, d), lambda i, j: (0, 0)),
                  _mod_spec(scale, tm, rpg, d, False),
                  _mod_spec(shift, tm, rpg, d, False)] + w_specs,
        out_specs=out_specs,
        out_shape=out_shape,
        scratch_shapes=[pltpu.VMEM((tm, d), BF16)],
        compiler_params=_params(("arbitrary", "arbitrary")),
        name="norm_mod_matmul",
    )(x, g.reshape(1, d), scale, shift, *weights)
    return outs if n_rows_out else outs[0]


def _mm2_kernel(a_ref, w_ref, x_ref, gate_ref, o_ref):
    y = jnp.dot(a_ref[...], w_ref[...].astype(BF16), preferred_element_type=F32)
    o_ref[...] = x_ref[...] + gate_ref[...] * y


def _matmul_gated_residual(a, w, layer, x, gate):
    m, k = a.shape
    d = w.shape[2]
    groups = gate.shape[0]
    rpg = m // groups
    tm = _pick(rpg, (1024, 512, 256, 128, 64, 32, 16))
    tn = _pick(d, (512, 256, 128))
    return pl.pallas_call(
        _mm2_kernel,
        grid=(m // tm, d // tn),
        in_specs=[pl.BlockSpec((tm, k), lambda i, j: (i, 0)),
                  pl.BlockSpec((None, k, tn), lambda i, j: (layer, 0, j)),
                  pl.BlockSpec((tm, tn), lambda i, j: (i, j)),
                  _mod_spec(gate, tm, rpg, tn, True)],
        out_specs=pl.BlockSpec((tm, tn), lambda i, j: (i, j)),
        out_shape=jax.ShapeDtypeStruct((m, d), F32),
        compiler_params=_params(("arbitrary", "arbitrary")),
        name="matmul_gated_residual",
    )(a, w, x, gate)


def _mlp_kernel(x_ref, g_ref, sc_ref, sh_ref, gate_ref, w1_ref, w2_ref, fg_ref, *refs, final, side):
    if side:
        xs_ref, scs_ref, shs_ref, gates_ref, o_ref, os_ref, hn_ref, hns_ref = refs
    else:
        o_ref, hn_ref = refs
    i = pl.program_id(0)
    j = pl.program_id(1)
    last = pl.num_programs(1) - 1

    def mlp_tile(hn):
        h = jnp.dot(hn, w1_ref[...].astype(BF16), preferred_element_type=F32)
        h = jnp.square(jnp.maximum(h, 0.0))
        return jnp.dot(h.astype(BF16), w2_ref[...].astype(BF16), preferred_element_type=F32)

    def finish(x, gate, acc):
        y = x + gate * acc
        if final:
            y = (y * lax.rsqrt(jnp.mean(y * y, axis=-1, keepdims=True) + EPS)) * fg_ref[...]
        return y

    @pl.when(j == 0)
    def _():
        _modnorm_rows(x_ref, g_ref, sc_ref, sh_ref, hn_ref)
        o_ref[...] = jnp.zeros_like(o_ref)

    o_ref[...] += mlp_tile(hn_ref[...])

    @pl.when(j == last)
    def _():
        o_ref[...] = finish(x_ref[...], gate_ref[...], o_ref[...])

    if side:
        @pl.when((i == 0) & (j == 0))
        def _():
            _modnorm_rows(xs_ref, g_ref, scs_ref, shs_ref, hns_ref)
            os_ref[...] = jnp.zeros_like(os_ref)

        @pl.when(i == 0)
        def _():
            os_ref[...] += mlp_tile(hns_ref[...])

        @pl.when((i == 0) & (j == last))
        def _():
            os_ref[...] = finish(xs_ref[...], gates_ref[...], os_ref[...])


def _mlp(x, g, scale, shift, gate, w1, w2, layer, final_g=None, side=None):
    m, d = x.shape
    final = final_g is not None
    fg = (final_g if final else g).reshape(1, d)
    f = w1.shape[2]
    groups = scale.shape[0]
    rpg = m // groups
    tm = _pick(rpg, (1024, 512, 256, 128, 64, 32, 16, 8))
    tf = _pick(f, (512, 256, 128))
    in_specs = [pl.BlockSpec((tm, d), lambda i, j: (i, 0)),
                pl.BlockSpec((1, d), lambda i, j: (0, 0)),
                _mod_spec(scale, tm, rpg, d, False),
                _mod_spec(shift, tm, rpg, d, False),
                _mod_spec(gate, tm, rpg, d, False),
                pl.BlockSpec((None, d, tf), lambda i, j: (layer, 0, j)),
                pl.BlockSpec((None, tf, d), lambda i, j: (layer, j, 0)),
                pl.BlockSpec((1, d), lambda i, j: (0, 0))]
    out_specs = [pl.BlockSpec((tm, d), lambda i, j: (i, 0), pipeline_mode=pl.Buffered(1))]
    out_shape = [jax.ShapeDtypeStruct((m, d), F32)]
    scratch = [pltpu.VMEM((tm, d), BF16)]
    operands = [x, g.reshape(1, d), scale, shift, gate, w1, w2, fg]
    if side is not None:
        xs = side[0]
        ms = xs.shape[0]
        in_specs.append(pl.BlockSpec((ms, d), lambda i, j: (0, 0)))
        in_specs += [pl.BlockSpec((None, ms, d), lambda i, j: (0, 0, 0))] * 3
        out_specs.append(pl.BlockSpec((ms, d), lambda i, j: (0, 0)))
        out_shape.append(jax.ShapeDtypeStruct((ms, d), F32))
        scratch.append(pltpu.VMEM((ms, d), BF16))
        operands += list(side)
    outs = pl.pallas_call(
        functools.partial(_mlp_kernel, final=final, side=side is not None),
        grid=(m // tm, f // tf),
        in_specs=in_specs,
        out_specs=out_specs,
        out_shape=out_shape,
        scratch_shapes=scratch,
        compiler_params=_params(("arbitrary", "arbitrary")),
        name="mlp",
    )(*operands)
    return outs if side is not None else outs[0]


def _split3(x):
    hi = x.astype(BF16)
    r = x - hi.astype(F32)
    mid = r.astype(BF16)
    lo = (r - mid.astype(F32)).astype(BF16)
    return hi, mid, lo


def _gdn_kernel(qkv_ref, z_ref, ab_ref, gb_ref, gcg_ref, hx_ref, cst_ref, scst_ref, cw_ref, scw_ref,
                alog_ref, dtb_ref, ng_ref, s0_ref,
                mix_ref, sout_ref, ulast_ref,
                xbuf, ubuf, s_ref, *, t_valid, n_kh, n_vh, dk, dv, n_sub):
    c = GDN_CHUNK
    rows = n_sub * c
    n = pl.program_id(1)
    nqk = n_kh * dk
    nv = n_vh * dv
    rep = n_vh // n_kh
    hpg = 2
    gw = hpg * c
    n_conv = cw_ref.shape[0]
    n_sc = scw_ref.shape[0]

    @pl.when(n == 0)
    def _():
        xbuf[0:SUBLANE, :] = cst_ref[...]
        ubuf[0:SUBLANE, :] = scst_ref[...]
        s_ref[...] = s0_ref[...]

    xbuf[SUBLANE:SUBLANE + rows, :] = qkv_ref[...]
    cw = cw_ref[...]
    off = SUBLANE - (n_conv - 1)
    xc_all = cw[0:1, :] * xbuf[off:off + rows, :]
    for j in range(1, n_conv):
        xc_all = xc_all + cw[j:j + 1, :] * xbuf[off + j:off + j + rows, :]
    xbuf[0:SUBLANE, :] = xbuf[rows:rows + SUBLANE, :]
    xc_all = _silu(xc_all)

    u = gcg_ref[...] * hx_ref[...]
    ubuf[SUBLANE:SUBLANE + rows, :] = u
    scw = scw_ref[...]
    offs = SUBLANE - (n_sc - 1)
    cu = scw[0:1, :] * ubuf[offs:offs + rows, :]
    for j in range(1, n_sc):
        cu = cu + scw[j:j + 1, :] * ubuf[offs + j:offs + j + rows, :]
    ubuf[0:SUBLANE, :] = ubuf[rows:rows + SUBLANE, :]
    ulast_ref[...] = u[rows - c:, :]
    mix_ref[:, nv:] = (gb_ref[...] * cu).astype(mix_ref.dtype)

    ab = ab_ref[...]
    g_rows = -jnp.exp(alog_ref[...]) * jax.nn.softplus(ab + dtb_ref[...])
    beta_rows = jax.nn.sigmoid(ab)
    if t_valid < rows:
        rowmask = lax.broadcasted_iota(jnp.int32, (rows, 1), 0) < t_valid
        xc_all = jnp.where(rowmask, xc_all, 0.0)
        g_rows = jnp.where(rowmask, g_rows, 0.0)
        beta_rows = jnp.where(rowmask, beta_rows, 0.0)

    ri = lax.broadcasted_iota(jnp.int32, (c, c), 0)
    ci = lax.broadcasted_iota(jnp.int32, (c, c), 1)
    tril = jnp.where(ri >= ci, 1.0, 0.0).astype(BF16)

    def l2n(x):
        return x * lax.rsqrt(jnp.sum(x * x, axis=-1, keepdims=True) + EPS)

    gi = lax.broadcasted_iota(jnp.int32, (gw, gw), 0)
    gj = lax.broadcasted_iota(jnp.int32, (gw, gw), 1)
    same = (gi // c) == (gj // c)
    low_incl = same & (gi >= gj)
    low_strict = same & (gi > gj)
    lane_blk = lax.broadcasted_iota(jnp.int32, (dk, gw), 1) // c
    level_mask = []
    size = 1
    while size < c:
        level_mask.append(((gi // (2 * size)) == (gj // (2 * size))) & ((gi // size) != (gj // size)) & (gi > gj))
        size *= 2

    def stack(cols):
        return jnp.concatenate(cols, axis=0)

    n_grp = n_vh // hpg
    pre, glasts = [], []
    for sub in range(n_sub):
        r0 = sub * c
        xc = xc_all[r0:r0 + c]
        beta_all = beta_rows[r0:r0 + c]
        ghi, gmid, glo = _split3(g_rows[r0:r0 + c])
        gcum = (jnp.dot(tril, ghi, preferred_element_type=F32)
                + jnp.dot(tril, gmid, preferred_element_type=F32)
                + jnp.dot(tril, glo, preferred_element_type=F32))
        glast = gcum[c - 1:c, :]
        glasts.append(glast)
        for grp in range(n_grp):
            heads = [grp * hpg + j for j in range(hpg)]
            kheads = [h // rep for h in heads]
            qs = {kh: l2n(xc[:, kh * dk:(kh + 1) * dk]) * (dk ** -0.5) for kh in set(kheads)}
            ks = {kh: l2n(xc[:, nqk + kh * dk:nqk + (kh + 1) * dk]) for kh in set(kheads)}
            q_st = stack([qs[kh] for kh in kheads])
            k_st = stack([ks[kh] for kh in kheads])
            v_st = stack([xc[:, 2 * nqk + h * dv:2 * nqk + (h + 1) * dv] for h in heads])
            beta_st = stack([beta_all[:, n_vh + h:n_vh + h + 1] for h in heads])
            gc_st = stack([gcum[:, h:h + 1] for h in heads])
            gl_st = stack([jnp.broadcast_to(glast[:, h:h + 1], (c, 1)) for h in heads])
            gc_row = jnp.broadcast_to(gc_st, (gw, LANE)).T[0:1, :]
            diff = gc_st - gc_row
            decay = jnp.where(low_incl, jnp.exp(jnp.where(low_incl, diff, 0.0)), 0.0)
            egc = jnp.exp(gc_st)
            kb_st = k_st * beta_st

            nmat = _bdot_nt(kb_st, k_st) * jnp.where(low_strict, decay, 0.0)
            qk = _bdot_nt(q_st, k_st) * decay
            rhs = jnp.concatenate([v_st * beta_st, kb_st * egc], axis=1)
            pre.append((nmat, qk, rhs, q_st * egc, k_st * jnp.exp(gl_st - gc_st)))

    tms = [-jnp.where(level_mask[0], p[0], 0.0) for p in pre]
    for lm in level_mask[1:]:
        lls = [jnp.where(lm, p[0], 0.0) for p in pre]
        ys = [ll + _bdot(tm, ll) for tm, ll in zip(tms, lls)]
        tms = [tm - y - _bdot(y, tm) for tm, y in zip(tms, ys)]
    sols = [p[2] + _bdot(tm, p[2]) for p, tm in zip(pre, tms)]

    for sub in range(n_sub):
        r0 = sub * c
        glast = glasts[sub]
        for grp in range(n_grp):
            heads = [grp * hpg + j for j in range(hpg)]
            nmat, qk, rhs, qg_st, kd_st = pre[sub * n_grp + grp]
            sol = sols[sub * n_grp + grp]
            u_st = sol[:, :dv]
            w_st = sol[:, dv:]

            vnew, qs_out = [], []
            for j, h in enumerate(heads):
                wq = jnp.concatenate([w_st[j * c:(j + 1) * c], qg_st[j * c:(j + 1) * c]], axis=0)
                r2 = _bdot(wq, s_ref[h])
                vnew.append(u_st[j * c:(j + 1) * c] - r2[:c])
                qs_out.append(r2[c:])
            vnew_st = stack(vnew)
            o_st = stack(qs_out) + _bdot(qk, vnew_st)
            kd_t = kd_st.T
            for j, h in enumerate(heads):
                upd = _bdot(jnp.where(lane_blk == j, kd_t, 0.0), vnew_st)
                s_ref[h] = s_ref[h] * jnp.exp(glast[:, h:h + 1]) + upd
                o_h = o_st[j * c:(j + 1) * c]
                on = (o_h * lax.rsqrt(jnp.mean(o_h * o_h, axis=-1, keepdims=True) + EPS)) * ng_ref[...]
                zz = z_ref[r0:r0 + c, h * dv:(h + 1) * dv]
                mix_ref[r0:r0 + c, h * dv:(h + 1) * dv] = (on * _silu(zz)).astype(mix_ref.dtype)

    sout_ref[...] = s_ref[...]


def _gdn_sconv(proj, conv_state, sc_state, s0, conv_w, sc_w, a_log, dt_bias, norm_g, t_valid):
    bsz, n_vh, dk, dv = s0.shape
    n_conv, conv_dim = conv_w.shape
    n_sc, sc = sc_w.shape
    nv = n_vh * dv
    n_kh = (conv_dim - nv) // (2 * dk)
    c = GDN_CHUNK
    rows = proj.shape[0]
    nchunk = rows // (bsz * c)
    assert conv_dim % nv == 0 and (conv_dim + nv) % sc == 0 and n_vh % 2 == 0 and 2 * n_vh <= LANE
    n_sub = 2 if (nchunk % 2 == 0 and t_valid == c) else 1
    assert t_valid == c or nchunk == 1
    nstep = nchunk // n_sub
    rs = n_sub * c
    zb = conv_dim // nv
    gbb = (conv_dim + nv) // sc
    abb = (conv_dim + nv + 3 * sc) // LANE
    pad_lane = lambda v: jnp.pad(v.reshape(1, -1), ((0, 0), (0, LANE - v.shape[-1])))
    cst = jnp.pad(conv_state.astype(F32), ((0, 0), (SUBLANE - (n_conv - 1), 0), (0, 0)))
    scst = jnp.pad(sc_state.astype(F32), ((0, 0), (SUBLANE - (n_sc - 1), 0), (0, 0)))
    row = lambda b, n: b * nstep + n
    kern = functools.partial(_gdn_kernel, t_valid=t_valid if n_sub == 1 else rs, n_kh=n_kh, n_vh=n_vh, dk=dk,
                             dv=dv, n_sub=n_sub)
    return pl.pallas_call(
        kern,
        grid=(bsz, nstep),
        in_specs=[pl.BlockSpec((rs, conv_dim), lambda b, n: (row(b, n), 0)),
                  pl.BlockSpec((rs, nv), lambda b, n: (row(b, n), zb)),
                  pl.BlockSpec((rs, LANE), lambda b, n: (row(b, n), abb)),
                  pl.BlockSpec((rs, sc), lambda b, n: (row(b, n), gbb)),
                  pl.BlockSpec((rs, sc), lambda b, n: (row(b, n), gbb + 1)),
                  pl.BlockSpec((rs, sc), lambda b, n: (row(b, n), gbb + 2)),
                  pl.BlockSpec((None, SUBLANE, conv_dim), lambda b, n: (b, 0, 0)),
                  pl.BlockSpec((None, SUBLANE, sc), lambda b, n: (b, 0, 0)),
                  pl.BlockSpec((n_conv, conv_dim), lambda b, n: (0, 0)),
                  pl.BlockSpec((n_sc, sc), lambda b, n: (0, 0)),
                  pl.BlockSpec((1, LANE), lambda b, n: (0, 0)),
                  pl.BlockSpec((1, LANE), lambda b, n: (0, 0)),
                  pl.BlockSpec((1, dv), lambda b, n: (0, 0)),
                  pl.BlockSpec((None, n_vh, dk, dv), lambda b, n: (b, 0, 0, 0))],
        out_specs=[pl.BlockSpec((rs, nv + sc), lambda b, n: (row(b, n), 0)),
                   pl.BlockSpec((None, n_vh, dk, dv), lambda b, n: (b, 0, 0, 0)),
                   pl.BlockSpec((None, c, sc), lambda b, n: (b, 0, 0))],
        out_shape=[jax.ShapeDtypeStruct((rows, nv + sc), BF16),
                   jax.ShapeDtypeStruct((bsz, n_vh, dk, dv), F32),
                   jax.ShapeDtypeStruct((bsz, c, sc), F32)],
        scratch_shapes=[pltpu.VMEM((rs + SUBLANE, conv_dim), F32),
                        pltpu.VMEM((rs + SUBLANE, sc), F32),
                        pltpu.VMEM((n_vh, dk, dv), F32)],
        compiler_params=_params(("arbitrary", "arbitrary")),
        name="gdn_sconv",
    )(proj, proj, proj, proj, proj, proj, cst, scst, conv_w, sc_w,
      pad_lane(a_log), pad_lane(dt_bias), norm_g.reshape(1, dv), s0.astype(F32))


def _cmp_prompt_kernel(kv_ref, pw_ref, pj_ref, o_ref, *, blk):
    t, hd = kv_ref.shape
    x = kv_ref[...].reshape(t // blk, blk, hd) * pw_ref[...][None]
    o_ref[...] = _bdot(jnp.sum(x, axis=1), pj_ref[...])


def _cmp_prompt(proj, bsz, t, pw, pj, col0):
    nch, blk, hd = pw.shape
    return pl.pallas_call(
        functools.partial(_cmp_prompt_kernel, blk=blk),
        grid=(bsz, nch),
        in_specs=[pl.BlockSpec((t, hd), lambda b, ch: (b, col0 + ch)),
                  pl.BlockSpec((None, blk, hd), lambda b, ch: (ch, 0, 0)),
                  pl.BlockSpec((None, hd, hd), lambda b, ch: (ch, 0, 0))],
        out_specs=pl.BlockSpec((None, None, t // blk, hd), lambda b, ch: (b, ch, 0, 0)),
        out_shape=jax.ShapeDtypeStruct((bsz, nch, t // blk, hd), F32),
        compiler_params=_params(("arbitrary", "arbitrary")),
        name="cmp_prompt",
    )(proj, pw, pj)


_NT = (((1,), (1,)), ((), ()))


def _slope_features(sl2, lane):
    hi = sl2.astype(BF16).astype(F32)
    lo = sl2 - hi
    return jnp.where(lane == 0, 64.0 * hi, jnp.where(lane == 1, hi, jnp.where(lane == 2, 64.0 * lo,
                                                                             jnp.where(lane == 3, lo, 0.0))))


def _pool_blocks(page_refs, pw_ref, o_ref):
    pw = pw_ref[...]
    rows_blk = pw.shape[0]
    per_page = page_refs[0].shape[0] // rows_blk
    nch = o_ref.shape[0] // (len(page_refs) * per_page)
    for k, page_ref in enumerate(page_refs):
        for hb in range(per_page):
            r = k * per_page + hb
            x = page_ref[hb * rows_blk:(hb + 1) * rows_blk, :] * pw
            o_ref[r * nch:(r + 1) * nch, :] = jnp.sum(x.reshape(rows_blk // nch, nch, x.shape[-1]), axis=0)


def _attn_prompt_kernel(*refs, tq, tk, wk, blk, n_g, n_kv, hd, n_pool):
    if n_pool:
        refs = refs[1:]
    q_ref, kc_ref, vc_ref, ks_ref, vs_ref, kw_ref, vw_ref, gt_ref, sl_ref, pos_ref, oh_ref = refs[:11]
    o_ref = refs[11 + n_pool + (1 if n_pool else 0)]
    h = pl.program_id(1)
    qt = pl.program_id(2)
    q0 = qt * tq
    nc = kc_ref.shape[0]
    t_all = ks_ref.shape[0]
    sl = sl_ref[...]
    lane = lax.broadcasted_iota(jnp.int32, (1, LANE), 1)
    qs = [q_ref[:, g * hd:(g + 1) * hd] * (hd ** -0.5) for g in range(n_g)]
    qb = jnp.concatenate(qs, axis=0).astype(BF16)
    qaug = jnp.concatenate(
        [jnp.concatenate([(qs[g] * LOG2E).astype(BF16),
                          jnp.broadcast_to(_slope_features(sl[:, g:g + 1] * LOG2E, lane), (tq, LANE)).astype(BF16)],
                         axis=1) for g in range(n_g)], axis=0)
    qpos = q0 + lax.broadcasted_iota(jnp.int32, (tq, 1), 0)

    bidx = lax.broadcasted_iota(jnp.int32, (1, nc), 1)
    start = bidx * blk
    s_c = _bdot_nt(qb, kc_ref[...])
    dist_c = jnp.abs(qpos.astype(F32) - (start.astype(F32) + (blk - 1) / 2))
    valid_c = (start + (blk - 1)) <= qpos
    p_c = jnp.concatenate([_masked_softmax(s_c[g * tq:(g + 1) * tq] - sl[:, g:g + 1] * dist_c, valid_c)
                           for g in range(n_g)], axis=0)
    o_c = _bdot(p_c, vc_ref[...])
    imp = p_c[0:tq]
    for g in range(1, n_g):
        imp = imp + p_c[g * tq:(g + 1) * tq]

    imp_t = jnp.concatenate([imp, jnp.zeros((tq, LANE - nc), F32)], axis=1).T[0:nc, :]
    bcol = lax.broadcasted_iota(jnp.int32, (nc, 1), 0)
    cur = (q0 + lax.broadcasted_iota(jnp.int32, (1, tq), 1)) // blk
    score = jnp.where(bcol < cur, imp_t, NEG)
    score = jnp.where((bcol == 0) | (bcol == cur), FORCE, score)
    rank = jnp.zeros((nc, tq), F32)
    for i in range(nc):
        ri = score[i:i + 1, :]
        rank = rank + jnp.where((ri > score) | ((ri == score) & (bcol > i)), 1.0, 0.0)
    selneg_t = jnp.where((rank < min(SEL_TOPK, nc)) & (score > 0.5 * NEG), 0.0, NEG)
    selneg = jnp.concatenate([selneg_t, jnp.zeros((LANE - nc, tq), F32)], axis=0).T.astype(BF16)

    def flash_step(k_ref, v_ref, k0, size, bias, carry):
        m, l, acc = carry
        kaug = jnp.concatenate([k_ref[pl.ds(k0, size), :].astype(BF16), pos_ref[pl.ds(k0, size), :]], axis=1)
        s = lax.dot_general(qaug, kaug, _NT, preferred_element_type=F32)
        gs = range(n_g)
        sgs = [s[g * tq:(g + 1) * tq] + bias for g in gs]
        m2 = [jnp.maximum(m[g], jnp.max(sgs[g], axis=-1, keepdims=True)) for g in gs]
        als = [jnp.exp2(m[g] - m2[g]) for g in gs]
        pfs = [jnp.exp2(sgs[g] - m2[g]) for g in gs]
        l2 = [als[g] * l[g] + jnp.sum(pfs[g], axis=-1, keepdims=True) for g in gs]
        ps = [p.astype(BF16) for p in pfs]
        accs = [als[g] * acc[g * tq:(g + 1) * tq] for g in gs]
        pv = jnp.dot(jnp.concatenate(ps, axis=0), v_ref[pl.ds(k0, size), :].astype(BF16),
                     preferred_element_type=F32)
        return tuple(m2), tuple(l2), jnp.concatenate(accs, axis=0) + pv

    def finish(carry):
        _, l, acc = carry
        return [acc[g * tq:(g + 1) * tq] / jnp.maximum(l[g], 1e-30) for g in range(n_g)]

    init = (tuple(jnp.full((tq, 1), NEG, F32) for _ in range(n_g)),
            tuple(jnp.zeros((tq, 1), F32) for _ in range(n_g)), jnp.zeros((n_g * tq, hd), F32))

    def sel_bias(k0):
        return lax.dot_general(selneg, oh_ref[pl.ds(k0, tk), :], _NT, preferred_element_type=F32)

    def sel_chunk(ci, carry):
        k0 = pl.multiple_of(ci * tk, tk)
        return flash_step(ks_ref, vs_ref, k0, tk, sel_bias(k0), carry)

    n_full = q0 // tk
    carry = lax.fori_loop(0, n_full, sel_chunk, init)
    k0 = pl.multiple_of(n_full * tk, tk)
    kpos = k0 + lax.broadcasted_iota(jnp.int32, (1, tk), 1)
    o_s = finish(flash_step(ks_ref, vs_ref, k0, tk, sel_bias(k0) + jnp.where(kpos <= qpos, 0.0, NEG), carry))

    w0 = pl.multiple_of(jnp.clip(q0 + tq - wk, 0, t_all - wk), tq)
    dist = qpos - (w0 + lax.broadcasted_iota(jnp.int32, (1, wk), 1))
    o_w = finish(flash_step(kw_ref, vw_ref, w0, wk, jnp.where((dist >= 0) & (dist < WINDOW), 0.0, NEG), init))

    gt = jax.nn.sigmoid(pltpu.roll(gt_ref[...], (LANE - n_g * h) % LANE, 1))
    n_h = n_g * n_kv
    for g in range(n_g):
        r = slice(g * tq, (g + 1) * tq)
        o = (gt[:, g:g + 1] * o_c[r] + gt[:, n_h + g:n_h + g + 1] * o_s[g]
             + gt[:, 2 * n_h + g:2 * n_h + g + 1] * o_w[g])
        o_ref[:, g * hd:(g + 1) * hd] = o.astype(o_ref.dtype)

    if n_pool:
        _pool_blocks(refs[11:11 + n_pool], refs[11 + n_pool], refs[-1])


def _attn_prompt_tiles(t):
    return 128, t // 128


def _attn_prompt(proj, cmp, slopes, bsz, t, n_h, n_kv, hd, blk, pool=None):
    n_g = n_h // n_kv
    tq, nqt = _attn_prompt_tiles(t)
    tk = _pick(t, (512, 256, 128))
    wk = WINDOW + tq
    assert t % tq == 0 and t >= wk and t // blk <= LANE and blk == 64 and n_h * 3 <= LANE
    nc = t // blk
    kvb = n_h
    pos = jnp.arange(t, dtype=jnp.int32)[:, None]
    lane = jnp.arange(LANE, dtype=jnp.int32)[None, :]
    pos_tab = jnp.where(lane < 4, jnp.where(lane % 2 == 0, pos // 64, pos % 64), 0).astype(BF16)
    onehot = (lane == pos // blk).astype(BF16)
    gate_blk = (n_h * hd + 6 * n_kv * hd) // LANE
    n_pool = pool[5] if pool else 0
    kern = functools.partial(_attn_prompt_kernel, tq=tq, tk=tk, wk=wk, blk=blk, n_g=n_g, n_kv=n_kv, hd=hd,
                             n_pool=n_pool)
    kv_spec = lambda off: pl.BlockSpec((t, hd), lambda b, h, q, *_: (b, kvb + off + h))
    in_specs = [pl.BlockSpec((tq, n_g * hd), lambda b, h, q, *_: (b * nqt + q, h)),
                pl.BlockSpec((None, None, nc, hd), lambda b, h, q, *_: (b, h, 0, 0)),
                pl.BlockSpec((None, None, nc, hd), lambda b, h, q, *_: (b, n_kv + h, 0, 0)),
                kv_spec(2 * n_kv), kv_spec(3 * n_kv), kv_spec(4 * n_kv), kv_spec(5 * n_kv),
                pl.BlockSpec((tq, LANE), lambda b, h, q, *_: (b * nqt + q, gate_blk)),
                pl.BlockSpec((None, 1, LANE), lambda b, h, q, *_: (h, 0, 0)),
                pl.BlockSpec((t, LANE), lambda b, h, q, *_: (0, 0)),
                pl.BlockSpec((t, LANE), lambda b, h, q, *_: (0, 0))]
    out_specs = [pl.BlockSpec((tq, n_g * hd), lambda b, h, q, *_: (b * nqt + q, h))]
    out_shape = [jax.ShapeDtypeStruct((bsz * t, n_h * hd), BF16)]
    operands = [proj, cmp, cmp, proj, proj, proj, proj, proj, slopes, pos_tab, onehot]
    prefetch = []
    if pool:
        cache, layer, page_table, pw_rows, nch, _ = pool
        _, _, prow, _ = cache.shape
        bd, n_pages = page_table.shape
        rows_blk = pw_rows.shape[0]
        assert bsz * n_kv == bd and nqt * n_pool == n_pages
        out_rows = n_pool * (prow // rows_blk) * nch
        for k in range(n_pool):
            in_specs.append(pl.BlockSpec((None, None, prow, hd),
                                         lambda b, h, q, pt, k=k: (layer, pt[b * n_kv + h, q * n_pool + k], 0, 0)))
        in_specs.append(pl.BlockSpec((rows_blk, hd), lambda b, h, q, pt: (0, 0)))
        out_specs.append(pl.BlockSpec((None, out_rows, hd), lambda b, h, q, pt: (b * n_kv + h, q, 0)))
        out_shape.append(jax.ShapeDtypeStruct((bd, nqt * out_rows, hd), F32))
        operands += [cache] * n_pool + [pw_rows]
        prefetch = [page_table]
    outs = pl.pallas_call(
        kern,
        grid_spec=pltpu.PrefetchScalarGridSpec(num_scalar_prefetch=len(prefetch), grid=(bsz, n_kv, nqt),
                                               in_specs=in_specs, out_specs=out_specs),
        out_shape=out_shape,
        compiler_params=_params(("arbitrary", "arbitrary", "arbitrary")),
        name="attn_prompt",
    )(*prefetch, *operands)
    return outs if pool else outs[0]


def _pool_pages_kernel(pt_ref, *refs, n_pp):
    _pool_blocks(refs[:n_pp], refs[n_pp], refs[n_pp + 1])


def _page_map(b, p, pt, *, layer, k, n_pp):
    return (layer, pt[b, p * n_pp + k], 0, 0)


def _pool_pages(cache, layer, page_table, pw_rows, nch):
    _, _, prow, hd = cache.shape
    bd, n_pages = page_table.shape
    rows_blk = pw_rows.shape[0]
    n_pp = _pick(n_pages, (8, 4, 2, 1))
    per_page = prow // rows_blk
    in_specs = [pl.BlockSpec((None, None, prow, hd),
                             functools.partial(_page_map, layer=layer, k=k, n_pp=n_pp)) for k in range(n_pp)]
    in_specs.append(pl.BlockSpec((rows_blk, hd), lambda b, p, pt: (0, 0)))
    return pl.pallas_call(
        functools.partial(_pool_pages_kernel, n_pp=n_pp),
        grid_spec=pltpu.PrefetchScalarGridSpec(
            num_scalar_prefetch=1, grid=(bd, n_pages // n_pp), in_specs=in_specs,
            out_specs=pl.BlockSpec((None, n_pp * per_page * nch, hd), lambda b, p, pt: (b, p, 0))),
        out_shape=jax.ShapeDtypeStruct((bd, n_pages * per_page * nch, hd), F32),
        compiler_params=_params(("arbitrary", "arbitrary")),
        name="pool_pages",
    )(page_table, *([cache] * n_pp), pw_rows)


def _attn_sample_a_kernel(q_ref, pooled_ref, pj_ref, kvw_ref, gt_ref, sl_ref, ocw_ref, selm_ref,
                          score_buf, *, past, n_q, n_kv, n_g, hd, blk, wb, nbp):
    nch = 2 * n_kv
    nc = pooled_ref.shape[0] // nch
    n_blocks = (past + n_q + blk - 1) // blk
    kvw_rows = kvw_ref.shape[0]
    half = n_kv * hd
    rep = lambda x: jnp.concatenate([x] * n_g, axis=0)
    sl = sl_ref[...]
    gt = jax.nn.sigmoid(gt_ref[...])
    qpos = past + lax.broadcasted_iota(jnp.int32, (n_q, 1), 0)
    qpos_r = rep(qpos)
    bidx_c = lax.broadcasted_iota(jnp.int32, (1, nc), 1)
    start = bidx_c * blk
    bidx = lax.broadcasted_iota(jnp.int32, (1, nbp), 1)
    cur = qpos // blk
    score_buf[...] = jnp.full(score_buf.shape, DEAD, F32)

    for h in range(n_kv):
        qb = jnp.concatenate([q_ref[:, (h * n_g + g) * hd:(h * n_g + g + 1) * hd] for g in range(n_g)], axis=0)
        qb = (qb * (hd ** -0.5)).astype(BF16)
        slope = jnp.concatenate(
            [jnp.broadcast_to(sl[:, h * n_g + g:h * n_g + g + 1], (n_q, 1)) for g in range(n_g)], axis=0)
        kc = _bdot(pooled_ref[pl.ds(h, nc, stride=nch), :], pj_ref[h])
        vc = _bdot(pooled_ref[pl.ds(n_kv + h, nc, stride=nch), :], pj_ref[n_kv + h])
        s = _bdot_nt(qb, kc)
        dist_c = qpos_r.astype(F32) - (start.astype(F32) + (blk - 1) / 2)
        s = s - slope * jnp.abs(dist_c)
        p_c = _masked_softmax(s, (start + (blk - 1)) <= qpos_r)
        o_c = _bdot(p_c, vc)
        imp = p_c[0:n_q]
        for g in range(1, n_g):
            imp = imp + p_c[g * n_q:(g + 1) * n_q]
        imp = jnp.concatenate([imp, jnp.zeros((n_q, nbp - nc), F32)], axis=1)
        score = jnp.where(bidx < cur, imp, NEG)
        score = jnp.where((bidx == 0) | (bidx == cur), FORCE, score)
        score = jnp.where(bidx < n_blocks, score, DEAD)
        score_buf[h * n_q:(h + 1) * n_q, :] = score

        kw = kvw_ref[:, h * hd:(h + 1) * hd]
        vw = kvw_ref[:, half + h * hd:half + (h + 1) * hd]
        kidx = lax.broadcasted_iota(jnp.int32, (1, kvw_rows), 1)
        dist = qpos_r - (past - wb + kidx)
        valid = (dist >= 0) & (dist < WINDOW) & (kidx < wb + n_q)
        sw = _bdot_nt(qb, kw) - slope * jnp.abs(dist).astype(F32)
        o_w = _bdot(_masked_softmax(sw, valid), vw)
        for g in range(n_g):
            hh = h * n_g + g
            r = slice(g * n_q, (g + 1) * n_q)
            ocw_ref[:, hh * hd:(hh + 1) * hd] = (gt[:, hh:hh + 1] * o_c[r]
                                                 + gt[:, 2 * n_kv * n_g + hh:2 * n_kv * n_g + hh + 1] * o_w[r])

    sc_all = score_buf[...]
    sc_t = sc_all.T
    ii = lax.broadcasted_iota(jnp.int32, (nbp, 1), 0)
    for h in range(n_kv):
        for t in range(n_q):
            r = h * n_q + t
            col = sc_t[:, r:r + 1]
            row = sc_all[r:r + 1, :]
            before = (col > row) | ((col == row) & (ii < bidx))
            rank = jnp.sum(jnp.where(before, 1.0, 0.0), axis=0, keepdims=True)
            sel = jnp.where((rank < min(SEL_TOPK, n_blocks)) & (row > 0.5 * NEG), 1.0, 0.0)
            for g in range(n_g):
                selm_ref[h, g * n_q + t:g * n_q + t + 1, :] = sel


def _attn_sample_a(proj3, pooled, pj, kv_all, slopes_row, past, n_h, n_kv, hd, blk, wb):
    bd, n_q, _ = proj3.shape
    n_g = n_h // n_kv
    n_blocks = (past + n_q + blk - 1) // blk
    nbp = pl.cdiv(n_blocks, LANE) * LANE
    kvw_rows = kv_all.shape[1]
    qw = n_h * hd
    assert n_kv * n_q <= LANE
    kern = functools.partial(_attn_sample_a_kernel, past=past, n_q=n_q, n_kv=n_kv, n_g=n_g, hd=hd, blk=blk,
                             wb=wb, nbp=nbp)
    return pl.pallas_call(
        kern,
        grid=(bd,),
        in_specs=[pl.BlockSpec((None, n_q, qw), lambda b: (b, 0, 0)),
                  pl.BlockSpec((None, pooled.shape[1], hd), lambda b: (b, 0, 0)),
                  pl.BlockSpec((2 * n_kv, hd, hd), lambda b: (0, 0, 0)),
                  pl.BlockSpec((None, kvw_rows, 2 * n_kv * hd), lambda b: (b, 0, 0)),
                  pl.BlockSpec((None, n_q, LANE), lambda b: (b, 0, (qw + 6 * n_kv * hd) // LANE)),
                  pl.BlockSpec((1, LANE), lambda b: (0, 0))],
        out_specs=[pl.BlockSpec((None, n_q, qw), lambda b: (b, 0, 0)),
                   pl.BlockSpec((None, n_kv, n_g * n_q, nbp), lambda b: (b, 0, 0, 0))],
        out_shape=[jax.ShapeDtypeStruct((bd, n_q, qw), F32),
                   jax.ShapeDtypeStruct((bd, n_kv, n_g * n_q, nbp), F32)],
        scratch_shapes=[pltpu.VMEM((LANE, nbp), F32)],
        compiler_params=_params(("arbitrary",)),
        name="attn_sample_a",
    )(proj3, pooled, pj, kv_all, proj3, slopes_row)


def _attn_sample_sel_kernel(pt_ref, *refs, n_pp, past, n_q, n_kv, n_g, hd, blk):
    pages = refs[:n_pp]
    q_ref, selm_ref, new_ref, ocw_ref, gt_ref, sl_ref, o_ref, m_ref, l_ref, acc_ref = refs[n_pp:]
    p = pl.program_id(1)
    nch = 2 * n_kv
    page = pages[0].shape[0] // nch
    keys = n_pp * page
    nbp = selm_ref.shape[-1]
    rows = n_g * n_q
    half = n_kv * hd
    rep = lambda x: jnp.concatenate([x] * n_g, axis=0)
    sl = sl_ref[...]
    qpos_r = rep(past + lax.broadcasted_iota(jnp.int32, (n_q, 1), 0))

    @pl.when(p == 0)
    def _():
        m_ref[...] = jnp.full(m_ref.shape, NEG, F32)
        l_ref[...] = jnp.zeros_like(l_ref)
        acc_ref[...] = jnp.zeros_like(acc_ref)

    def heads():
        for h in range(n_kv):
            qb = jnp.concatenate([q_ref[:, (h * n_g + g) * hd:(h * n_g + g + 1) * hd] for g in range(n_g)],
                                 axis=0) * (hd ** -0.5)
            slope = jnp.concatenate(
                [jnp.broadcast_to(sl[:, h * n_g + g:h * n_g + g + 1], (n_q, 1)) for g in range(n_g)], axis=0)
            yield h, qb, slope

    def update(h, sc, mask, pv_fn):
        sc = jnp.where(mask, sc, NEG)
        m = m_ref[h]
        m_new = jnp.maximum(m, jnp.max(sc, axis=-1, keepdims=True))
        alpha = jnp.exp(m - m_new)
        pr = jnp.where(mask, jnp.exp(sc - m_new), 0.0)
        l_ref[h] = alpha * l_ref[h] + jnp.sum(pr, axis=-1, keepdims=True)
        acc_ref[h] = alpha * acc_ref[h] + pv_fn(pr)
        m_ref[h] = m_new

    k0 = p * keys
    kk = lax.broadcasted_iota(jnp.int32, (nbp, keys), 1)
    nn = lax.broadcasted_iota(jnp.int32, (nbp, keys), 0)
    expand = jnp.where(nn == (k0 + kk) // blk, 1.0, 0.0).astype(BF16)
    dist = qpos_r - (k0 + lax.broadcasted_iota(jnp.int32, (1, keys), 1))
    staged = []
    for h, qb, slope in heads():
        mask = (jnp.dot(selm_ref[h].astype(BF16), expand, preferred_element_type=F32) > 0.5) & (dist >= 0)
        kx = jnp.concatenate([pages[k][pl.ds(h, page, stride=nch), :].astype(BF16) for k in range(n_pp)], axis=0)
        sc = jnp.where(mask, _bdot_nt(qb, kx) - slope * dist.astype(F32), NEG)
        staged.append((h, mask, sc))
    probs = []
    for h, mask, sc in staged:
        m = m_ref[h]
        m_new = jnp.maximum(m, jnp.max(sc, axis=-1, keepdims=True))
        alpha = jnp.exp(m - m_new)
        pr = jnp.where(mask, jnp.exp(sc - m_new), 0.0)
        l_ref[h] = alpha * l_ref[h] + jnp.sum(pr, axis=-1, keepdims=True)
        m_ref[h] = m_new
        probs.append((h, alpha, pr))
    for h, alpha, pr in probs:
        vx = jnp.concatenate([pages[k][pl.ds(n_kv + h, page, stride=nch), :].astype(BF16) for k in range(n_pp)],
                             axis=0)
        acc_ref[h] = alpha * acc_ref[h] + _bdot(pr, vx)

    @pl.when(p == pl.num_programs(1) - 1)
    def _():
        gt = jax.nn.sigmoid(gt_ref[...])
        for h, qb, slope in heads():
            nb0 = past // blk
            seln = selm_ref[h][:, nb0:nb0 + 1] > 0.5
            qr = qb.astype(BF16).astype(F32)
            for j in range(n_q):
                kj = new_ref[j:j + 1, h * hd:(h + 1) * hd].astype(BF16).astype(F32)
                vj = new_ref[j:j + 1, half + h * hd:half + (h + 1) * hd].astype(BF16).astype(F32)
                dj = qpos_r - (past + j)
                sc = jnp.sum(qr * kj, axis=-1, keepdims=True) - slope * dj.astype(F32)
                update(h, sc, seln & (dj >= 0), lambda pr, vj=vj: pr.astype(BF16).astype(F32) * vj)
            o_s = acc_ref[h] / jnp.maximum(l_ref[h], 1e-30)
            for g in range(n_g):
                hh = h * n_g + g
                r = slice(g * n_q, (g + 1) * n_q)
                gcol = n_kv * n_g + hh
                o = ocw_ref[:, hh * hd:(hh + 1) * hd] + gt[:, gcol:gcol + 1] * o_s[r]
                o_ref[:, hh * hd:(hh + 1) * hd] = o.astype(o_ref.dtype)


def _attn_sample_sel(cache, layer, page_table, proj3, selm, new_rows, ocw, slopes_row, past, n_h, n_kv, hd, blk):
    _, _, prow, _ = cache.shape
    bd, n_pages = page_table.shape
    n_q = proj3.shape[1]
    n_g = n_h // n_kv
    qw = n_h * hd
    nbp = selm.shape[-1]
    n_pp = _pick(n_pages, (32, 16, 8, 4, 2, 1))
    rows = n_g * n_q
    in_specs = [pl.BlockSpec((None, None, prow, hd),
                             functools.partial(_page_map, layer=layer, k=k, n_pp=n_pp)) for k in range(n_pp)]
    in_specs += [pl.BlockSpec((None, n_q, qw), lambda b, p, pt: (b, 0, 0)),
                 pl.BlockSpec((None, n_kv, rows, nbp), lambda b, p, pt: (b, 0, 0, 0)),
                 pl.BlockSpec((None, new_rows.shape[1], new_rows.shape[2]), lambda b, p, pt: (b, 0, 0)),
                 pl.BlockSpec((None, n_q, qw), lambda b, p, pt: (b, 0, 0)),
                 pl.BlockSpec((None, n_q, LANE), lambda b, p, pt: (b, 0, (qw + 6 * n_kv * hd) // LANE)),
                 pl.BlockSpec((1, LANE), lambda b, p, pt: (0, 0))]
    kern = functools.partial(_attn_sample_sel_kernel, n_pp=n_pp, past=past, n_q=n_q, n_kv=n_kv, n_g=n_g,
                             hd=hd, blk=blk)
    return pl.pallas_call(
        kern,
        grid_spec=pltpu.PrefetchScalarGridSpec(
            num_scalar_prefetch=1, grid=(bd, n_pages // n_pp), in_specs=in_specs,
            out_specs=pl.BlockSpec((None, n_q, qw), lambda b, p, pt: (b, 0, 0)),
            scratch_shapes=[pltpu.VMEM((n_kv, rows, 1), F32), pltpu.VMEM((n_kv, rows, 1), F32),
                            pltpu.VMEM((n_kv, rows, hd), F32)]),
        out_shape=jax.ShapeDtypeStruct((bd, n_q, qw), BF16),
        compiler_params=_params(("arbitrary", "arbitrary")),
        name="attn_sample_sel",
    )(page_table, *([cache] * n_pp), proj3, selm, new_rows, ocw, proj3, slopes_row)


def _alibi(n_h):
    h = jnp.arange(1, n_h + 1, dtype=F32)
    return jnp.exp2(-8.0 * h / n_h)


def kernel(x_prompt, x_sample, c_prompt, c_sample, state_gdn, state_gdn_conv, state_sconv, cache_kv_cmp,
           cache_kv_sel, state_kv_win, page_table, norm_g, w_ada, b_ada, w_in_e, conv_w_gdn, a_log, dt_bias,
           gdn_norm_g, conv_w_sc, w_out_e, w_in_o, cmp_pool, cmp_proj, w_out_o, w_mlp1, w_mlp2, final_g):
    bsz, seq, d = x_prompt.shape
    bd, n_q, _ = x_sample.shape
    depth = norm_g.shape[0]
    n_vh, dk, dv = state_gdn.shape[2:]
    conv_dim = conv_w_gdn.shape[2]
    sc_dim = conv_w_sc.shape[2]
    nv = n_vh * dv
    blk, _, n_kv = cmp_pool.shape[1:]
    hd = cmp_proj.shape[-1]
    n_h = w_out_o.shape[1] // hd
    kv_row = 2 * n_kv * hd
    qw = n_h * hd
    page = cache_kv_cmp.shape[2]
    past = page_table.shape[1] * page
    wb = state_kv_win.shape[2]
    c = GDN_CHUNK
    assert seq % c == 0 and seq % blk == 0 and seq >= WINDOW and n_q <= c and n_q < blk and past % blk == 0
    assert n_q >= conv_w_gdn.shape[1] - 1 and hd == LANE and dk == LANE and dv == LANE

    rows_c = -(-(bsz + bd) // SUBLANE) * SUBLANE
    c_all = jnp.pad(jnp.concatenate([c_prompt, c_sample], axis=0).astype(F32), ((0, rows_c - bsz - bd), (0, 0)))
    mod = _ada_mod(c_all, w_ada, b_ada).reshape(depth, rows_c, 6, d)

    def mods(l, sample):
        if sample:
            return [jnp.repeat(mod[l, bsz:bsz + bd, j], n_q, axis=0)[None] for j in range(6)]
        return [mod[l, :bsz, j][:, None, :] for j in range(6)]

    slopes = _alibi(n_h)
    slopes_kv = jnp.pad(slopes.reshape(n_kv, 1, n_h // n_kv), ((0, 0), (0, 0), (0, LANE - n_h // n_kv)))
    slopes_row = jnp.pad(slopes.reshape(1, n_h), ((0, 0), (0, LANE - n_h)))
    nch = 2 * n_kv
    cache_cmp = cache_kv_cmp.reshape(cache_kv_cmp.shape[:2] + (page * nch, hd))
    cache_sel = cache_kv_sel.reshape(cache_kv_sel.shape[:2] + (page * nch, hd))

    w_in_e_t = jnp.swapaxes(w_in_e, 1, 2).astype(BF16)
    w_in_o_t = jnp.swapaxes(w_in_o, 1, 2).astype(BF16)

    def even_tail(i):
        o_ab = conv_dim + nv
        o_rest = o_ab + 2 * n_vh
        return jnp.concatenate([w_in_e_t[i, o_rest:], w_in_e_t[i, o_ab:o_rest],
                                jnp.zeros((LANE - 2 * n_vh, d), w_in_e_t.dtype)], axis=0)

    pooled_by_layer = {}

    def mixer(x, l, sample, ev, od):
        nb, t = (bd, n_q) if sample else (bsz, seq)
        i = l // 2
        sh1, sc1, g1 = mods(l, sample)[:3]
        if l % 2 == 0:
            proj = _norm_mod_matmul(x, norm_g[l, 0], sc1, sh1, w_in_e_t, i, main_cols=conv_dim + nv,
                                    w_tail=even_tail(i))
            n_cst = conv_w_gdn.shape[1] - 1
            conv_new = proj.reshape(nb, t, -1)[:, t - n_cst:, :conv_dim]
            if sample:
                projp = jnp.pad(proj.reshape(nb, t, -1), ((0, 0), (0, c - t), (0, 0))).reshape(nb * c, -1)
                cst, scst, s0, tv = state_gdn_conv[i], state_sconv[i], state_gdn[i], t
            else:
                projp = proj
                cst = jnp.zeros((nb, conv_w_gdn.shape[1] - 1, conv_dim), F32)
                scst = jnp.zeros((nb, conv_w_sc.shape[1] - 1, sc_dim), F32)
                s0, tv = jnp.zeros((nb, n_vh, dk, dv), F32), c
            mix, s_new, u_last = _gdn_sconv(projp, cst, scst, s0, conv_w_gdn[i], conv_w_sc[i], a_log[i],
                                            dt_bias[i], gdn_norm_g[i], tv)
            if sample:
                mix = mix.reshape(nb, c, -1)[:, :t].reshape(nb * t, -1)
            sc_new = u_last[:, tv - scst.shape[1]:tv]
            ev.append((s_new, conv_new, sc_new))
            return _matmul_gated_residual(mix, w_out_e, i, x, g1)

        proj, kvc8, kvs8, kvw8 = _norm_mod_matmul(x, norm_g[l, 0], sc1, sh1, w_in_o_t, i,
                                                  rows_out=(qw, 3, nch, hd))
        kv_c, kv_s, kv_w = [a.reshape(nb, t, 2, n_kv, hd) for a in (kvc8, kvs8, kvw8)]
        pj = cmp_proj[i].reshape(2 * n_kv, hd, hd)
        pw_rows = jnp.broadcast_to(cmp_pool[i].reshape(blk * nch, 1), (blk * nch, hd))
        if sample:
            pooled = pooled_by_layer.get(i)
            if pooled is None:
                pooled = _pool_pages(cache_cmp, i, page_table, pw_rows, nch)
            proj3 = proj.reshape(nb, t, -1)
            kv_all = jnp.concatenate([state_kv_win[i].reshape(nb, wb, kv_row).astype(F32),
                                      proj3[:, :, qw + 2 * kv_row:qw + 3 * kv_row]], axis=1)
            kv_all_p = jnp.pad(kv_all, ((0, 0), (0, (-kv_all.shape[1]) % SUBLANE), (0, 0)))
            ocw, selm = _attn_sample_a(proj3, pooled, pj, kv_all_p, slopes_row, past, n_h, n_kv, hd, blk, wb)
            new_rows = jnp.pad(proj3[:, :, qw + kv_row:qw + 2 * kv_row], ((0, 0), (0, (-t) % SUBLANE), (0, 0)))
            o = _attn_sample_sel(cache_sel, i, page_table, proj3, selm, new_rows, ocw, slopes_row, past,
                                 n_h, n_kv, hd, blk).reshape(nb * t, qw)
            kv_win_new = kv_all[:, t:].reshape(nb, wb, 2, n_kv, hd)
        else:
            pw = jnp.broadcast_to(cmp_pool[i].reshape(blk, 2 * n_kv).T[:, :, None], (2 * n_kv, blk, hd))
            cmp = _cmp_prompt(proj, nb, t, pw, pj, n_h)
            n_steps = nb * n_kv * _attn_prompt_tiles(t)[1]
            n_pool = (bd * page_table.shape[1]) // n_steps
            if nb * n_kv == bd and n_pool * n_steps == bd * page_table.shape[1] and 1 <= n_pool <= 8:
                o, pooled_by_layer[i] = _attn_prompt(proj, cmp, slopes_kv, nb, t, n_h, n_kv, hd, blk,
                                                     pool=(cache_cmp, i, page_table, pw_rows, nch, n_pool))
            else:
                o = _attn_prompt(proj, cmp, slopes_kv, nb, t, n_h, n_kv, hd, blk)
            kv_win_new = kv_w[:, t - min(WINDOW, t):]
        od.append((kv_c, kv_s, kv_win_new))
        return _matmul_gated_residual(o, w_out_o, i, x, g1)

    xp = x_prompt.reshape(bsz * seq, d).astype(F32)
    xs = x_sample.reshape(bd * n_q, d).astype(F32)
    ev_p, od_p, ev_s, od_s = [], [], [], []
    for l in range(depth):
        xp = mixer(xp, l, False, ev_p, od_p)
        xs = mixer(xs, l, True, ev_s, od_s)
        _, _, _, sh2, sc2, g2 = mods(l, False)
        _, _, _, sh2s, sc2s, g2s = mods(l, True)
        xp, xs = _mlp(xp, norm_g[l, 1], sc2, sh2, g2, w_mlp1, w_mlp2, l, final_g if l == depth - 1 else None,
                      side=(xs, sc2s, sh2s, g2s))
    stack3 = lambda items: [jnp.stack([s[j] for s in items]) for j in range(3)]
    gdn_p, gconv_p, sconv_p = stack3(ev_p)
    gdn_s, gconv_s, sconv_s = stack3(ev_s)
    kvc_p, kvs_p, kvw_p = stack3(od_p)
    kvc_s, kvs_s, kvw_s = stack3(od_s)
    y_p = xp.reshape(bsz, seq, d)
    y_s = xs.reshape(bd, n_q, d)
    return (y_p, y_s, gdn_p, gdn_s, gconv_p, gconv_s, sconv_p, sconv_s, kvc_p, kvc_s, kvs_p, kvs_s, kvw_p, kvw_s)
```

```python
import functools

import jax
import jax.numpy as jnp
from jax import lax
from jax.experimental import pallas as pl
from jax.experimental.pallas import tpu as pltpu

F32 = jnp.float32
BF16 = jnp.bfloat16

EPS = 1e-6
NEG = -1e30
FORCE = 1e6
DEAD = -3e38
SEL_TOPK = 16
LOG2E = 1.4426950408889634
WINDOW = 512
GDN_CHUNK = 64
LANE = 128
SUBLANE = 8
VMEM_BUDGET = 56 * 1024 * 1024


def _params(sem):
    return pltpu.CompilerParams(dimension_semantics=sem, vmem_limit_bytes=VMEM_BUDGET)


def _pick(n, cands):
    for c in cands:
        if n % c == 0:
            return c
    return n


def _bdot(a, b):
    return jnp.dot(a.astype(BF16), b.astype(BF16), preferred_element_type=F32)


def _bdot_nt(a, b):
    return lax.dot_general(a.astype(BF16), b.astype(BF16), (((1,), (1,)), ((), ())),
                           preferred_element_type=F32)


def _silu(x):
    return x * jax.nn.sigmoid(x)


def _modnorm(x, g, scale, shift):
    y = x * lax.rsqrt(jnp.mean(x * x, axis=-1, keepdims=True) + EPS)
    return (y * g) * (1.0 + scale) + shift


def _modnorm_rows(x_ref, g_ref, sc_ref, sh_ref, hn_ref):
    tm = x_ref.shape[0]
    ch = min(tm, 256)

    def body(r, carry):
        rows = pl.ds(pl.multiple_of(r * ch, ch), ch)
        sc = sc_ref[...] if sc_ref.shape[0] == 1 else sc_ref[rows, :]
        sh = sh_ref[...] if sh_ref.shape[0] == 1 else sh_ref[rows, :]
        hn_ref[rows, :] = _modnorm(x_ref[rows, :], g_ref[...], sc, sh).astype(BF16)
        return carry

    lax.fori_loop(0, tm // ch, body, 0)


def _masked_softmax(s, valid):
    s = jnp.where(valid, s, NEG)
    m = jnp.max(s, axis=-1, keepdims=True)
    p = jnp.where(valid, jnp.exp(s - m), 0.0)
    return p / jnp.maximum(jnp.sum(p, axis=-1, keepdims=True), 1e-30)


def _ada_kernel(c_ref, w_ref, b_ref, o_ref):
    o_ref[...] = _bdot(_silu(c_ref[...]), w_ref[...]) + b_ref[...]


def _ada_mod(c_all, w_ada, b_ada):
    depth, d, n6 = w_ada.shape
    rows = c_all.shape[0]
    tn = _pick(n6, (1024, 512, 256, 128))
    return pl.pallas_call(
        _ada_kernel,
        grid=(depth, n6 // tn),
        in_specs=[pl.BlockSpec((rows, d), lambda l, j: (0, 0)),
                  pl.BlockSpec((None, d, tn), lambda l, j: (l, 0, j)),
                  pl.BlockSpec((None, 1, tn), lambda l, j: (l, 0, j))],
        out_specs=pl.BlockSpec((None, rows, tn), lambda l, j: (l, 0, j)),
        out_shape=jax.ShapeDtypeStruct((depth, rows, n6), F32),
        compiler_params=_params(("arbitrary", "arbitrary")),
        name="ada_mod",
    )(c_all, w_ada, b_ada.reshape(depth, 1, n6))


def _mm1_kernel(x_ref, g_ref, sc_ref, sh_ref, *refs, n_main, n_w, rows_first, n_rows_out, nch):
    w_refs = refs[:n_w]
    o_ref = refs[n_w]
    row_refs = refs[n_w + 1:n_w + 1 + n_rows_out]
    hn_ref = refs[-1]
    j = pl.program_id(1)

    @pl.when(j == 0)
    def _():
        _modnorm_rows(x_ref, g_ref, sc_ref, sh_ref, hn_ref)

    def emit(wt_ref):
        y = lax.dot_general(hn_ref[...], wt_ref[...].astype(BF16), (((1,), (1,)), ((), ())),
                            preferred_element_type=F32)
        o_ref[...] = y
        return y

    if n_w == 1:
        y = emit(w_refs[0])
        tm, tn = o_ref.shape
        hd = row_refs[0].shape[-1] if n_rows_out else LANE
        per_tile = tn // hd
        tiles_per_out = nch // per_tile if n_rows_out else 1
        for r in range(n_rows_out):
            for part in range(tiles_per_out):
                @pl.when(j == rows_first + r * tiles_per_out + part)
                def _(r=r, part=part):
                    for cc in range(per_tile):
                        row_refs[r][pl.ds(part * per_tile + cc, tm, stride=nch), :] = y[:, cc * hd:(cc + 1) * hd]
    else:
        @pl.when(j < n_main)
        def _():
            emit(w_refs[0])

        @pl.when(j >= n_main)
        def _():
            emit(w_refs[1])


def _mod_spec(mod, tm, rows_per_group, width, col):
    r = mod.shape[1]
    tiles = rows_per_group // tm
    if col:
        return pl.BlockSpec((None, r, width), lambda i, j: (i // tiles, 0, j))
    return pl.BlockSpec((None, r, width), lambda i, j: (i // tiles, 0, 0))


def _norm_mod_matmul(x, g, scale, shift, w, layer, main_cols=None, w_tail=None, rows_out=None):
    m, d = x.shape
    tn = 512
    groups = scale.shape[0]
    rpg = m // groups
    if w_tail is None:
        n = w.shape[1]
        n_main = pl.cdiv(n, tn)
        weights = [w]
        w_specs = [pl.BlockSpec((None, tn, d), lambda i, j: (layer, j, 0))]
    else:
        assert main_cols % tn == 0
        n = main_cols + w_tail.shape[0]
        n_main = main_cols // tn
        weights = [w, w_tail]
        w_specs = [pl.BlockSpec((None, tn, d), lambda i, j: (layer, jnp.minimum(j, n_main - 1), 0)),
                   pl.BlockSpec((tn, d), lambda i, j: (jnp.maximum(j - n_main, 0), 0))]
    n_rows_out, rows_first, nch, hd = 0, 0, 1, LANE
    out_shape = [jax.ShapeDtypeStruct((m, n), F32)]
    if rows_out is not None:
        first_col, n_rows_out, nch, hd = rows_out
        assert first_col % tn == 0 and (nch * hd) % tn == 0 and w_tail is None
        rows_first = first_col // tn
    big = n_rows_out == 0 and w.dtype == BF16
    tm = _pick(rpg, ((2048,) if big else ()) + (1024, 512, 256, 128, 64, 32, 16, 8))
    out_specs = [pl.BlockSpec((tm, tn), lambda i, j: (i, j))]
    for _ in range(n_rows_out):
        out_specs.append(pl.BlockSpec((tm * nch, hd), lambda i, j: (i, 0)))
        out_shape.append(jax.ShapeDtypeStruct((m * nch, hd), F32))
    kern = functools.partial(_mm1_kernel, n_main=n_main, n_w=len(weights), rows_first=rows_first,
                             n_rows_out=n_rows_out, nch=nch)
    outs = pl.pallas_call(
        kern,
        grid=(m // tm, pl.cdiv(n, tn)),
        in_specs=[pl.BlockSpec((tm, d), lambda i, j: (i, 0), pipeline_mode=pl.Buffered(1)),
                  pl.BlockSpec((1, d), lambda i, j: (0, 0)),
                  _mod_spec(scale, tm, rpg, d, False),
                  _mod_spec(shift, tm, rpg, d, False)] + w_specs,
        out_specs=out_specs,
        out_shape=out_shape,
        scratch_shapes=[pltpu.VMEM((tm, d), BF16)],
        compiler_params=_params(("arbitrary", "arbitrary")),
        name="norm_mod_matmul",
    )(x, g.reshape(1, d), scale, shift, *weights)
    return outs if n_rows_out else outs[0]


def _mm2_kernel(a_ref, w_ref, x_ref, gate_ref, o_ref):
    y = jnp.dot(a_ref[...], w_ref[...].astype(BF16), preferred_element_type=F32)
    o_ref[...] = x_ref[...] + gate_ref[...] * y


def _matmul_gated_residual(a, w, layer, x, gate):
    m, k = a.shape
    d = w.shape[2]
    groups = gate.shape[0]
    rpg = m // groups
    tm = _pick(rpg, (1024, 512, 256, 128, 64, 32, 16))
    tn = _pick(d, (512, 256, 128))
    return pl.pallas_call(
        _mm2_kernel,
        grid=(m // tm, d // tn),
        in_specs=[pl.BlockSpec((tm, k), lambda i, j: (i, 0)),
                  pl.BlockSpec((None, k, tn), lambda i, j: (layer, 0, j)),
                  pl.BlockSpec((tm, tn), lambda i, j: (i, j)),
                  _mod_spec(gate, tm, rpg, tn, True)],
        out_specs=pl.BlockSpec((tm, tn), lambda i, j: (i, j)),
        out_shape=jax.ShapeDtypeStruct((m, d), F32),
        compiler_params=_params(("arbitrary", "arbitrary")),
        name="matmul_gated_residual",
    )(a, w, x, gate)


def _mlp_kernel(x_ref, g_ref, sc_ref, sh_ref, gate_ref, w1_ref, w2_ref, fg_ref, *refs, final, side):
    if side:
        xs_ref, scs_ref, shs_ref, gates_ref, o_ref, os_ref, hn_ref, hns_ref = refs
    else:
        o_ref, hn_ref = refs
    i = pl.program_id(0)
    j = pl.program_id(1)
    last = pl.num_programs(1) - 1

    def mlp_tile(hn):
        h = jnp.dot(hn, w1_ref[...].astype(BF16), preferred_element_type=F32)
        h = jnp.square(jnp.maximum(h, 0.0))
        return jnp.dot(h.astype(BF16), w2_ref[...].astype(BF16), preferred_element_type=F32)

    def finish(x, gate, acc):
        y = x + gate * acc
        if final:
            y = (y * lax.rsqrt(jnp.mean(y * y, axis=-1, keepdims=True) + EPS)) * fg_ref[...]
        return y

    @pl.when(j == 0)
    def _():
        _modnorm_rows(x_ref, g_ref, sc_ref, sh_ref, hn_ref)
        o_ref[...] = jnp.zeros_like(o_ref)

    o_ref[...] += mlp_tile(hn_ref[...])

    @pl.when(j == last)
    def _():
        o_ref[...] = finish(x_ref[...], gate_ref[...], o_ref[...])

    if side:
        @pl.when((i == 0) & (j == 0))
        def _():
            _modnorm_rows(xs_ref, g_ref, scs_ref, shs_ref, hns_ref)
            os_ref[...] = jnp.zeros_like(os_ref)

        @pl.when(i == 0)
        def _():
            os_ref[...] += mlp_tile(hns_ref[...])

        @pl.when((i == 0) & (j == last))
        def _():
            os_ref[...] = finish(xs_ref[...], gates_ref[...], os_ref[...])


def _mlp(x, g, scale, shift, gate, w1, w2, layer, final_g=None, side=None):
    m, d = x.shape
    final = final_g is not None
    fg = (final_g if final else g).reshape(1, d)
    f = w1.shape[2]
    groups = scale.shape[0]
    rpg = m // groups
    tm = _pick(rpg, (1024, 512, 256, 128, 64, 32, 16, 8))
    tf = _pick(f, (512, 256, 128))
    in_specs = [pl.BlockSpec((tm, d), lambda i, j: (i, 0)),
                pl.BlockSpec((1, d), lambda i, j: (0, 0)),
                _mod_spec(scale, tm, rpg, d, False),
                _mod_spec(shift, tm, rpg, d, False),
                _mod_spec(gate, tm, rpg, d, False),
                pl.BlockSpec((None, d, tf), lambda i, j: (layer, 0, j)),
                pl.BlockSpec((None, tf, d), lambda i, j: (layer, j, 0)),
                pl.BlockSpec((1, d), lambda i, j: (0, 0))]
    out_specs = [pl.BlockSpec((tm, d), lambda i, j: (i, 0), pipeline_mode=pl.Buffered(1))]
    out_shape = [jax.ShapeDtypeStruct((m, d), F32)]
    scratch = [pltpu.VMEM((tm, d), BF16)]
    operands = [x, g.reshape(1, d), scale, shift, gate, w1, w2, fg]
    if side is not None:
        xs = side[0]
        ms = xs.shape[0]
        in_specs.append(pl.BlockSpec((ms, d), lambda i, j: (0, 0)))
        in_specs += [pl.BlockSpec((None, ms, d), lambda i, j: (0, 0, 0))] * 3
        out_specs.append(pl.BlockSpec((ms, d), lambda i, j: (0, 0)))
        out_shape.append(jax.ShapeDtypeStruct((ms, d), F32))
        scratch.append(pltpu.VMEM((ms, d), BF16))
        operands += list(side)
    outs = pl.pallas_call(
        functools.partial(_mlp_kernel, final=final, side=side is not None),
        grid=(m // tm, f // tf),
        in_specs=in_specs,
        out_specs=out_specs,
        out_shape=out_shape,
        scratch_shapes=scratch,
        compiler_params=_params(("arbitrary", "arbitrary")),
        name="mlp",
    )(*operands)
    return outs if side is not None else outs[0]


def _split3(x):
    hi = x.astype(BF16)
    r = x - hi.astype(F32)
    mid = r.astype(BF16)
    lo = (r - mid.astype(F32)).astype(BF16)
    return hi, mid, lo


def _gdn_kernel(qkv_ref, z_ref, ab_ref, gb_ref, gcg_ref, hx_ref, cst_ref, scst_ref, cw_ref, scw_ref,
                alog_ref, dtb_ref, ng_ref, s0_ref,
                mix_ref, sout_ref, ulast_ref,
                xbuf, ubuf, s_ref, *, t_valid, n_kh, n_vh, dk, dv, n_sub):
    c = GDN_CHUNK
    rows = n_sub * c
    n = pl.program_id(1)
    nqk = n_kh * dk
    nv = n_vh * dv
    rep = n_vh // n_kh
    hpg = 2
    gw = hpg * c
    n_conv = cw_ref.shape[0]
    n_sc = scw_ref.shape[0]

    @pl.when(n == 0)
    def _():
        xbuf[0:SUBLANE, :] = cst_ref[...]
        ubuf[0:SUBLANE, :] = scst_ref[...]
        s_ref[...] = s0_ref[...]

    xbuf[SUBLANE:SUBLANE + rows, :] = qkv_ref[...]
    cw = cw_ref[...]
    off = SUBLANE - (n_conv - 1)
    xc_all = cw[0:1, :] * xbuf[off:off + rows, :]
    for j in range(1, n_conv):
        xc_all = xc_all + cw[j:j + 1, :] * xbuf[off + j:off + j + rows, :]
    xbuf[0:SUBLANE, :] = xbuf[rows:rows + SUBLANE, :]
    xc_all = _silu(xc_all)

    u = gcg_ref[...] * hx_ref[...]
    ubuf[SUBLANE:SUBLANE + rows, :] = u
    scw = scw_ref[...]
    offs = SUBLANE - (n_sc - 1)
    cu = scw[0:1, :] * ubuf[offs:offs + rows, :]
    for j in range(1, n_sc):
        cu = cu + scw[j:j + 1, :] * ubuf[offs + j:offs + j + rows, :]
    ubuf[0:SUBLANE, :] = ubuf[rows:rows + SUBLANE, :]
    ulast_ref[...] = u[rows - c:, :]
    mix_ref[:, nv:] = (gb_ref[...] * cu).astype(mix_ref.dtype)

    ab = ab_ref[...]
    g_rows = -jnp.exp(alog_ref[...]) * jax.nn.softplus(ab + dtb_ref[...])
    beta_rows = jax.nn.sigmoid(ab)
    if t_valid < rows:
        rowmask = lax.broadcasted_iota(jnp.int32, (rows, 1), 0) < t_valid
        xc_all = jnp.where(rowmask, xc_all, 0.0)
        g_rows = jnp.where(rowmask, g_rows, 0.0)
        beta_rows = jnp.where(rowmask, beta_rows, 0.0)

    ri = lax.broadcasted_iota(jnp.int32, (c, c), 0)
    ci = lax.broadcasted_iota(jnp.int32, (c, c), 1)
    tril = jnp.where(ri >= ci, 1.0, 0.0).astype(BF16)

    def l2n(x):
        return x * lax.rsqrt(jnp.sum(x * x, axis=-1, keepdims=True) + EPS)

    gi = lax.broadcasted_iota(jnp.int32, (gw, gw), 0)
    gj = lax.broadcasted_iota(jnp.int32, (gw, gw), 1)
    same = (gi // c) == (gj // c)
    low_incl = same & (gi >= gj)
    low_strict = same & (gi > gj)
    lane_blk = lax.broadcasted_iota(jnp.int32, (dk, gw), 1) // c
    level_mask = []
    size = 1
    while size < c:
        level_mask.append(((gi // (2 * size)) == (gj // (2 * size))) & ((gi // size) != (gj // size)) & (gi > gj))
        size *= 2

    def stack(cols):
        return jnp.concatenate(cols, axis=0)

    n_grp = n_vh // hpg
    pre, glasts = [], []
    for sub in range(n_sub):
        r0 = sub * c
        xc = xc_all[r0:r0 + c]
        beta_all = beta_rows[r0:r0 + c]
        ghi, gmid, glo = _split3(g_rows[r0:r0 + c])
        gcum = (jnp.dot(tril, ghi, preferred_element_type=F32)
                + jnp.dot(tril, gmid, preferred_element_type=F32)
                + jnp.dot(tril, glo, preferred_element_type=F32))
        glast = gcum[c - 1:c, :]
        glasts.append(glast)
        for grp in range(n_grp):
            heads = [grp * hpg + j for j in range(hpg)]
            kheads = [h // rep for h in heads]
            qs = {kh: l2n(xc[:, kh * dk:(kh + 1) * dk]) * (dk ** -0.5) for kh in set(kheads)}
            ks = {kh: l2n(xc[:, nqk + kh * dk:nqk + (kh + 1) * dk]) for kh in set(kheads)}
            q_st = stack([qs[kh] for kh in kheads])
            k_st = stack([ks[kh] for kh in kheads])
            v_st = stack([xc[:, 2 * nqk + h * dv:2 * nqk + (h + 1) * dv] for h in heads])
            beta_st = stack([beta_all[:, n_vh + h:n_vh + h + 1] for h in heads])
            gc_st = stack([gcum[:, h:h + 1] for h in heads])
            gl_st = stack([jnp.broadcast_to(glast[:, h:h + 1], (c, 1)) for h in heads])
            gc_row = jnp.broadcast_to(gc_st, (gw, LANE)).T[0:1, :]
            diff = gc_st - gc_row
            decay = jnp.where(low_incl, jnp.exp(jnp.where(low_incl, diff, 0.0)), 0.0)
            egc = jnp.exp(gc_st)
            kb_st = k_st * beta_st

            nmat = _bdot_nt(kb_st, k_st) * jnp.where(low_strict, decay, 0.0)
            qk = _bdot_nt(q_st, k_st) * decay
            rhs = jnp.concatenate([v_st * beta_st, kb_st * egc], axis=1)
            pre.append((nmat, qk, rhs, q_st * egc, k_st * jnp.exp(gl_st - gc_st)))

    tms = [-jnp.where(level_mask[0], p[0], 0.0) for p in pre]
    for lm in level_mask[1:]:
        lls = [jnp.where(lm, p[0], 0.0) for p in pre]
        ys = [ll + _bdot(tm, ll) for tm, ll in zip(tms, lls)]
        tms = [tm - y - _bdot(y, tm) for tm, y in zip(tms, ys)]
    sols = [p[2] + _bdot(tm, p[2]) for p, tm in zip(pre, tms)]

    for sub in range(n_sub):
        r0 = sub * c
        glast = glasts[sub]
        for grp in range(n_grp):
            heads = [grp * hpg + j for j in range(hpg)]
            nmat, qk, rhs, qg_st, kd_st = pre[sub * n_grp + grp]
            sol = sols[sub * n_grp + grp]
            u_st = sol[:, :dv]
            w_st = sol[:, dv:]

            vnew, qs_out = [], []
            for j, h in enumerate(heads):
                wq = jnp.concatenate([w_st[j * c:(j + 1) * c], qg_st[j * c:(j + 1) * c]], axis=0)
                r2 = _bdot(wq, s_ref[h])
                vnew.append(u_st[j * c:(j + 1) * c] - r2[:c])
                qs_out.append(r2[c:])
            vnew_st = stack(vnew)
            o_st = stack(qs_out) + _bdot(qk, vnew_st)
            kd_t = kd_st.T
            for j, h in enumerate(heads):
                upd = _bdot(jnp.where(lane_blk == j, kd_t, 0.0), vnew_st)
                s_ref[h] = s_ref[h] * jnp.exp(glast[:, h:h + 1]) + upd
                o_h = o_st[j * c:(j + 1) * c]
                on = (o_h * lax.rsqrt(jnp.mean(o_h * o_h, axis=-1, keepdims=True) + EPS)) * ng_ref[...]
                zz = z_ref[r0:r0 + c, h * dv:(h + 1) * dv]
                mix_ref[r0:r0 + c, h * dv:(h + 1) * dv] = (on * _silu(zz)).astype(mix_ref.dtype)

    sout_ref[...] = s_ref[...]


def _gdn_sconv(proj, conv_state, sc_state, s0, conv_w, sc_w, a_log, dt_bias, norm_g, t_valid):
    bsz, n_vh, dk, dv = s0.shape
    n_conv, conv_dim = conv_w.shape
    n_sc, sc = sc_w.shape
    nv = n_vh * dv
    n_kh = (conv_dim - nv) // (2 * dk)
    c = GDN_CHUNK
    rows = proj.shape[0]
    nchunk = rows // (bsz * c)
    assert conv_dim % nv == 0 and (conv_dim + nv) % sc == 0 and n_vh % 2 == 0 and 2 * n_vh <= LANE
    n_sub = 2 if (nchunk % 2 == 0 and t_valid == c) else 1
    assert t_valid == c or nchunk == 1
    nstep = nchunk // n_sub
    rs = n_sub * c
    zb = conv_dim // nv
    gbb = (conv_dim + nv) // sc
    abb = (conv_dim + nv + 3 * sc) // LANE
    pad_lane = lambda v: jnp.pad(v.reshape(1, -1), ((0, 0), (0, LANE - v.shape[-1])))
    cst = jnp.pad(conv_state.astype(F32), ((0, 0), (SUBLANE - (n_conv - 1), 0), (0, 0)))
    scst = jnp.pad(sc_state.astype(F32), ((0, 0), (SUBLANE - (n_sc - 1), 0), (0, 0)))
    row = lambda b, n: b * nstep + n
    kern = functools.partial(_gdn_kernel, t_valid=t_valid if n_sub == 1 else rs, n_kh=n_kh, n_vh=n_vh, dk=dk,
                             dv=dv, n_sub=n_sub)
    return pl.pallas_call(
        kern,
        grid=(bsz, nstep),
        in_specs=[pl.BlockSpec((rs, conv_dim), lambda b, n: (row(b, n), 0)),
                  pl.BlockSpec((rs, nv), lambda b, n: (row(b, n), zb)),
                  pl.BlockSpec((rs, LANE), lambda b, n: (row(b, n), abb)),
                  pl.BlockSpec((rs, sc), lambda b, n: (row(b, n), gbb)),
                  pl.BlockSpec((rs, sc), lambda b, n: (row(b, n), gbb + 1)),
                  pl.BlockSpec((rs, sc), lambda b, n: (row(b, n), gbb + 2)),
                  pl.BlockSpec((None, SUBLANE, conv_dim), lambda b, n: (b, 0, 0)),
                  pl.BlockSpec((None, SUBLANE, sc), lambda b, n: (b, 0, 0)),
                  pl.BlockSpec((n_conv, conv_dim), lambda b, n: (0, 0)),
                  pl.BlockSpec((n_sc, sc), lambda b, n: (0, 0)),
                  pl.BlockSpec((1, LANE), lambda b, n: (0, 0)),
                  pl.BlockSpec((1, LANE), lambda b, n: (0, 0)),
                  pl.BlockSpec((1, dv), lambda b, n: (0, 0)),
                  pl.BlockSpec((None, n_vh, dk, dv), lambda b, n: (b, 0, 0, 0))],
        out_specs=[pl.BlockSpec((rs, nv + sc), lambda b, n: (row(b, n), 0)),
                   pl.BlockSpec((None, n_vh, dk, dv), lambda b, n: (b, 0, 0, 0)),
                   pl.BlockSpec((None, c, sc), lambda b, n: (b, 0, 0))],
        out_shape=[jax.ShapeDtypeStruct((rows, nv + sc), BF16),
                   jax.ShapeDtypeStruct((bsz, n_vh, dk, dv), F32),
                   jax.ShapeDtypeStruct((bsz, c, sc), F32)],
        scratch_shapes=[pltpu.VMEM((rs + SUBLANE, conv_dim), F32),
                        pltpu.VMEM((rs + SUBLANE, sc), F32),
                        pltpu.VMEM((n_vh, dk, dv), F32)],
        compiler_params=_params(("arbitrary", "arbitrary")),
        name="gdn_sconv",
    )(proj, proj, proj, proj, proj, proj, cst, scst, conv_w, sc_w,
      pad_lane(a_log), pad_lane(dt_bias), norm_g.reshape(1, dv), s0.astype(F32))


def _cmp_prompt_kernel(kv_ref, pw_ref, pj_ref, o_ref, *, blk):
    t, hd = kv_ref.shape
    x = kv_ref[...].reshape(t // blk, blk, hd) * pw_ref[...][None]
    o_ref[...] = _bdot(jnp.sum(x, axis=1), pj_ref[...])


def _cmp_prompt(proj, bsz, t, pw, pj, col0):
    nch, blk, hd = pw.shape
    return pl.pallas_call(
        functools.partial(_cmp_prompt_kernel, blk=blk),
        grid=(bsz, nch),
        in_specs=[pl.BlockSpec((t, hd), lambda b, ch: (b, col0 + ch)),
                  pl.BlockSpec((None, blk, hd), lambda b, ch: (ch, 0, 0)),
                  pl.BlockSpec((None, hd, hd), lambda b, ch: (ch, 0, 0))],
        out_specs=pl.BlockSpec((None, None, t // blk, hd), lambda b, ch: (b, ch, 0, 0)),
        out_shape=jax.ShapeDtypeStruct((bsz, nch, t // blk, hd), F32),
        compiler_params=_params(("arbitrary", "arbitrary")),
        name="cmp_prompt",
    )(proj, pw, pj)


_NT = (((1,), (1,)), ((), ()))


def _slope_features(sl2, lane):
    hi = sl2.astype(BF16).astype(F32)
    lo = sl2 - hi
    return jnp.where(lane == 0, 64.0 * hi, jnp.where(lane == 1, hi, jnp.where(lane == 2, 64.0 * lo,
                                                                             jnp.where(lane == 3, lo, 0.0))))


def _pool_blocks(page_refs, pw_ref, o_ref):
    pw = pw_ref[...]
    rows_blk = pw.shape[0]
    per_page = page_refs[0].shape[0] // rows_blk
    nch = o_ref.shape[0] // (len(page_refs) * per_page)
    for k, page_ref in enumerate(page_refs):
        for hb in range(per_page):
            r = k * per_page + hb
            x = page_ref[hb * rows_blk:(hb + 1) * rows_blk, :] * pw
            o_ref[r * nch:(r + 1) * nch, :] = jnp.sum(x.reshape(rows_blk // nch, nch, x.shape[-1]), axis=0)


def _attn_prompt_kernel(*refs, tq, tk, wk, blk, n_g, n_kv, hd, n_pool):
    if n_pool:
        refs = refs[1:]
    q_ref, kc_ref, vc_ref, ks_ref, vs_ref, kw_ref, vw_ref, gt_ref, sl_ref, pos_ref, oh_ref = refs[:11]
    o_ref = refs[11 + n_pool + (1 if n_pool else 0)]
    h = pl.program_id(1)
    qt = pl.program_id(2)
    q0 = qt * tq
    nc = kc_ref.shape[0]
    t_all = ks_ref.shape[0]
    sl = sl_ref[...]
    lane = lax.broadcasted_iota(jnp.int32, (1, LANE), 1)
    qs = [q_ref[:, g * hd:(g + 1) * hd] * (hd ** -0.5) for g in range(n_g)]
    qb = jnp.concatenate(qs, axis=0).astype(BF16)
    qaug = jnp.concatenate(
        [jnp.concatenate([(qs[g] * LOG2E).astype(BF16),
                          jnp.broadcast_to(_slope_features(sl[:, g:g + 1] * LOG2E, lane), (tq, LANE)).astype(BF16)],
                         axis=1) for g in range(n_g)], axis=0)
    qpos = q0 + lax.broadcasted_iota(jnp.int32, (tq, 1), 0)

    bidx = lax.broadcasted_iota(jnp.int32, (1, nc), 1)
    start = bidx * blk
    s_c = _bdot_nt(qb, kc_ref[...])
    dist_c = jnp.abs(qpos.astype(F32) - (start.astype(F32) + (blk - 1) / 2))
    valid_c = (start + (blk - 1)) <= qpos
    p_c = jnp.concatenate([_masked_softmax(s_c[g * tq:(g + 1) * tq] - sl[:, g:g + 1] * dist_c, valid_c)
                           for g in range(n_g)], axis=0)
    o_c = _bdot(p_c, vc_ref[...])
    imp = p_c[0:tq]
    for g in range(1, n_g):
        imp = imp + p_c[g * tq:(g + 1) * tq]

    imp_t = jnp.concatenate([imp, jnp.zeros((tq, LANE - nc), F32)], axis=1).T[0:nc, :]
    bcol = lax.broadcasted_iota(jnp.int32, (nc, 1), 0)
    cur = (q0 + lax.broadcasted_iota(jnp.int32, (1, tq), 1)) // blk
    score = jnp.where(bcol < cur, imp_t, NEG)
    score = jnp.where((bcol == 0) | (bcol == cur), FORCE, score)
    rank = jnp.zeros((nc, tq), F32)
    for i in range(nc):
        ri = score[i:i + 1, :]
        rank = rank + jnp.where((ri > score) | ((ri == score) & (bcol > i)), 1.0, 0.0)
    selneg_t = jnp.where((rank < min(SEL_TOPK, nc)) & (score > 0.5 * NEG), 0.0, NEG)
    selneg = jnp.concatenate([selneg_t, jnp.zeros((LANE - nc, tq), F32)], axis=0).T.astype(BF16)

    def flash_step(k_ref, v_ref, k0, size, bias, carry):
        m, l, acc = carry
        kaug = jnp.concatenate([k_ref[pl.ds(k0, size), :].astype(BF16), pos_ref[pl.ds(k0, size), :]], axis=1)
        s = lax.dot_general(qaug, kaug, _NT, preferred_element_type=F32)
        gs = range(n_g)
        sgs = [s[g * tq:(g + 1) * tq] + bias for g in gs]
        m2 = [jnp.maximum(m[g], jnp.max(sgs[g], axis=-1, keepdims=True)) for g in gs]
        als = [jnp.exp2(m[g] - m2[g]) for g in gs]
        pfs = [jnp.exp2(sgs[g] - m2[g]) for g in gs]
        l2 = [als[g] * l[g] + jnp.sum(pfs[g], axis=-1, keepdims=True) for g in gs]
        ps = [p.astype(BF16) for p in pfs]
        accs = [als[g] * acc[g * tq:(g + 1) * tq] for g in gs]
        pv = jnp.dot(jnp.concatenate(ps, axis=0), v_ref[pl.ds(k0, size), :].astype(BF16),
                     preferred_element_type=F32)
        return tuple(m2), tuple(l2), jnp.concatenate(accs, axis=0) + pv

    def finish(carry):
        _, l, acc = carry
        return [acc[g * tq:(g + 1) * tq] / jnp.maximum(l[g], 1e-30) for g in range(n_g)]

    init = (tuple(jnp.full((tq, 1), NEG, F32) for _ in range(n_g)),
            tuple(jnp.zeros((tq, 1), F32) for _ in range(n_g)), jnp.zeros((n_g * tq, hd), F32))

    def sel_bias(k0):
        return lax.dot_general(selneg, oh_ref[pl.ds(k0, tk), :], _NT, preferred_element_type=F32)

    def sel_chunk(ci, carry):
        k0 = pl.multiple_of(ci * tk, tk)
        return flash_step(ks_ref, vs_ref, k0, tk, sel_bias(k0), carry)

    n_full = q0 // tk
    carry = lax.fori_loop(0, n_full, sel_chunk, init)
    k0 = pl.multiple_of(n_full * tk, tk)
    kpos = k0 + lax.broadcasted_iota(jnp.int32, (1, tk), 1)
    o_s = finish(flash_step(ks_ref, vs_ref, k0, tk, sel_bias(k0) + jnp.where(kpos <= qpos, 0.0, NEG), carry))

    w0 = pl.multiple_of(jnp.clip(q0 + tq - wk, 0, t_all - wk), tq)
    dist = qpos - (w0 + lax.broadcasted_iota(jnp.int32, (1, wk), 1))
    o_w = finish(flash_step(kw_ref, vw_ref, w0, wk, jnp.where((dist >= 0) & (dist < WINDOW), 0.0, NEG), init))

    gt = jax.nn.sigmoid(pltpu.roll(gt_ref[...], (LANE - n_g * h) % LANE, 1))
    n_h = n_g * n_kv
    for g in range(n_g):
        r = slice(g * tq, (g + 1) * tq)
        o = (gt[:, g:g + 1] * o_c[r] + gt[:, n_h + g:n_h + g + 1] * o_s[g]
             + gt[:, 2 * n_h + g:2 * n_h + g + 1] * o_w[g])
        o_ref[:, g * hd:(g + 1) * hd] = o.astype(o_ref.dtype)

    if n_pool:
        _pool_blocks(refs[11:11 + n_pool], refs[11 + n_pool], refs[-1])


def _attn_prompt_tiles(t):
    return 256, t // 256


def _attn_prompt(proj, cmp, slopes, bsz, t, n_h, n_kv, hd, blk, pool=None):
    n_g = n_h // n_kv
    tq, nqt = _attn_prompt_tiles(t)
    tk = _pick(t, (512, 256, 128))
    wk = WINDOW + tq
    assert t % tq == 0 and t >= wk and t // blk <= LANE and blk == 64 and n_h * 3 <= LANE
    nc = t // blk
    kvb = n_h
    pos = jnp.arange(t, dtype=jnp.int32)[:, None]
    lane = jnp.arange(LANE, dtype=jnp.int32)[None, :]
    pos_tab = jnp.where(lane < 4, jnp.where(lane % 2 == 0, pos // 64, pos % 64), 0).astype(BF16)
    onehot = (lane == pos // blk).astype(BF16)
    gate_blk = (n_h * hd + 6 * n_kv * hd) // LANE
    n_pool = pool[5] if pool else 0
    kern = functools.partial(_attn_prompt_kernel, tq=tq, tk=tk, wk=wk, blk=blk, n_g=n_g, n_kv=n_kv, hd=hd,
                             n_pool=n_pool)
    kv_spec = lambda off: pl.BlockSpec((t, hd), lambda b, h, q, *_: (b, kvb + off + h))
    in_specs = [pl.BlockSpec((tq, n_g * hd), lambda b, h, q, *_: (b * nqt + q, h)),
                pl.BlockSpec((None, None, nc, hd), lambda b, h, q, *_: (b, h, 0, 0)),
                pl.BlockSpec((None, None, nc, hd), lambda b, h, q, *_: (b, n_kv + h, 0, 0)),
                kv_spec(2 * n_kv), kv_spec(3 * n_kv), kv_spec(4 * n_kv), kv_spec(5 * n_kv),
                pl.BlockSpec((tq, LANE), lambda b, h, q, *_: (b * nqt + q, gate_blk)),
                pl.BlockSpec((None, 1, LANE), lambda b, h, q, *_: (h, 0, 0)),
                pl.BlockSpec((t, LANE), lambda b, h, q, *_: (0, 0)),
                pl.BlockSpec((t, LANE), lambda b, h, q, *_: (0, 0))]
    out_specs = [pl.BlockSpec((tq, n_g * hd), lambda b, h, q, *_: (b * nqt + q, h))]
    out_shape = [jax.ShapeDtypeStruct((bsz * t, n_h * hd), BF16)]
    operands = [proj, cmp, cmp, proj, proj, proj, proj, proj, slopes, pos_tab, onehot]
    prefetch = []
    if pool:
        cache, layer, page_table, pw_rows, nch, _ = pool
        _, _, prow, _ = cache.shape
        bd, n_pages = page_table.shape
        rows_blk = pw_rows.shape[0]
        assert bsz * n_kv == bd and nqt * n_pool == n_pages
        out_rows = n_pool * (prow // rows_blk) * nch
        for k in range(n_pool):
            in_specs.append(pl.BlockSpec((None, None, prow, hd),
                                         lambda b, h, q, pt, k=k: (layer, pt[b * n_kv + h, q * n_pool + k], 0, 0)))
        in_specs.append(pl.BlockSpec((rows_blk, hd), lambda b, h, q, pt: (0, 0)))
        out_specs.append(pl.BlockSpec((None, out_rows, hd), lambda b, h, q, pt: (b * n_kv + h, q, 0)))
        out_shape.append(jax.ShapeDtypeStruct((bd, nqt * out_rows, hd), F32))
        operands += [cache] * n_pool + [pw_rows]
        prefetch = [page_table]
    outs = pl.pallas_call(
        kern,
        grid_spec=pltpu.PrefetchScalarGridSpec(num_scalar_prefetch=len(prefetch), grid=(bsz, n_kv, nqt),
                                               in_specs=in_specs, out_specs=out_specs),
        out_shape=out_shape,
        compiler_params=_params(("arbitrary", "arbitrary", "arbitrary")),
        name="attn_prompt",
    )(*prefetch, *operands)
    return outs if pool else outs[0]


def _pool_pages_kernel(pt_ref, *refs, n_pp):
    _pool_blocks(refs[:n_pp], refs[n_pp], refs[n_pp + 1])


def _page_map(b, p, pt, *, layer, k, n_pp):
    return (layer, pt[b, p * n_pp + k], 0, 0)


def _pool_pages(cache, layer, page_table, pw_rows, nch):
    _, _, prow, hd = cache.shape
    bd, n_pages = page_table.shape
    rows_blk = pw_rows.shape[0]
    n_pp = _pick(n_pages, (8, 4, 2, 1))
    per_page = prow // rows_blk
    in_specs = [pl.BlockSpec((None, None, prow, hd),
                             functools.partial(_page_map, layer=layer, k=k, n_pp=n_pp)) for k in range(n_pp)]
    in_specs.append(pl.BlockSpec((rows_blk, hd), lambda b, p, pt: (0, 0)))
    return pl.pallas_call(
        functools.partial(_pool_pages_kernel, n_pp=n_pp),
        grid_spec=pltpu.PrefetchScalarGridSpec(
            num_scalar_prefetch=1, grid=(bd, n_pages // n_pp), in_specs=in_specs,
            out_specs=pl.BlockSpec((None, n_pp * per_page * nch, hd), lambda b, p, pt: (b, p, 0))),
        out_shape=jax.ShapeDtypeStruct((bd, n_pages * per_page * nch, hd), F32),
        compiler_params=_params(("arbitrary", "arbitrary")),
        name="pool_pages",
    )(page_table, *([cache] * n_pp), pw_rows)


def _attn_sample_a_kernel(q_ref, pooled_ref, pj_ref, kvw_ref, gt_ref, sl_ref, ocw_ref, selm_ref,
                          score_buf, *, past, n_q, n_kv, n_g, hd, blk, wb, nbp):
    nch = 2 * n_kv
    nc = pooled_ref.shape[0] // nch
    n_blocks = (past + n_q + blk - 1) // blk
    kvw_rows = kvw_ref.shape[0]
    half = n_kv * hd
    rep = lambda x: jnp.concatenate([x] * n_g, axis=0)
    sl = sl_ref[...]
    gt = jax.nn.sigmoid(gt_ref[...])
    qpos = past + lax.broadcasted_iota(jnp.int32, (n_q, 1), 0)
    qpos_r = rep(qpos)
    bidx_c = lax.broadcasted_iota(jnp.int32, (1, nc), 1)
    start = bidx_c * blk
    bidx = lax.broadcasted_iota(jnp.int32, (1, nbp), 1)
    cur = qpos // blk
    score_buf[...] = jnp.full(score_buf.shape, DEAD, F32)

    for h in range(n_kv):
        qb = jnp.concatenate([q_ref[:, (h * n_g + g) * hd:(h * n_g + g + 1) * hd] for g in range(n_g)], axis=0)
        qb = (qb * (hd ** -0.5)).astype(BF16)
        slope = jnp.concatenate(
            [jnp.broadcast_to(sl[:, h * n_g + g:h * n_g + g + 1], (n_q, 1)) for g in range(n_g)], axis=0)
        kc = _bdot(pooled_ref[pl.ds(h, nc, stride=nch), :], pj_ref[h])
        vc = _bdot(pooled_ref[pl.ds(n_kv + h, nc, stride=nch), :], pj_ref[n_kv + h])
        s = _bdot_nt(qb, kc)
        dist_c = qpos_r.astype(F32) - (start.astype(F32) + (blk - 1) / 2)
        s = s - slope * jnp.abs(dist_c)
        p_c = _masked_softmax(s, (start + (blk - 1)) <= qpos_r)
        o_c = _bdot(p_c, vc)
        imp = p_c[0:n_q]
        for g in range(1, n_g):
            imp = imp + p_c[g * n_q:(g + 1) * n_q]
        imp = jnp.concatenate([imp, jnp.zeros((n_q, nbp - nc), F32)], axis=1)
        score = jnp.where(bidx < cur, imp, NEG)
        score = jnp.where((bidx == 0) | (bidx == cur), FORCE, score)
        score = jnp.where(bidx < n_blocks, score, DEAD)
        score_buf[h * n_q:(h + 1) * n_q, :] = score

        kw = kvw_ref[:, h * hd:(h + 1) * hd]
        vw = kvw_ref[:, half + h * hd:half + (h + 1) * hd]
        kidx = lax.broadcasted_iota(jnp.int32, (1, kvw_rows), 1)
        dist = qpos_r - (past - wb + kidx)
        valid = (dist >= 0) & (dist < WINDOW) & (kidx < wb + n_q)
        sw = _bdot_nt(qb, kw) - slope * jnp.abs(dist).astype(F32)
        o_w = _bdot(_masked_softmax(sw, valid), vw)
        for g in range(n_g):
            hh = h * n_g + g
            r = slice(g * n_q, (g + 1) * n_q)
            ocw_ref[:, hh * hd:(hh + 1) * hd] = (gt[:, hh:hh + 1] * o_c[r]
                                                 + gt[:, 2 * n_kv * n_g + hh:2 * n_kv * n_g + hh + 1] * o_w[r])

    sc_all = score_buf[...]
    sc_t = sc_all.T
    ii = lax.broadcasted_iota(jnp.int32, (nbp, 1), 0)
    for h in range(n_kv):
        for t in range(n_q):
            r = h * n_q + t
            col = sc_t[:, r:r + 1]
            row = sc_all[r:r + 1, :]
            before = (col > row) | ((col == row) & (ii < bidx))
            rank = jnp.sum(jnp.where(before, 1.0, 0.0), axis=0, keepdims=True)
            sel = jnp.where((rank < min(SEL_TOPK, n_blocks)) & (row > 0.5 * NEG), 1.0, 0.0)
            for g in range(n_g):
                selm_ref[h, g * n_q + t:g * n_q + t + 1, :] = sel


def _attn_sample_a(proj3, pooled, pj, kv_all, slopes_row, past, n_h, n_kv, hd, blk, wb):
    bd, n_q, _ = proj3.shape
    n_g = n_h // n_kv
    n_blocks = (past + n_q + blk - 1) // blk
    nbp = pl.cdiv(n_blocks, LANE) * LANE
    kvw_rows = kv_all.shape[1]
    qw = n_h * hd
    assert n_kv * n_q <= LANE
    kern = functools.partial(_attn_sample_a_kernel, past=past, n_q=n_q, n_kv=n_kv, n_g=n_g, hd=hd, blk=blk,
                             wb=wb, nbp=nbp)
    return pl.pallas_call(
        kern,
        grid=(bd,),
        in_specs=[pl.BlockSpec((None, n_q, qw), lambda b: (b, 0, 0)),
                  pl.BlockSpec((None, pooled.shape[1], hd), lambda b: (b, 0, 0)),
                  pl.BlockSpec((2 * n_kv, hd, hd), lambda b: (0, 0, 0)),
                  pl.BlockSpec((None, kvw_rows, 2 * n_kv * hd), lambda b: (b, 0, 0)),
                  pl.BlockSpec((None, n_q, LANE), lambda b: (b, 0, (qw + 6 * n_kv * hd) // LANE)),
                  pl.BlockSpec((1, LANE), lambda b: (0, 0))],
        out_specs=[pl.BlockSpec((None, n_q, qw), lambda b: (b, 0, 0)),
                   pl.BlockSpec((None, n_kv, n_g * n_q, nbp), lambda b: (b, 0, 0, 0))],
        out_shape=[jax.ShapeDtypeStruct((bd, n_q, qw), F32),
                   jax.ShapeDtypeStruct((bd, n_kv, n_g * n_q, nbp), F32)],
        scratch_shapes=[pltpu.VMEM((LANE, nbp), F32)],
        compiler_params=_params(("arbitrary",)),
        name="attn_sample_a",
    )(proj3, pooled, pj, kv_all, proj3, slopes_row)


def _attn_sample_sel_kernel(pt_ref, *refs, n_pp, past, n_q, n_kv, n_g, hd, blk):
    pages = refs[:n_pp]
    q_ref, selm_ref, new_ref, ocw_ref, gt_ref, sl_ref, o_ref, m_ref, l_ref, acc_ref = refs[n_pp:]
    p = pl.program_id(1)
    nch = 2 * n_kv
    page = pages[0].shape[0] // nch
    keys = n_pp * page
    nbp = selm_ref.shape[-1]
    rows = n_g * n_q
    half = n_kv * hd
    rep = lambda x: jnp.concatenate([x] * n_g, axis=0)
    sl = sl_ref[...]
    qpos_r = rep(past + lax.broadcasted_iota(jnp.int32, (n_q, 1), 0))

    @pl.when(p == 0)
    def _():
        m_ref[...] = jnp.full(m_ref.shape, NEG, F32)
        l_ref[...] = jnp.zeros_like(l_ref)
        acc_ref[...] = jnp.zeros_like(acc_ref)

    def heads():
        for h in range(n_kv):
            qb = jnp.concatenate([q_ref[:, (h * n_g + g) * hd:(h * n_g + g + 1) * hd] for g in range(n_g)],
                                 axis=0) * (hd ** -0.5)
            slope = jnp.concatenate(
                [jnp.broadcast_to(sl[:, h * n_g + g:h * n_g + g + 1], (n_q, 1)) for g in range(n_g)], axis=0)
            yield h, qb, slope

    def update(h, sc, mask, pv_fn):
        sc = jnp.where(mask, sc, NEG)
        m = m_ref[h]
        m_new = jnp.maximum(m, jnp.max(sc, axis=-1, keepdims=True))
        alpha = jnp.exp(m - m_new)
        pr = jnp.where(mask, jnp.exp(sc - m_new), 0.0)
        l_ref[h] = alpha * l_ref[h] + jnp.sum(pr, axis=-1, keepdims=True)
        acc_ref[h] = alpha * acc_ref[h] + pv_fn(pr)
        m_ref[h] = m_new

    k0 = p * keys
    kk = lax.broadcasted_iota(jnp.int32, (nbp, keys), 1)
    nn = lax.broadcasted_iota(jnp.int32, (nbp, keys), 0)
    expand = jnp.where(nn == (k0 + kk) // blk, 1.0, 0.0).astype(BF16)
    dist = qpos_r - (k0 + lax.broadcasted_iota(jnp.int32, (1, keys), 1))
    staged = []
    for h, qb, slope in heads():
        mask = (jnp.dot(selm_ref[h].astype(BF16), expand, preferred_element_type=F32) > 0.5) & (dist >= 0)
        kx = jnp.concatenate([pages[k][pl.ds(h, page, stride=nch), :].astype(BF16) for k in range(n_pp)], axis=0)
        sc = jnp.where(mask, _bdot_nt(qb, kx) - slope * dist.astype(F32), NEG)
        staged.append((h, mask, sc))
    probs = []
    for h, mask, sc in staged:
        m = m_ref[h]
        m_new = jnp.maximum(m, jnp.max(sc, axis=-1, keepdims=True))
        alpha = jnp.exp(m - m_new)
        pr = jnp.where(mask, jnp.exp(sc - m_new), 0.0)
        l_ref[h] = alpha * l_ref[h] + jnp.sum(pr, axis=-1, keepdims=True)
        m_ref[h] = m_new
        probs.append((h, alpha, pr))
    for h, alpha, pr in probs:
        vx = jnp.concatenate([pages[k][pl.ds(n_kv + h, page, stride=nch), :].astype(BF16) for k in range(n_pp)],
                             axis=0)
        acc_ref[h] = alpha * acc_ref[h] + _bdot(pr, vx)

    @pl.when(p == pl.num_programs(1) - 1)
    def _():
        gt = jax.nn.sigmoid(gt_ref[...])
        for h, qb, slope in heads():
            nb0 = past // blk
            seln = selm_ref[h][:, nb0:nb0 + 1] > 0.5
            qr = qb.astype(BF16).astype(F32)
            for j in range(n_q):
                kj = new_ref[j:j + 1, h * hd:(h + 1) * hd].astype(BF16).astype(F32)
                vj = new_ref[j:j + 1, half + h * hd:half + (h + 1) * hd].astype(BF16).astype(F32)
                dj = qpos_r - (past + j)
                sc = jnp.sum(qr * kj, axis=-1, keepdims=True) - slope * dj.astype(F32)
                update(h, sc, seln & (dj >= 0), lambda pr, vj=vj: pr.astype(BF16).astype(F32) * vj)
            o_s = acc_ref[h] / jnp.maximum(l_ref[h], 1e-30)
            for g in range(n_g):
                hh = h * n_g + g
                r = slice(g * n_q, (g + 1) * n_q)
                gcol = n_kv * n_g + hh
                o = ocw_ref[:, hh * hd:(hh + 1) * hd] + gt[:, gcol:gcol + 1] * o_s[r]
                o_ref[:, hh * hd:(hh + 1) * hd] = o.astype(o_ref.dtype)


def _attn_sample_sel(cache, layer, page_table, proj3, selm, new_rows, ocw, slopes_row, past, n_h, n_kv, hd, blk):
    _, _, prow, _ = cache.shape
    bd, n_pages = page_table.shape
    n_q = proj3.shape[1]
    n_g = n_h // n_kv
    qw = n_h * hd
    nbp = selm.shape[-1]
    n_pp = _pick(n_pages, (32, 16, 8, 4, 2, 1))
    rows = n_g * n_q
    in_specs = [pl.BlockSpec((None, None, prow, hd),
                             functools.partial(_page_map, layer=layer, k=k, n_pp=n_pp)) for k in range(n_pp)]
    in_specs += [pl.BlockSpec((None, n_q, qw), lambda b, p, pt: (b, 0, 0)),
                 pl.BlockSpec((None, n_kv, rows, nbp), lambda b, p, pt: (b, 0, 0, 0)),
                 pl.BlockSpec((None, new_rows.shape[1], new_rows.shape[2]), lambda b, p, pt: (b, 0, 0)),
                 pl.BlockSpec((None, n_q, qw), lambda b, p, pt: (b, 0, 0)),
                 pl.BlockSpec((None, n_q, LANE), lambda b, p, pt: (b, 0, (qw + 6 * n_kv * hd) // LANE)),
                 pl.BlockSpec((1, LANE), lambda b, p, pt: (0, 0))]
    kern = functools.partial(_attn_sample_sel_kernel, n_pp=n_pp, past=past, n_q=n_q, n_kv=n_kv, n_g=n_g,
                             hd=hd, blk=blk)
    return pl.pallas_call(
        kern,
        grid_spec=pltpu.PrefetchScalarGridSpec(
            num_scalar_prefetch=1, grid=(bd, n_pages // n_pp), in_specs=in_specs,
            out_specs=pl.BlockSpec((None, n_q, qw), lambda b, p, pt: (b, 0, 0)),
            scratch_shapes=[pltpu.VMEM((n_kv, rows, 1), F32), pltpu.VMEM((n_kv, rows, 1), F32),
                            pltpu.VMEM((n_kv, rows, hd), F32)]),
        out_shape=jax.ShapeDtypeStruct((bd, n_q, qw), BF16),
        compiler_params=_params(("arbitrary", "arbitrary")),
        name="attn_sample_sel",
    )(page_table, *([cache] * n_pp), proj3, selm, new_rows, ocw, proj3, slopes_row)


def _alibi(n_h):
    h = jnp.arange(1, n_h + 1, dtype=F32)
    return jnp.exp2(-8.0 * h / n_h)


def kernel(x_prompt, x_sample, c_prompt, c_sample, state_gdn, state_gdn_conv, state_sconv, cache_kv_cmp,
           cache_kv_sel, state_kv_win, page_table, norm_g, w_ada, b_ada, w_in_e, conv_w_gdn, a_log, dt_bias,
           gdn_norm_g, conv_w_sc, w_out_e, w_in_o, cmp_pool, cmp_proj, w_out_o, w_mlp1, w_mlp2, final_g):
    bsz, seq, d = x_prompt.shape
    bd, n_q, _ = x_sample.shape
    depth = norm_g.shape[0]
    n_vh, dk, dv = state_gdn.shape[2:]
    conv_dim = conv_w_gdn.shape[2]
    sc_dim = conv_w_sc.shape[2]
    nv = n_vh * dv
    blk, _, n_kv = cmp_pool.shape[1:]
    hd = cmp_proj.shape[-1]
    n_h = w_out_o.shape[1] // hd
    kv_row = 2 * n_kv * hd
    qw = n_h * hd
    page = cache_kv_cmp.shape[2]
    past = page_table.shape[1] * page
    wb = state_kv_win.shape[2]
    c = GDN_CHUNK
    assert seq % c == 0 and seq % blk == 0 and seq >= WINDOW and n_q <= c and n_q < blk and past % blk == 0
    assert n_q >= conv_w_gdn.shape[1] - 1 and hd == LANE and dk == LANE and dv == LANE

    rows_c = -(-(bsz + bd) // SUBLANE) * SUBLANE
    c_all = jnp.pad(jnp.concatenate([c_prompt, c_sample], axis=0).astype(F32), ((0, rows_c - bsz - bd), (0, 0)))
    mod = _ada_mod(c_all, w_ada, b_ada).reshape(depth, rows_c, 6, d)

    def mods(l, sample):
        if sample:
            return [jnp.repeat(mod[l, bsz:bsz + bd, j], n_q, axis=0)[None] for j in range(6)]
        return [mod[l, :bsz, j][:, None, :] for j in range(6)]

    slopes = _alibi(n_h)
    slopes_kv = jnp.pad(slopes.reshape(n_kv, 1, n_h // n_kv), ((0, 0), (0, 0), (0, LANE - n_h // n_kv)))
    slopes_row = jnp.pad(slopes.reshape(1, n_h), ((0, 0), (0, LANE - n_h)))
    nch = 2 * n_kv
    cache_cmp = cache_kv_cmp.reshape(cache_kv_cmp.shape[:2] + (page * nch, hd))
    cache_sel = cache_kv_sel.reshape(cache_kv_sel.shape[:2] + (page * nch, hd))

    w_in_e_t = jnp.swapaxes(w_in_e, 1, 2).astype(BF16)
    w_in_o_t = jnp.swapaxes(w_in_o, 1, 2).astype(BF16)

    def even_tail(i):
        o_ab = conv_dim + nv
        o_rest = o_ab + 2 * n_vh
        return jnp.concatenate([w_in_e_t[i, o_rest:], w_in_e_t[i, o_ab:o_rest],
                                jnp.zeros((LANE - 2 * n_vh, d), w_in_e_t.dtype)], axis=0)

    pooled_by_layer = {}

    def mixer(x, l, sample, ev, od):
        nb, t = (bd, n_q) if sample else (bsz, seq)
        i = l // 2
        sh1, sc1, g1 = mods(l, sample)[:3]
        if l % 2 == 0:
            proj = _norm_mod_matmul(x, norm_g[l, 0], sc1, sh1, w_in_e_t, i, main_cols=conv_dim + nv,
                                    w_tail=even_tail(i))
            n_cst = conv_w_gdn.shape[1] - 1
            conv_new = proj.reshape(nb, t, -1)[:, t - n_cst:, :conv_dim]
            if sample:
                projp = jnp.pad(proj.reshape(nb, t, -1), ((0, 0), (0, c - t), (0, 0))).reshape(nb * c, -1)
                cst, scst, s0, tv = state_gdn_conv[i], state_sconv[i], state_gdn[i], t
            else:
                projp = proj
                cst = jnp.zeros((nb, conv_w_gdn.shape[1] - 1, conv_dim), F32)
                scst = jnp.zeros((nb, conv_w_sc.shape[1] - 1, sc_dim), F32)
                s0, tv = jnp.zeros((nb, n_vh, dk, dv), F32), c
            mix, s_new, u_last = _gdn_sconv(projp, cst, scst, s0, conv_w_gdn[i], conv_w_sc[i], a_log[i],
                                            dt_bias[i], gdn_norm_g[i], tv)
            if sample:
                mix = mix.reshape(nb, c, -1)[:, :t].reshape(nb * t, -1)
            sc_new = u_last[:, tv - scst.shape[1]:tv]
            ev.append((s_new, conv_new, sc_new))
            return _matmul_gated_residual(mix, w_out_e, i, x, g1)

        proj, kvc8, kvs8, kvw8 = _norm_mod_matmul(x, norm_g[l, 0], sc1, sh1, w_in_o_t, i,
                                                  rows_out=(qw, 3, nch, hd))
        kv_c, kv_s, kv_w = [a.reshape(nb, t, 2, n_kv, hd) for a in (kvc8, kvs8, kvw8)]
        pj = cmp_proj[i].reshape(2 * n_kv, hd, hd)
        pw_rows = jnp.broadcast_to(cmp_pool[i].reshape(blk * nch, 1), (blk * nch, hd))
        if sample:
            pooled = pooled_by_layer.get(i)
            if pooled is None:
                pooled = _pool_pages(cache_cmp, i, page_table, pw_rows, nch)
            proj3 = proj.reshape(nb, t, -1)
            kv_all = jnp.concatenate([state_kv_win[i].reshape(nb, wb, kv_row).astype(F32),
                                      proj3[:, :, qw + 2 * kv_row:qw + 3 * kv_row]], axis=1)
            kv_all_p = jnp.pad(kv_all, ((0, 0), (0, (-kv_all.shape[1]) % SUBLANE), (0, 0)))
            ocw, selm = _attn_sample_a(proj3, pooled, pj, kv_all_p, slopes_row, past, n_h, n_kv, hd, blk, wb)
            new_rows = jnp.pad(proj3[:, :, qw + kv_row:qw + 2 * kv_row], ((0, 0), (0, (-t) % SUBLANE), (0, 0)))
            o = _attn_sample_sel(cache_sel, i, page_table, proj3, selm, new_rows, ocw, slopes_row, past,
                                 n_h, n_kv, hd, blk).reshape(nb * t, qw)
            kv_win_new = kv_all[:, t:].reshape(nb, wb, 2, n_kv, hd)
        else:
            pw = jnp.broadcast_to(cmp_pool[i].reshape(blk, 2 * n_kv).T[:, :, None], (2 * n_kv, blk, hd))
            cmp = _cmp_prompt(proj, nb, t, pw, pj, n_h)
            n_steps = nb * n_kv * _attn_prompt_tiles(t)[1]
            n_pool = (bd * page_table.shape[1]) // n_steps
            if nb * n_kv == bd and n_pool * n_steps == bd * page_table.shape[1] and 1 <= n_pool <= 8:
                o, pooled_by_layer[i] = _attn_prompt(proj, cmp, slopes_kv, nb, t, n_h, n_kv, hd, blk,
                                                     pool=(cache_cmp, i, page_table, pw_rows, nch, n_pool))
            else:
                o = _attn_prompt(proj, cmp, slopes_kv, nb, t, n_h, n_kv, hd, blk)
            kv_win_new = kv_w[:, t - min(WINDOW, t):]
        od.append((kv_c, kv_s, kv_win_new))
        return _matmul_gated_residual(o, w_out_o, i, x, g1)

    xp = x_prompt.reshape(bsz * seq, d).astype(F32)
    xs = x_sample.reshape(bd * n_q, d).astype(F32)
    ev_p, od_p, ev_s, od_s = [], [], [], []
    for l in range(depth):
        xp = mixer(xp, l, False, ev_p, od_p)
        xs = mixer(xs, l, True, ev_s, od_s)
        _, _, _, sh2, sc2, g2 = mods(l, False)
        _, _, _, sh2s, sc2s, g2s = mods(l, True)
        xp, xs = _mlp(xp, norm_g[l, 1], sc2, sh2, g2, w_mlp1, w_mlp2, l, final_g if l == depth - 1 else None,
                      side=(xs, sc2s, sh2s, g2s))
    stack3 = lambda items: [jnp.stack([s[j] for s in items]) for j in range(3)]
    gdn_p, gconv_p, sconv_p = stack3(ev_p)
    gdn_s, gconv_s, sconv_s = stack3(ev_s)
    kvc_p, kvs_p, kvw_p = stack3(od_p)
    kvc_s, kvs_s, kvw_s = stack3(od_s)
    y_p = xp.reshape(bsz, seq, d)
    y_s = xs.reshape(bd, n_q, d)
    return (y_p, y_s, gdn_p, gdn_s, gconv_p, gconv_s, sconv_p, sconv_s, kvc_p, kvc_s, kvs_p, kvs_s, kvw_p, kvw_s)
```

```python
import functools

import jax
import jax.numpy as jnp
from jax import lax
from jax.experimental import pallas as pl
from jax.experimental.pallas import tpu as pltpu

F32 = jnp.float32
BF16 = jnp.bfloat16

EPS = 1e-6
NEG = -1e30
FORCE = 1e6
DEAD = -3e38
SEL_TOPK = 16
LOG2E = 1.4426950408889634
WINDOW = 512
GDN_CHUNK = 64
LANE = 128
SUBLANE = 8
VMEM_BUDGET = 56 * 1024 * 1024


def _params(sem):
    return pltpu.CompilerParams(dimension_semantics=sem, vmem_limit_bytes=VMEM_BUDGET)


def _pick(n, cands):
    for c in cands:
        if n % c == 0:
            return c
    return n


def _bdot(a, b):
    return jnp.dot(a.astype(BF16), b.astype(BF16), preferred_element_type=F32)


def _bdot_nt(a, b):
    return lax.dot_general(a.astype(BF16), b.astype(BF16), (((1,), (1,)), ((), ())),
                           preferred_element_type=F32)


def _silu(x):
    return x * jax.nn.sigmoid(x)


def _modnorm(x, g, scale, shift):
    y = x * lax.rsqrt(jnp.mean(x * x, axis=-1, keepdims=True) + EPS)
    return (y * g) * (1.0 + scale) + shift


def _modnorm_rows(x_ref, g_ref, sc_ref, sh_ref, hn_ref):
    tm = x_ref.shape[0]
    ch = min(tm, 256)

    def body(r, carry):
        rows = pl.ds(pl.multiple_of(r * ch, ch), ch)
        sc = sc_ref[...] if sc_ref.shape[0] == 1 else sc_ref[rows, :]
        sh = sh_ref[...] if sh_ref.shape[0] == 1 else sh_ref[rows, :]
        hn_ref[rows, :] = _modnorm(x_ref[rows, :], g_ref[...], sc, sh).astype(BF16)
        return carry

    lax.fori_loop(0, tm // ch, body, 0)


def _masked_softmax(s, valid):
    s = jnp.where(valid, s, NEG)
    m = jnp.max(s, axis=-1, keepdims=True)
    p = jnp.where(valid, jnp.exp(s - m), 0.0)
    return p / jnp.maximum(jnp.sum(p, axis=-1, keepdims=True), 1e-30)


def _ada_kernel(c_ref, w_ref, b_ref, o_ref):
    o_ref[...] = _bdot(_silu(c_ref[...]), w_ref[...]) + b_ref[...]


def _ada_mod(c_all, w_ada, b_ada):
    depth, d, n6 = w_ada.shape
    rows = c_all.shape[0]
    tn = _pick(n6, (1024, 512, 256, 128))
    return pl.pallas_call(
        _ada_kernel,
        grid=(depth, n6 // tn),
        in_specs=[pl.BlockSpec((rows, d), lambda l, j: (0, 0)),
                  pl.BlockSpec((None, d, tn), lambda l, j: (l, 0, j)),
                  pl.BlockSpec((None, 1, tn), lambda l, j: (l, 0, j))],
        out_specs=pl.BlockSpec((None, rows, tn), lambda l, j: (l, 0, j)),
        out_shape=jax.ShapeDtypeStruct((depth, rows, n6), F32),
        compiler_params=_params(("arbitrary", "arbitrary")),
        name="ada_mod",
    )(c_all, w_ada, b_ada.reshape(depth, 1, n6))


def _mm1_kernel(x_ref, g_ref, sc_ref, sh_ref, *refs, n_main, n_w, rows_first, n_rows_out, nch):
    w_refs = refs[:n_w]
    o_ref = refs[n_w]
    row_refs = refs[n_w + 1:n_w + 1 + n_rows_out]
    hn_ref = refs[-1]
    j = pl.program_id(1)

    @pl.when(j == 0)
    def _():
        _modnorm_rows(x_ref, g_ref, sc_ref, sh_ref, hn_ref)

    def emit(wt_ref):
        y = lax.dot_general(hn_ref[...], wt_ref[...].astype(BF16), (((1,), (1,)), ((), ())),
                            preferred_element_type=F32)
        o_ref[...] = y
        return y

    if n_w == 1:
        y = emit(w_refs[0])
        tm, tn = o_ref.shape
        hd = row_refs[0].shape[-1] if n_rows_out else LANE
        per_tile = tn // hd
        tiles_per_out = nch // per_tile if n_rows_out else 1
        for r in range(n_rows_out):
            for part in range(tiles_per_out):
                @pl.when(j == rows_first + r * tiles_per_out + part)
                def _(r=r, part=part):
                    for cc in range(per_tile):
                        row_refs[r][pl.ds(part * per_tile + cc, tm, stride=nch), :] = y[:, cc * hd:(cc + 1) * hd]
    else:
        @pl.when(j < n_main)
        def _():
            emit(w_refs[0])

        @pl.when(j >= n_main)
        def _():
            emit(w_refs[1])


def _mod_spec(mod, tm, rows_per_group, width, col):
    r = mod.shape[1]
    tiles = rows_per_group // tm
    if col:
        return pl.BlockSpec((None, r, width), lambda i, j: (i // tiles, 0, j))
    return pl.BlockSpec((None, r, width), lambda i, j: (i // tiles, 0, 0))


def _norm_mod_matmul(x, g, scale, shift, w, layer, main_cols=None, w_tail=None, rows_out=None):
    m, d = x.shape
    tn = 512
    groups = scale.shape[0]
    rpg = m // groups
    if w_tail is None:
        n = w.shape[1]
        n_main = pl.cdiv(n, tn)
        weights = [w]
        w_specs = [pl.BlockSpec((None, tn, d), lambda i, j: (layer, j, 0))]
    else:
        assert main_cols % tn == 0
        n = main_cols + w_tail.shape[0]
        n_main = main_cols // tn
        weights = [w, w_tail]
        w_specs = [pl.BlockSpec((None, tn, d), lambda i, j: (layer, jnp.minimum(j, n_main - 1), 0)),
                   pl.BlockSpec((tn, d), lambda i, j: (jnp.maximum(j - n_main, 0), 0))]
    n_rows_out, rows_first, nch, hd = 0, 0, 1, LANE
    out_shape = [jax.ShapeDtypeStruct((m, n), F32)]
    if rows_out is not None:
        first_col, n_rows_out, nch, hd = rows_out
        assert first_col % tn == 0 and (nch * hd) % tn == 0 and w_tail is None
        rows_first = first_col // tn
    big = n_rows_out == 0 and w.dtype == BF16
    tm = _pick(rpg, ((2048,) if big else ()) + (1024, 512, 256, 128, 64, 32, 16, 8))
    out_specs = [pl.BlockSpec((tm, tn), lambda i, j: (i, j))]
    for _ in range(n_rows_out):
        out_specs.append(pl.BlockSpec((tm * nch, hd), lambda i, j: (i, 0)))
        out_shape.append(jax.ShapeDtypeStruct((m * nch, hd), F32))
    kern = functools.partial(_mm1_kernel, n_main=n_main, n_w=len(weights), rows_first=rows_first,
                             n_rows_out=n_rows_out, nch=nch)
    outs = pl.pallas_call(
        kern,
        grid=(m // tm, pl.cdiv(n, tn)),
        in_specs=[pl.BlockSpec((tm, d), lambda i, j: (i, 0), pipeline_mode=pl.Buffered(1)),
                  pl.BlockSpec((1, d), lambda i, j: (0, 0)),
                  _mod_spec(scale, tm, rpg, d, False),
                  _mod_spec(shift, tm, rpg, d, False)] + w_specs,
        out_specs=out_specs,
        out_shape=out_shape,
        scratch_shapes=[pltpu.VMEM((tm, d), BF16)],
        compiler_params=_params(("arbitrary", "arbitrary")),
        name="norm_mod_matmul",
    )(x, g.reshape(1, d), scale, shift, *weights)
    return outs if n_rows_out else outs[0]


def _mm2_kernel(a_ref, w_ref, x_ref, gate_ref, o_ref):
    y = jnp.dot(a_ref[...], w_ref[...].astype(BF16), preferred_element_type=F32)
    o_ref[...] = x_ref[...] + gate_ref[...] * y


def _matmul_gated_residual(a, w, layer, x, gate):
    m, k = a.shape
    d = w.shape[2]
    groups = gate.shape[0]
    rpg = m // groups
    tm = _pick(rpg, (1024, 512, 256, 128, 64, 32, 16))
    tn = _pick(d, (512, 256, 128))
    return pl.pallas_call(
        _mm2_kernel,
        grid=(m // tm, d // tn),
        in_specs=[pl.BlockSpec((tm, k), lambda i, j: (i, 0)),
                  pl.BlockSpec((None, k, tn), lambda i, j: (layer, 0, j)),
                  pl.BlockSpec((tm, tn), lambda i, j: (i, j)),
                  _mod_spec(gate, tm, rpg, tn, True)],
        out_specs=pl.BlockSpec((tm, tn), lambda i, j: (i, j)),
        out_shape=jax.ShapeDtypeStruct((m, d), F32),
        compiler_params=_params(("arbitrary", "arbitrary")),
        name="matmul_gated_residual",
    )(a, w, x, gate)


def _mlp_kernel(x_ref, g_ref, sc_ref, sh_ref, gate_ref, w1_ref, w2_ref, fg_ref, *refs, final, side):
    if side:
        xs_ref, scs_ref, shs_ref, gates_ref, o_ref, os_ref, hn_ref, hns_ref = refs
    else:
        o_ref, hn_ref = refs
    i = pl.program_id(0)
    j = pl.program_id(1)
    last = pl.num_programs(1) - 1

    def mlp_tile(hn):
        h = jnp.dot(hn, w1_ref[...].astype(BF16), preferred_element_type=F32)
        h = jnp.square(jnp.maximum(h, 0.0))
        return jnp.dot(h.astype(BF16), w2_ref[...].astype(BF16), preferred_element_type=F32)

    def finish(x, gate, acc):
        y = x + gate * acc
        if final:
            y = (y * lax.rsqrt(jnp.mean(y * y, axis=-1, keepdims=True) + EPS)) * fg_ref[...]
        return y

    @pl.when(j == 0)
    def _():
        _modnorm_rows(x_ref, g_ref, sc_ref, sh_ref, hn_ref)
        o_ref[...] = jnp.zeros_like(o_ref)

    o_ref[...] += mlp_tile(hn_ref[...])

    @pl.when(j == last)
    def _():
        o_ref[...] = finish(x_ref[...], gate_ref[...], o_ref[...])

    if side:
        @pl.when((i == 0) & (j == 0))
        def _():
            _modnorm_rows(xs_ref, g_ref, scs_ref, shs_ref, hns_ref)
            os_ref[...] = jnp.zeros_like(os_ref)

        @pl.when(i == 0)
        def _():
            os_ref[...] += mlp_tile(hns_ref[...])

        @pl.when((i == 0) & (j == last))
        def _():
            os_ref[...] = finish(xs_ref[...], gates_ref[...], os_ref[...])


def _mlp(x, g, scale, shift, gate, w1, w2, layer, final_g=None, side=None):
    m, d = x.shape
    final = final_g is not None
    fg = (final_g if final else g).reshape(1, d)
    f = w1.shape[2]
    groups = scale.shape[0]
    rpg = m // groups
    tm = _pick(rpg, (1024, 512, 256, 128, 64, 32, 16, 8))
    tf = _pick(f, (512, 256, 128))
    in_specs = [pl.BlockSpec((tm, d), lambda i, j: (i, 0)),
                pl.BlockSpec((1, d), lambda i, j: (0, 0)),
                _mod_spec(scale, tm, rpg, d, False),
                _mod_spec(shift, tm, rpg, d, False),
                _mod_spec(gate, tm, rpg, d, False),
                pl.BlockSpec((None, d, tf), lambda i, j: (layer, 0, j)),
                pl.BlockSpec((None, tf, d), lambda i, j: (layer, j, 0)),
                pl.BlockSpec((1, d), lambda i, j: (0, 0))]
    out_specs = [pl.BlockSpec((tm, d), lambda i, j: (i, 0), pipeline_mode=pl.Buffered(1))]
    out_shape = [jax.ShapeDtypeStruct((m, d), F32)]
    scratch = [pltpu.VMEM((tm, d), BF16)]
    operands = [x, g.reshape(1, d), scale, shift, gate, w1, w2, fg]
    if side is not None:
        xs = side[0]
        ms = xs.shape[0]
        in_specs.append(pl.BlockSpec((ms, d), lambda i, j: (0, 0)))
        in_specs += [pl.BlockSpec((None, ms, d), lambda i, j: (0, 0, 0))] * 3
        out_specs.append(pl.BlockSpec((ms, d), lambda i, j: (0, 0)))
        out_shape.append(jax.ShapeDtypeStruct((ms, d), F32))
        scratch.append(pltpu.VMEM((ms, d), BF16))
        operands += list(side)
    outs = pl.pallas_call(
        functools.partial(_mlp_kernel, final=final, side=side is not None),
        grid=(m // tm, f // tf),
        in_specs=in_specs,
        out_specs=out_specs,
        out_shape=out_shape,
        scratch_shapes=scratch,
        compiler_params=_params(("arbitrary", "arbitrary")),
        name="mlp",
    )(*operands)
    return outs if side is not None else outs[0]


def _split3(x):
    hi = x.astype(BF16)
    r = x - hi.astype(F32)
    mid = r.astype(BF16)
    lo = (r - mid.astype(F32)).astype(BF16)
    return hi, mid, lo


def _gdn_kernel(qkv_ref, z_ref, ab_ref, gb_ref, gcg_ref, hx_ref, cst_ref, scst_ref, cw_ref, scw_ref,
                alog_ref, dtb_ref, ng_ref, s0_ref,
                mix_ref, sout_ref, ulast_ref,
                xbuf, ubuf, s_ref, *, t_valid, n_kh, n_vh, dk, dv, n_sub):
    c = GDN_CHUNK
    rows = n_sub * c
    n = pl.program_id(1)
    nqk = n_kh * dk
    nv = n_vh * dv
    rep = n_vh // n_kh
    hpg = 2
    gw = hpg * c
    n_conv = cw_ref.shape[0]
    n_sc = scw_ref.shape[0]

    @pl.when(n == 0)
    def _():
        xbuf[0:SUBLANE, :] = cst_ref[...]
        ubuf[0:SUBLANE, :] = scst_ref[...]
        s_ref[...] = s0_ref[...]

    xbuf[SUBLANE:SUBLANE + rows, :] = qkv_ref[...]
    cw = cw_ref[...]
    off = SUBLANE - (n_conv - 1)
    xc_all = cw[0:1, :] * xbuf[off:off + rows, :]
    for j in range(1, n_conv):
        xc_all = xc_all + cw[j:j + 1, :] * xbuf[off + j:off + j + rows, :]
    xbuf[0:SUBLANE, :] = xbuf[rows:rows + SUBLANE, :]
    xc_all = _silu(xc_all)

    u = gcg_ref[...] * hx_ref[...]
    ubuf[SUBLANE:SUBLANE + rows, :] = u
    scw = scw_ref[...]
    offs = SUBLANE - (n_sc - 1)
    cu = scw[0:1, :] * ubuf[offs:offs + rows, :]
    for j in range(1, n_sc):
        cu = cu + scw[j:j + 1, :] * ubuf[offs + j:offs + j + rows, :]
    ubuf[0:SUBLANE, :] = ubuf[rows:rows + SUBLANE, :]
    ulast_ref[...] = u[rows - c:, :]
    mix_ref[:, nv:] = (gb_ref[...] * cu).astype(mix_ref.dtype)

    ab = ab_ref[...]
    g_rows = -jnp.exp(alog_ref[...]) * jax.nn.softplus(ab + dtb_ref[...])
    beta_rows = jax.nn.sigmoid(ab)
    if t_valid < rows:
        rowmask = lax.broadcasted_iota(jnp.int32, (rows, 1), 0) < t_valid
        xc_all = jnp.where(rowmask, xc_all, 0.0)
        g_rows = jnp.where(rowmask, g_rows, 0.0)
        beta_rows = jnp.where(rowmask, beta_rows, 0.0)

    ri = lax.broadcasted_iota(jnp.int32, (c, c), 0)
    ci = lax.broadcasted_iota(jnp.int32, (c, c), 1)
    tril = jnp.where(ri >= ci, 1.0, 0.0).astype(BF16)

    def l2n(x):
        return x * lax.rsqrt(jnp.sum(x * x, axis=-1, keepdims=True) + EPS)

    gi = lax.broadcasted_iota(jnp.int32, (gw, gw), 0)
    gj = lax.broadcasted_iota(jnp.int32, (gw, gw), 1)
    same = (gi // c) == (gj // c)
    low_incl = same & (gi >= gj)
    low_strict = same & (gi > gj)
    lane_blk = lax.broadcasted_iota(jnp.int32, (dk, gw), 1) // c
    level_mask = []
    size = 1
    while size < c:
        level_mask.append(((gi // (2 * size)) == (gj // (2 * size))) & ((gi // size) != (gj // size)) & (gi > gj))
        size *= 2

    def stack(cols):
        return jnp.concatenate(cols, axis=0)

    n_grp = n_vh // hpg
    pre, glasts = [], []
    for sub in range(n_sub):
        r0 = sub * c
        xc = xc_all[r0:r0 + c]
        beta_all = beta_rows[r0:r0 + c]
        ghi, gmid, glo = _split3(g_rows[r0:r0 + c])
        gcum = (jnp.dot(tril, ghi, preferred_element_type=F32)
                + jnp.dot(tril, gmid, preferred_element_type=F32)
                + jnp.dot(tril, glo, preferred_element_type=F32))
        glast = gcum[c - 1:c, :]
        glasts.append(glast)
        for grp in range(n_grp):
            heads = [grp * hpg + j for j in range(hpg)]
            kheads = [h // rep for h in heads]
            qs = {kh: l2n(xc[:, kh * dk:(kh + 1) * dk]) * (dk ** -0.5) for kh in set(kheads)}
            ks = {kh: l2n(xc[:, nqk + kh * dk:nqk + (kh + 1) * dk]) for kh in set(kheads)}
            q_st = stack([qs[kh] for kh in kheads])
            k_st = stack([ks[kh] for kh in kheads])
            v_st = stack([xc[:, 2 * nqk + h * dv:2 * nqk + (h + 1) * dv] for h in heads])
            beta_st = stack([beta_all[:, n_vh + h:n_vh + h + 1] for h in heads])
            gc_st = stack([gcum[:, h:h + 1] for h in heads])
            gl_st = stack([jnp.broadcast_to(glast[:, h:h + 1], (c, 1)) for h in heads])
            gc_row = jnp.broadcast_to(gc_st, (gw, LANE)).T[0:1, :]
            diff = gc_st - gc_row
            decay = jnp.where(low_incl, jnp.exp(jnp.where(low_incl, diff, 0.0)), 0.0)
            egc = jnp.exp(gc_st)
            kb_st = k_st * beta_st

            nmat = _bdot_nt(kb_st, k_st) * jnp.where(low_strict, decay, 0.0)
            qk = _bdot_nt(q_st, k_st) * decay
            rhs = jnp.concatenate([v_st * beta_st, kb_st * egc], axis=1)
            pre.append((nmat, qk, rhs, q_st * egc, k_st * jnp.exp(gl_st - gc_st)))

    tms = [-jnp.where(level_mask[0], p[0], 0.0) for p in pre]
    for lm in level_mask[1:]:
        lls = [jnp.where(lm, p[0], 0.0) for p in pre]
        ys = [ll + _bdot(tm, ll) for tm, ll in zip(tms, lls)]
        tms = [tm - y - _bdot(y, tm) for tm, y in zip(tms, ys)]
    sols = [p[2] + _bdot(tm, p[2]) for p, tm in zip(pre, tms)]

    for sub in range(n_sub):
        r0 = sub * c
        glast = glasts[sub]
        for grp in range(n_grp):
            heads = [grp * hpg + j for j in range(hpg)]
            nmat, qk, rhs, qg_st, kd_st = pre[sub * n_grp + grp]
            sol = sols[sub * n_grp + grp]
            u_st = sol[:, :dv]
            w_st = sol[:, dv:]

            vnew, qs_out = [], []
            for j, h in enumerate(heads):
                wq = jnp.concatenate([w_st[j * c:(j + 1) * c], qg_st[j * c:(j + 1) * c]], axis=0)
                r2 = _bdot(wq, s_ref[h])
                vnew.append(u_st[j * c:(j + 1) * c] - r2[:c])
                qs_out.append(r2[c:])
            vnew_st = stack(vnew)
            o_st = stack(qs_out) + _bdot(qk, vnew_st)
            kd_t = kd_st.T
            for j, h in enumerate(heads):
                upd = _bdot(jnp.where(lane_blk == j, kd_t, 0.0), vnew_st)
                s_ref[h] = s_ref[h] * jnp.exp(glast[:, h:h + 1]) + upd
                o_h = o_st[j * c:(j + 1) * c]
                on = (o_h * lax.rsqrt(jnp.mean(o_h * o_h, axis=-1, keepdims=True) + EPS)) * ng_ref[...]
                zz = z_ref[r0:r0 + c, h * dv:(h + 1) * dv]
                mix_ref[r0:r0 + c, h * dv:(h + 1) * dv] = (on * _silu(zz)).astype(mix_ref.dtype)

    sout_ref[...] = s_ref[...]


def _gdn_sconv(proj, conv_state, sc_state, s0, conv_w, sc_w, a_log, dt_bias, norm_g, t_valid):
    bsz, n_vh, dk, dv = s0.shape
    n_conv, conv_dim = conv_w.shape
    n_sc, sc = sc_w.shape
    nv = n_vh * dv
    n_kh = (conv_dim - nv) // (2 * dk)
    c = GDN_CHUNK
    rows = proj.shape[0]
    nchunk = rows // (bsz * c)
    assert conv_dim % nv == 0 and (conv_dim + nv) % sc == 0 and n_vh % 2 == 0 and 2 * n_vh <= LANE
    n_sub = 2 if (nchunk % 2 == 0 and t_valid == c) else 1
    assert t_valid == c or nchunk == 1
    nstep = nchunk // n_sub
    rs = n_sub * c
    zb = conv_dim // nv
    gbb = (conv_dim + nv) // sc
    abb = (conv_dim + nv + 3 * sc) // LANE
    pad_lane = lambda v: jnp.pad(v.reshape(1, -1), ((0, 0), (0, LANE - v.shape[-1])))
    cst = jnp.pad(conv_state.astype(F32), ((0, 0), (SUBLANE - (n_conv - 1), 0), (0, 0)))
    scst = jnp.pad(sc_state.astype(F32), ((0, 0), (SUBLANE - (n_sc - 1), 0), (0, 0)))
    row = lambda b, n: b * nstep + n
    kern = functools.partial(_gdn_kernel, t_valid=t_valid if n_sub == 1 else rs, n_kh=n_kh, n_vh=n_vh, dk=dk,
                             dv=dv, n_sub=n_sub)
    return pl.pallas_call(
        kern,
        grid=(bsz, nstep),
        in_specs=[pl.BlockSpec((rs, conv_dim), lambda b, n: (row(b, n), 0)),
                  pl.BlockSpec((rs, nv), lambda b, n: (row(b, n), zb)),
                  pl.BlockSpec((rs, LANE), lambda b, n: (row(b, n), abb)),
                  pl.BlockSpec((rs, sc), lambda b, n: (row(b, n), gbb)),
                  pl.BlockSpec((rs, sc), lambda b, n: (row(b, n), gbb + 1)),
                  pl.BlockSpec((rs, sc), lambda b, n: (row(b, n), gbb + 2)),
                  pl.BlockSpec((None, SUBLANE, conv_dim), lambda b, n: (b, 0, 0)),
                  pl.BlockSpec((None, SUBLANE, sc), lambda b, n: (b, 0, 0)),
                  pl.BlockSpec((n_conv, conv_dim), lambda b, n: (0, 0)),
                  pl.BlockSpec((n_sc, sc), lambda b, n: (0, 0)),
                  pl.BlockSpec((1, LANE), lambda b, n: (0, 0)),
                  pl.BlockSpec((1, LANE), lambda b, n: (0, 0)),
                  pl.BlockSpec((1, dv), lambda b, n: (0, 0)),
                  pl.BlockSpec((None, n_vh, dk, dv), lambda b, n: (b, 0, 0, 0))],
        out_specs=[pl.BlockSpec((rs, nv + sc), lambda b, n: (row(b, n), 0)),
                   pl.BlockSpec((None, n_vh, dk, dv), lambda b, n: (b, 0, 0, 0)),
                   pl.BlockSpec((None, c, sc), lambda b, n: (b, 0, 0))],
        out_shape=[jax.ShapeDtypeStruct((rows, nv + sc), BF16),
                   jax.ShapeDtypeStruct((bsz, n_vh, dk, dv), F32),
                   jax.ShapeDtypeStruct((bsz, c, sc), F32)],
        scratch_shapes=[pltpu.VMEM((rs + SUBLANE, conv_dim), F32),
                        pltpu.VMEM((rs + SUBLANE, sc), F32),
                        pltpu.VMEM((n_vh, dk, dv), F32)],
        compiler_params=_params(("arbitrary", "arbitrary")),
        name="gdn_sconv",
    )(proj, proj, proj, proj, proj, proj, cst, scst, conv_w, sc_w,
      pad_lane(a_log), pad_lane(dt_bias), norm_g.reshape(1, dv), s0.astype(F32))


def _cmp_prompt_kernel(kv_ref, pw_ref, pj_ref, o_ref, *, blk):
    t, hd = kv_ref.shape
    x = kv_ref[...].reshape(t // blk, blk, hd) * pw_ref[...][None]
    o_ref[...] = _bdot(jnp.sum(x, axis=1), pj_ref[...])


def _cmp_prompt(proj, bsz, t, pw, pj, col0):
    nch, blk, hd = pw.shape
    return pl.pallas_call(
        functools.partial(_cmp_prompt_kernel, blk=blk),
        grid=(bsz, nch),
        in_specs=[pl.BlockSpec((t, hd), lambda b, ch: (b, col0 + ch)),
                  pl.BlockSpec((None, blk, hd), lambda b, ch: (ch, 0, 0)),
                  pl.BlockSpec((None, hd, hd), lambda b, ch: (ch, 0, 0))],
        out_specs=pl.BlockSpec((None, None, t // blk, hd), lambda b, ch: (b, ch, 0, 0)),
        out_shape=jax.ShapeDtypeStruct((bsz, nch, t // blk, hd), F32),
        compiler_params=_params(("arbitrary", "arbitrary")),
        name="cmp_prompt",
    )(proj, pw, pj)


_NT = (((1,), (1,)), ((), ()))


def _slope_features(sl2, lane):
    hi = sl2.astype(BF16).astype(F32)
    lo = sl2 - hi
    return jnp.where(lane == 0, 64.0 * hi, jnp.where(lane == 1, hi, jnp.where(lane == 2, 64.0 * lo,
                                                                             jnp.where(lane == 3, lo, 0.0))))


def _pool_blocks(page_refs, pw_ref, o_ref):
    pw = pw_ref[...]
    rows_blk = pw.shape[0]
    per_page = page_refs[0].shape[0] // rows_blk
    nch = o_ref.shape[0] // (len(page_refs) * per_page)
    for k, page_ref in enumerate(page_refs):
        for hb in range(per_page):
            r = k * per_page + hb
            x = page_ref[hb * rows_blk:(hb + 1) * rows_blk, :] * pw
            o_ref[r * nch:(r + 1) * nch, :] = jnp.sum(x.reshape(rows_blk // nch, nch, x.shape[-1]), axis=0)


def _attn_prompt_kernel(*refs, tq, tk, wk, blk, n_g, n_kv, hd, n_pool):
    if n_pool:
        refs = refs[1:]
    q_ref, kc_ref, vc_ref, ks_ref, vs_ref, kw_ref, vw_ref, gt_ref, sl_ref, pos_ref, oh_ref = refs[:11]
    n_in = 11 + n_pool + (1 if n_pool else 0)
    o_ref = refs[n_in]
    ksb, vsb, kwb, vwb = refs[-4:]
    h = pl.program_id(1)
    qt = pl.program_id(2)

    @pl.when(qt == 0)
    def _():
        ksb[...] = ks_ref[...].astype(BF16)
        vsb[...] = vs_ref[...].astype(BF16)
        kwb[...] = kw_ref[...].astype(BF16)
        vwb[...] = vw_ref[...].astype(BF16)
    q0 = qt * tq
    nc = kc_ref.shape[0]
    t_all = ks_ref.shape[0]
    sl = sl_ref[...]
    lane = lax.broadcasted_iota(jnp.int32, (1, LANE), 1)
    qs = [q_ref[:, g * hd:(g + 1) * hd] * (hd ** -0.5) for g in range(n_g)]
    qb = jnp.concatenate(qs, axis=0).astype(BF16)
    qaug = jnp.concatenate(
        [jnp.concatenate([(qs[g] * LOG2E).astype(BF16),
                          jnp.broadcast_to(_slope_features(sl[:, g:g + 1] * LOG2E, lane), (tq, LANE)).astype(BF16)],
                         axis=1) for g in range(n_g)], axis=0)
    qpos = q0 + lax.broadcasted_iota(jnp.int32, (tq, 1), 0)

    bidx = lax.broadcasted_iota(jnp.int32, (1, nc), 1)
    start = bidx * blk
    s_c = _bdot_nt(qb, kc_ref[...])
    dist_c = jnp.abs(qpos.astype(F32) - (start.astype(F32) + (blk - 1) / 2))
    valid_c = (start + (blk - 1)) <= qpos
    p_c = jnp.concatenate([_masked_softmax(s_c[g * tq:(g + 1) * tq] - sl[:, g:g + 1] * dist_c, valid_c)
                           for g in range(n_g)], axis=0)
    o_c = _bdot(p_c, vc_ref[...])
    imp = p_c[0:tq]
    for g in range(1, n_g):
        imp = imp + p_c[g * tq:(g + 1) * tq]

    imp_t = jnp.concatenate([imp, jnp.zeros((tq, LANE - nc), F32)], axis=1).T[0:nc, :]
    bcol = lax.broadcasted_iota(jnp.int32, (nc, 1), 0)
    cur = (q0 + lax.broadcasted_iota(jnp.int32, (1, tq), 1)) // blk
    score = jnp.where(bcol < cur, imp_t, NEG)
    score = jnp.where((bcol == 0) | (bcol == cur), FORCE, score)
    rank = jnp.zeros((nc, tq), F32)
    for i in range(nc):
        ri = score[i:i + 1, :]
        rank = rank + jnp.where((ri > score) | ((ri == score) & (bcol > i)), 1.0, 0.0)
    selneg_t = jnp.where((rank < min(SEL_TOPK, nc)) & (score > 0.5 * NEG), 0.0, NEG)
    selneg = jnp.concatenate([selneg_t, jnp.zeros((LANE - nc, tq), F32)], axis=0).T.astype(BF16)

    def flash_step(k_ref, v_ref, k0, size, bias, carry):
        m, l, acc = carry
        kaug = jnp.concatenate([k_ref[pl.ds(k0, size), :], pos_ref[pl.ds(k0, size), :]], axis=1)
        s = lax.dot_general(qaug, kaug, _NT, preferred_element_type=F32)
        gs = range(n_g)
        sgs = [s[g * tq:(g + 1) * tq] + bias for g in gs]
        m2 = [jnp.maximum(m[g], jnp.max(sgs[g], axis=-1, keepdims=True)) for g in gs]
        als = [jnp.exp2(m[g] - m2[g]) for g in gs]
        pfs = [jnp.exp2(sgs[g] - m2[g]) for g in gs]
        l2 = [als[g] * l[g] + jnp.sum(pfs[g], axis=-1, keepdims=True) for g in gs]
        ps = [p.astype(BF16) for p in pfs]
        accs = [als[g] * acc[g * tq:(g + 1) * tq] for g in gs]
        pv = jnp.dot(jnp.concatenate(ps, axis=0), v_ref[pl.ds(k0, size), :],
                     preferred_element_type=F32)
        return tuple(m2), tuple(l2), jnp.concatenate(accs, axis=0) + pv

    def finish(carry):
        _, l, acc = carry
        return [acc[g * tq:(g + 1) * tq] / jnp.maximum(l[g], 1e-30) for g in range(n_g)]

    init = (tuple(jnp.full((tq, 1), NEG, F32) for _ in range(n_g)),
            tuple(jnp.zeros((tq, 1), F32) for _ in range(n_g)), jnp.zeros((n_g * tq, hd), F32))

    def sel_bias(k0):
        return lax.dot_general(selneg, oh_ref[pl.ds(k0, tk), :], _NT, preferred_element_type=F32)

    def sel_chunk(ci, carry):
        k0 = pl.multiple_of(ci * tk, tk)
        return flash_step(ksb, vsb, k0, tk, sel_bias(k0), carry)

    n_full = q0 // tk
    carry = lax.fori_loop(0, n_full, sel_chunk, init)
    k0 = pl.multiple_of(n_full * tk, tk)
    kpos = k0 + lax.broadcasted_iota(jnp.int32, (1, tk), 1)
    o_s = finish(flash_step(ksb, vsb, k0, tk, sel_bias(k0) + jnp.where(kpos <= qpos, 0.0, NEG), carry))

    w0 = pl.multiple_of(jnp.clip(q0 + tq - wk, 0, t_all - wk), tq)
    dist = qpos - (w0 + lax.broadcasted_iota(jnp.int32, (1, wk), 1))
    o_w = finish(flash_step(kwb, vwb, w0, wk, jnp.where((dist >= 0) & (dist < WINDOW), 0.0, NEG), init))

    gt = jax.nn.sigmoid(pltpu.roll(gt_ref[...], (LANE - n_g * h) % LANE, 1))
    n_h = n_g * n_kv
    for g in range(n_g):
        r = slice(g * tq, (g + 1) * tq)
        o = (gt[:, g:g + 1] * o_c[r] + gt[:, n_h + g:n_h + g + 1] * o_s[g]
             + gt[:, 2 * n_h + g:2 * n_h + g + 1] * o_w[g])
        o_ref[:, g * hd:(g + 1) * hd] = o.astype(o_ref.dtype)

    if n_pool:
        _pool_blocks(refs[11:11 + n_pool], refs[11 + n_pool], refs[n_in + 1])


def _attn_prompt_tiles(t):
    return 256, t // 256


def _attn_prompt(proj, cmp, slopes, bsz, t, n_h, n_kv, hd, blk, pool=None):
    n_g = n_h // n_kv
    tq, nqt = _attn_prompt_tiles(t)
    tk = _pick(t, (512, 256, 128))
    wk = WINDOW + tq
    assert t % tq == 0 and t >= wk and t // blk <= LANE and blk == 64 and n_h * 3 <= LANE
    nc = t // blk
    kvb = n_h
    pos = jnp.arange(t, dtype=jnp.int32)[:, None]
    lane = jnp.arange(LANE, dtype=jnp.int32)[None, :]
    pos_tab = jnp.where(lane < 4, jnp.where(lane % 2 == 0, pos // 64, pos % 64), 0).astype(BF16)
    onehot = (lane == pos // blk).astype(BF16)
    gate_blk = (n_h * hd + 6 * n_kv * hd) // LANE
    n_pool = pool[5] if pool else 0
    kern = functools.partial(_attn_prompt_kernel, tq=tq, tk=tk, wk=wk, blk=blk, n_g=n_g, n_kv=n_kv, hd=hd,
                             n_pool=n_pool)
    kv_spec = lambda off: pl.BlockSpec((t, hd), lambda b, h, q, *_: (b, kvb + off + h))
    in_specs = [pl.BlockSpec((tq, n_g * hd), lambda b, h, q, *_: (b * nqt + q, h)),
                pl.BlockSpec((None, None, nc, hd), lambda b, h, q, *_: (b, h, 0, 0)),
                pl.BlockSpec((None, None, nc, hd), lambda b, h, q, *_: (b, n_kv + h, 0, 0)),
                kv_spec(2 * n_kv), kv_spec(3 * n_kv), kv_spec(4 * n_kv), kv_spec(5 * n_kv),
                pl.BlockSpec((tq, LANE), lambda b, h, q, *_: (b * nqt + q, gate_blk)),
                pl.BlockSpec((None, 1, LANE), lambda b, h, q, *_: (h, 0, 0)),
                pl.BlockSpec((t, LANE), lambda b, h, q, *_: (0, 0)),
                pl.BlockSpec((t, LANE), lambda b, h, q, *_: (0, 0))]
    out_specs = [pl.BlockSpec((tq, n_g * hd), lambda b, h, q, *_: (b * nqt + q, h))]
    out_shape = [jax.ShapeDtypeStruct((bsz * t, n_h * hd), BF16)]
    operands = [proj, cmp, cmp, proj, proj, proj, proj, proj, slopes, pos_tab, onehot]
    prefetch = []
    if pool:
        cache, layer, page_table, pw_rows, nch, _ = pool
        _, _, prow, _ = cache.shape
        bd, n_pages = page_table.shape
        rows_blk = pw_rows.shape[0]
        assert bsz * n_kv == bd and nqt * n_pool == n_pages
        out_rows = n_pool * (prow // rows_blk) * nch
        for k in range(n_pool):
            in_specs.append(pl.BlockSpec((None, None, prow, hd),
                                         lambda b, h, q, pt, k=k: (layer, pt[b * n_kv + h, q * n_pool + k], 0, 0)))
        in_specs.append(pl.BlockSpec((rows_blk, hd), lambda b, h, q, pt: (0, 0)))
        out_specs.append(pl.BlockSpec((None, out_rows, hd), lambda b, h, q, pt: (b * n_kv + h, q, 0)))
        out_shape.append(jax.ShapeDtypeStruct((bd, nqt * out_rows, hd), F32))
        operands += [cache] * n_pool + [pw_rows]
        prefetch = [page_table]
    outs = pl.pallas_call(
        kern,
        grid_spec=pltpu.PrefetchScalarGridSpec(num_scalar_prefetch=len(prefetch), grid=(bsz, n_kv, nqt),
                                               in_specs=in_specs, out_specs=out_specs,
                                               scratch_shapes=[pltpu.VMEM((t, hd), BF16)] * 4),
        out_shape=out_shape,
        compiler_params=_params(("arbitrary", "arbitrary", "arbitrary")),
        name="attn_prompt",
    )(*prefetch, *operands)
    return outs if pool else outs[0]


def _pool_pages_kernel(pt_ref, *refs, n_pp):
    _pool_blocks(refs[:n_pp], refs[n_pp], refs[n_pp + 1])


def _page_map(b, p, pt, *, layer, k, n_pp):
    return (layer, pt[b, p * n_pp + k], 0, 0)


def _pool_pages(cache, layer, page_table, pw_rows, nch):
    _, _, prow, hd = cache.shape
    bd, n_pages = page_table.shape
    rows_blk = pw_rows.shape[0]
    n_pp = _pick(n_pages, (8, 4, 2, 1))
    per_page = prow // rows_blk
    in_specs = [pl.BlockSpec((None, None, prow, hd),
                             functools.partial(_page_map, layer=layer, k=k, n_pp=n_pp)) for k in range(n_pp)]
    in_specs.append(pl.BlockSpec((rows_blk, hd), lambda b, p, pt: (0, 0)))
    return pl.pallas_call(
        functools.partial(_pool_pages_kernel, n_pp=n_pp),
        grid_spec=pltpu.PrefetchScalarGridSpec(
            num_scalar_prefetch=1, grid=(bd, n_pages // n_pp), in_specs=in_specs,
            out_specs=pl.BlockSpec((None, n_pp * per_page * nch, hd), lambda b, p, pt: (b, p, 0))),
        out_shape=jax.ShapeDtypeStruct((bd, n_pages * per_page * nch, hd), F32),
        compiler_params=_params(("arbitrary", "arbitrary")),
        name="pool_pages",
    )(page_table, *([cache] * n_pp), pw_rows)


def _attn_sample_a_kernel(q_ref, pooled_ref, pj_ref, kvw_ref, gt_ref, sl_ref, ocw_ref, selm_ref,
                          score_buf, *, past, n_q, n_kv, n_g, hd, blk, wb, nbp):
    nch = 2 * n_kv
    nc = pooled_ref.shape[0] // nch
    n_blocks = (past + n_q + blk - 1) // blk
    kvw_rows = kvw_ref.shape[0]
    half = n_kv * hd
    rep = lambda x: jnp.concatenate([x] * n_g, axis=0)
    sl = sl_ref[...]
    gt = jax.nn.sigmoid(gt_ref[...])
    qpos = past + lax.broadcasted_iota(jnp.int32, (n_q, 1), 0)
    qpos_r = rep(qpos)
    bidx_c = lax.broadcasted_iota(jnp.int32, (1, nc), 1)
    start = bidx_c * blk
    bidx = lax.broadcasted_iota(jnp.int32, (1, nbp), 1)
    cur = qpos // blk
    score_buf[...] = jnp.full(score_buf.shape, DEAD, F32)

    for h in range(n_kv):
        qb = jnp.concatenate([q_ref[:, (h * n_g + g) * hd:(h * n_g + g + 1) * hd] for g in range(n_g)], axis=0)
        qb = (qb * (hd ** -0.5)).astype(BF16)
        slope = jnp.concatenate(
            [jnp.broadcast_to(sl[:, h * n_g + g:h * n_g + g + 1], (n_q, 1)) for g in range(n_g)], axis=0)
        kc = _bdot(pooled_ref[pl.ds(h, nc, stride=nch), :], pj_ref[h])
        vc = _bdot(pooled_ref[pl.ds(n_kv + h, nc, stride=nch), :], pj_ref[n_kv + h])
        s = _bdot_nt(qb, kc)
        dist_c = qpos_r.astype(F32) - (start.astype(F32) + (blk - 1) / 2)
        s = s - slope * jnp.abs(dist_c)
        p_c = _masked_softmax(s, (start + (blk - 1)) <= qpos_r)
        o_c = _bdot(p_c, vc)
        imp = p_c[0:n_q]
        for g in range(1, n_g):
            imp = imp + p_c[g * n_q:(g + 1) * n_q]
        imp = jnp.concatenate([imp, jnp.zeros((n_q, nbp - nc), F32)], axis=1)
        score = jnp.where(bidx < cur, imp, NEG)
        score = jnp.where((bidx == 0) | (bidx == cur), FORCE, score)
        score = jnp.where(bidx < n_blocks, score, DEAD)
        score_buf[h * n_q:(h + 1) * n_q, :] = score

        kw = kvw_ref[:, h * hd:(h + 1) * hd]
        vw = kvw_ref[:, half + h * hd:half + (h + 1) * hd]
        kidx = lax.broadcasted_iota(jnp.int32, (1, kvw_rows), 1)
        dist = qpos_r - (past - wb + kidx)
        valid = (dist >= 0) & (dist < WINDOW) & (kidx < wb + n_q)
        sw = _bdot_nt(qb, kw) - slope * jnp.abs(dist).astype(F32)
        o_w = _bdot(_masked_softmax(sw, valid), vw)
        for g in range(n_g):
            hh = h * n_g + g
            r = slice(g * n_q, (g + 1) * n_q)
            ocw_ref[:, hh * hd:(hh + 1) * hd] = (gt[:, hh:hh + 1] * o_c[r]
                                                 + gt[:, 2 * n_kv * n_g + hh:2 * n_kv * n_g + hh + 1] * o_w[r])

    sc_all = score_buf[...]
    sc_t = sc_all.T
    ii = lax.broadcasted_iota(jnp.int32, (nbp, 1), 0)
    for h in range(n_kv):
        for t in range(n_q):
            r = h * n_q + t
            col = sc_t[:, r:r + 1]
            row = sc_all[r:r + 1, :]
            before = (col > row) | ((col == row) & (ii < bidx))
            rank = jnp.sum(jnp.where(before, 1.0, 0.0), axis=0, keepdims=True)
            sel = jnp.where((rank < min(SEL_TOPK, n_blocks)) & (row > 0.5 * NEG), 1.0, 0.0)
            for g in range(n_g):
                selm_ref[h, g * n_q + t:g * n_q + t + 1, :] = sel


def _attn_sample_a(proj3, pooled, pj, kv_all, slopes_row, past, n_h, n_kv, hd, blk, wb):
    bd, n_q, _ = proj3.shape
    n_g = n_h // n_kv
    n_blocks = (past + n_q + blk - 1) // blk
    nbp = pl.cdiv(n_blocks, LANE) * LANE
    kvw_rows = kv_all.shape[1]
    qw = n_h * hd
    assert n_kv * n_q <= LANE
    kern = functools.partial(_attn_sample_a_kernel, past=past, n_q=n_q, n_kv=n_kv, n_g=n_g, hd=hd, blk=blk,
                             wb=wb, nbp=nbp)
    return pl.pallas_call(
        kern,
        grid=(bd,),
        in_specs=[pl.BlockSpec((None, n_q, qw), lambda b: (b, 0, 0)),
                  pl.BlockSpec((None, pooled.shape[1], hd), lambda b: (b, 0, 0)),
                  pl.BlockSpec((2 * n_kv, hd, hd), lambda b: (0, 0, 0)),
                  pl.BlockSpec((None, kvw_rows, 2 * n_kv * hd), lambda b: (b, 0, 0)),
                  pl.BlockSpec((None, n_q, LANE), lambda b: (b, 0, (qw + 6 * n_kv * hd) // LANE)),
                  pl.BlockSpec((1, LANE), lambda b: (0, 0))],
        out_specs=[pl.BlockSpec((None, n_q, qw), lambda b: (b, 0, 0)),
                   pl.BlockSpec((None, n_kv, n_g * n_q, nbp), lambda b: (b, 0, 0, 0))],
        out_shape=[jax.ShapeDtypeStruct((bd, n_q, qw), F32),
                   jax.ShapeDtypeStruct((bd, n_kv, n_g * n_q, nbp), F32)],
        scratch_shapes=[pltpu.VMEM((LANE, nbp), F32)],
        compiler_params=_params(("arbitrary",)),
        name="attn_sample_a",
    )(proj3, pooled, pj, kv_all, proj3, slopes_row)


def _attn_sample_sel_kernel(pt_ref, *refs, n_pp, past, n_q, n_kv, n_g, hd, blk):
    pages = refs[:n_pp]
    q_ref, selm_ref, new_ref, ocw_ref, gt_ref, sl_ref, o_ref, m_ref, l_ref, acc_ref = refs[n_pp:]
    p = pl.program_id(1)
    nch = 2 * n_kv
    page = pages[0].shape[0] // nch
    keys = n_pp * page
    nbp = selm_ref.shape[-1]
    rows = n_g * n_q
    half = n_kv * hd
    rep = lambda x: jnp.concatenate([x] * n_g, axis=0)
    sl = sl_ref[...]
    qpos_r = rep(past + lax.broadcasted_iota(jnp.int32, (n_q, 1), 0))

    @pl.when(p == 0)
    def _():
        m_ref[...] = jnp.full(m_ref.shape, NEG, F32)
        l_ref[...] = jnp.zeros_like(l_ref)
        acc_ref[...] = jnp.zeros_like(acc_ref)

    def heads():
        for h in range(n_kv):
            qb = jnp.concatenate([q_ref[:, (h * n_g + g) * hd:(h * n_g + g + 1) * hd] for g in range(n_g)],
                                 axis=0) * (hd ** -0.5)
            slope = jnp.concatenate(
                [jnp.broadcast_to(sl[:, h * n_g + g:h * n_g + g + 1], (n_q, 1)) for g in range(n_g)], axis=0)
            yield h, qb, slope

    def update(h, sc, mask, pv_fn):
        sc = jnp.where(mask, sc, NEG)
        m = m_ref[h]
        m_new = jnp.maximum(m, jnp.max(sc, axis=-1, keepdims=True))
        alpha = jnp.exp(m - m_new)
        pr = jnp.where(mask, jnp.exp(sc - m_new), 0.0)
        l_ref[h] = alpha * l_ref[h] + jnp.sum(pr, axis=-1, keepdims=True)
        acc_ref[h] = alpha * acc_ref[h] + pv_fn(pr)
        m_ref[h] = m_new

    k0 = p * keys
    kk = lax.broadcasted_iota(jnp.int32, (nbp, keys), 1)
    nn = lax.broadcasted_iota(jnp.int32, (nbp, keys), 0)
    expand = jnp.where(nn == (k0 + kk) // blk, 1.0, 0.0).astype(BF16)
    dist = qpos_r - (k0 + lax.broadcasted_iota(jnp.int32, (1, keys), 1))
    staged = []
    for h, qb, slope in heads():
        mask = (jnp.dot(selm_ref[h].astype(BF16), expand, preferred_element_type=F32) > 0.5) & (dist >= 0)
        kx = jnp.concatenate([pages[k][pl.ds(h, page, stride=nch), :].astype(BF16) for k in range(n_pp)], axis=0)
        sc = jnp.where(mask, _bdot_nt(qb, kx) - slope * dist.astype(F32), NEG)
        staged.append((h, mask, sc))
    probs = []
    for h, mask, sc in staged:
        m = m_ref[h]
        m_new = jnp.maximum(m, jnp.max(sc, axis=-1, keepdims=True))
        alpha = jnp.exp(m - m_new)
        pr = jnp.where(mask, jnp.exp(sc - m_new), 0.0)
        l_ref[h] = alpha * l_ref[h] + jnp.sum(pr, axis=-1, keepdims=True)
        m_ref[h] = m_new
        probs.append((h, alpha, pr))
    for h, alpha, pr in probs:
        vx = jnp.concatenate([pages[k][pl.ds(n_kv + h, page, stride=nch), :].astype(BF16) for k in range(n_pp)],
                             axis=0)
        acc_ref[h] = alpha * acc_ref[h] + _bdot(pr, vx)

    @pl.when(p == pl.num_programs(1) - 1)
    def _():
        gt = jax.nn.sigmoid(gt_ref[...])
        for h, qb, slope in heads():
            nb0 = past // blk
            seln = selm_ref[h][:, nb0:nb0 + 1] > 0.5
            qr = qb.astype(BF16).astype(F32)
            for j in range(n_q):
                kj = new_ref[j:j + 1, h * hd:(h + 1) * hd].astype(BF16).astype(F32)
                vj = new_ref[j:j + 1, half + h * hd:half + (h + 1) * hd].astype(BF16).astype(F32)
                dj = qpos_r - (past + j)
                sc = jnp.sum(qr * kj, axis=-1, keepdims=True) - slope * dj.astype(F32)
                update(h, sc, seln & (dj >= 0), lambda pr, vj=vj: pr.astype(BF16).astype(F32) * vj)
            o_s = acc_ref[h] / jnp.maximum(l_ref[h], 1e-30)
            for g in range(n_g):
                hh = h * n_g + g
                r = slice(g * n_q, (g + 1) * n_q)
                gcol = n_kv * n_g + hh
                o = ocw_ref[:, hh * hd:(hh + 1) * hd] + gt[:, gcol:gcol + 1] * o_s[r]
                o_ref[:, hh * hd:(hh + 1) * hd] = o.astype(o_ref.dtype)


def _attn_sample_sel(cache, layer, page_table, proj3, selm, new_rows, ocw, slopes_row, past, n_h, n_kv, hd, blk):
    _, _, prow, _ = cache.shape
    bd, n_pages = page_table.shape
    n_q = proj3.shape[1]
    n_g = n_h // n_kv
    qw = n_h * hd
    nbp = selm.shape[-1]
    n_pp = _pick(n_pages, (32, 16, 8, 4, 2, 1))
    rows = n_g * n_q
    in_specs = [pl.BlockSpec((None, None, prow, hd),
                             functools.partial(_page_map, layer=layer, k=k, n_pp=n_pp)) for k in range(n_pp)]
    in_specs += [pl.BlockSpec((None, n_q, qw), lambda b, p, pt: (b, 0, 0)),
                 pl.BlockSpec((None, n_kv, rows, nbp), lambda b, p, pt: (b, 0, 0, 0)),
                 pl.BlockSpec((None, new_rows.shape[1], new_rows.shape[2]), lambda b, p, pt: (b, 0, 0)),
                 pl.BlockSpec((None, n_q, qw), lambda b, p, pt: (b, 0, 0)),
                 pl.BlockSpec((None, n_q, LANE), lambda b, p, pt: (b, 0, (qw + 6 * n_kv * hd) // LANE)),
                 pl.BlockSpec((1, LANE), lambda b, p, pt: (0, 0))]
    kern = functools.partial(_attn_sample_sel_kernel, n_pp=n_pp, past=past, n_q=n_q, n_kv=n_kv, n_g=n_g,
                             hd=hd, blk=blk)
    return pl.pallas_call(
        kern,
        grid_spec=pltpu.PrefetchScalarGridSpec(
            num_scalar_prefetch=1, grid=(bd, n_pages // n_pp), in_specs=in_specs,
            out_specs=pl.BlockSpec((None, n_q, qw), lambda b, p, pt: (b, 0, 0)),
            scratch_shapes=[pltpu.VMEM((n_kv, rows, 1), F32), pltpu.VMEM((n_kv, rows, 1), F32),
                            pltpu.VMEM((n_kv, rows, hd), F32)]),
        out_shape=jax.ShapeDtypeStruct((bd, n_q, qw), BF16),
        compiler_params=_params(("arbitrary", "arbitrary")),
        name="attn_sample_sel",
    )(page_table, *([cache] * n_pp), proj3, selm, new_rows, ocw, proj3, slopes_row)


def _alibi(n_h):
    h = jnp.arange(1, n_h + 1, dtype=F32)
    return jnp.exp2(-8.0 * h / n_h)


def kernel(x_prompt, x_sample, c_prompt, c_sample, state_gdn, state_gdn_conv, state_sconv, cache_kv_cmp,
           cache_kv_sel, state_kv_win, page_table, norm_g, w_ada, b_ada, w_in_e, conv_w_gdn, a_log, dt_bias,
           gdn_norm_g, conv_w_sc, w_out_e, w_in_o, cmp_pool, cmp_proj, w_out_o, w_mlp1, w_mlp2, final_g):
    bsz, seq, d = x_prompt.shape
    bd, n_q, _ = x_sample.shape
    depth = norm_g.shape[0]
    n_vh, dk, dv = state_gdn.shape[2:]
    conv_dim = conv_w_gdn.shape[2]
    sc_dim = conv_w_sc.shape[2]
    nv = n_vh * dv
    blk, _, n_kv = cmp_pool.shape[1:]
    hd = cmp_proj.shape[-1]
    n_h = w_out_o.shape[1] // hd
    kv_row = 2 * n_kv * hd
    qw = n_h * hd
    page = cache_kv_cmp.shape[2]
    past = page_table.shape[1] * page
    wb = state_kv_win.shape[2]
    c = GDN_CHUNK
    assert seq % c == 0 and seq % blk == 0 and seq >= WINDOW and n_q <= c and n_q < blk and past % blk == 0
    assert n_q >= conv_w_gdn.shape[1] - 1 and hd == LANE and dk == LANE and dv == LANE

    rows_c = -(-(bsz + bd) // SUBLANE) * SUBLANE
    c_all = jnp.pad(jnp.concatenate([c_prompt, c_sample], axis=0).astype(F32), ((0, rows_c - bsz - bd), (0, 0)))
    mod = _ada_mod(c_all, w_ada, b_ada).reshape(depth, rows_c, 6, d)

    def mods(l, sample):
        if sample:
            return [jnp.repeat(mod[l, bsz:bsz + bd, j], n_q, axis=0)[None] for j in range(6)]
        return [mod[l, :bsz, j][:, None, :] for j in range(6)]

    slopes = _alibi(n_h)
    slopes_kv = jnp.pad(slopes.reshape(n_kv, 1, n_h // n_kv), ((0, 0), (0, 0), (0, LANE - n_h // n_kv)))
    slopes_row = jnp.pad(slopes.reshape(1, n_h), ((0, 0), (0, LANE - n_h)))
    nch = 2 * n_kv
    cache_cmp = cache_kv_cmp.reshape(cache_kv_cmp.shape[:2] + (page * nch, hd))
    cache_sel = cache_kv_sel.reshape(cache_kv_sel.shape[:2] + (page * nch, hd))

    w_in_e_t = jnp.swapaxes(w_in_e, 1, 2).astype(BF16)
    w_in_o_t = jnp.swapaxes(w_in_o, 1, 2).astype(BF16)

    def even_tail(i):
        o_ab = conv_dim + nv
        o_rest = o_ab + 2 * n_vh
        return jnp.concatenate([w_in_e_t[i, o_rest:], w_in_e_t[i, o_ab:o_rest],
                                jnp.zeros((LANE - 2 * n_vh, d), w_in_e_t.dtype)], axis=0)

    pooled_by_layer = {}

    def mixer(x, l, sample, ev, od):
        nb, t = (bd, n_q) if sample else (bsz, seq)
        i = l // 2
        sh1, sc1, g1 = mods(l, sample)[:3]
        if l % 2 == 0:
            proj = _norm_mod_matmul(x, norm_g[l, 0], sc1, sh1, w_in_e_t, i, main_cols=conv_dim + nv,
                                    w_tail=even_tail(i))
            n_cst = conv_w_gdn.shape[1] - 1
            conv_new = proj.reshape(nb, t, -1)[:, t - n_cst:, :conv_dim]
            if sample:
                projp = jnp.pad(proj.reshape(nb, t, -1), ((0, 0), (0, c - t), (0, 0))).reshape(nb * c, -1)
                cst, scst, s0, tv = state_gdn_conv[i], state_sconv[i], state_gdn[i], t
            else:
                projp = proj
                cst = jnp.zeros((nb, conv_w_gdn.shape[1] - 1, conv_dim), F32)
                scst = jnp.zeros((nb, conv_w_sc.shape[1] - 1, sc_dim), F32)
                s0, tv = jnp.zeros((nb, n_vh, dk, dv), F32), c
            mix, s_new, u_last = _gdn_sconv(projp, cst, scst, s0, conv_w_gdn[i], conv_w_sc[i], a_log[i],
                                            dt_bias[i], gdn_norm_g[i], tv)
            if sample:
                mix = mix.reshape(nb, c, -1)[:, :t].reshape(nb * t, -1)
            sc_new = u_last[:, tv - scst.shape[1]:tv]
            ev.append((s_new, conv_new, sc_new))
            return _matmul_gated_residual(mix, w_out_e, i, x, g1)

        proj, kvc8, kvs8, kvw8 = _norm_mod_matmul(x, norm_g[l, 0], sc1, sh1, w_in_o_t, i,
                                                  rows_out=(qw, 3, nch, hd))
        kv_c, kv_s, kv_w = [a.reshape(nb, t, 2, n_kv, hd) for a in (kvc8, kvs8, kvw8)]
        pj = cmp_proj[i].reshape(2 * n_kv, hd, hd)
        pw_rows = jnp.broadcast_to(cmp_pool[i].reshape(blk * nch, 1), (blk * nch, hd))
        if sample:
            pooled = pooled_by_layer.get(i)
            if pooled is None:
                pooled = _pool_pages(cache_cmp, i, page_table, pw_rows, nch)
            proj3 = proj.reshape(nb, t, -1)
            kv_all = jnp.concatenate([state_kv_win[i].reshape(nb, wb, kv_row).astype(F32),
                                      proj3[:, :, qw + 2 * kv_row:qw + 3 * kv_row]], axis=1)
            kv_all_p = jnp.pad(kv_all, ((0, 0), (0, (-kv_all.shape[1]) % SUBLANE), (0, 0)))
            ocw, selm = _attn_sample_a(proj3, pooled, pj, kv_all_p, slopes_row, past, n_h, n_kv, hd, blk, wb)
            new_rows = jnp.pad(proj3[:, :, qw + kv_row:qw + 2 * kv_row], ((0, 0), (0, (-t) % SUBLANE), (0, 0)))
            o = _attn_sample_sel(cache_sel, i, page_table, proj3, selm, new_rows, ocw, slopes_row, past,
                                 n_h, n_kv, hd, blk).reshape(nb * t, qw)
            kv_win_new = kv_all[:, t:].reshape(nb, wb, 2, n_kv, hd)
        else:
            pw = jnp.broadcast_to(cmp_pool[i].reshape(blk, 2 * n_kv).T[:, :, None], (2 * n_kv, blk, hd))
            cmp = _cmp_prompt(proj, nb, t, pw, pj, n_h)
            n_steps = nb * n_kv * _attn_prompt_tiles(t)[1]
            n_pool = (bd * page_table.shape[1]) // n_steps
            if nb * n_kv == bd and n_pool * n_steps == bd * page_table.shape[1] and 1 <= n_pool <= 8:
                o, pooled_by_layer[i] = _attn_prompt(proj, cmp, slopes_kv, nb, t, n_h, n_kv, hd, blk,
                                                     pool=(cache_cmp, i, page_table, pw_rows, nch, n_pool))
            else:
                o = _attn_prompt(proj, cmp, slopes_kv, nb, t, n_h, n_kv, hd, blk)
            kv_win_new = kv_w[:, t - min(WINDOW, t):]
        od.append((kv_c, kv_s, kv_win_new))
        return _matmul_gated_residual(o, w_out_o, i, x, g1)

    xp = x_prompt.reshape(bsz * seq, d).astype(F32)
    xs = x_sample.reshape(bd * n_q, d).astype(F32)
    ev_p, od_p, ev_s, od_s = [], [], [], []
    for l in range(depth):
        xp = mixer(xp, l, False, ev_p, od_p)
        xs = mixer(xs, l, True, ev_s, od_s)
        _, _, _, sh2, sc2, g2 = mods(l, False)
        _, _, _, sh2s, sc2s, g2s = mods(l, True)
        xp, xs = _mlp(xp, norm_g[l, 1], sc2, sh2, g2, w_mlp1, w_mlp2, l, final_g if l == depth - 1 else None,
                      side=(xs, sc2s, sh2s, g2s))
    stack3 = lambda items: [jnp.stack([s[j] for s in items]) for j in range(3)]
    gdn_p, gconv_p, sconv_p = stack3(ev_p)
    gdn_s, gconv_s, sconv_s = stack3(ev_s)
    kvc_p, kvs_p, kvw_p = stack3(od_p)
    kvc_s, kvs_s, kvw_s = stack3(od_s)
    y_p = xp.reshape(bsz, seq, d)
    y_s = xs.reshape(bd, n_q, d)
    return (y_p, y_s, gdn_p, gdn_s, gconv_p, gconv_s, sconv_p, sconv_s, kvc_p, kvc_s, kvs_p, kvs_s, kvw_p, kvw_s)
```
